```python
import jax, jax.numpy as jnp
from jax import lax
import numpy as np

D_MODEL = 4096
BATCH = 4
SEQ = 4096
DEPTH = 1

ATT_HEADS = 32
ATT_KV_HEADS = 4
ATT_HEAD_DIM = 64
ATT_GROUP = ATT_HEADS // ATT_KV_HEADS
ATT_Q_WIDTH = ATT_HEADS * ATT_HEAD_DIM
ATT_KV_WIDTH = ATT_KV_HEADS * ATT_HEAD_DIM
WINDOW = 128
ATT_BLOCK = 128
RET_HEADS = 8
RET_QK_DIM = 256
RET_V_DIM = 512
RET_QK_WIDTH = RET_HEADS * RET_QK_DIM
RET_V_WIDTH = RET_HEADS * RET_V_DIM
RET_CHUNK = 128
IN_SIZES = (ATT_Q_WIDTH, ATT_KV_WIDTH, ATT_KV_WIDTH, RET_QK_WIDTH, RET_QK_WIDTH, RET_V_WIDTH, RET_V_WIDTH, D_MODEL, D_MODEL)
N_IN = ATT_Q_WIDTH + 2 * ATT_KV_WIDTH + 2 * RET_QK_WIDTH + 2 * RET_V_WIDTH + 2 * D_MODEL
N_EXPERTS = 64
N_GROUPS = 8
EXPERTS_PER_GROUP = N_EXPERTS // N_GROUPS
TOPK_GROUPS = 4
TOP_K = 8
EXPERT_DIM = 512
SHARED_DIM = 512
ROUTED_SCALE = 2.5
DISPATCH_BLOCK = 256
EPS = 1e-6

kernel_name = "hybrid_swa_sink_retention_moe_adaln"


def rms_norm(x, g):
    xf = x.astype(jnp.float32)
    y = xf * lax.rsqrt(jnp.mean(xf * xf, axis=-1, keepdims=True) + EPS)
    return (y * g.astype(jnp.float32)).astype(x.dtype)


def modulate(h, shift, scale):
    return h * (1.0 + scale[:, None, :]) + shift[:, None, :]


def alibi_slopes(n):
    return jnp.exp2(-8.0 * jnp.arange(1, n + 1, dtype=jnp.float32) / n)


def sliding_window_attention(q, k, v, sinks):
    b, s = q.shape[0], q.shape[1]
    nb = s // ATT_BLOCK
    qb = q.reshape(b, nb, ATT_BLOCK, ATT_KV_HEADS, ATT_GROUP, ATT_HEAD_DIM)

    def band(t):
        tb = t.reshape(b, nb, ATT_BLOCK, ATT_KV_HEADS, ATT_HEAD_DIM)
        prev = jnp.pad(tb, ((0, 0), (1, 0), (0, 0), (0, 0), (0, 0)))[:, :-1]
        return jnp.concatenate([prev, tb], axis=2)

    kb, vb = band(k), band(v)
    scores = jnp.einsum('bnqhgd,bnkhd->bnhgqk', qb, kb).astype(jnp.float32) * (ATT_HEAD_DIM ** -0.5)
    qi = jnp.arange(ATT_BLOCK)[:, None]
    kj = jnp.arange(2 * ATT_BLOCK)[None, :]
    dist = qi + ATT_BLOCK - kj
    key_pos = jnp.arange(nb)[:, None, None] * ATT_BLOCK - ATT_BLOCK + kj[None]
    valid = (dist >= 0) & (dist < WINDOW) & (key_pos >= 0)
    slopes = alibi_slopes(ATT_HEADS).reshape(ATT_KV_HEADS, ATT_GROUP)
    scores = scores - slopes[:, :, None, None] * dist.astype(jnp.float32)
    scores = jnp.where(valid[None, :, None, None], scores, -jnp.inf)
    sink = sinks.astype(jnp.float32).reshape(ATT_KV_HEADS, ATT_GROUP)[:, :, None, None]
    m = jnp.maximum(scores.max(-1, keepdims=True), sink)
    p = jnp.exp(scores - m)
    probs = p / (p.sum(-1, keepdims=True) + jnp.exp(sink - m))
    out = jnp.einsum('bnhgqk,bnkhd->bnqhgd', probs.astype(v.dtype), vb)
    return out.reshape(b, s, ATT_Q_WIDTH)


def retention(q, k, v):
    b, s = q.shape[0], q.shape[1]
    n = s // RET_CHUNK
    log_g = jnp.log1p(-jnp.exp2(-5.0 - jnp.arange(RET_HEADS, dtype=jnp.float32)))
    pos = jnp.arange(RET_CHUNK, dtype=jnp.float32)
    rel = pos[:, None] - pos[None, :]
    decay_mask = jnp.where(rel[None] >= 0, jnp.exp(rel[None] * log_g[:, None, None]), 0.0)
    q_decay = jnp.exp((pos[:, None] + 1.0) * log_g[None])
    k_decay = jnp.exp((RET_CHUNK - 1.0 - pos[:, None]) * log_g[None])
    chunk_decay = jnp.exp(RET_CHUNK * log_g)

    def chunks(t):
        return t.astype(jnp.float32).reshape(b, n, RET_CHUNK, RET_HEADS, t.shape[-1]).swapaxes(0, 1)

    qc, kc, vc = chunks(q), chunks(k * (RET_QK_DIM ** -0.5)), chunks(v)

    def step(state, xs):
        qi, ki, vi = xs
        attn = jnp.einsum('bihd,bjhd->bhij', qi, ki) * decay_mask
        intra = jnp.einsum('bhij,bjhe->bihe', attn, vi)
        inter = jnp.einsum('bihd,bhde->bihe', qi, state) * q_decay[None, :, :, None]
        state = state * chunk_decay[None, :, None, None] + jnp.einsum('bjhd,bjhe->bhde', ki * k_decay[None, :, :, None], vi)
        return state, intra + inter

    state0 = jnp.zeros((b, RET_HEADS, RET_QK_DIM, RET_V_DIM), jnp.float32)
    _, o = lax.scan(step, state0, (qc, kc, vc))
    o = o.swapaxes(0, 1).reshape(b, s, RET_HEADS, RET_V_DIM)
    mu = o.mean(-1, keepdims=True)
    var = jnp.mean(jnp.square(o - mu), axis=-1, keepdims=True)
    return (o - mu) * lax.rsqrt(var + EPS)


def swiglu(h, wg, wu, wd):
    return (jax.nn.silu(h @ wg) * (h @ wu)) @ wd


def route(h, w_router, b_router):
    t = h.shape[0]
    scores = jax.nn.sigmoid((h @ w_router).astype(jnp.float32))
    choice = scores + b_router.astype(jnp.float32)
    grp = choice.reshape(t, N_GROUPS, EXPERTS_PER_GROUP)
    grp_score = lax.top_k(grp, 2)[0].sum(-1)
    _, grp_idx = lax.top_k(grp_score, TOPK_GROUPS)
    grp_mask = jax.nn.one_hot(grp_idx, N_GROUPS, dtype=jnp.float32).sum(1) > 0
    expert_mask = jnp.repeat(grp_mask, EXPERTS_PER_GROUP, axis=-1)
    _, idx = lax.top_k(jnp.where(expert_mask, choice, -jnp.inf), TOP_K)
    w = jnp.take_along_axis(scores, idx, axis=-1)
    w = w / w.sum(-1, keepdims=True) * ROUTED_SCALE
    return idx, w


def routed_experts(h, idx, w, w_gate, w_up, w_down):
    t, d = h.shape
    a = t * TOP_K
    flat_e = idx.reshape(a)
    flat_w = w.reshape(a).astype(h.dtype)
    order = jnp.argsort(flat_e)
    sorted_e = flat_e[order]
    tok = (order // TOP_K).astype(jnp.int32)
    counts = jnp.bincount(flat_e, length=N_EXPERTS)
    padded = (counts + DISPATCH_BLOCK - 1) // DISPATCH_BLOCK * DISPATCH_BLOCK
    pad_end = jnp.cumsum(padded)
    pad_start = pad_end - padded
    start = jnp.cumsum(counts) - counts
    dest = pad_start[sorted_e] + (jnp.arange(a) - start[sorted_e])
    n_blocks = (a + DISPATCH_BLOCK - 1) // DISPATCH_BLOCK + N_EXPERTS
    p = n_blocks * DISPATCH_BLOCK
    buf_tok = jnp.full((p,), t, jnp.int32).at[dest].set(tok)
    buf_w = jnp.zeros((p,), h.dtype).at[dest].set(flat_w[order])
    blk_e = jnp.minimum(jnp.searchsorted(pad_end, jnp.arange(n_blocks) * DISPATCH_BLOCK, side='right'), N_EXPERTS - 1)
    h_pad = jnp.concatenate([h, jnp.zeros((1, d), h.dtype)], axis=0)

    def body(acc, xs):
        tok_b, w_b, e = xs
        y = swiglu(h_pad[tok_b], w_gate[e], w_up[e], w_down[e])
        return acc.at[tok_b].add(y * w_b[:, None]), None

    acc0 = jnp.zeros((t + 1, d), h.dtype)
    acc, _ = lax.scan(body, acc0, (buf_tok.reshape(n_blocks, DISPATCH_BLOCK), buf_w.reshape(n_blocks, DISPATCH_BLOCK), blk_e))
    return acc[:t]


def setup_inputs(seed: int = 0) -> dict:
    key = jax.random.key(seed)
    ks = jax.random.split(key, 22)
    f32 = jnp.float32

    def nrm(k, shape, fan_in, s=1.0):
        return (s * fan_in ** -0.5) * jax.random.normal(k, shape, f32)

    D = D_MODEL
    return {
        "x": jax.random.normal(ks[0], (BATCH, SEQ, D), f32),
        "c": jax.random.normal(ks[1], (BATCH, D), f32),
        "w_ada": nrm(ks[2], (DEPTH, D, 6 * D), D, 0.5),
        "b_ada": 0.02 * jax.random.normal(ks[3], (DEPTH, 6 * D), f32),
        "g_norm_mix": 1.0 + 0.05 * jax.random.normal(ks[4], (DEPTH, D), f32),
        "w_in": nrm(ks[5], (DEPTH, D, N_IN), D),
        "attn_sinks": 0.5 * jax.random.normal(ks[6], (DEPTH, ATT_HEADS), f32),
        "w_attn_out": nrm(ks[7], (DEPTH, ATT_Q_WIDTH, D), ATT_Q_WIDTH),
        "w_ret_out": nrm(ks[8], (DEPTH, RET_V_WIDTH, D), RET_V_WIDTH),
        "w_o": nrm(ks[9], (DEPTH, D, D), D),
        "g_norm_ffn": 1.0 + 0.05 * jax.random.normal(ks[10], (DEPTH, D), f32),
        "w_router": nrm(ks[11], (DEPTH, D, N_EXPERTS), D),
        "b_router": 0.01 * jax.random.normal(ks[12], (DEPTH, N_EXPERTS), f32),
        "w_gate": nrm(ks[13], (DEPTH, N_EXPERTS, D, EXPERT_DIM), D),
        "w_up": nrm(ks[14], (DEPTH, N_EXPERTS, D, EXPERT_DIM), D),
        "w_down": nrm(ks[15], (DEPTH, N_EXPERTS, EXPERT_DIM, D), EXPERT_DIM),
        "w_sh_gate": nrm(ks[16], (DEPTH, D, SHARED_DIM), D),
        "w_sh_up": nrm(ks[17], (DEPTH, D, SHARED_DIM), D),
        "w_sh_down": nrm(ks[18], (DEPTH, SHARED_DIM, D), SHARED_DIM),
        "g_norm_final": 1.0 + 0.05 * jax.random.normal(ks[19], (D,), f32),
    }


def reference(x, c, w_ada, b_ada, g_norm_mix, w_in, attn_sinks, w_attn_out, w_ret_out, w_o, g_norm_ffn, w_router, b_router, w_gate, w_up, w_down, w_sh_gate, w_sh_up, w_sh_down, g_norm_final):
    b, s, d = x.shape
    offsets = []
    acc_off = 0
    for size in IN_SIZES[:-1]:
        acc_off += size
        offsets.append(acc_off)
    cond = jax.nn.silu(c)
    for l in range(DEPTH):
        mod = cond @ w_ada[l] + b_ada[l]
        sh1, sc1, gt1, sh2, sc2, gt2 = jnp.split(mod, 6, axis=-1)

        h = modulate(rms_norm(x, g_norm_mix[l]), sh1, sc1)
        proj = h @ w_in[l]
        qa, ka, va, qr, kr, vr, gr, gate_a, gate_b = jnp.split(proj, offsets, axis=-1)
        ya = sliding_window_attention(
            qa.reshape(b, s, ATT_HEADS, ATT_HEAD_DIM),
            ka.reshape(b, s, ATT_KV_HEADS, ATT_HEAD_DIM),
            va.reshape(b, s, ATT_KV_HEADS, ATT_HEAD_DIM),
            attn_sinks[l]) @ w_attn_out[l]
        ret = retention(
            qr.reshape(b, s, RET_HEADS, RET_QK_DIM),
            kr.reshape(b, s, RET_HEADS, RET_QK_DIM),
            vr.reshape(b, s, RET_HEADS, RET_V_DIM))
        yr = (jax.nn.silu(gr.astype(jnp.float32)) * ret.reshape(b, s, RET_V_WIDTH)).astype(x.dtype) @ w_ret_out[l]
        mix = jax.nn.sigmoid(gate_a) * ya + jax.nn.sigmoid(gate_b) * yr
        x = x + gt1[:, None, :] * (mix @ w_o[l])

        h2 = modulate(rms_norm(x, g_norm_ffn[l]), sh2, sc2).reshape(b * s, d)
        idx, wts = route(h2, w_router[l], b_router[l])
        y = routed_experts(h2, idx, wts, w_gate[l], w_up[l], w_down[l]) + swiglu(h2, w_sh_gate[l], w_sh_up[l], w_sh_down[l])
        x = x + gt2[:, None, :] * y.reshape(b, s, d)
    return rms_norm(x, g_norm_final)
```

```python
import functools

import jax
import jax.numpy as jnp
from jax import lax
from jax.experimental import pallas as pl
from jax.experimental.pallas import tpu as pltpu

ATT_HEADS = 32
ATT_KV_HEADS = 4
ATT_HEAD_DIM = 64
WINDOW = 128
ATT_BLOCK = 128
RET_HEADS = 8
RET_QK_DIM = 256
RET_V_DIM = 512
RET_CHUNK = 128
N_EXPERTS = 64
N_GROUPS = 8
TOPK_GROUPS = 4
TOP_K = 8
ROUTED_SCALE = 2.5
EPS = 1e-6

V7X_LANES = 128
V7X_SUBLANES = 8
V7X_VMEM_LIMIT_BYTES = 60000 * 1024

MOE_TILE = 256
COMBINE_TILE = 128
NEG_BIG = -1e30


def _div_block(n, target, align):
    best = None
    b = align
    while b <= min(n, target):
        if n % b == 0:
            best = b
        b += align
    assert best is not None, (n, target, align)
    return best


def _params(semantics):
    return pltpu.CompilerParams(dimension_semantics=semantics,
                                vmem_limit_bytes=V7X_VMEM_LIMIT_BYTES)


def _sigmoid(v):
    return 1.0 / (1.0 + jnp.exp(-v))


def _silu(v):
    return v * _sigmoid(v)


def _bf16_bits(v):
    u = pltpu.bitcast(v, jnp.uint32)
    r = u + jnp.uint32(0x7FFF) + ((u >> 16) & jnp.uint32(1))
    return r & jnp.uint32(0xFFFF0000)


def _pack_pair(lo, hi):
    return (_bf16_bits(lo) >> 16) | _bf16_bits(hi)


def _unpack_pair(p):
    lo = pltpu.bitcast(p << 16, jnp.float32)
    hi = pltpu.bitcast(p & jnp.uint32(0xFFFF0000), jnp.float32)
    return lo, hi


def _slab_load(ref, n_rows, slab):
    return jnp.concatenate([ref[pl.ds(s, n_rows, stride=slab), :] for s in range(slab)], axis=1)


def _slab_store(ref, val, n_rows, slab):
    for s in range(slab):
        ref[pl.ds(s, n_rows, stride=slab), :] = val[:, s * V7X_LANES:(s + 1) * V7X_LANES]


def _slab_rows(r, slab):
    return pl.ds(pl.multiple_of(r * slab, slab), slab)


def _ada_kernel(c_ref, w_ref, b_ref, o_ref):
    cs = _silu(c_ref[...]).astype(jnp.bfloat16)
    o_ref[...] = jnp.dot(cs, w_ref[...].astype(jnp.bfloat16),
                         preferred_element_type=jnp.float32) + b_ref[...]


def _ada(c_pad, w, b):
    m, d = c_pad.shape
    n = w.shape[1]
    tn = _div_block(n, 512, V7X_LANES)
    return pl.pallas_call(
        _ada_kernel,
        grid=(n // tn,),
        in_specs=[pl.BlockSpec((m, d), lambda j: (0, 0)),
                  pl.BlockSpec((d, tn), lambda j: (0, j)),
                  pl.BlockSpec((1, tn), lambda j: (0, j))],
        out_specs=pl.BlockSpec((m, tn), lambda j: (0, j)),
        out_shape=jax.ShapeDtypeStruct((m, n), jnp.float32),
        compiler_params=_params(("parallel",)),
        name="ada",
    )(c_pad, w, b.reshape(1, n))


def _norm_mod_kernel(x_ref, g_ref, sh_ref, sc_ref, o_ref):
    x = x_ref[0]
    ms = jnp.mean(x * x, axis=-1, keepdims=True)
    y = x * lax.rsqrt(ms + EPS) * g_ref[...]
    o_ref[0] = (y * (1.0 + sc_ref[0]) + sh_ref[0]).astype(o_ref.dtype)


def _norm_mod(x3, g, mod3, shift_idx, scale_idx):
    b, s, d = x3.shape
    ts = _div_block(s, 256, V7X_SUBLANES)
    return pl.pallas_call(
        _norm_mod_kernel,
        grid=(b, s // ts),
        in_specs=[pl.BlockSpec((1, ts, d), lambda bi, i: (bi, i, 0)),
                  pl.BlockSpec((1, d), lambda bi, i: (0, 0)),
                  pl.BlockSpec((1, 1, d), lambda bi, i: (bi * 6 + shift_idx, 0, 0)),
                  pl.BlockSpec((1, 1, d), lambda bi, i: (bi * 6 + scale_idx, 0, 0))],
        out_specs=pl.BlockSpec((1, ts, d), lambda bi, i: (bi, i, 0)),
        out_shape=jax.ShapeDtypeStruct((b, s, d), jnp.bfloat16),
        compiler_params=_params(("parallel", "parallel")),
        name="norm_mod",
    )(x3, g.reshape(1, d), mod3, mod3)


def _mm_kernel(a_ref, b_ref, o_ref):
    o_ref[...] = jnp.dot(a_ref[...], b_ref[...],
                         preferred_element_type=jnp.float32).astype(o_ref.dtype)


def _matmul(a, b, out_dtype, tm_target, tn_target):
    m, k = a.shape
    n = b.shape[1]
    tm = _div_block(m, tm_target, V7X_SUBLANES)
    tn = _div_block(n, tn_target, V7X_LANES)
    return pl.pallas_call(
        _mm_kernel,
        grid=(m // tm, n // tn),
        in_specs=[pl.BlockSpec((tm, k), lambda i, j: (i, 0)),
                  pl.BlockSpec((k, tn), lambda i, j: (0, j))],
        out_specs=pl.BlockSpec((tm, tn), lambda i, j: (i, j)),
        out_shape=jax.ShapeDtypeStruct((m, n), out_dtype),
        compiler_params=_params(("parallel", "parallel")),
        name="matmul",
    )(a, b)


def _attn_kernel(sink_ref, q_ref, kc_ref, kp_ref, vc_ref, vp_ref, o_ref):
    i = pl.program_id(1)
    blk = ATT_BLOCK
    hd = ATT_HEAD_DIM
    group = ATT_HEADS // ATT_KV_HEADS
    qi = lax.broadcasted_iota(jnp.int32, (blk, blk), 0)
    kj = lax.broadcasted_iota(jnp.int32, (blk, blk), 1)
    dist_c = (qi - kj).astype(jnp.float32)
    dist_p = dist_c + float(blk)
    valid_c = qi >= kj
    valid_p = (kj > qi) & (i > 0)
    nt = (((1,), (1,)), ((), ()))
    for h in range(ATT_KV_HEADS):
        kc = kc_ref[0, :, h * hd:(h + 1) * hd]
        kp = kp_ref[0, :, h * hd:(h + 1) * hd]
        vc = vc_ref[0, :, h * hd:(h + 1) * hd]
        vp = vp_ref[0, :, h * hd:(h + 1) * hd]
        for g in range(group):
            hq = h * group + g
            slope = 2.0 ** (-8.0 * (hq + 1) / ATT_HEADS)
            q = q_ref[0, :, hq * hd:(hq + 1) * hd]
            sc = lax.dot_general(q, kc, nt, preferred_element_type=jnp.float32) * (hd ** -0.5)
            sp = lax.dot_general(q, kp, nt, preferred_element_type=jnp.float32) * (hd ** -0.5)
            sc = jnp.where(valid_c, sc - slope * dist_c, NEG_BIG)
            sp = jnp.where(valid_p, sp - slope * dist_p, NEG_BIG)
            sink = sink_ref[hq]
            m = jnp.maximum(jnp.maximum(sc.max(-1, keepdims=True), sp.max(-1, keepdims=True)), sink)
            pc = jnp.exp(sc - m)
            pp = jnp.exp(sp - m)
            denom = pc.sum(-1, keepdims=True) + pp.sum(-1, keepdims=True) + jnp.exp(sink - m)
            acc = jnp.dot(pc.astype(vc.dtype), vc, preferred_element_type=jnp.float32)
            acc = acc + jnp.dot(pp.astype(vp.dtype), vp, preferred_element_type=jnp.float32)
            o_ref[0, :, hq * hd:(hq + 1) * hd] = (acc / denom).astype(o_ref.dtype)


def _attention(proj3, sinks, q_off, k_off, v_off):
    b, s, _ = proj3.shape
    qw = ATT_HEADS * ATT_HEAD_DIM
    kvw = ATT_KV_HEADS * ATT_HEAD_DIM
    nb = s // ATT_BLOCK
    assert q_off % qw == 0 and k_off % kvw == 0 and v_off % kvw == 0
    qb, kb, vb = q_off // qw, k_off // kvw, v_off // kvw
    return pl.pallas_call(
        _attn_kernel,
        grid_spec=pltpu.PrefetchScalarGridSpec(
            num_scalar_prefetch=1,
            grid=(b, nb),
            in_specs=[pl.BlockSpec((1, ATT_BLOCK, qw), lambda bi, i, sk: (bi, i, qb)),
                      pl.BlockSpec((1, ATT_BLOCK, kvw), lambda bi, i, sk: (bi, i, kb)),
                      pl.BlockSpec((1, ATT_BLOCK, kvw), lambda bi, i, sk: (bi, jnp.maximum(i - 1, 0), kb)),
                      pl.BlockSpec((1, ATT_BLOCK, kvw), lambda bi, i, sk: (bi, i, vb)),
                      pl.BlockSpec((1, ATT_BLOCK, kvw), lambda bi, i, sk: (bi, jnp.maximum(i - 1, 0), vb))],
            out_specs=pl.BlockSpec((1, ATT_BLOCK, qw), lambda bi, i, sk: (bi, i, 0)),
        ),
        out_shape=jax.ShapeDtypeStruct((b, s, qw), jnp.bfloat16),
        compiler_params=_params(("parallel", "parallel")),
        name="attention",
    )(sinks, proj3, proj3, proj3, proj3, proj3)


def _ret_kernel(q_ref, k_ref, v_ref, gr_ref, mask_ref, qd_ref, kd_ref, cd_ref, o_ref, state_ref):
    c = pl.program_id(2)

    @pl.when(c == 0)
    def _():
        state_ref[...] = jnp.zeros_like(state_ref)

    q = q_ref[0]
    k = k_ref[0]
    v = v_ref[0]
    nt = (((1,), (1,)), ((), ()))
    tn = (((0,), (0,)), ((), ()))
    attn = lax.dot_general(q, k, nt, preferred_element_type=jnp.float32) * mask_ref[0]
    intra = jnp.dot(attn.astype(v.dtype), v, preferred_element_type=jnp.float32)
    state = state_ref[...]
    inter = jnp.dot(q, state.astype(q.dtype), preferred_element_type=jnp.float32) * qd_ref[0]
    o = intra + inter
    kd = (k.astype(jnp.float32) * kd_ref[0]).astype(k.dtype)
    state_ref[...] = state * cd_ref[0] + lax.dot_general(kd, v, tn, preferred_element_type=jnp.float32)
    mu = jnp.mean(o, axis=-1, keepdims=True)
    oc = o - mu
    var = jnp.mean(oc * oc, axis=-1, keepdims=True)
    y = oc * lax.rsqrt(var + EPS)
    o_ref[0] = (_silu(gr_ref[0].astype(jnp.float32)) * y).astype(o_ref.dtype)


def _retention(proj3, q_off, k_off, v_off, g_off):
    b, s, _ = proj3.shape
    dk, dv, ch = RET_QK_DIM, RET_V_DIM, RET_CHUNK
    assert q_off % dk == 0 and k_off % dk == 0 and v_off % dv == 0 and g_off % dv == 0
    qb, kb, vb, gb = q_off // dk, k_off // dk, v_off // dv, g_off // dv
    n = s // ch
    log_g = jnp.log1p(-jnp.exp2(-5.0 - jnp.arange(RET_HEADS, dtype=jnp.float32)))
    pos = jnp.arange(ch, dtype=jnp.float32)
    rel = pos[:, None] - pos[None, :]
    scale = dk ** -0.5
    mask = jnp.where(rel[None] >= 0, jnp.exp(rel[None] * log_g[:, None, None]), 0.0) * scale
    q_decay = jnp.exp((pos[None, :, None] + 1.0) * log_g[:, None, None])
    k_decay = jnp.exp((ch - 1.0 - pos[None, :, None]) * log_g[:, None, None]) * scale
    c_decay = jnp.exp(ch * log_g)[:, None, None]
    return pl.pallas_call(
        _ret_kernel,
        grid=(b, RET_HEADS, n),
        in_specs=[pl.BlockSpec((1, ch, dk), lambda bi, h, c: (bi, c, qb + h)),
                  pl.BlockSpec((1, ch, dk), lambda bi, h, c: (bi, c, kb + h)),
                  pl.BlockSpec((1, ch, dv), lambda bi, h, c: (bi, c, vb + h)),
                  pl.BlockSpec((1, ch, dv), lambda bi, h, c: (bi, c, gb + h)),
                  pl.BlockSpec((1, ch, ch), lambda bi, h, c: (h, 0, 0)),
                  pl.BlockSpec((1, ch, 1), lambda bi, h, c: (h, 0, 0)),
                  pl.BlockSpec((1, ch, 1), lambda bi, h, c: (h, 0, 0)),
                  pl.BlockSpec((1, 1, 1), lambda bi, h, c: (h, 0, 0))],
        out_specs=pl.BlockSpec((1, ch, dv), lambda bi, h, c: (bi, c, h)),
        out_shape=jax.ShapeDtypeStruct((b, s, RET_HEADS * dv), jnp.bfloat16),
        scratch_shapes=[pltpu.VMEM((dk, dv), jnp.float32)],
        compiler_params=_params(("parallel", "parallel", "arbitrary")),
        name="retention",
    )(proj3, proj3, proj3, proj3, mask, q_decay, k_decay, c_decay)


def _mix_kernel(a_ref, r_ref, wa_ref, wr_ref, ga_ref, gb_ref, o_ref):
    ya = jnp.dot(a_ref[...], wa_ref[...], preferred_element_type=jnp.float32)
    yr = jnp.dot(r_ref[...], wr_ref[...], preferred_element_type=jnp.float32)
    ga = _sigmoid(ga_ref[...].astype(jnp.float32))
    gb = _sigmoid(gb_ref[...].astype(jnp.float32))
    o_ref[...] = (ga * ya + gb * yr).astype(o_ref.dtype)


def _mix(attn2, ret2, wa, wr, proj2, ga_off, gb_off):
    m, ka = attn2.shape
    kr = ret2.shape[1]
    d = wa.shape[1]
    tm = _div_block(m, 512, V7X_SUBLANES)
    tn = _div_block(d, 512, V7X_LANES)
    assert ga_off % tn == 0 and gb_off % tn == 0
    gab, gbb = ga_off // tn, gb_off // tn
    return pl.pallas_call(
        _mix_kernel,
        grid=(m // tm, d // tn),
        in_specs=[pl.BlockSpec((tm, ka), lambda i, j: (i, 0)),
                  pl.BlockSpec((tm, kr), lambda i, j: (i, 0)),
                  pl.BlockSpec((ka, tn), lambda i, j: (0, j)),
                  pl.BlockSpec((kr, tn), lambda i, j: (0, j)),
                  pl.BlockSpec((tm, tn), lambda i, j: (i, gab + j)),
                  pl.BlockSpec((tm, tn), lambda i, j: (i, gbb + j))],
        out_specs=pl.BlockSpec((tm, tn), lambda i, j: (i, j)),
        out_shape=jax.ShapeDtypeStruct((m, d), jnp.bfloat16),
        compiler_params=_params(("parallel", "parallel")),
        name="mix",
    )(attn2, ret2, wa, wr, proj2, proj2)


def _resid_kernel(a_ref, w_ref, x_ref, gt_ref, o_ref):
    y = jnp.dot(a_ref[...], w_ref[...], preferred_element_type=jnp.float32)
    o_ref[...] = x_ref[...] + gt_ref[0] * y


def _out_resid(mix2, w, x2, mod3, gate_idx, seq):
    m, k = mix2.shape
    d = w.shape[1]
    tm = _div_block(seq, 1024, V7X_SUBLANES)
    tn = _div_block(d, 512, V7X_LANES)
    per_b = seq // tm
    return pl.pallas_call(
        _resid_kernel,
        grid=(m // tm, d // tn),
        in_specs=[pl.BlockSpec((tm, k), lambda i, j: (i, 0)),
                  pl.BlockSpec((k, tn), lambda i, j: (0, j)),
                  pl.BlockSpec((tm, tn), lambda i, j: (i, j)),
                  pl.BlockSpec((1, 1, tn), lambda i, j: ((i // per_b) * 6 + gate_idx, 0, j))],
        out_specs=pl.BlockSpec((tm, tn), lambda i, j: (i, j)),
        out_shape=jax.ShapeDtypeStruct((m, d), jnp.float32),
        compiler_params=_params(("parallel", "parallel")),
        name="out_resid",
    )(mix2, w, x2, mod3)


def _router_kernel(x_ref, g_ref, sh_ref, sc_ref, wr_ref, br_ref,
                   h_ref, hp_ref, idx_ref, pos_ref, w_ref, cnt_ref, carry_ref):
    i = pl.program_id(0)
    e = N_EXPERTS
    per_g = e // N_GROUPS
    tb = x_ref.shape[0]

    @pl.when(i == 0)
    def _():
        carry_ref[...] = jnp.zeros_like(carry_ref)

    x = x_ref[...]
    ms = jnp.mean(x * x, axis=-1, keepdims=True)
    h = x * lax.rsqrt(ms + EPS) * g_ref[...]
    h = h * (1.0 + sc_ref[0]) + sh_ref[0]
    hb = h.astype(jnp.bfloat16)
    h_ref[...] = hb
    half = h.shape[1] // 2
    _slab_store(hp_ref, _pack_pair(h[:, :half], h[:, half:]), tb, half // V7X_LANES)

    nt = (((1,), (1,)), ((), ()))
    logits = lax.dot_general(wr_ref[...], hb, nt, preferred_element_type=jnp.float32)
    scores = _sigmoid(logits)
    choice = scores + br_ref[...]

    c3 = choice.reshape(N_GROUPS, per_g, tb)
    j_iota = lax.broadcasted_iota(jnp.int32, c3.shape, 1).astype(jnp.float32)
    m1 = c3.max(axis=1, keepdims=True)
    first = jnp.min(jnp.where(c3 == m1, j_iota, float(per_g)), axis=1, keepdims=True)
    m2 = jnp.where(j_iota == first, -jnp.inf, c3).max(axis=1, keepdims=True)
    gs = (m1 + m2).reshape(N_GROUPS, tb)

    g_iota = lax.broadcasted_iota(jnp.int32, gs.shape, 0)
    grank = jnp.zeros(gs.shape, jnp.int32)
    for gp in range(N_GROUPS):
        row = gs[gp:gp + 1, :]
        ahead = (row > gs) | ((row == gs) & (gp < g_iota))
        grank = grank + ahead.astype(jnp.int32)
    gmask = grank < TOPK_GROUPS
    emask = jnp.broadcast_to(gmask.reshape(N_GROUPS, 1, tb), c3.shape).reshape(e, tb)
    masked = jnp.where(emask, choice, -jnp.inf)

    e_iota = lax.broadcasted_iota(jnp.int32, masked.shape, 0)
    erank = jnp.zeros(masked.shape, jnp.int32)
    for ep in range(e):
        row = masked[ep:ep + 1, :]
        ahead = (row > masked) | ((row == masked) & (ep < e_iota))
        erank = erank + ahead.astype(jnp.int32)
    sel = (erank < TOP_K) & emask
    self32 = sel.astype(jnp.float32)

    wsel = scores * self32
    wn = wsel / jnp.sum(wsel, axis=0, keepdims=True) * ROUTED_SCALE

    selb = self32.astype(jnp.bfloat16)
    t_r = lax.broadcasted_iota(jnp.int32, (tb, tb), 0)
    t_c = lax.broadcasted_iota(jnp.int32, (tb, tb), 1)
    upper = (t_r <= t_c).astype(jnp.bfloat16)
    incl = jnp.dot(selb, upper, preferred_element_type=jnp.float32)
    carry = carry_ref[...]
    rank_in_e = carry + incl - 1.0
    carry_new = carry + jnp.sum(self32, axis=1, keepdims=True)
    carry_ref[...] = carry_new
    cnt_ref[...] = jnp.broadcast_to(carry_new, cnt_ref.shape).astype(jnp.int32)

    e_r = lax.broadcasted_iota(jnp.int32, (e, e), 0)
    e_c = lax.broadcasted_iota(jnp.int32, (e, e), 1)
    lower = (e_c < e_r).astype(jnp.bfloat16)
    before = jnp.dot(lower, selb, preferred_element_type=jnp.float32)
    e_f = e_iota.astype(jnp.float32)
    idx_rows, pos_rows, w_rows = [], [], []
    for k in range(TOP_K):
        hit = jnp.where(sel & (before == float(k)), 1.0, 0.0)
        idx_rows.append(jnp.sum(hit * e_f, axis=0, keepdims=True))
        pos_rows.append(jnp.sum(hit * rank_in_e, axis=0, keepdims=True))
        w_rows.append(jnp.sum(hit * wn, axis=0, keepdims=True))
    idx_ref[0] = jnp.concatenate(idx_rows, axis=0).astype(jnp.int32)
    pos_ref[0] = jnp.concatenate(pos_rows, axis=0).astype(jnp.int32)
    w_ref[0] = jnp.concatenate(w_rows, axis=0)


def _router(x1, g, mod3, shift_idx, scale_idx, w_router_t, b_router, seq):
    t, d = x1.shape
    e = N_EXPERTS
    tb = MOE_TILE
    assert seq % tb == 0 and d % (2 * V7X_LANES * V7X_SUBLANES) == 0
    slab = d // 2 // V7X_LANES
    per_b = seq // tb
    nt = t // tb
    return pl.pallas_call(
        _router_kernel,
        grid=(nt,),
        in_specs=[pl.BlockSpec((tb, d), lambda i: (i, 0)),
                  pl.BlockSpec((1, d), lambda i: (0, 0)),
                  pl.BlockSpec((1, 1, d), lambda i: ((i // per_b) * 6 + shift_idx, 0, 0)),
                  pl.BlockSpec((1, 1, d), lambda i: ((i // per_b) * 6 + scale_idx, 0, 0)),
                  pl.BlockSpec((e, d), lambda i: (0, 0)),
                  pl.BlockSpec((e, 1), lambda i: (0, 0))],
        out_specs=[pl.BlockSpec((tb, d), lambda i: (i, 0)),
                   pl.BlockSpec((tb * slab, V7X_LANES), lambda i: (i, 0)),
                   pl.BlockSpec((1, TOP_K, tb), lambda i: (i, 0, 0)),
                   pl.BlockSpec((1, TOP_K, tb), lambda i: (i, 0, 0)),
                   pl.BlockSpec((1, TOP_K, tb), lambda i: (i, 0, 0)),
                   pl.BlockSpec((e, V7X_LANES), lambda i: (0, 0))],
        out_shape=[jax.ShapeDtypeStruct((t, d), jnp.bfloat16),
                   jax.ShapeDtypeStruct((t * slab, V7X_LANES), jnp.uint32),
                   jax.ShapeDtypeStruct((nt, TOP_K, tb), jnp.int32),
                   jax.ShapeDtypeStruct((nt, TOP_K, tb), jnp.int32),
                   jax.ShapeDtypeStruct((nt, TOP_K, tb), jnp.float32),
                   jax.ShapeDtypeStruct((e, V7X_LANES), jnp.int32)],
        scratch_shapes=[pltpu.VMEM((e, 1), jnp.float32)],
        compiler_params=_params(("arbitrary",)),
        name="router",
    )(x1, g.reshape(1, d), mod3, mod3, w_router_t, b_router.reshape(e, 1))


def _dispatch_kernel(pstart_ref, pend_ref, hp_ref, idx_hbm, pos_hbm, xs_hbm,
                     idx_s, pos_s, zero_v, sem_i, sem_z, sem_r, *, slab):
    i = pl.program_id(0)
    tb = hp_ref.shape[0] // slab

    def idx_copy():
        return pltpu.make_async_copy(idx_hbm.at[i], idx_s, sem_i.at[0])

    def pos_copy():
        return pltpu.make_async_copy(pos_hbm.at[i], pos_s, sem_i.at[1])

    idx_copy().start()
    pos_copy().start()

    def zero_copy(ex):
        first = pl.multiple_of((pend_ref[ex] - tb) * slab, tb * slab)
        return pltpu.make_async_copy(zero_v, xs_hbm.at[pl.ds(first, tb * slab)], sem_z)

    @pl.when(i == 0)
    def _():
        zero_v[...] = jnp.zeros_like(zero_v)

        def start(ex, carry):
            @pl.when(pend_ref[ex] > pstart_ref[ex])
            def _():
                zero_copy(ex).start()
            return carry

        def wait(ex, carry):
            @pl.when(pend_ref[ex] > pstart_ref[ex])
            def _():
                zero_copy(ex).wait()
            return carry

        lax.fori_loop(0, N_EXPERTS, start, 0)
        lax.fori_loop(0, N_EXPERTS, wait, 0)

    idx_copy().wait()
    pos_copy().wait()

    def row_copy(t, k):
        dest = pstart_ref[idx_s[k, t]] + pos_s[k, t]
        return pltpu.make_async_copy(hp_ref.at[_slab_rows(t, slab)], xs_hbm.at[_slab_rows(dest, slab)], sem_r)

    def start_rows(t, carry):
        for k in range(TOP_K):
            row_copy(t, k).start()
        return carry

    def wait_rows(t, carry):
        for k in range(TOP_K):
            row_copy(t, k).wait()
        return carry

    lax.fori_loop(0, tb, start_rows, 0)
    lax.fori_loop(0, tb, wait_rows, 0)


def _dispatch(pad_start, pad_end, h2p, idx3, pos3, n_rows, slab):
    tb = MOE_TILE
    t = h2p.shape[0] // slab
    return pl.pallas_call(
        functools.partial(_dispatch_kernel, slab=slab),
        grid_spec=pltpu.PrefetchScalarGridSpec(
            num_scalar_prefetch=2,
            grid=(t // tb,),
            in_specs=[pl.BlockSpec((tb * slab, V7X_LANES), lambda i, ps, pe: (i, 0)),
                      pl.BlockSpec(memory_space=pl.ANY),
                      pl.BlockSpec(memory_space=pl.ANY)],
            out_specs=pl.BlockSpec(memory_space=pl.ANY),
            scratch_shapes=[pltpu.SMEM((TOP_K, tb), jnp.int32),
                            pltpu.SMEM((TOP_K, tb), jnp.int32),
                            pltpu.VMEM((tb * slab, V7X_LANES), jnp.uint32),
                            pltpu.SemaphoreType.DMA((2,)),
                            pltpu.SemaphoreType.DMA,
                            pltpu.SemaphoreType.DMA],
        ),
        out_shape=jax.ShapeDtypeStruct((n_rows * slab, V7X_LANES), jnp.uint32),
        compiler_params=_params(("arbitrary",)),
        name="dispatch",
    )(pad_start, pad_end, h2p, idx3, pos3)


def _expert_kernel(blk_e_ref, nused_ref, x_ref, wg_ref, wu_ref, wd_ref, o_ref, *, slab):
    i = pl.program_id(0)

    @pl.when(i < nused_ref[0])
    def _():
        tb = x_ref.shape[0] // slab
        half = slab * V7X_LANES
        lo, hi = _unpack_pair(_slab_load(x_ref, tb, slab))
        lo = lo.astype(jnp.bfloat16)
        hi = hi.astype(jnp.bfloat16)
        g = (jnp.dot(lo, wg_ref[0, :half, :], preferred_element_type=jnp.float32)
             + jnp.dot(hi, wg_ref[0, half:, :], preferred_element_type=jnp.float32))
        u = (jnp.dot(lo, wu_ref[0, :half, :], preferred_element_type=jnp.float32)
             + jnp.dot(hi, wu_ref[0, half:, :], preferred_element_type=jnp.float32))
        a = (_silu(g) * u).astype(jnp.bfloat16)
        y = jnp.dot(a, wd_ref[0], preferred_element_type=jnp.float32)
        _slab_store(o_ref, _pack_pair(y[:, :half], y[:, half:]), tb, slab)


def _experts(blk_e, n_used, xs, wg, wu, wd, slab):
    tb = MOE_TILE
    p = xs.shape[0] // slab
    d = 2 * slab * V7X_LANES
    f = wg.shape[2]
    nblk = p // tb

    def row_map(i, be, nu):
        return (jnp.minimum(i, nu[0] - 1), 0)

    def w_map(i, be, nu):
        return (be[jnp.minimum(i, nu[0] - 1)], 0, 0)

    return pl.pallas_call(
        functools.partial(_expert_kernel, slab=slab),
        grid_spec=pltpu.PrefetchScalarGridSpec(
            num_scalar_prefetch=2,
            grid=(nblk,),
            in_specs=[pl.BlockSpec((tb * slab, V7X_LANES), row_map),
                      pl.BlockSpec((1, d, f), w_map),
                      pl.BlockSpec((1, d, f), w_map),
                      pl.BlockSpec((1, f, d), w_map)],
            out_specs=pl.BlockSpec((tb * slab, V7X_LANES), row_map),
        ),
        out_shape=jax.ShapeDtypeStruct(xs.shape, jnp.uint32),
        compiler_params=_params(("arbitrary",)),
        name="experts",
    )(blk_e, n_used, xs, wg, wu, wd)


def _combine_kernel(pstart_ref, x_ref, h_ref, wt_ref, gt_ref, gf_ref, wsg_ref, wsu_ref, wsd_ref,
                    idx_hbm, pos_hbm, ys_hbm, o_ref, idx_s, pos_s, rows_v, sem_i, sem_r):
    i = pl.program_id(0)
    tb = x_ref.shape[0]
    half = x_ref.shape[1] // 2
    slab = half // V7X_LANES
    per_tile = MOE_TILE // tb
    win = pl.ds((i % per_tile) * tb, tb)

    def idx_copy():
        return pltpu.make_async_copy(idx_hbm.at[i // per_tile, :, win], idx_s, sem_i.at[0])

    def pos_copy():
        return pltpu.make_async_copy(pos_hbm.at[i // per_tile, :, win], pos_s, sem_i.at[1])

    idx_copy().start()
    pos_copy().start()
    idx_copy().wait()
    pos_copy().wait()

    def row_copy(t, k):
        src = pstart_ref[idx_s[k, t]] + pos_s[k, t]
        return pltpu.make_async_copy(ys_hbm.at[_slab_rows(src, slab)], rows_v.at[k, _slab_rows(t, slab)], sem_r)

    def start_rows(t, carry):
        for k in range(TOP_K):
            row_copy(t, k).start()
        return carry

    def wait_rows(t, carry):
        for k in range(TOP_K):
            row_copy(t, k).wait()
        return carry

    lax.fori_loop(0, tb, start_rows, 0)

    h = h_ref[...]
    g = jnp.dot(h, wsg_ref[...], preferred_element_type=jnp.float32)
    u = jnp.dot(h, wsu_ref[...], preferred_element_type=jnp.float32)
    a = (_silu(g) * u).astype(jnp.bfloat16)
    y = jnp.dot(a, wsd_ref[...], preferred_element_type=jnp.float32)
    y_lo = y[:, :half]
    y_hi = y[:, half:]

    lax.fori_loop(0, tb, wait_rows, 0)

    wt = wt_ref[...]
    for k in range(TOP_K):
        lo, hi = _unpack_pair(_slab_load(rows_v.at[k], tb, slab))
        wk = wt[:, k:k + 1]
        y_lo = y_lo + wk * lo
        y_hi = y_hi + wk * hi
    yy = jnp.concatenate([y_lo, y_hi], axis=1)
    xo = x_ref[...] + gt_ref[0] * yy
    ms = jnp.mean(xo * xo, axis=-1, keepdims=True)
    o_ref[...] = xo * lax.rsqrt(ms + EPS) * gf_ref[...]


def _combine(pad_start, x1, h2, w_t, mod3, gate_idx, g_final, wsg, wsu, wsd, idx3, pos3, ys, seq):
    t, d = x1.shape
    tb = COMBINE_TILE
    assert MOE_TILE % tb == 0 and seq % tb == 0
    per_b = seq // tb
    half = d // 2
    f = wsg.shape[1]
    return pl.pallas_call(
        _combine_kernel,
        grid_spec=pltpu.PrefetchScalarGridSpec(
            num_scalar_prefetch=1,
            grid=(t // tb,),
            in_specs=[pl.BlockSpec((tb, d), lambda i, ps: (i, 0)),
                      pl.BlockSpec((tb, d), lambda i, ps: (i, 0)),
                      pl.BlockSpec((tb, TOP_K), lambda i, ps: (i, 0)),
                      pl.BlockSpec((1, 1, d), lambda i, ps: ((i // per_b) * 6 + gate_idx, 0, 0)),
                      pl.BlockSpec((1, d), lambda i, ps: (0, 0)),
                      pl.BlockSpec((d, f), lambda i, ps: (0, 0)),
                      pl.BlockSpec((d, f), lambda i, ps: (0, 0)),
                      pl.BlockSpec((f, d), lambda i, ps: (0, 0)),
                      pl.BlockSpec(memory_space=pl.ANY),
                      pl.BlockSpec(memory_space=pl.ANY),
                      pl.BlockSpec(memory_space=pl.ANY)],
            out_specs=pl.BlockSpec((tb, d), lambda i, ps: (i, 0)),
            scratch_shapes=[pltpu.SMEM((TOP_K, tb), jnp.int32),
                            pltpu.SMEM((TOP_K, tb), jnp.int32),
                            pltpu.VMEM((TOP_K, tb * half // V7X_LANES, V7X_LANES), jnp.uint32),
                            pltpu.SemaphoreType.DMA((2,)),
                            pltpu.SemaphoreType.DMA],
        ),
        out_shape=jax.ShapeDtypeStruct((t, d), jnp.float32),
        compiler_params=_params(("arbitrary",)),
        name="combine",
    )(pad_start, x1, h2, w_t, mod3, g_final.reshape(1, d), wsg, wsu, wsd, idx3, pos3, ys)


def _in_layout(d):
    qw = ATT_HEADS * ATT_HEAD_DIM
    kvw = ATT_KV_HEADS * ATT_HEAD_DIM
    rqk = RET_HEADS * RET_QK_DIM
    rv = RET_HEADS * RET_V_DIM
    src_order = [("qa", qw), ("ka", kvw), ("va", kvw), ("qr", rqk), ("kr", rqk),
                 ("vr", rv), ("gr", rv), ("ga", d), ("gb", d)]
    src = {}
    off = 0
    for name, width in src_order:
        src[name] = (off, width)
        off += width
    dst_order = ["ga", "gb", "gr", "vr", "qr", "kr", "qa", "ka", "va"]
    dst = {}
    off = 0
    for name in dst_order:
        dst[name] = off
        off += src[name][1]
    return src, dst_order, dst


def kernel(x, c, w_ada, b_ada, g_norm_mix, w_in, attn_sinks, w_attn_out, w_ret_out, w_o, g_norm_ffn,
           w_router, b_router, w_gate, w_up, w_down, w_sh_gate, w_sh_up, w_sh_down, g_norm_final):
    b, s, d = x.shape
    t = b * s
    depth = w_ada.shape[0]
    bf = jnp.bfloat16
    src, dst_order, dst = _in_layout(d)

    c_pad = jnp.zeros((V7X_SUBLANES, d), jnp.float32).at[:b].set(c)
    x2 = x.reshape(t, d)
    for l in range(depth):
        mod = _ada(c_pad, w_ada[l], b_ada[l])
        mod3 = mod[:b].reshape(b * 6, 1, d)

        h = _norm_mod(x2.reshape(b, s, d), g_norm_mix[l], mod3, 0, 1)
        w_in_p = jnp.concatenate(
            [w_in[l][:, src[n][0]:src[n][0] + src[n][1]] for n in dst_order], axis=1).astype(bf)
        proj = _matmul(h.reshape(t, d), w_in_p, bf, 1024, 768)
        proj3 = proj.reshape(b, s, proj.shape[1])
        attn = _attention(proj3, attn_sinks[l], dst["qa"], dst["ka"], dst["va"])
        ret = _retention(proj3, dst["qr"], dst["kr"], dst["vr"], dst["gr"])
        mix = _mix(attn.reshape(t, -1), ret.reshape(t, -1), w_attn_out[l].astype(bf),
                   w_ret_out[l].astype(bf), proj, dst["ga"], dst["gb"])
        x1 = _out_resid(mix, w_o[l].astype(bf), x2, mod3, 2, s)

        h2, h2p, idx3, pos3, w3, cnt = _router(x1, g_norm_ffn[l], mod3, 3, 4,
                                               w_router[l].T.astype(bf), b_router[l], s)
        counts = cnt[:, 0]
        tile = MOE_TILE
        padded = (counts + tile - 1) // tile * tile
        pad_end = jnp.cumsum(padded).astype(jnp.int32)
        pad_start = (pad_end - padded).astype(jnp.int32)
        n_blocks = (t * TOP_K) // tile + N_EXPERTS
        n_used = (pad_end[-1] // tile).reshape(1).astype(jnp.int32)
        blk_e = jnp.minimum(jnp.searchsorted(pad_end, jnp.arange(n_blocks, dtype=jnp.int32) * tile,
                                             side="right"), N_EXPERTS - 1).astype(jnp.int32)
        slab = d // 2 // V7X_LANES
        xs = _dispatch(pad_start, pad_end, h2p, idx3, pos3, n_blocks * tile, slab)
        ys = _experts(blk_e, n_used, xs, w_gate[l].astype(bf), w_up[l].astype(bf), w_down[l].astype(bf), slab)
        w_t = w3.transpose(0, 2, 1).reshape(t, TOP_K)
        is_last = l == depth - 1
        assert is_last, "the final norm is fused into the last layer's combine"
        x2 = _combine(pad_start, x1, h2, w_t, mod3, 5, g_norm_final, w_sh_gate[l].astype(bf),
                      w_sh_up[l].astype(bf), w_sh_down[l].astype(bf), idx3, pos3, ys, s)
    return x2.reshape(b, s, d)
```

```python
import functools
import math

import jax
import jax.numpy as jnp
from jax import lax
from jax.experimental import pallas as pl
from jax.experimental.pallas import tpu as pltpu

ATT_HEADS = 32
ATT_KV_HEADS = 4
ATT_HEAD_DIM = 64
WINDOW = 128
ATT_BLOCK = 128
RET_HEADS = 8
RET_QK_DIM = 256
RET_V_DIM = 512
RET_CHUNK = 128
N_EXPERTS = 64
N_GROUPS = 8
TOPK_GROUPS = 4
TOP_K = 8
ROUTED_SCALE = 2.5
EPS = 1e-6

V7X_LANES = 128
V7X_SUBLANES = 8
V7X_VMEM_LIMIT_BYTES = 60000 * 1024

MOE_TILE = 256
COMBINE_TILE = 128
NEG_BIG = -1e30


def _div_block(n, target, align):
    best = None
    b = align
    while b <= min(n, target):
        if n % b == 0:
            best = b
        b += align
    assert best is not None, (n, target, align)
    return best


def _params(semantics):
    return pltpu.CompilerParams(dimension_semantics=semantics,
                                vmem_limit_bytes=V7X_VMEM_LIMIT_BYTES)


def _sigmoid(v):
    return 1.0 / (1.0 + jnp.exp(-v))


def _silu(v):
    return v * _sigmoid(v)


def _bf16_bits(v):
    u = pltpu.bitcast(v, jnp.uint32)
    r = u + jnp.uint32(0x7FFF) + ((u >> 16) & jnp.uint32(1))
    return r & jnp.uint32(0xFFFF0000)


def _pack_pair(lo, hi):
    return (_bf16_bits(lo) >> 16) | _bf16_bits(hi)


def _unpack_pair(p):
    lo = pltpu.bitcast(p << 16, jnp.float32)
    hi = pltpu.bitcast(p & jnp.uint32(0xFFFF0000), jnp.float32)
    return lo, hi


def _slab_load(ref, n_rows, slab):
    return jnp.concatenate([ref[pl.ds(s, n_rows, stride=slab), :] for s in range(slab)], axis=1)


def _slab_store(ref, val, n_rows, slab):
    for s in range(slab):
        ref[pl.ds(s, n_rows, stride=slab), :] = val[:, s * V7X_LANES:(s + 1) * V7X_LANES]


def _slab_rows(r, slab):
    return pl.ds(pl.multiple_of(r * slab, slab), slab)


def _ada_kernel(c_ref, w_ref, b_ref, o_ref):
    cs = _silu(c_ref[...]).astype(jnp.bfloat16)
    o_ref[...] = jnp.dot(cs, w_ref[...].astype(jnp.bfloat16),
                         preferred_element_type=jnp.float32) + b_ref[...]


def _ada(c_pad, w, b):
    m, d = c_pad.shape
    n = w.shape[1]
    tn = _div_block(n, 512, V7X_LANES)
    return pl.pallas_call(
        _ada_kernel,
        grid=(n // tn,),
        in_specs=[pl.BlockSpec((m, d), lambda j: (0, 0)),
                  pl.BlockSpec((d, tn), lambda j: (0, j)),
                  pl.BlockSpec((1, tn), lambda j: (0, j))],
        out_specs=pl.BlockSpec((m, tn), lambda j: (0, j)),
        out_shape=jax.ShapeDtypeStruct((m, n), jnp.float32),
        compiler_params=_params(("parallel",)),
        name="ada",
    )(c_pad, w, b.reshape(1, n))


def _norm_mod_kernel(x_ref, g_ref, sh_ref, sc_ref, o_ref):
    x = x_ref[0]
    ms = jnp.mean(x * x, axis=-1, keepdims=True)
    y = x * lax.rsqrt(ms + EPS) * g_ref[...]
    o_ref[0] = (y * (1.0 + sc_ref[0]) + sh_ref[0]).astype(o_ref.dtype)


def _norm_mod(x3, g, mod3, shift_idx, scale_idx):
    b, s, d = x3.shape
    ts = _div_block(s, 256, V7X_SUBLANES)
    return pl.pallas_call(
        _norm_mod_kernel,
        grid=(b, s // ts),
        in_specs=[pl.BlockSpec((1, ts, d), lambda bi, i: (bi, i, 0)),
                  pl.BlockSpec((1, d), lambda bi, i: (0, 0)),
                  pl.BlockSpec((1, 1, d), lambda bi, i: (bi * 6 + shift_idx, 0, 0)),
                  pl.BlockSpec((1, 1, d), lambda bi, i: (bi * 6 + scale_idx, 0, 0))],
        out_specs=pl.BlockSpec((1, ts, d), lambda bi, i: (bi, i, 0)),
        out_shape=jax.ShapeDtypeStruct((b, s, d), jnp.bfloat16),
        compiler_params=_params(("parallel", "parallel")),
        name="norm_mod",
    )(x3, g.reshape(1, d), mod3, mod3)


def _mm_kernel(a_ref, b_ref, o_ref):
    o_ref[...] = jnp.dot(a_ref[...], b_ref[...],
                         preferred_element_type=jnp.float32).astype(o_ref.dtype)


def _matmul(a, b, out_dtype, tm_target, tn_target):
    m, k = a.shape
    n = b.shape[1]
    tm = _div_block(m, tm_target, V7X_SUBLANES)
    tn = _div_block(n, tn_target, V7X_LANES)
    return pl.pallas_call(
        _mm_kernel,
        grid=(m // tm, n // tn),
        in_specs=[pl.BlockSpec((tm, k), lambda i, j: (i, 0)),
                  pl.BlockSpec((k, tn), lambda i, j: (0, j))],
        out_specs=pl.BlockSpec((tm, tn), lambda i, j: (i, j)),
        out_shape=jax.ShapeDtypeStruct((m, n), out_dtype),
        compiler_params=_params(("parallel", "parallel")),
        name="matmul",
    )(a, b)


def _attn_kernel(sink_ref, q_ref, kc_ref, kp_ref, vc_ref, vp_ref, bias_ref, o_ref):
    i = pl.program_id(1)
    blk = ATT_BLOCK
    hd = ATT_HEAD_DIM
    group = ATT_HEADS // ATT_KV_HEADS
    pairs = group // 2
    nt = (((1,), (1,)), ((), ()))
    zpad = jnp.zeros((2 * blk, hd), jnp.bfloat16)
    for h in range(ATT_KV_HEADS):
        sl = slice(h * hd, (h + 1) * hd)
        k2 = jnp.concatenate([kp_ref[0, :, sl], kc_ref[0, :, sl]], axis=0) * (hd ** -0.5)
        v2 = jnp.concatenate([vp_ref[0, :, sl], vc_ref[0, :, sl]], axis=0)
        qp = jnp.concatenate([q_ref[0, :, (h * pairs + p) * 2 * hd:(h * pairs + p + 1) * 2 * hd]
                              for p in range(pairs)], axis=0)
        acc = None
        for par in range(2):
            kz = jnp.concatenate([k2, zpad] if par == 0 else [zpad, k2], axis=1)
            vz = jnp.concatenate([v2, zpad] if par == 0 else [zpad, v2], axis=1)
            s = lax.dot_general(qp, kz, nt, preferred_element_type=jnp.float32) + bias_ref[h * 2 + par]
            s = jnp.concatenate([jnp.where(i == 0, NEG_BIG, s[:, :blk]), s[:, blk:]], axis=1)
            sink = jnp.concatenate([jnp.full((blk, 1), sink_ref[h * group + 2 * p + par], jnp.float32)
                                    for p in range(pairs)], axis=0)
            m = jnp.maximum(s.max(-1, keepdims=True), sink)
            pr = jnp.exp(s - m)
            denom = pr.sum(-1, keepdims=True) + jnp.exp(sink - m)
            o = jnp.dot(pr.astype(jnp.bfloat16), vz, preferred_element_type=jnp.float32) * (1.0 / denom)
            acc = o if acc is None else acc + o
        for p in range(pairs):
            o_ref[0, :, (h * pairs + p) * 2 * hd:(h * pairs + p + 1) * 2 * hd] = (
                acc[p * blk:(p + 1) * blk].astype(o_ref.dtype))


def _attn_bias():
    blk = ATT_BLOCK
    group = ATT_HEADS // ATT_KV_HEADS
    pairs = group // 2
    qi = jnp.arange(blk)[:, None]
    kj = jnp.arange(2 * blk)[None, :]
    dist = qi + blk - kj
    valid = (dist >= 0) & (dist < WINDOW)
    slopes = jnp.exp2(-8.0 * jnp.arange(1, ATT_HEADS + 1, dtype=jnp.float32) / ATT_HEADS)
    slopes = slopes.reshape(ATT_KV_HEADS, pairs, 2)
    bias = jnp.where(valid, -slopes[..., None, None] * dist.astype(jnp.float32), NEG_BIG)
    return bias.transpose(0, 2, 1, 3, 4).reshape(ATT_KV_HEADS * 2, pairs * blk, 2 * blk)


def _attention(proj3, sinks, q_off, k_off, v_off):
    b, s, _ = proj3.shape
    qw = ATT_HEADS * ATT_HEAD_DIM
    kvw = ATT_KV_HEADS * ATT_HEAD_DIM
    nb = s // ATT_BLOCK
    group = ATT_HEADS // ATT_KV_HEADS
    assert q_off % qw == 0 and k_off % kvw == 0 and v_off % kvw == 0
    assert group % 2 == 0 and 2 * ATT_HEAD_DIM == V7X_LANES and WINDOW == ATT_BLOCK
    assert 4 ** round(math.log(ATT_HEAD_DIM, 4)) == ATT_HEAD_DIM, "score scale must be a power of two"
    qb, kb, vb = q_off // qw, k_off // kvw, v_off // kvw
    bias = _attn_bias()
    return pl.pallas_call(
        _attn_kernel,
        grid_spec=pltpu.PrefetchScalarGridSpec(
            num_scalar_prefetch=1,
            grid=(b, nb),
            in_specs=[pl.BlockSpec((1, ATT_BLOCK, qw), lambda bi, i, sk: (bi, i, qb)),
                      pl.BlockSpec((1, ATT_BLOCK, kvw), lambda bi, i, sk: (bi, i, kb)),
                      pl.BlockSpec((1, ATT_BLOCK, kvw), lambda bi, i, sk: (bi, jnp.maximum(i - 1, 0), kb)),
                      pl.BlockSpec((1, ATT_BLOCK, kvw), lambda bi, i, sk: (bi, i, vb)),
                      pl.BlockSpec((1, ATT_BLOCK, kvw), lambda bi, i, sk: (bi, jnp.maximum(i - 1, 0), vb)),
                      pl.BlockSpec(bias.shape, lambda bi, i, sk: (0, 0, 0), pipeline_mode=pl.Buffered(1))],
            out_specs=pl.BlockSpec((1, ATT_BLOCK, qw), lambda bi, i, sk: (bi, i, 0)),
        ),
        out_shape=jax.ShapeDtypeStruct((b, s, qw), jnp.bfloat16),
        compiler_params=_params(("parallel", "parallel")),
        name="attention",
    )(sinks, proj3, proj3, proj3, proj3, proj3, bias)


def _ret_kernel(q_ref, k_ref, v_ref, gr_ref, mask_ref, qd_ref, kd_ref, cd_ref, o_ref, state_ref):
    c = pl.program_id(1)

    @pl.when(c == 0)
    def _():
        state_ref[...] = jnp.zeros_like(state_ref)

    nt = (((1,), (1,)), ((), ()))
    tn = (((0,), (0,)), ((), ()))
    for bi in range(q_ref.shape[0]):
        q = q_ref[bi]
        k = k_ref[bi]
        v = v_ref[bi]
        attn = lax.dot_general(q, k, nt, preferred_element_type=jnp.float32) * mask_ref[0]
        intra = jnp.dot(attn.astype(v.dtype), v, preferred_element_type=jnp.float32)
        state = state_ref[bi]
        inter = jnp.dot(q, state.astype(q.dtype), preferred_element_type=jnp.float32) * qd_ref[0]
        o = intra + inter
        kd = (k.astype(jnp.float32) * kd_ref[0]).astype(k.dtype)
        state_ref[bi] = state * cd_ref[0] + lax.dot_general(kd, v, tn, preferred_element_type=jnp.float32)
        mu = jnp.mean(o, axis=-1, keepdims=True)
        oc = o - mu
        var = jnp.mean(oc * oc, axis=-1, keepdims=True)
        y = oc * lax.rsqrt(var + EPS)
        o_ref[bi] = (_silu(gr_ref[bi].astype(jnp.float32)) * y).astype(o_ref.dtype)


def _retention(proj3, q_off, k_off, v_off, g_off):
    b, s, _ = proj3.shape
    dk, dv, ch = RET_QK_DIM, RET_V_DIM, RET_CHUNK
    assert q_off % dk == 0 and k_off % dk == 0 and v_off % dv == 0 and g_off % dv == 0
    qb, kb, vb, gb = q_off // dk, k_off // dk, v_off // dv, g_off // dv
    n = s // ch
    log_g = jnp.log1p(-jnp.exp2(-5.0 - jnp.arange(RET_HEADS, dtype=jnp.float32)))
    pos = jnp.arange(ch, dtype=jnp.float32)
    rel = pos[:, None] - pos[None, :]
    scale = dk ** -0.5
    mask = jnp.where(rel[None] >= 0, jnp.exp(rel[None] * log_g[:, None, None]), 0.0) * scale
    q_decay = jnp.exp((pos[None, :, None] + 1.0) * log_g[:, None, None])
    k_decay = jnp.exp((ch - 1.0 - pos[None, :, None]) * log_g[:, None, None]) * scale
    c_decay = jnp.exp(ch * log_g)[:, None, None]
    return pl.pallas_call(
        _ret_kernel,
        grid=(RET_HEADS, n),
        in_specs=[pl.BlockSpec((b, ch, dk), lambda h, c: (0, c, qb + h)),
                  pl.BlockSpec((b, ch, dk), lambda h, c: (0, c, kb + h)),
                  pl.BlockSpec((b, ch, dv), lambda h, c: (0, c, vb + h)),
                  pl.BlockSpec((b, ch, dv), lambda h, c: (0, c, gb + h)),
                  pl.BlockSpec((1, ch, ch), lambda h, c: (h, 0, 0)),
                  pl.BlockSpec((1, ch, 1), lambda h, c: (h, 0, 0)),
                  pl.BlockSpec((1, ch, 1), lambda h, c: (h, 0, 0)),
                  pl.BlockSpec((1, 1, 1), lambda h, c: (h, 0, 0))],
        out_specs=pl.BlockSpec((b, ch, dv), lambda h, c: (0, c, h)),
        out_shape=jax.ShapeDtypeStruct((b, s, RET_HEADS * dv), jnp.bfloat16),
        scratch_shapes=[pltpu.VMEM((b, dk, dv), jnp.float32)],
        compiler_params=_params(("parallel", "arbitrary")),
        name="retention",
    )(proj3, proj3, proj3, proj3, mask, q_decay, k_decay, c_decay)


def _mix_kernel(a_ref, r_ref, wa_ref, wr_ref, ga_ref, gb_ref, o_ref):
    ya = jnp.dot(a_ref[...], wa_ref[...], preferred_element_type=jnp.float32)
    yr = jnp.dot(r_ref[...], wr_ref[...], preferred_element_type=jnp.float32)
    ga = _sigmoid(ga_ref[...].astype(jnp.float32))
    gb = _sigmoid(gb_ref[...].astype(jnp.float32))
    o_ref[...] = (ga * ya + gb * yr).astype(o_ref.dtype)


def _mix(attn2, ret2, wa, wr, proj2, ga_off, gb_off):
    m, ka = attn2.shape
    kr = ret2.shape[1]
    d = wa.shape[1]
    tm = _div_block(m, 512, V7X_SUBLANES)
    tn = _div_block(d, 512, V7X_LANES)
    assert ga_off % tn == 0 and gb_off % tn == 0
    gab, gbb = ga_off // tn, gb_off // tn
    return pl.pallas_call(
        _mix_kernel,
        grid=(m // tm, d // tn),
        in_specs=[pl.BlockSpec((tm, ka), lambda i, j: (i, 0)),
                  pl.BlockSpec((tm, kr), lambda i, j: (i, 0)),
                  pl.BlockSpec((ka, tn), lambda i, j: (0, j)),
                  pl.BlockSpec((kr, tn), lambda i, j: (0, j)),
                  pl.BlockSpec((tm, tn), lambda i, j: (i, gab + j)),
                  pl.BlockSpec((tm, tn), lambda i, j: (i, gbb + j))],
        out_specs=pl.BlockSpec((tm, tn), lambda i, j: (i, j)),
        out_shape=jax.ShapeDtypeStruct((m, d), jnp.bfloat16),
        compiler_params=_params(("parallel", "parallel")),
        name="mix",
    )(attn2, ret2, wa, wr, proj2, proj2)


def _resid_kernel(a_ref, w_ref, x_ref, gt_ref, o_ref):
    y = jnp.dot(a_ref[...], w_ref[...], preferred_element_type=jnp.float32)
    o_ref[...] = x_ref[...] + gt_ref[0] * y


def _out_resid(mix2, w, x2, mod3, gate_idx, seq):
    m, k = mix2.shape
    d = w.shape[1]
    tm = _div_block(seq, 1024, V7X_SUBLANES)
    tn = _div_block(d, 512, V7X_LANES)
    per_b = seq // tm
    return pl.pallas_call(
        _resid_kernel,
        grid=(m // tm, d // tn),
        in_specs=[pl.BlockSpec((tm, k), lambda i, j: (i, 0)),
                  pl.BlockSpec((k, tn), lambda i, j: (0, j)),
                  pl.BlockSpec((tm, tn), lambda i, j: (i, j)),
                  pl.BlockSpec((1, 1, tn), lambda i, j: ((i // per_b) * 6 + gate_idx, 0, j))],
        out_specs=pl.BlockSpec((tm, tn), lambda i, j: (i, j)),
        out_shape=jax.ShapeDtypeStruct((m, d), jnp.float32),
        compiler_params=_params(("parallel", "parallel")),
        name="out_resid",
    )(mix2, w, x2, mod3)


def _router_kernel(x_ref, g_ref, sh_ref, sc_ref, wr_ref, br_ref,
                   h_ref, hp_ref, idx_ref, pos_ref, w_ref, cnt_ref, carry_ref):
    i = pl.program_id(0)
    e = N_EXPERTS
    per_g = e // N_GROUPS
    tb = x_ref.shape[0]

    @pl.when(i == 0)
    def _():
        carry_ref[...] = jnp.zeros_like(carry_ref)

    x = x_ref[...]
    ms = jnp.mean(x * x, axis=-1, keepdims=True)
    h = x * lax.rsqrt(ms + EPS) * g_ref[...]
    h = h * (1.0 + sc_ref[0]) + sh_ref[0]
    hb = h.astype(jnp.bfloat16)
    h_ref[...] = hb
    half = h.shape[1] // 2
    _slab_store(hp_ref, _pack_pair(h[:, :half], h[:, half:]), tb, half // V7X_LANES)

    nt = (((1,), (1,)), ((), ()))
    logits = lax.dot_general(wr_ref[...], hb, nt, preferred_element_type=jnp.float32)
    scores = _sigmoid(logits)
    choice = scores + br_ref[...]

    c3 = choice.reshape(N_GROUPS, per_g, tb)
    j_iota = lax.broadcasted_iota(jnp.int32, c3.shape, 1).astype(jnp.float32)
    m1 = c3.max(axis=1, keepdims=True)
    first = jnp.min(jnp.where(c3 == m1, j_iota, float(per_g)), axis=1, keepdims=True)
    m2 = jnp.where(j_iota == first, -jnp.inf, c3).max(axis=1, keepdims=True)
    gs = (m1 + m2).reshape(N_GROUPS, tb)

    g_iota = lax.broadcasted_iota(jnp.int32, gs.shape, 0)
    grank = jnp.zeros(gs.shape, jnp.int32)
    for gp in range(N_GROUPS):
        row = gs[gp:gp + 1, :]
        ahead = (row > gs) | ((row == gs) & (gp < g_iota))
        grank = grank + ahead.astype(jnp.int32)
    gmask = grank < TOPK_GROUPS
    emask = jnp.broadcast_to(gmask.reshape(N_GROUPS, 1, tb), c3.shape).reshape(e, tb)
    masked = jnp.where(emask, choice, -jnp.inf)

    e_iota = lax.broadcasted_iota(jnp.int32, masked.shape, 0)
    erank = jnp.zeros(masked.shape, jnp.int32)
    for ep in range(e):
        row = masked[ep:ep + 1, :]
        ahead = (row > masked) | ((row == masked) & (ep < e_iota))
        erank = erank + ahead.astype(jnp.int32)
    sel = (erank < TOP_K) & emask
    self32 = sel.astype(jnp.float32)

    wsel = scores * self32
    wn = wsel / jnp.sum(wsel, axis=0, keepdims=True) * ROUTED_SCALE

    selb = self32.astype(jnp.bfloat16)
    t_r = lax.broadcasted_iota(jnp.int32, (tb, tb), 0)
    t_c = lax.broadcasted_iota(jnp.int32, (tb, tb), 1)
    upper = (t_r <= t_c).astype(jnp.bfloat16)
    incl = jnp.dot(selb, upper, preferred_element_type=jnp.float32)
    carry = carry_ref[...]
    rank_in_e = carry + incl - 1.0
    carry_new = carry + jnp.sum(self32, axis=1, keepdims=True)
    carry_ref[...] = carry_new
    cnt_ref[...] = jnp.broadcast_to(carry_new, cnt_ref.shape).astype(jnp.int32)

    e_r = lax.broadcasted_iota(jnp.int32, (e, e), 0)
    e_c = lax.broadcasted_iota(jnp.int32, (e, e), 1)
    lower = (e_c < e_r).astype(jnp.bfloat16)
    before = jnp.dot(lower, selb, preferred_element_type=jnp.float32)
    e_f = e_iota.astype(jnp.float32)
    idx_rows, pos_rows, w_rows = [], [], []
    for k in range(TOP_K):
        hit = jnp.where(sel & (before == float(k)), 1.0, 0.0)
        idx_rows.append(jnp.sum(hit * e_f, axis=0, keepdims=True))
        pos_rows.append(jnp.sum(hit * rank_in_e, axis=0, keepdims=True))
        w_rows.append(jnp.sum(hit * wn, axis=0, keepdims=True))
    idx_ref[0] = jnp.concatenate(idx_rows, axis=0).astype(jnp.int32)
    pos_ref[0] = jnp.concatenate(pos_rows, axis=0).astype(jnp.int32)
    w_ref[0] = jnp.concatenate(w_rows, axis=0)


def _router(x1, g, mod3, shift_idx, scale_idx, w_router_t, b_router, seq):
    t, d = x1.shape
    e = N_EXPERTS
    tb = MOE_TILE
    assert seq % tb == 0 and d % (2 * V7X_LANES * V7X_SUBLANES) == 0
    slab = d // 2 // V7X_LANES
    per_b = seq // tb
    nt = t // tb
    return pl.pallas_call(
        _router_kernel,
        grid=(nt,),
        in_specs=[pl.BlockSpec((tb, d), lambda i: (i, 0)),
                  pl.BlockSpec((1, d), lambda i: (0, 0)),
                  pl.BlockSpec((1, 1, d), lambda i: ((i // per_b) * 6 + shift_idx, 0, 0)),
                  pl.BlockSpec((1, 1, d), lambda i: ((i // per_b) * 6 + scale_idx, 0, 0)),
                  pl.BlockSpec((e, d), lambda i: (0, 0)),
                  pl.BlockSpec((e, 1), lambda i: (0, 0))],
        out_specs=[pl.BlockSpec((tb, d), lambda i: (i, 0)),
                   pl.BlockSpec((tb * slab, V7X_LANES), lambda i: (i, 0)),
                   pl.BlockSpec((1, TOP_K, tb), lambda i: (i, 0, 0)),
                   pl.BlockSpec((1, TOP_K, tb), lambda i: (i, 0, 0)),
                   pl.BlockSpec((1, TOP_K, tb), lambda i: (i, 0, 0)),
                   pl.BlockSpec((e, V7X_LANES), lambda i: (0, 0))],
        out_shape=[jax.ShapeDtypeStruct((t, d), jnp.bfloat16),
                   jax.ShapeDtypeStruct((t * slab, V7X_LANES), jnp.uint32),
                   jax.ShapeDtypeStruct((nt, TOP_K, tb), jnp.int32),
                   jax.ShapeDtypeStruct((nt, TOP_K, tb), jnp.int32),
                   jax.ShapeDtypeStruct((nt, TOP_K, tb), jnp.float32),
                   jax.ShapeDtypeStruct((e, V7X_LANES), jnp.int32)],
        scratch_shapes=[pltpu.VMEM((e, 1), jnp.float32)],
        compiler_params=_params(("arbitrary",)),
        name="router",
    )(x1, g.reshape(1, d), mod3, mod3, w_router_t, b_router.reshape(e, 1))


def _slots_kernel(pstart_ref, idx_ref, pos_ref, o_ref):
    idx = idx_ref[...]
    base = jnp.zeros(idx.shape, jnp.int32)
    for ex in range(N_EXPERTS):
        base = jnp.where(idx == ex, pstart_ref[ex], base)
    o_ref[...] = base + pos_ref[...]


def _slots(pad_start, idx3, pos3):
    nt = idx3.shape[0]
    blk = (1,) + idx3.shape[1:]
    spec = pl.BlockSpec(blk, lambda i, ps: (i, 0, 0))
    return pl.pallas_call(
        _slots_kernel,
        grid_spec=pltpu.PrefetchScalarGridSpec(num_scalar_prefetch=1, grid=(nt,),
                                               in_specs=[spec, spec], out_specs=spec),
        out_shape=jax.ShapeDtypeStruct(idx3.shape, jnp.int32),
        compiler_params=_params(("parallel",)),
        name="slots",
    )(pad_start, idx3, pos3)


def _dispatch_kernel(pstart_ref, pend_ref, hp_ref, slot_hbm, xs_hbm,
                     slot_s, zero_v, sem_i, sem_z, sem_r, *, slab):
    i = pl.program_id(0)
    tb = hp_ref.shape[0] // slab

    def slot_copy():
        return pltpu.make_async_copy(slot_hbm.at[i], slot_s, sem_i)

    slot_copy().start()

    def zero_copy(ex):
        first = pl.multiple_of((pend_ref[ex] - tb) * slab, tb * slab)
        return pltpu.make_async_copy(zero_v, xs_hbm.at[pl.ds(first, tb * slab)], sem_z)

    @pl.when(i == 0)
    def _():
        zero_v[...] = jnp.zeros_like(zero_v)

        def start(ex, carry):
            @pl.when(pend_ref[ex] > pstart_ref[ex])
            def _():
                zero_copy(ex).start()
            return carry

        def wait(ex, carry):
            @pl.when(pend_ref[ex] > pstart_ref[ex])
            def _():
                zero_copy(ex).wait()
            return carry

        lax.fori_loop(0, N_EXPERTS, start, 0)
        lax.fori_loop(0, N_EXPERTS, wait, 0)

    slot_copy().wait()

    def row_copy(t, k):
        return pltpu.make_async_copy(hp_ref.at[_slab_rows(t, slab)],
                                     xs_hbm.at[_slab_rows(slot_s[k, t], slab)], sem_r)

    def start_rows(t, carry):
        for k in range(TOP_K):
            row_copy(t, k).start(priority=k % 2)
        return carry

    def wait_rows(t, carry):
        for k in range(TOP_K):
            row_copy(t, k).wait()
        return carry

    lax.fori_loop(0, tb, start_rows, 0)
    lax.fori_loop(0, tb, wait_rows, 0)


def _dispatch(pad_start, pad_end, h2p, slot3, n_rows, slab):
    tb = MOE_TILE
    t = h2p.shape[0] // slab
    return pl.pallas_call(
        functools.partial(_dispatch_kernel, slab=slab),
        grid_spec=pltpu.PrefetchScalarGridSpec(
            num_scalar_prefetch=2,
            grid=(t // tb,),
            in_specs=[pl.BlockSpec((tb * slab, V7X_LANES), lambda i, ps, pe: (i, 0)),
                      pl.BlockSpec(memory_space=pl.ANY)],
            out_specs=pl.BlockSpec(memory_space=pl.ANY),
            scratch_shapes=[pltpu.SMEM((TOP_K, tb), jnp.int32),
                            pltpu.VMEM((tb * slab, V7X_LANES), jnp.uint32),
                            pltpu.SemaphoreType.DMA,
                            pltpu.SemaphoreType.DMA,
                            pltpu.SemaphoreType.DMA],
        ),
        out_shape=jax.ShapeDtypeStruct((n_rows * slab, V7X_LANES), jnp.uint32),
        compiler_params=_params(("arbitrary",)),
        name="dispatch",
    )(pad_start, pad_end, h2p, slot3)


def _expert_kernel(blk_e_ref, nused_ref, x_ref, wg_ref, wu_ref, wd_ref, o_ref, *, slab):
    i = pl.program_id(0)

    @pl.when(i < nused_ref[0])
    def _():
        tb = x_ref.shape[0] // slab
        half = slab * V7X_LANES
        lo, hi = _unpack_pair(_slab_load(x_ref, tb, slab))
        lo = lo.astype(jnp.bfloat16)
        hi = hi.astype(jnp.bfloat16)
        g = (jnp.dot(lo, wg_ref[0, :half, :], preferred_element_type=jnp.float32)
             + jnp.dot(hi, wg_ref[0, half:, :], preferred_element_type=jnp.float32))
        u = (jnp.dot(lo, wu_ref[0, :half, :], preferred_element_type=jnp.float32)
             + jnp.dot(hi, wu_ref[0, half:, :], preferred_element_type=jnp.float32))
        a = (_silu(g) * u).astype(jnp.bfloat16)
        y = jnp.dot(a, wd_ref[0], preferred_element_type=jnp.float32)
        _slab_store(o_ref, _pack_pair(y[:, :half], y[:, half:]), tb, slab)


def _experts(blk_e, n_used, xs, wg, wu, wd, slab):
    tb = MOE_TILE
    p = xs.shape[0] // slab
    d = 2 * slab * V7X_LANES
    f = wg.shape[2]
    nblk = p // tb

    def row_map(i, be, nu):
        return (jnp.minimum(i, nu[0] - 1), 0)

    def w_map(i, be, nu):
        return (be[jnp.minimum(i, nu[0] - 1)], 0, 0)

    return pl.pallas_call(
        functools.partial(_expert_kernel, slab=slab),
        grid_spec=pltpu.PrefetchScalarGridSpec(
            num_scalar_prefetch=2,
            grid=(nblk,),
            in_specs=[pl.BlockSpec((tb * slab, V7X_LANES), row_map),
                      pl.BlockSpec((1, d, f), w_map),
                      pl.BlockSpec((1, d, f), w_map),
                      pl.BlockSpec((1, f, d), w_map)],
            out_specs=pl.BlockSpec((tb * slab, V7X_LANES), row_map),
        ),
        out_shape=jax.ShapeDtypeStruct(xs.shape, jnp.uint32),
        compiler_params=_params(("arbitrary",)),
        name="experts",
    )(blk_e, n_used, xs, wg, wu, wd)


def _combine_kernel(x_ref, h_ref, wt_ref, gt_ref, gf_ref, wsg_ref, wsu_ref, wsd_ref,
                    slot_hbm, ys_hbm, o_ref, slot_s, rows_v, sem_i, sem_r):
    i = pl.program_id(0)
    n = pl.num_programs(0)
    tb = x_ref.shape[0]
    half = x_ref.shape[1] // 2
    slab = half // V7X_LANES
    per_tile = MOE_TILE // tb
    cur = i % 2

    def slot_copy(step):
        win = pl.ds((step % per_tile) * tb, tb)
        return pltpu.make_async_copy(slot_hbm.at[step // per_tile, :, win], slot_s.at[step % 2],
                                     sem_i.at[step % 2])

    def start_rows(step):
        buf = step % 2

        def body(t, carry):
            for k in range(TOP_K):
                pltpu.make_async_copy(ys_hbm.at[_slab_rows(slot_s[buf, k, t], slab)],
                                      rows_v.at[buf, k, _slab_rows(t, slab)],
                                      sem_r.at[buf]).start(priority=k % 2)
            return carry

        lax.fori_loop(0, tb, body, 0)

    def wait_rows(buf):
        for k in range(TOP_K):
            pltpu.make_async_copy(ys_hbm.at[pl.ds(0, tb * slab)], rows_v.at[buf, k], sem_r.at[buf]).wait()

    @pl.when(i == 0)
    def _():
        slot_copy(i).start()
        slot_copy(i).wait()
        start_rows(i)

    @pl.when(i + 1 < n)
    def _():
        slot_copy(i + 1).start()

    h = h_ref[...]
    g = jnp.dot(h, wsg_ref[...], preferred_element_type=jnp.float32)
    u = jnp.dot(h, wsu_ref[...], preferred_element_type=jnp.float32)
    a = (_silu(g) * u).astype(jnp.bfloat16)
    y = jnp.dot(a, wsd_ref[...], preferred_element_type=jnp.float32)
    y_lo = y[:, :half]
    y_hi = y[:, half:]

    @pl.when(i + 1 < n)
    def _():
        slot_copy(i + 1).wait()
        start_rows(i + 1)

    wait_rows(cur)

    wt = wt_ref[...]
    for k in range(TOP_K):
        lo, hi = _unpack_pair(_slab_load(rows_v.at[cur, k], tb, slab))
        wk = wt[:, k:k + 1]
        y_lo = y_lo + wk * lo
        y_hi = y_hi + wk * hi
    yy = jnp.concatenate([y_lo, y_hi], axis=1)
    xo = x_ref[...] + gt_ref[0] * yy
    ms = jnp.mean(xo * xo, axis=-1, keepdims=True)
    o_ref[...] = xo * lax.rsqrt(ms + EPS) * gf_ref[...]


def _combine(x1, h2, w_t, mod3, gate_idx, g_final, wsg, wsu, wsd, slot3, ys, seq):
    t, d = x1.shape
    tb = COMBINE_TILE
    assert MOE_TILE % tb == 0 and seq % tb == 0
    per_b = seq // tb
    half = d // 2
    f = wsg.shape[1]
    once = pl.Buffered(1)
    return pl.pallas_call(
        _combine_kernel,
        grid=(t // tb,),
        in_specs=[pl.BlockSpec((tb, d), lambda i: (i, 0)),
                  pl.BlockSpec((tb, d), lambda i: (i, 0)),
                  pl.BlockSpec((tb, TOP_K), lambda i: (i, 0)),
                  pl.BlockSpec((1, 1, d), lambda i: ((i // per_b) * 6 + gate_idx, 0, 0)),
                  pl.BlockSpec((1, d), lambda i: (0, 0)),
                  pl.BlockSpec((d, f), lambda i: (0, 0), pipeline_mode=once),
                  pl.BlockSpec((d, f), lambda i: (0, 0), pipeline_mode=once),
                  pl.BlockSpec((f, d), lambda i: (0, 0), pipeline_mode=once),
                  pl.BlockSpec(memory_space=pl.ANY),
                  pl.BlockSpec(memory_space=pl.ANY)],
        out_specs=pl.BlockSpec((tb, d), lambda i: (i, 0)),
        scratch_shapes=[pltpu.SMEM((2, TOP_K, tb), jnp.int32),
                        pltpu.VMEM((2, TOP_K, tb * half // V7X_LANES, V7X_LANES), jnp.uint32),
                        pltpu.SemaphoreType.DMA((2,)),
                        pltpu.SemaphoreType.DMA((2,))],
        out_shape=jax.ShapeDtypeStruct((t, d), jnp.float32),
        compiler_params=_params(("arbitrary",)),
        name="combine",
    )(x1, h2, w_t, mod3, g_final.reshape(1, d), wsg, wsu, wsd, slot3, ys)


def _in_layout(d):
    qw = ATT_HEADS * ATT_HEAD_DIM
    kvw = ATT_KV_HEADS * ATT_HEAD_DIM
    rqk = RET_HEADS * RET_QK_DIM
    rv = RET_HEADS * RET_V_DIM
    src_order = [("qa", qw), ("ka", kvw), ("va", kvw), ("qr", rqk), ("kr", rqk),
                 ("vr", rv), ("gr", rv), ("ga", d), ("gb", d)]
    src = {}
    off = 0
    for name, width in src_order:
        src[name] = (off, width)
        off += width
    dst_order = ["ga", "gb", "gr", "vr", "qr", "kr", "qa", "ka", "va"]
    dst = {}
    off = 0
    for name in dst_order:
        dst[name] = off
        off += src[name][1]
    return src, dst_order, dst


def kernel(x, c, w_ada, b_ada, g_norm_mix, w_in, attn_sinks, w_attn_out, w_ret_out, w_o, g_norm_ffn,
           w_router, b_router, w_gate, w_up, w_down, w_sh_gate, w_sh_up, w_sh_down, g_norm_final):
    b, s, d = x.shape
    t = b * s
    depth = w_ada.shape[0]
    bf = jnp.bfloat16
    src, dst_order, dst = _in_layout(d)

    c_pad = jnp.zeros((V7X_SUBLANES, d), jnp.float32).at[:b].set(c)
    x2 = x.reshape(t, d)
    for l in range(depth):
        mod = _ada(c_pad, w_ada[l], b_ada[l])
        mod3 = mod[:b].reshape(b * 6, 1, d)

        h = _norm_mod(x2.reshape(b, s, d), g_norm_mix[l], mod3, 0, 1)
        w_in_p = jnp.concatenate(
            [w_in[l][:, src[n][0]:src[n][0] + src[n][1]] for n in dst_order], axis=1).astype(bf)
        proj = _matmul(h.reshape(t, d), w_in_p, bf, 1024, 768)
        proj3 = proj.reshape(b, s, proj.shape[1])
        attn = _attention(proj3, attn_sinks[l], dst["qa"], dst["ka"], dst["va"])
        ret = _retention(proj3, dst["qr"], dst["kr"], dst["vr"], dst["gr"])
        mix = _mix(attn.reshape(t, -1), ret.reshape(t, -1), w_attn_out[l].astype(bf),
                   w_ret_out[l].astype(bf), proj, dst["ga"], dst["gb"])
        x1 = _out_resid(mix, w_o[l].astype(bf), x2, mod3, 2, s)

        h2, h2p, idx3, pos3, w3, cnt = _router(x1, g_norm_ffn[l], mod3, 3, 4,
                                               w_router[l].T.astype(bf), b_router[l], s)
        counts = cnt[:, 0]
        tile = MOE_TILE
        padded = (counts + tile - 1) // tile * tile
        pad_end = jnp.cumsum(padded).astype(jnp.int32)
        pad_start = (pad_end - padded).astype(jnp.int32)
        n_blocks = (t * TOP_K) // tile + N_EXPERTS
        n_used = (pad_end[-1] // tile).reshape(1).astype(jnp.int32)
        blk_first = jnp.arange(n_blocks, dtype=jnp.int32) * tile
        blk_e = jnp.minimum(jnp.sum((pad_end[None, :] <= blk_first[:, None]).astype(jnp.int32), axis=1),
                            N_EXPERTS - 1)
        slab = d // 2 // V7X_LANES
        slot3 = _slots(pad_start, idx3, pos3)
        xs = _dispatch(pad_start, pad_end, h2p, slot3, n_blocks * tile, slab)
        ys = _experts(blk_e, n_used, xs, w_gate[l].astype(bf), w_up[l].astype(bf), w_down[l].astype(bf), slab)
        w_t = w3.transpose(0, 2, 1).reshape(t, TOP_K)
        is_last = l == depth - 1
        assert is_last, "the final norm is fused into the last layer's combine"
        x2 = _combine(x1, h2, w_t, mod3, 5, g_norm_final, w_sh_gate[l].astype(bf),
                      w_sh_up[l].astype(bf), w_sh_down[l].astype(bf), slot3, ys, s)
    return x2.reshape(b, s, d)
```

```python
import functools
import math

import jax
import jax.numpy as jnp
from jax import lax
from jax.experimental import pallas as pl
from jax.experimental.pallas import tpu as pltpu

ATT_HEADS = 32
ATT_KV_HEADS = 4
ATT_HEAD_DIM = 64
WINDOW = 128
ATT_BLOCK = 128
RET_HEADS = 8
RET_QK_DIM = 256
RET_V_DIM = 512
RET_CHUNK = 128
N_EXPERTS = 64
N_GROUPS = 8
TOPK_GROUPS = 4
TOP_K = 8
ROUTED_SCALE = 2.5
EPS = 1e-6

V7X_LANES = 128
V7X_SUBLANES = 8
V7X_VMEM_LIMIT_BYTES = 60000 * 1024

MOE_TILE = 256
COMBINE_TILE = 128
NEG_BIG = -1e30


def _div_block(n, target, align):
    best = None
    b = align
    while b <= min(n, target):
        if n % b == 0:
            best = b
        b += align
    assert best is not None, (n, target, align)
    return best


def _params(semantics):
    return pltpu.CompilerParams(dimension_semantics=semantics,
                                vmem_limit_bytes=V7X_VMEM_LIMIT_BYTES)


def _sigmoid(v):
    return 1.0 / (1.0 + jnp.exp(-v))


def _silu(v):
    return v * _sigmoid(v)


def _pack_pair(lo, hi):
    return pltpu.pack_elementwise([lo, hi], packed_dtype=jnp.bfloat16)


def _unpack_pair(p):
    lo = pltpu.unpack_elementwise(p, index=0, packed_dtype=jnp.bfloat16, unpacked_dtype=jnp.float32)
    hi = pltpu.unpack_elementwise(p, index=1, packed_dtype=jnp.bfloat16, unpacked_dtype=jnp.float32)
    return lo, hi


def _slab_load(ref, n_rows, slab):
    return jnp.concatenate([ref[pl.ds(s, n_rows, stride=slab), :] for s in range(slab)], axis=1)


def _slab_store(ref, val, n_rows, slab):
    for s in range(slab):
        ref[pl.ds(s, n_rows, stride=slab), :] = val[:, s * V7X_LANES:(s + 1) * V7X_LANES]


def _slab_rows(r, slab):
    return pl.ds(pl.multiple_of(r * slab, slab), slab)


def _ada_kernel(c_ref, w_ref, b_ref, o_ref):
    cs = _silu(c_ref[...]).astype(jnp.bfloat16)
    o_ref[...] = jnp.dot(cs, w_ref[...].astype(jnp.bfloat16),
                         preferred_element_type=jnp.float32) + b_ref[...]


def _ada(c_pad, w, b):
    m, d = c_pad.shape
    n = w.shape[1]
    tn = _div_block(n, 512, V7X_LANES)
    return pl.pallas_call(
        _ada_kernel,
        grid=(n // tn,),
        in_specs=[pl.BlockSpec((m, d), lambda j: (0, 0)),
                  pl.BlockSpec((d, tn), lambda j: (0, j)),
                  pl.BlockSpec((1, tn), lambda j: (0, j))],
        out_specs=pl.BlockSpec((m, tn), lambda j: (0, j)),
        out_shape=jax.ShapeDtypeStruct((m, n), jnp.float32),
        compiler_params=_params(("parallel",)),
        name="ada",
    )(c_pad, w, b.reshape(1, n))


def _norm_mod_kernel(x_ref, g_ref, sh_ref, sc_ref, o_ref):
    x = x_ref[0]
    ms = jnp.mean(x * x, axis=-1, keepdims=True)
    y = x * lax.rsqrt(ms + EPS) * g_ref[...]
    o_ref[0] = (y * (1.0 + sc_ref[0]) + sh_ref[0]).astype(o_ref.dtype)


def _norm_mod(x3, g, mod3, shift_idx, scale_idx):
    b, s, d = x3.shape
    ts = _div_block(s, 256, V7X_SUBLANES)
    return pl.pallas_call(
        _norm_mod_kernel,
        grid=(b, s // ts),
        in_specs=[pl.BlockSpec((1, ts, d), lambda bi, i: (bi, i, 0)),
                  pl.BlockSpec((1, d), lambda bi, i: (0, 0)),
                  pl.BlockSpec((1, 1, d), lambda bi, i: (bi * 6 + shift_idx, 0, 0)),
                  pl.BlockSpec((1, 1, d), lambda bi, i: (bi * 6 + scale_idx, 0, 0))],
        out_specs=pl.BlockSpec((1, ts, d), lambda bi, i: (bi, i, 0)),
        out_shape=jax.ShapeDtypeStruct((b, s, d), jnp.bfloat16),
        compiler_params=_params(("parallel", "parallel")),
        name="norm_mod",
    )(x3, g.reshape(1, d), mod3, mod3)


def _mm_kernel(a_ref, b_ref, o_ref):
    o_ref[...] = jnp.dot(a_ref[...], b_ref[...],
                         preferred_element_type=jnp.float32).astype(o_ref.dtype)


def _matmul(a, b, out_dtype, tm_target, tn_target):
    m, k = a.shape
    n = b.shape[1]
    tm = _div_block(m, tm_target, V7X_SUBLANES)
    tn = _div_block(n, tn_target, V7X_LANES)
    return pl.pallas_call(
        _mm_kernel,
        grid=(m // tm, n // tn),
        in_specs=[pl.BlockSpec((tm, k), lambda i, j: (i, 0)),
                  pl.BlockSpec((k, tn), lambda i, j: (0, j))],
        out_specs=pl.BlockSpec((tm, tn), lambda i, j: (i, j)),
        out_shape=jax.ShapeDtypeStruct((m, n), out_dtype),
        compiler_params=_params(("parallel", "parallel")),
        name="matmul",
    )(a, b)


def _attn_kernel(sink_ref, q_ref, kc_ref, kp_ref, vc_ref, vp_ref, bias_ref, o_ref):
    i = pl.program_id(1)
    blk = ATT_BLOCK
    hd = ATT_HEAD_DIM
    group = ATT_HEADS // ATT_KV_HEADS
    pairs = group // 2
    nt = (((1,), (1,)), ((), ()))
    zpad = jnp.zeros((2 * blk, hd), jnp.bfloat16)
    for h in range(ATT_KV_HEADS):
        sl = slice(h * hd, (h + 1) * hd)
        k2 = jnp.concatenate([kp_ref[0, :, sl], kc_ref[0, :, sl]], axis=0) * (hd ** -0.5)
        v2 = jnp.concatenate([vp_ref[0, :, sl], vc_ref[0, :, sl]], axis=0)
        qp = jnp.concatenate([q_ref[0, :, (h * pairs + p) * 2 * hd:(h * pairs + p + 1) * 2 * hd]
                              for p in range(pairs)], axis=0)
        acc = None
        for par in range(2):
            kz = jnp.concatenate([k2, zpad] if par == 0 else [zpad, k2], axis=1)
            vz = jnp.concatenate([v2, zpad] if par == 0 else [zpad, v2], axis=1)
            s = lax.dot_general(qp, kz, nt, preferred_element_type=jnp.float32) + bias_ref[h * 2 + par]
            s = jnp.concatenate([jnp.where(i == 0, NEG_BIG, s[:, :blk]), s[:, blk:]], axis=1)
            sink = jnp.concatenate([jnp.full((blk, 1), sink_ref[h * group + 2 * p + par], jnp.float32)
                                    for p in range(pairs)], axis=0)
            m = jnp.maximum(s.max(-1, keepdims=True), sink)
            pr = jnp.exp(s - m)
            denom = pr.sum(-1, keepdims=True) + jnp.exp(sink - m)
            o = jnp.dot(pr.astype(jnp.bfloat16), vz, preferred_element_type=jnp.float32) * (1.0 / denom)
            acc = o if acc is None else acc + o
        for p in range(pairs):
            o_ref[0, :, (h * pairs + p) * 2 * hd:(h * pairs + p + 1) * 2 * hd] = (
                acc[p * blk:(p + 1) * blk].astype(o_ref.dtype))


def _attn_bias():
    blk = ATT_BLOCK
    group = ATT_HEADS // ATT_KV_HEADS
    pairs = group // 2
    qi = jnp.arange(blk)[:, None]
    kj = jnp.arange(2 * blk)[None, :]
    dist = qi + blk - kj
    valid = (dist >= 0) & (dist < WINDOW)
    slopes = jnp.exp2(-8.0 * jnp.arange(1, ATT_HEADS + 1, dtype=jnp.float32) / ATT_HEADS)
    slopes = slopes.reshape(ATT_KV_HEADS, pairs, 2)
    bias = jnp.where(valid, -slopes[..., None, None] * dist.astype(jnp.float32), NEG_BIG)
    return bias.transpose(0, 2, 1, 3, 4).reshape(ATT_KV_HEADS * 2, pairs * blk, 2 * blk)


def _attention(proj3, sinks, q_off, k_off, v_off):
    b, s, _ = proj3.shape
    qw = ATT_HEADS * ATT_HEAD_DIM
    kvw = ATT_KV_HEADS * ATT_HEAD_DIM
    nb = s // ATT_BLOCK
    group = ATT_HEADS // ATT_KV_HEADS
    assert q_off % qw == 0 and k_off % kvw == 0 and v_off % kvw == 0
    assert group % 2 == 0 and 2 * ATT_HEAD_DIM == V7X_LANES and WINDOW == ATT_BLOCK
    assert 4 ** round(math.log(ATT_HEAD_DIM, 4)) == ATT_HEAD_DIM, "score scale must be a power of two"
    qb, kb, vb = q_off // qw, k_off // kvw, v_off // kvw
    bias = _attn_bias()
    return pl.pallas_call(
        _attn_kernel,
        grid_spec=pltpu.PrefetchScalarGridSpec(
            num_scalar_prefetch=1,
            grid=(b, nb),
            in_specs=[pl.BlockSpec((1, ATT_BLOCK, qw), lambda bi, i, sk: (bi, i, qb)),
                      pl.BlockSpec((1, ATT_BLOCK, kvw), lambda bi, i, sk: (bi, i, kb)),
                      pl.BlockSpec((1, ATT_BLOCK, kvw), lambda bi, i, sk: (bi, jnp.maximum(i - 1, 0), kb)),
                      pl.BlockSpec((1, ATT_BLOCK, kvw), lambda bi, i, sk: (bi, i, vb)),
                      pl.BlockSpec((1, ATT_BLOCK, kvw), lambda bi, i, sk: (bi, jnp.maximum(i - 1, 0), vb)),
                      pl.BlockSpec(bias.shape, lambda bi, i, sk: (0, 0, 0), pipeline_mode=pl.Buffered(1))],
            out_specs=pl.BlockSpec((1, ATT_BLOCK, qw), lambda bi, i, sk: (bi, i, 0)),
        ),
        out_shape=jax.ShapeDtypeStruct((b, s, qw), jnp.bfloat16),
        compiler_params=_params(("parallel", "parallel")),
        name="attention",
    )(sinks, proj3, proj3, proj3, proj3, proj3, bias)


def _ret_kernel(q_ref, k_ref, v_ref, gr_ref, mask_ref, qd_ref, kd_ref, cd_ref, o_ref, state_ref):
    c = pl.program_id(1)

    @pl.when(c == 0)
    def _():
        state_ref[...] = jnp.zeros_like(state_ref)

    nt = (((1,), (1,)), ((), ()))
    tn = (((0,), (0,)), ((), ()))
    for bi in range(q_ref.shape[0]):
        q = q_ref[bi]
        k = k_ref[bi]
        v = v_ref[bi]
        attn = lax.dot_general(q, k, nt, preferred_element_type=jnp.float32) * mask_ref[0]
        intra = jnp.dot(attn.astype(v.dtype), v, preferred_element_type=jnp.float32)
        state = state_ref[bi]
        inter = jnp.dot(q, state.astype(q.dtype), preferred_element_type=jnp.float32) * qd_ref[0]
        o = intra + inter
        kd = (k.astype(jnp.float32) * kd_ref[0]).astype(k.dtype)
        state_ref[bi] = state * cd_ref[0] + lax.dot_general(kd, v, tn, preferred_element_type=jnp.float32)
        mu = jnp.mean(o, axis=-1, keepdims=True)
        oc = o - mu
        var = jnp.mean(oc * oc, axis=-1, keepdims=True)
        y = oc * lax.rsqrt(var + EPS)
        o_ref[bi] = (_silu(gr_ref[bi].astype(jnp.float32)) * y).astype(o_ref.dtype)


def _retention(proj3, q_off, k_off, v_off, g_off):
    b, s, _ = proj3.shape
    dk, dv, ch = RET_QK_DIM, RET_V_DIM, RET_CHUNK
    assert q_off % dk == 0 and k_off % dk == 0 and v_off % dv == 0 and g_off % dv == 0
    qb, kb, vb, gb = q_off // dk, k_off // dk, v_off // dv, g_off // dv
    n = s // ch
    log_g = jnp.log1p(-jnp.exp2(-5.0 - jnp.arange(RET_HEADS, dtype=jnp.float32)))
    pos = jnp.arange(ch, dtype=jnp.float32)
    rel = pos[:, None] - pos[None, :]
    scale = dk ** -0.5
    mask = jnp.where(rel[None] >= 0, jnp.exp(rel[None] * log_g[:, None, None]), 0.0) * scale
    q_decay = jnp.exp((pos[None, :, None] + 1.0) * log_g[:, None, None])
    k_decay = jnp.exp((ch - 1.0 - pos[None, :, None]) * log_g[:, None, None]) * scale
    c_decay = jnp.exp(ch * log_g)[:, None, None]
    return pl.pallas_call(
        _ret_kernel,
        grid=(RET_HEADS, n),
        in_specs=[pl.BlockSpec((b, ch, dk), lambda h, c: (0, c, qb + h)),
                  pl.BlockSpec((b, ch, dk), lambda h, c: (0, c, kb + h)),
                  pl.BlockSpec((b, ch, dv), lambda h, c: (0, c, vb + h)),
                  pl.BlockSpec((b, ch, dv), lambda h, c: (0, c, gb + h)),
                  pl.BlockSpec((1, ch, ch), lambda h, c: (h, 0, 0)),
                  pl.BlockSpec((1, ch, 1), lambda h, c: (h, 0, 0)),
                  pl.BlockSpec((1, ch, 1), lambda h, c: (h, 0, 0)),
                  pl.BlockSpec((1, 1, 1), lambda h, c: (h, 0, 0))],
        out_specs=pl.BlockSpec((b, ch, dv), lambda h, c: (0, c, h)),
        out_shape=jax.ShapeDtypeStruct((b, s, RET_HEADS * dv), jnp.bfloat16),
        scratch_shapes=[pltpu.VMEM((b, dk, dv), jnp.float32)],
        compiler_params=_params(("parallel", "arbitrary")),
        name="retention",
    )(proj3, proj3, proj3, proj3, mask, q_decay, k_decay, c_decay)


def _mix_kernel(a_ref, r_ref, wa_ref, wr_ref, ga_ref, gb_ref, o_ref):
    ya = jnp.dot(a_ref[...], wa_ref[...], preferred_element_type=jnp.float32)
    yr = jnp.dot(r_ref[...], wr_ref[...], preferred_element_type=jnp.float32)
    ga = _sigmoid(ga_ref[...].astype(jnp.float32))
    gb = _sigmoid(gb_ref[...].astype(jnp.float32))
    o_ref[...] = (ga * ya + gb * yr).astype(o_ref.dtype)


def _mix(attn2, ret2, wa, wr, proj2, ga_off, gb_off):
    m, ka = attn2.shape
    kr = ret2.shape[1]
    d = wa.shape[1]
    tm = _div_block(m, 512, V7X_SUBLANES)
    tn = _div_block(d, 512, V7X_LANES)
    assert ga_off % tn == 0 and gb_off % tn == 0
    gab, gbb = ga_off // tn, gb_off // tn
    return pl.pallas_call(
        _mix_kernel,
        grid=(m // tm, d // tn),
        in_specs=[pl.BlockSpec((tm, ka), lambda i, j: (i, 0)),
                  pl.BlockSpec((tm, kr), lambda i, j: (i, 0)),
                  pl.BlockSpec((ka, tn), lambda i, j: (0, j)),
                  pl.BlockSpec((kr, tn), lambda i, j: (0, j)),
                  pl.BlockSpec((tm, tn), lambda i, j: (i, gab + j)),
                  pl.BlockSpec((tm, tn), lambda i, j: (i, gbb + j))],
        out_specs=pl.BlockSpec((tm, tn), lambda i, j: (i, j)),
        out_shape=jax.ShapeDtypeStruct((m, d), jnp.bfloat16),
        compiler_params=_params(("parallel", "parallel")),
        name="mix",
    )(attn2, ret2, wa, wr, proj2, proj2)


def _resid_kernel(a_ref, w_ref, x_ref, gt_ref, o_ref):
    y = jnp.dot(a_ref[...], w_ref[...], preferred_element_type=jnp.float32)
    o_ref[...] = x_ref[...] + gt_ref[0] * y


def _out_resid(mix2, w, x2, mod3, gate_idx, seq):
    m, k = mix2.shape
    d = w.shape[1]
    tm = _div_block(seq, 1024, V7X_SUBLANES)
    tn = _div_block(d, 512, V7X_LANES)
    per_b = seq // tm
    return pl.pallas_call(
        _resid_kernel,
        grid=(m // tm, d // tn),
        in_specs=[pl.BlockSpec((tm, k), lambda i, j: (i, 0)),
                  pl.BlockSpec((k, tn), lambda i, j: (0, j)),
                  pl.BlockSpec((tm, tn), lambda i, j: (i, j)),
                  pl.BlockSpec((1, 1, tn), lambda i, j: ((i // per_b) * 6 + gate_idx, 0, j))],
        out_specs=pl.BlockSpec((tm, tn), lambda i, j: (i, j)),
        out_shape=jax.ShapeDtypeStruct((m, d), jnp.float32),
        compiler_params=_params(("parallel", "parallel")),
        name="out_resid",
    )(mix2, w, x2, mod3)


def _router_kernel(x_ref, g_ref, sh_ref, sc_ref, wr_ref, br_ref,
                   h_ref, hp_ref, idx_ref, pos_ref, w_ref, cnt_ref, carry_ref):
    i = pl.program_id(0)
    e = N_EXPERTS
    per_g = e // N_GROUPS
    tb = x_ref.shape[0]

    @pl.when(i == 0)
    def _():
        carry_ref[...] = jnp.zeros_like(carry_ref)

    x = x_ref[...]
    ms = jnp.mean(x * x, axis=-1, keepdims=True)
    h = x * lax.rsqrt(ms + EPS) * g_ref[...]
    h = h * (1.0 + sc_ref[0]) + sh_ref[0]
    hb = h.astype(jnp.bfloat16)
    h_ref[...] = hb
    half = h.shape[1] // 2
    _slab_store(hp_ref, _pack_pair(h[:, :half], h[:, half:]), tb, half // V7X_LANES)

    nt = (((1,), (1,)), ((), ()))
    logits = lax.dot_general(wr_ref[...], hb, nt, preferred_element_type=jnp.float32)
    scores = _sigmoid(logits)
    choice = scores + br_ref[...]

    c3 = choice.reshape(N_GROUPS, per_g, tb)
    j_iota = lax.broadcasted_iota(jnp.int32, c3.shape, 1).astype(jnp.float32)
    m1 = c3.max(axis=1, keepdims=True)
    first = jnp.min(jnp.where(c3 == m1, j_iota, float(per_g)), axis=1, keepdims=True)
    m2 = jnp.where(j_iota == first, -jnp.inf, c3).max(axis=1, keepdims=True)
    gs = (m1 + m2).reshape(N_GROUPS, tb)

    g_iota = lax.broadcasted_iota(jnp.int32, gs.shape, 0)
    grank = jnp.zeros(gs.shape, jnp.int32)
    for gp in range(N_GROUPS):
        row = gs[gp:gp + 1, :]
        ahead = (row > gs) | ((row == gs) & (gp < g_iota))
        grank = grank + ahead.astype(jnp.int32)
    gmask = grank < TOPK_GROUPS
    emask = jnp.broadcast_to(gmask.reshape(N_GROUPS, 1, tb), c3.shape).reshape(e, tb)
    masked = jnp.where(emask, choice, -jnp.inf)

    e_iota = lax.broadcasted_iota(jnp.int32, masked.shape, 0)
    erank = jnp.zeros(masked.shape, jnp.int32)
    for ep in range(e):
        row = masked[ep:ep + 1, :]
        ahead = (row > masked) | ((row == masked) & (ep < e_iota))
        erank = erank + ahead.astype(jnp.int32)
    sel = (erank < TOP_K) & emask
    self32 = sel.astype(jnp.float32)

    wsel = scores * self32
    wn = wsel / jnp.sum(wsel, axis=0, keepdims=True) * ROUTED_SCALE

    selb = self32.astype(jnp.bfloat16)
    t_r = lax.broadcasted_iota(jnp.int32, (tb, tb), 0)
    t_c = lax.broadcasted_iota(jnp.int32, (tb, tb), 1)
    upper = (t_r <= t_c).astype(jnp.bfloat16)
    incl = jnp.dot(selb, upper, preferred_element_type=jnp.float32)
    carry = carry_ref[...]
    rank_in_e = carry + incl - 1.0
    carry_new = carry + jnp.sum(self32, axis=1, keepdims=True)
    carry_ref[...] = carry_new
    cnt_ref[...] = jnp.broadcast_to(carry_new, cnt_ref.shape).astype(jnp.int32)

    e_r = lax.broadcasted_iota(jnp.int32, (e, e), 0)
    e_c = lax.broadcasted_iota(jnp.int32, (e, e), 1)
    lower = (e_c < e_r).astype(jnp.bfloat16)
    before = jnp.dot(lower, selb, preferred_element_type=jnp.float32)
    e_f = e_iota.astype(jnp.float32)
    idx_rows, pos_rows, w_rows = [], [], []
    for k in range(TOP_K):
        hit = jnp.where(sel & (before == float(k)), 1.0, 0.0)
        idx_rows.append(jnp.sum(hit * e_f, axis=0, keepdims=True))
        pos_rows.append(jnp.sum(hit * rank_in_e, axis=0, keepdims=True))
        w_rows.append(jnp.sum(hit * wn, axis=0, keepdims=True))
    idx_ref[0] = jnp.concatenate(idx_rows, axis=0).astype(jnp.int32)
    pos_ref[0] = jnp.concatenate(pos_rows, axis=0).astype(jnp.int32)
    w_ref[0] = jnp.concatenate(w_rows, axis=0)


def _router(x1, g, mod3, shift_idx, scale_idx, w_router_t, b_router, seq):
    t, d = x1.shape
    e = N_EXPERTS
    tb = MOE_TILE
    assert seq % tb == 0 and d % (2 * V7X_LANES * V7X_SUBLANES) == 0
    slab = d // 2 // V7X_LANES
    per_b = seq // tb
    nt = t // tb
    return pl.pallas_call(
        _router_kernel,
        grid=(nt,),
        in_specs=[pl.BlockSpec((tb, d), lambda i: (i, 0)),
                  pl.BlockSpec((1, d), lambda i: (0, 0)),
                  pl.BlockSpec((1, 1, d), lambda i: ((i // per_b) * 6 + shift_idx, 0, 0)),
                  pl.BlockSpec((1, 1, d), lambda i: ((i // per_b) * 6 + scale_idx, 0, 0)),
                  pl.BlockSpec((e, d), lambda i: (0, 0)),
                  pl.BlockSpec((e, 1), lambda i: (0, 0))],
        out_specs=[pl.BlockSpec((tb, d), lambda i: (i, 0)),
                   pl.BlockSpec((tb * slab, V7X_LANES), lambda i: (i, 0)),
                   pl.BlockSpec((1, TOP_K, tb), lambda i: (i, 0, 0)),
                   pl.BlockSpec((1, TOP_K, tb), lambda i: (i, 0, 0)),
                   pl.BlockSpec((1, TOP_K, tb), lambda i: (i, 0, 0)),
                   pl.BlockSpec((e, V7X_LANES), lambda i: (0, 0))],
        out_shape=[jax.ShapeDtypeStruct((t, d), jnp.bfloat16),
                   jax.ShapeDtypeStruct((t * slab, V7X_LANES), jnp.int32),
                   jax.ShapeDtypeStruct((nt, TOP_K, tb), jnp.int32),
                   jax.ShapeDtypeStruct((nt, TOP_K, tb), jnp.int32),
                   jax.ShapeDtypeStruct((nt, TOP_K, tb), jnp.float32),
                   jax.ShapeDtypeStruct((e, V7X_LANES), jnp.int32)],
        scratch_shapes=[pltpu.VMEM((e, 1), jnp.float32)],
        compiler_params=_params(("arbitrary",)),
        name="router",
    )(x1, g.reshape(1, d), mod3, mod3, w_router_t, b_router.reshape(e, 1))


def _slots_kernel(pstart_ref, idx_ref, pos_ref, o_ref):
    idx = idx_ref[...]
    base = jnp.zeros(idx.shape, jnp.int32)
    for ex in range(N_EXPERTS):
        base = jnp.where(idx == ex, pstart_ref[ex], base)
    o_ref[...] = base + pos_ref[...]


def _slots(pad_start, idx3, pos3):
    nt = idx3.shape[0]
    blk = (1,) + idx3.shape[1:]
    spec = pl.BlockSpec(blk, lambda i, ps: (i, 0, 0))
    return pl.pallas_call(
        _slots_kernel,
        grid_spec=pltpu.PrefetchScalarGridSpec(num_scalar_prefetch=1, grid=(nt,),
                                               in_specs=[spec, spec], out_specs=spec),
        out_shape=jax.ShapeDtypeStruct(idx3.shape, jnp.int32),
        compiler_params=_params(("parallel",)),
        name="slots",
    )(pad_start, idx3, pos3)


def _dispatch_kernel(pstart_ref, pend_ref, hp_ref, slot_hbm, xs_hbm,
                     slot_s, zero_v, sem_i, sem_z, sem_r, *, slab):
    i = pl.program_id(0)
    tb = hp_ref.shape[0] // slab

    def slot_copy():
        return pltpu.make_async_copy(slot_hbm.at[i], slot_s, sem_i)

    slot_copy().start()

    def zero_copy(ex):
        first = pl.multiple_of((pend_ref[ex] - tb) * slab, tb * slab)
        return pltpu.make_async_copy(zero_v, xs_hbm.at[pl.ds(first, tb * slab)], sem_z)

    @pl.when(i == 0)
    def _():
        zero_v[...] = jnp.zeros_like(zero_v)

        def start(ex, carry):
            @pl.when(pend_ref[ex] > pstart_ref[ex])
            def _():
                zero_copy(ex).start()
            return carry

        def wait(ex, carry):
            @pl.when(pend_ref[ex] > pstart_ref[ex])
            def _():
                zero_copy(ex).wait()
            return carry

        lax.fori_loop(0, N_EXPERTS, start, 0)
        lax.fori_loop(0, N_EXPERTS, wait, 0)

    slot_copy().wait()

    def row_copy(t, k):
        return pltpu.make_async_copy(hp_ref.at[_slab_rows(t, slab)],
                                     xs_hbm.at[_slab_rows(slot_s[k, t], slab)], sem_r)

    def start_rows(t, carry):
        for k in range(TOP_K):
            row_copy(t, k).start(priority=k % 2)
        return carry

    lax.fori_loop(0, tb, start_rows, 0, unroll=2)
    for k in range(TOP_K):
        pltpu.make_async_copy(hp_ref, xs_hbm.at[pl.ds(0, tb * slab)], sem_r).wait()


def _dispatch(pad_start, pad_end, h2p, slot3, n_rows, slab):
    tb = MOE_TILE
    t = h2p.shape[0] // slab
    return pl.pallas_call(
        functools.partial(_dispatch_kernel, slab=slab),
        grid_spec=pltpu.PrefetchScalarGridSpec(
            num_scalar_prefetch=2,
            grid=(t // tb,),
            in_specs=[pl.BlockSpec((tb * slab, V7X_LANES), lambda i, ps, pe: (i, 0)),
                      pl.BlockSpec(memory_space=pl.ANY)],
            out_specs=pl.BlockSpec(memory_space=pl.ANY),
            scratch_shapes=[pltpu.SMEM((TOP_K, tb), jnp.int32),
                            pltpu.VMEM((tb * slab, V7X_LANES), jnp.int32),
                            pltpu.SemaphoreType.DMA,
                            pltpu.SemaphoreType.DMA,
                            pltpu.SemaphoreType.DMA],
        ),
        out_shape=jax.ShapeDtypeStruct((n_rows * slab, V7X_LANES), jnp.int32),
        compiler_params=_params(("arbitrary",)),
        name="dispatch",
    )(pad_start, pad_end, h2p, slot3)


def _expert_kernel(blk_e_ref, nused_ref, x_ref, wg_ref, wu_ref, wd_ref, o_ref, *, slab):
    i = pl.program_id(0)

    @pl.when(i < nused_ref[0])
    def _():
        tb = x_ref.shape[0] // slab
        half = slab * V7X_LANES
        lo, hi = _unpack_pair(_slab_load(x_ref, tb, slab))
        lo = lo.astype(jnp.bfloat16)
        hi = hi.astype(jnp.bfloat16)
        g = (jnp.dot(lo, wg_ref[0, :half, :], preferred_element_type=jnp.float32)
             + jnp.dot(hi, wg_ref[0, half:, :], preferred_element_type=jnp.float32))
        u = (jnp.dot(lo, wu_ref[0, :half, :], preferred_element_type=jnp.float32)
             + jnp.dot(hi, wu_ref[0, half:, :], preferred_element_type=jnp.float32))
        a = (_silu(g) * u).astype(jnp.bfloat16)
        y = jnp.dot(a, wd_ref[0], preferred_element_type=jnp.float32)
        _slab_store(o_ref, _pack_pair(y[:, :half], y[:, half:]), tb, slab)


def _experts(blk_e, n_used, xs, wg, wu, wd, slab):
    tb = MOE_TILE
    p = xs.shape[0] // slab
    d = 2 * slab * V7X_LANES
    f = wg.shape[2]
    nblk = p // tb

    def row_map(i, be, nu):
        return (jnp.minimum(i, nu[0] - 1), 0)

    def w_map(i, be, nu):
        return (be[jnp.minimum(i, nu[0] - 1)], 0, 0)

    return pl.pallas_call(
        functools.partial(_expert_kernel, slab=slab),
        grid_spec=pltpu.PrefetchScalarGridSpec(
            num_scalar_prefetch=2,
            grid=(nblk,),
            in_specs=[pl.BlockSpec((tb * slab, V7X_LANES), row_map),
                      pl.BlockSpec((1, d, f), w_map),
                      pl.BlockSpec((1, d, f), w_map),
                      pl.BlockSpec((1, f, d), w_map)],
            out_specs=pl.BlockSpec((tb * slab, V7X_LANES), row_map),
        ),
        out_shape=jax.ShapeDtypeStruct(xs.shape, jnp.int32),
        compiler_params=_params(("arbitrary",)),
        name="experts",
    )(blk_e, n_used, xs, wg, wu, wd)


def _combine_kernel(x_ref, h_ref, wt_ref, gt_ref, gf_ref, wsg_ref, wsu_ref, wsd_ref,
                    slot_hbm, ys_hbm, o_ref, slot_s, rows_v, sem_i, sem_r):
    i = pl.program_id(0)
    n = pl.num_programs(0)
    tb = x_ref.shape[0]
    half = x_ref.shape[1] // 2
    slab = half // V7X_LANES
    per_tile = MOE_TILE // tb
    cur = i % 2

    def slot_copy(step):
        win = pl.ds((step % per_tile) * tb, tb)
        return pltpu.make_async_copy(slot_hbm.at[step // per_tile, :, win], slot_s.at[step % 2],
                                     sem_i.at[step % 2])

    def start_rows(step):
        buf = step % 2

        def body(t, carry):
            for k in range(TOP_K):
                pltpu.make_async_copy(ys_hbm.at[_slab_rows(slot_s[buf, k, t], slab)],
                                      rows_v.at[buf, k, _slab_rows(t, slab)],
                                      sem_r.at[buf]).start(priority=k % 2)
            return carry

        lax.fori_loop(0, tb, body, 0, unroll=2)

    def wait_rows(buf):
        for k in range(TOP_K):
            pltpu.make_async_copy(ys_hbm.at[pl.ds(0, tb * slab)], rows_v.at[buf, k], sem_r.at[buf]).wait()

    @pl.when(i == 0)
    def _():
        slot_copy(i).start()
        slot_copy(i).wait()
        start_rows(i)

    @pl.when(i + 1 < n)
    def _():
        slot_copy(i + 1).start()

    h = h_ref[...]
    g = jnp.dot(h, wsg_ref[...], preferred_element_type=jnp.float32)
    u = jnp.dot(h, wsu_ref[...], preferred_element_type=jnp.float32)
    a = (_silu(g) * u).astype(jnp.bfloat16)
    y = jnp.dot(a, wsd_ref[...], preferred_element_type=jnp.float32)
    y_lo = y[:, :half]
    y_hi = y[:, half:]

    @pl.when(i + 1 < n)
    def _():
        slot_copy(i + 1).wait()
        start_rows(i + 1)

    wait_rows(cur)

    wt = wt_ref[...]
    for k in range(TOP_K):
        lo, hi = _unpack_pair(_slab_load(rows_v.at[cur, k], tb, slab))
        wk = wt[:, k:k + 1]
        y_lo = y_lo + wk * lo
        y_hi = y_hi + wk * hi
    yy = jnp.concatenate([y_lo, y_hi], axis=1)
    xo = x_ref[...] + gt_ref[0] * yy
    ms = jnp.mean(xo * xo, axis=-1, keepdims=True)
    o_ref[...] = xo * lax.rsqrt(ms + EPS) * gf_ref[...]


def _combine(x1, h2, w_t, mod3, gate_idx, g_final, wsg, wsu, wsd, slot3, ys, seq):
    t, d = x1.shape
    tb = COMBINE_TILE
    assert MOE_TILE % tb == 0 and seq % tb == 0
    per_b = seq // tb
    half = d // 2
    f = wsg.shape[1]
    once = pl.Buffered(1)
    return pl.pallas_call(
        _combine_kernel,
        grid=(t // tb,),
        in_specs=[pl.BlockSpec((tb, d), lambda i: (i, 0)),
                  pl.BlockSpec((tb, d), lambda i: (i, 0)),
                  pl.BlockSpec((tb, TOP_K), lambda i: (i, 0)),
                  pl.BlockSpec((1, 1, d), lambda i: ((i // per_b) * 6 + gate_idx, 0, 0)),
                  pl.BlockSpec((1, d), lambda i: (0, 0)),
                  pl.BlockSpec((d, f), lambda i: (0, 0), pipeline_mode=once),
                  pl.BlockSpec((d, f), lambda i: (0, 0), pipeline_mode=once),
                  pl.BlockSpec((f, d), lambda i: (0, 0), pipeline_mode=once),
                  pl.BlockSpec(memory_space=pl.ANY),
                  pl.BlockSpec(memory_space=pl.ANY)],
        out_specs=pl.BlockSpec((tb, d), lambda i: (i, 0)),
        scratch_shapes=[pltpu.SMEM((2, TOP_K, tb), jnp.int32),
                        pltpu.VMEM((2, TOP_K, tb * half // V7X_LANES, V7X_LANES), jnp.int32),
                        pltpu.SemaphoreType.DMA((2,)),
                        pltpu.SemaphoreType.DMA((2,))],
        out_shape=jax.ShapeDtypeStruct((t, d), jnp.float32),
        compiler_params=_params(("arbitrary",)),
        name="combine",
    )(x1, h2, w_t, mod3, g_final.reshape(1, d), wsg, wsu, wsd, slot3, ys)


def _in_layout(d):
    qw = ATT_HEADS * ATT_HEAD_DIM
    kvw = ATT_KV_HEADS * ATT_HEAD_DIM
    rqk = RET_HEADS * RET_QK_DIM
    rv = RET_HEADS * RET_V_DIM
    src_order = [("qa", qw), ("ka", kvw), ("va", kvw), ("qr", rqk), ("kr", rqk),
                 ("vr", rv), ("gr", rv), ("ga", d), ("gb", d)]
    src = {}
    off = 0
    for name, width in src_order:
        src[name] = (off, width)
        off += width
    dst_order = ["ga", "gb", "gr", "vr", "qr", "kr", "qa", "ka", "va"]
    dst = {}
    off = 0
    for name in dst_order:
        dst[name] = off
        off += src[name][1]
    return src, dst_order, dst


def kernel(x, c, w_ada, b_ada, g_norm_mix, w_in, attn_sinks, w_attn_out, w_ret_out, w_o, g_norm_ffn,
           w_router, b_router, w_gate, w_up, w_down, w_sh_gate, w_sh_up, w_sh_down, g_norm_final):
    b, s, d = x.shape
    t = b * s
    depth = w_ada.shape[0]
    bf = jnp.bfloat16
    src, dst_order, dst = _in_layout(d)

    c_pad = jnp.zeros((V7X_SUBLANES, d), jnp.float32).at[:b].set(c)
    x2 = x.reshape(t, d)
    for l in range(depth):
        mod = _ada(c_pad, w_ada[l], b_ada[l])
        mod3 = mod[:b].reshape(b * 6, 1, d)

        h = _norm_mod(x2.reshape(b, s, d), g_norm_mix[l], mod3, 0, 1)
        w_in_p = jnp.concatenate(
            [w_in[l][:, src[n][0]:src[n][0] + src[n][1]] for n in dst_order], axis=1).astype(bf)
        proj = _matmul(h.reshape(t, d), w_in_p, bf, 1024, 768)
        proj3 = proj.reshape(b, s, proj.shape[1])
        attn = _attention(proj3, attn_sinks[l], dst["qa"], dst["ka"], dst["va"])
        ret = _retention(proj3, dst["qr"], dst["kr"], dst["vr"], dst["gr"])
        mix = _mix(attn.reshape(t, -1), ret.reshape(t, -1), w_attn_out[l].astype(bf),
                   w_ret_out[l].astype(bf), proj, dst["ga"], dst["gb"])
        x1 = _out_resid(mix, w_o[l].astype(bf), x2, mod3, 2, s)

        h2, h2p, idx3, pos3, w3, cnt = _router(x1, g_norm_ffn[l], mod3, 3, 4,
                                               w_router[l].T.astype(bf), b_router[l], s)
        counts = cnt[:, 0]
        tile = MOE_TILE
        padded = (counts + tile - 1) // tile * tile
        pad_end = jnp.cumsum(padded).astype(jnp.int32)
        pad_start = (pad_end - padded).astype(jnp.int32)
        n_blocks = (t * TOP_K) // tile + N_EXPERTS
        n_used = (pad_end[-1] // tile).reshape(1).astype(jnp.int32)
        blk_first = jnp.arange(n_blocks, dtype=jnp.int32) * tile
        blk_e = jnp.minimum(jnp.sum((pad_end[None, :] <= blk_first[:, None]).astype(jnp.int32), axis=1),
                            N_EXPERTS - 1)
        slab = d // 2 // V7X_LANES
        slot3 = _slots(pad_start, idx3, pos3)
        xs = _dispatch(pad_start, pad_end, h2p, slot3, n_blocks * tile, slab)
        ys = _experts(blk_e, n_used, xs, w_gate[l].astype(bf), w_up[l].astype(bf), w_down[l].astype(bf), slab)
        w_t = w3.transpose(0, 2, 1).reshape(t, TOP_K)
        is_last = l == depth - 1
        assert is_last, "the final norm is fused into the last layer's combine"
        x2 = _combine(x1, h2, w_t, mod3, 5, g_norm_final, w_sh_gate[l].astype(bf),
                      w_sh_up[l].astype(bf), w_sh_down[l].astype(bf), slot3, ys, s)
    return x2.reshape(b, s, d)
```

```python
import functools
import math

import jax
import jax.numpy as jnp
from jax import lax
from jax.experimental import pallas as pl
from jax.experimental.pallas import tpu as pltpu

ATT_HEADS = 32
ATT_KV_HEADS = 4
ATT_HEAD_DIM = 64
WINDOW = 128
ATT_BLOCK = 128
RET_HEADS = 8
RET_QK_DIM = 256
RET_V_DIM = 512
RET_CHUNK = 128
N_EXPERTS = 64
N_GROUPS = 8
TOPK_GROUPS = 4
TOP_K = 8
ROUTED_SCALE = 2.5
EPS = 1e-6

V7X_LANES = 128
V7X_SUBLANES = 8
V7X_VMEM_LIMIT_BYTES = 60000 * 1024

MOE_TILE = 256
COMBINE_TILE = 128
NEG_BIG = -1e30


def _div_block(n, target, align):
    best = None
    b = align
    while b <= min(n, target):
        if n % b == 0:
            best = b
        b += align
    assert best is not None, (n, target, align)
    return best


def _params(semantics):
    return pltpu.CompilerParams(dimension_semantics=semantics,
                                vmem_limit_bytes=V7X_VMEM_LIMIT_BYTES)


def _sigmoid(v):
    return 1.0 / (1.0 + jnp.exp(-v))


def _silu(v):
    return v * _sigmoid(v)


def _pack_pair(lo, hi):
    return pltpu.pack_elementwise([lo, hi], packed_dtype=jnp.bfloat16)


def _unpack_pair(p):
    lo = pltpu.unpack_elementwise(p, index=0, packed_dtype=jnp.bfloat16, unpacked_dtype=jnp.float32)
    hi = pltpu.unpack_elementwise(p, index=1, packed_dtype=jnp.bfloat16, unpacked_dtype=jnp.float32)
    return lo, hi


def _slab_load(ref, n_rows, slab):
    return jnp.concatenate([ref[pl.ds(s, n_rows, stride=slab), :] for s in range(slab)], axis=1)


def _slab_store(ref, val, n_rows, slab):
    for s in range(slab):
        ref[pl.ds(s, n_rows, stride=slab), :] = val[:, s * V7X_LANES:(s + 1) * V7X_LANES]


def _slab_rows(r, slab):
    return pl.ds(pl.multiple_of(r * slab, slab), slab)


def _ada_kernel(c_ref, w_ref, b_ref, o_ref):
    cs = _silu(c_ref[...]).astype(jnp.bfloat16)
    o_ref[...] = jnp.dot(cs, w_ref[...].astype(jnp.bfloat16),
                         preferred_element_type=jnp.float32) + b_ref[...]


def _ada(c_pad, w, b):
    m, d = c_pad.shape
    n = w.shape[1]
    tn = _div_block(n, 512, V7X_LANES)
    return pl.pallas_call(
        _ada_kernel,
        grid=(n // tn,),
        in_specs=[pl.BlockSpec((m, d), lambda j: (0, 0)),
                  pl.BlockSpec((d, tn), lambda j: (0, j)),
                  pl.BlockSpec((1, tn), lambda j: (0, j))],
        out_specs=pl.BlockSpec((m, tn), lambda j: (0, j)),
        out_shape=jax.ShapeDtypeStruct((m, n), jnp.float32),
        compiler_params=_params(("parallel",)),
        name="ada",
    )(c_pad, w, b.reshape(1, n))


def _norm_mod_kernel(x_ref, g_ref, sh_ref, sc_ref, o_ref):
    x = x_ref[0]
    ms = jnp.mean(x * x, axis=-1, keepdims=True)
    y = x * lax.rsqrt(ms + EPS) * g_ref[...]
    o_ref[0] = (y * (1.0 + sc_ref[0]) + sh_ref[0]).astype(o_ref.dtype)


def _norm_mod(x3, g, mod3, shift_idx, scale_idx):
    b, s, d = x3.shape
    ts = _div_block(s, 256, V7X_SUBLANES)
    return pl.pallas_call(
        _norm_mod_kernel,
        grid=(b, s // ts),
        in_specs=[pl.BlockSpec((1, ts, d), lambda bi, i: (bi, i, 0)),
                  pl.BlockSpec((1, d), lambda bi, i: (0, 0)),
                  pl.BlockSpec((1, 1, d), lambda bi, i: (bi * 6 + shift_idx, 0, 0)),
                  pl.BlockSpec((1, 1, d), lambda bi, i: (bi * 6 + scale_idx, 0, 0))],
        out_specs=pl.BlockSpec((1, ts, d), lambda bi, i: (bi, i, 0)),
        out_shape=jax.ShapeDtypeStruct((b, s, d), jnp.bfloat16),
        compiler_params=_params(("parallel", "parallel")),
        name="norm_mod",
    )(x3, g.reshape(1, d), mod3, mod3)


def _in_proj_kernel(a_ref, b_ref, *rest, n_side, n_cast, n_j):
    src = rest[:n_side]
    o_ref = rest[n_side]
    dst = rest[n_side + 1:]
    o_ref[...] = jnp.dot(a_ref[...], b_ref[...],
                         preferred_element_type=jnp.float32).astype(o_ref.dtype)
    step = pl.program_id(0) * n_j + pl.program_id(1)

    @pl.when(step < n_cast)
    def _():
        for s_ref, d_ref in zip(src, dst):
            d_ref[...] = s_ref[...].astype(d_ref.dtype)


def _in_proj(a, b, side, tm_target, tn_target):
    m, k = a.shape
    n = b.shape[1]
    tm = _div_block(m, tm_target, V7X_SUBLANES)
    tn = _div_block(n, tn_target, V7X_LANES)
    n_i, n_j = m // tm, n // tn
    n_cast = 1 << ((n_i * n_j).bit_length() - 1)
    chunked = []
    for w in side:
        rows = w.size // w.shape[-1]
        assert rows % (n_cast * V7X_SUBLANES * 2) == 0, (w.shape, n_cast)
        chunked.append(w.reshape(n_cast, rows // n_cast, w.shape[-1]))

    def side_map(i, j):
        return (jnp.minimum(i * n_j + j, n_cast - 1), 0, 0)

    side_specs = [pl.BlockSpec((1,) + c.shape[1:], side_map) for c in chunked]
    outs = pl.pallas_call(
        functools.partial(_in_proj_kernel, n_side=len(side), n_cast=n_cast, n_j=n_j),
        grid=(n_i, n_j),
        in_specs=[pl.BlockSpec((tm, k), lambda i, j: (i, 0)),
                  pl.BlockSpec((k, tn), lambda i, j: (0, j))] + side_specs,
        out_specs=[pl.BlockSpec((tm, tn), lambda i, j: (i, j))] + side_specs,
        out_shape=[jax.ShapeDtypeStruct((m, n), jnp.bfloat16)]
        + [jax.ShapeDtypeStruct(c.shape, jnp.bfloat16) for c in chunked],
        compiler_params=_params(("arbitrary", "arbitrary")),
        name="in_proj",
    )(a, b, *chunked)
    return outs[0], [o.reshape(w.shape) for o, w in zip(outs[1:], side)]


def _attn_kernel(sink_ref, q_ref, kc_ref, kp_ref, vc_ref, vp_ref, bias_ref, o_ref):
    i = pl.program_id(1)
    blk = ATT_BLOCK
    hd = ATT_HEAD_DIM
    group = ATT_HEADS // ATT_KV_HEADS
    pairs = group // 2
    nt = (((1,), (1,)), ((), ()))
    zpad = jnp.zeros((2 * blk, hd), jnp.bfloat16)

    def scores(h, par):
        sl = slice(h * hd, (h + 1) * hd)
        k2 = jnp.concatenate([kp_ref[0, :, sl], kc_ref[0, :, sl]], axis=0) * (hd ** -0.5)
        qp = jnp.concatenate([q_ref[0, :, (h * pairs + p) * 2 * hd:(h * pairs + p + 1) * 2 * hd]
                              for p in range(pairs)], axis=0)
        kz = jnp.concatenate([k2, zpad] if par == 0 else [zpad, k2], axis=1)
        return lax.dot_general(qp, kz, nt, preferred_element_type=jnp.float32)

    def softmax(s, h, par):
        s = s + bias_ref[h * 2 + par]
        s = jnp.concatenate([jnp.where(i == 0, NEG_BIG, s[:, :blk]), s[:, blk:]], axis=1)
        sink = jnp.concatenate([jnp.full((blk, 1), sink_ref[h * group + 2 * p + par], jnp.float32)
                                for p in range(pairs)], axis=0)
        m = jnp.maximum(s.max(-1, keepdims=True), sink)
        pr = jnp.exp(s - m)
        denom = pr.sum(-1, keepdims=True) + jnp.exp(sink - m)
        return pr.astype(jnp.bfloat16), 1.0 / denom

    def values(pr, inv, h, par):
        sl = slice(h * hd, (h + 1) * hd)
        v2 = jnp.concatenate([vp_ref[0, :, sl], vc_ref[0, :, sl]], axis=0)
        vz = jnp.concatenate([v2, zpad] if par == 0 else [zpad, v2], axis=1)
        return jnp.dot(pr, vz, preferred_element_type=jnp.float32) * inv

    items = [(h, par) for h in range(ATT_KV_HEADS) for par in range(2)]
    s_of, p_of, acc = {}, {}, {}
    for n in range(len(items) + 2):
        if n < len(items):
            s_of[n] = scores(*items[n])
        if 1 <= n <= len(items):
            p_of[n - 1] = softmax(s_of.pop(n - 1), *items[n - 1])
        if n >= 2:
            h, par = items[n - 2]
            o = values(*p_of.pop(n - 2), h, par)
            acc[h] = o if par == 0 else acc[h] + o
            if par == 1:
                out = acc.pop(h)
                for p in range(pairs):
                    o_ref[0, :, (h * pairs + p) * 2 * hd:(h * pairs + p + 1) * 2 * hd] = (
                        out[p * blk:(p + 1) * blk].astype(o_ref.dtype))


def _attn_bias():
    blk = ATT_BLOCK
    group = ATT_HEADS // ATT_KV_HEADS
    pairs = group // 2
    qi = jnp.arange(blk)[:, None]
    kj = jnp.arange(2 * blk)[None, :]
    dist = qi + blk - kj
    valid = (dist >= 0) & (dist < WINDOW)
    slopes = jnp.exp2(-8.0 * jnp.arange(1, ATT_HEADS + 1, dtype=jnp.float32) / ATT_HEADS)
    slopes = slopes.reshape(ATT_KV_HEADS, pairs, 2)
    bias = jnp.where(valid, -slopes[..., None, None] * dist.astype(jnp.float32), NEG_BIG)
    return bias.transpose(0, 2, 1, 3, 4).reshape(ATT_KV_HEADS * 2, pairs * blk, 2 * blk)


def _attention(proj3, sinks, q_off, k_off, v_off):
    b, s, _ = proj3.shape
    qw = ATT_HEADS * ATT_HEAD_DIM
    kvw = ATT_KV_HEADS * ATT_HEAD_DIM
    nb = s // ATT_BLOCK
    group = ATT_HEADS // ATT_KV_HEADS
    assert q_off % qw == 0 and k_off % kvw == 0 and v_off % kvw == 0
    assert group % 2 == 0 and 2 * ATT_HEAD_DIM == V7X_LANES and WINDOW == ATT_BLOCK
    assert 4 ** round(math.log(ATT_HEAD_DIM, 4)) == ATT_HEAD_DIM, "score scale must be a power of two"
    qb, kb, vb = q_off // qw, k_off // kvw, v_off // kvw
    bias = _attn_bias()
    return pl.pallas_call(
        _attn_kernel,
        grid_spec=pltpu.PrefetchScalarGridSpec(
            num_scalar_prefetch=1,
            grid=(b, nb),
            in_specs=[pl.BlockSpec((1, ATT_BLOCK, qw), lambda bi, i, sk: (bi, i, qb)),
                      pl.BlockSpec((1, ATT_BLOCK, kvw), lambda bi, i, sk: (bi, i, kb)),
                      pl.BlockSpec((1, ATT_BLOCK, kvw), lambda bi, i, sk: (bi, jnp.maximum(i - 1, 0), kb)),
                      pl.BlockSpec((1, ATT_BLOCK, kvw), lambda bi, i, sk: (bi, i, vb)),
                      pl.BlockSpec((1, ATT_BLOCK, kvw), lambda bi, i, sk: (bi, jnp.maximum(i - 1, 0), vb)),
                      pl.BlockSpec(bias.shape, lambda bi, i, sk: (0, 0, 0), pipeline_mode=pl.Buffered(1))],
            out_specs=pl.BlockSpec((1, ATT_BLOCK, qw), lambda bi, i, sk: (bi, i, 0)),
        ),
        out_shape=jax.ShapeDtypeStruct((b, s, qw), jnp.bfloat16),
        compiler_params=_params(("parallel", "parallel")),
        name="attention",
    )(sinks, proj3, proj3, proj3, proj3, proj3, bias)


def _ret_kernel(q_ref, k_ref, v_ref, gr_ref, mask_ref, qd_ref, kd_ref, cd_ref, o_ref, state_ref):
    c = pl.program_id(1)

    @pl.when(c == 0)
    def _():
        state_ref[...] = jnp.zeros_like(state_ref)

    nt = (((1,), (1,)), ((), ()))
    tn = (((0,), (0,)), ((), ()))
    for bi in range(q_ref.shape[0]):
        q = q_ref[bi]
        k = k_ref[bi]
        v = v_ref[bi]
        attn = lax.dot_general(q, k, nt, preferred_element_type=jnp.float32) * mask_ref[0]
        intra = jnp.dot(attn.astype(v.dtype), v, preferred_element_type=jnp.float32)
        state = state_ref[bi]
        inter = jnp.dot(q, state.astype(q.dtype), preferred_element_type=jnp.float32) * qd_ref[0]
        o = intra + inter
        kd = (k.astype(jnp.float32) * kd_ref[0]).astype(k.dtype)
        state_ref[bi] = state * cd_ref[0] + lax.dot_general(kd, v, tn, preferred_element_type=jnp.float32)
        mu = jnp.mean(o, axis=-1, keepdims=True)
        oc = o - mu
        var = jnp.mean(oc * oc, axis=-1, keepdims=True)
        y = oc * lax.rsqrt(var + EPS)
        o_ref[bi] = (_silu(gr_ref[bi].astype(jnp.float32)) * y).astype(o_ref.dtype)


def _retention(proj3, q_off, k_off, v_off, g_off):
    b, s, _ = proj3.shape
    dk, dv, ch = RET_QK_DIM, RET_V_DIM, RET_CHUNK
    assert q_off % dk == 0 and k_off % dk == 0 and v_off % dv == 0 and g_off % dv == 0
    qb, kb, vb, gb = q_off // dk, k_off // dk, v_off // dv, g_off // dv
    n = s // ch
    log_g = jnp.log1p(-jnp.exp2(-5.0 - jnp.arange(RET_HEADS, dtype=jnp.float32)))
    pos = jnp.arange(ch, dtype=jnp.float32)
    rel = pos[:, None] - pos[None, :]
    scale = dk ** -0.5
    mask = jnp.where(rel[None] >= 0, jnp.exp(rel[None] * log_g[:, None, None]), 0.0) * scale
    q_decay = jnp.exp((pos[None, :, None] + 1.0) * log_g[:, None, None])
    k_decay = jnp.exp((ch - 1.0 - pos[None, :, None]) * log_g[:, None, None]) * scale
    c_decay = jnp.exp(ch * log_g)[:, None, None]
    return pl.pallas_call(
        _ret_kernel,
        grid=(RET_HEADS, n),
        in_specs=[pl.BlockSpec((b, ch, dk), lambda h, c: (0, c, qb + h)),
                  pl.BlockSpec((b, ch, dk), lambda h, c: (0, c, kb + h)),
                  pl.BlockSpec((b, ch, dv), lambda h, c: (0, c, vb + h)),
                  pl.BlockSpec((b, ch, dv), lambda h, c: (0, c, gb + h)),
                  pl.BlockSpec((1, ch, ch), lambda h, c: (h, 0, 0)),
                  pl.BlockSpec((1, ch, 1), lambda h, c: (h, 0, 0)),
                  pl.BlockSpec((1, ch, 1), lambda h, c: (h, 0, 0)),
                  pl.BlockSpec((1, 1, 1), lambda h, c: (h, 0, 0))],
        out_specs=pl.BlockSpec((b, ch, dv), lambda h, c: (0, c, h)),
        out_shape=jax.ShapeDtypeStruct((b, s, RET_HEADS * dv), jnp.bfloat16),
        scratch_shapes=[pltpu.VMEM((b, dk, dv), jnp.float32)],
        compiler_params=_params(("parallel", "arbitrary")),
        name="retention",
    )(proj3, proj3, proj3, proj3, mask, q_decay, k_decay, c_decay)


def _mix_kernel(a_ref, r_ref, wa_ref, wr_ref, ga_ref, gb_ref, o_ref):
    ya = jnp.dot(a_ref[...], wa_ref[...], preferred_element_type=jnp.float32)
    yr = jnp.dot(r_ref[...], wr_ref[...], preferred_element_type=jnp.float32)
    ga = _sigmoid(ga_ref[...].astype(jnp.float32))
    gb = _sigmoid(gb_ref[...].astype(jnp.float32))
    o_ref[...] = (ga * ya + gb * yr).astype(o_ref.dtype)


def _mix(attn2, ret2, wa, wr, proj2, ga_off, gb_off):
    m, ka = attn2.shape
    kr = ret2.shape[1]
    d = wa.shape[1]
    tm = _div_block(m, 512, V7X_SUBLANES)
    tn = _div_block(d, 512, V7X_LANES)
    assert ga_off % tn == 0 and gb_off % tn == 0
    gab, gbb = ga_off // tn, gb_off // tn
    return pl.pallas_call(
        _mix_kernel,
        grid=(m // tm, d // tn),
        in_specs=[pl.BlockSpec((tm, ka), lambda i, j: (i, 0)),
                  pl.BlockSpec((tm, kr), lambda i, j: (i, 0)),
                  pl.BlockSpec((ka, tn), lambda i, j: (0, j)),
                  pl.BlockSpec((kr, tn), lambda i, j: (0, j)),
                  pl.BlockSpec((tm, tn), lambda i, j: (i, gab + j)),
                  pl.BlockSpec((tm, tn), lambda i, j: (i, gbb + j))],
        out_specs=pl.BlockSpec((tm, tn), lambda i, j: (i, j)),
        out_shape=jax.ShapeDtypeStruct((m, d), jnp.bfloat16),
        compiler_params=_params(("parallel", "parallel")),
        name="mix",
    )(attn2, ret2, wa, wr, proj2, proj2)


def _resid_kernel(a_ref, w_ref, x_ref, gt_ref, o_ref):
    y = jnp.dot(a_ref[...], w_ref[...], preferred_element_type=jnp.float32)
    o_ref[...] = x_ref[...] + gt_ref[0] * y


def _out_resid(mix2, w, x2, mod3, gate_idx, seq):
    m, k = mix2.shape
    d = w.shape[1]
    tm = _div_block(seq, 1024, V7X_SUBLANES)
    tn = _div_block(d, 512, V7X_LANES)
    per_b = seq // tm
    return pl.pallas_call(
        _resid_kernel,
        grid=(m // tm, d // tn),
        in_specs=[pl.BlockSpec((tm, k), lambda i, j: (i, 0)),
                  pl.BlockSpec((k, tn), lambda i, j: (0, j)),
                  pl.BlockSpec((tm, tn), lambda i, j: (i, j)),
                  pl.BlockSpec((1, 1, tn), lambda i, j: ((i // per_b) * 6 + gate_idx, 0, j))],
        out_specs=pl.BlockSpec((tm, tn), lambda i, j: (i, j)),
        out_shape=jax.ShapeDtypeStruct((m, d), jnp.float32),
        compiler_params=_params(("parallel", "parallel")),
        name="out_resid",
    )(mix2, w, x2, mod3)


def _router_kernel(x_ref, g_ref, sh_ref, sc_ref, wr_ref, br_ref,
                   h_ref, hp_ref, idx_ref, pos_ref, w_ref, cnt_ref, carry_ref):
    i = pl.program_id(0)
    e = N_EXPERTS
    per_g = e // N_GROUPS
    tb = x_ref.shape[0]

    @pl.when(i == 0)
    def _():
        carry_ref[...] = jnp.zeros_like(carry_ref)

    x = x_ref[...]
    ms = jnp.mean(x * x, axis=-1, keepdims=True)
    h = x * lax.rsqrt(ms + EPS) * g_ref[...]
    h = h * (1.0 + sc_ref[0]) + sh_ref[0]
    hb = h.astype(jnp.bfloat16)
    h_ref[...] = hb
    half = h.shape[1] // 2
    _slab_store(hp_ref, _pack_pair(h[:, :half], h[:, half:]), tb, half // V7X_LANES)

    nt = (((1,), (1,)), ((), ()))
    logits = lax.dot_general(wr_ref[...], hb, nt, preferred_element_type=jnp.float32)
    scores = _sigmoid(logits)
    choice = scores + br_ref[...]

    c3 = choice.reshape(N_GROUPS, per_g, tb)
    j_iota = lax.broadcasted_iota(jnp.int32, c3.shape, 1).astype(jnp.float32)
    m1 = c3.max(axis=1, keepdims=True)
    first = jnp.min(jnp.where(c3 == m1, j_iota, float(per_g)), axis=1, keepdims=True)
    m2 = jnp.where(j_iota == first, -jnp.inf, c3).max(axis=1, keepdims=True)
    gs = (m1 + m2).reshape(N_GROUPS, tb)

    g_iota = lax.broadcasted_iota(jnp.int32, gs.shape, 0)
    grank = jnp.zeros(gs.shape, jnp.int32)
    for gp in range(N_GROUPS):
        row = gs[gp:gp + 1, :]
        ahead = (row > gs) | ((row == gs) & (gp < g_iota))
        grank = grank + ahead.astype(jnp.int32)
    gmask = grank < TOPK_GROUPS
    emask = jnp.broadcast_to(gmask.reshape(N_GROUPS, 1, tb), c3.shape).reshape(e, tb)
    masked = jnp.where(emask, choice, -jnp.inf)

    e_iota = lax.broadcasted_iota(jnp.int32, masked.shape, 0)
    erank = jnp.zeros(masked.shape, jnp.int32)
    for ep in range(e):
        row = masked[ep:ep + 1, :]
        ahead = (row > masked) | ((row == masked) & (ep < e_iota))
        erank = erank + ahead.astype(jnp.int32)
    sel = (erank < TOP_K) & emask
    self32 = sel.astype(jnp.float32)

    wsel = scores * self32
    wn = wsel / jnp.sum(wsel, axis=0, keepdims=True) * ROUTED_SCALE

    selb = self32.astype(jnp.bfloat16)
    t_r = lax.broadcasted_iota(jnp.int32, (tb, tb), 0)
    t_c = lax.broadcasted_iota(jnp.int32, (tb, tb), 1)
    upper = (t_r <= t_c).astype(jnp.bfloat16)
    incl = jnp.dot(selb, upper, preferred_element_type=jnp.float32)
    carry = carry_ref[...]
    rank_in_e = carry + incl - 1.0
    carry_new = carry + jnp.sum(self32, axis=1, keepdims=True)
    carry_ref[...] = carry_new
    cnt_ref[...] = jnp.broadcast_to(carry_new, cnt_ref.shape).astype(jnp.int32)

    e_r = lax.broadcasted_iota(jnp.int32, (e, e), 0)
    e_c = lax.broadcasted_iota(jnp.int32, (e, e), 1)
    lower = (e_c < e_r).astype(jnp.bfloat16)
    before = jnp.dot(lower, selb, preferred_element_type=jnp.float32)
    e_f = e_iota.astype(jnp.float32)
    idx_rows, pos_rows, w_rows = [], [], []
    for k in range(TOP_K):
        hit = jnp.where(sel & (before == float(k)), 1.0, 0.0)
        idx_rows.append(jnp.sum(hit * e_f, axis=0, keepdims=True))
        pos_rows.append(jnp.sum(hit * rank_in_e, axis=0, keepdims=True))
        w_rows.append(jnp.sum(hit * wn, axis=0, keepdims=True))
    idx_ref[0] = jnp.concatenate(idx_rows, axis=0).astype(jnp.int32)
    pos_ref[0] = jnp.concatenate(pos_rows, axis=0).astype(jnp.int32)
    w_ref[0] = jnp.concatenate(w_rows, axis=0)


def _router(x1, g, mod3, shift_idx, scale_idx, w_router_t, b_router, seq):
    t, d = x1.shape
    e = N_EXPERTS
    tb = MOE_TILE
    assert seq % tb == 0 and d % (2 * V7X_LANES * V7X_SUBLANES) == 0
    slab = d // 2 // V7X_LANES
    per_b = seq // tb
    nt = t // tb
    return pl.pallas_call(
        _router_kernel,
        grid=(nt,),
        in_specs=[pl.BlockSpec((tb, d), lambda i: (i, 0)),
                  pl.BlockSpec((1, d), lambda i: (0, 0)),
                  pl.BlockSpec((1, 1, d), lambda i: ((i // per_b) * 6 + shift_idx, 0, 0)),
                  pl.BlockSpec((1, 1, d), lambda i: ((i // per_b) * 6 + scale_idx, 0, 0)),
                  pl.BlockSpec((e, d), lambda i: (0, 0)),
                  pl.BlockSpec((e, 1), lambda i: (0, 0))],
        out_specs=[pl.BlockSpec((tb, d), lambda i: (i, 0)),
                   pl.BlockSpec((tb * slab, V7X_LANES), lambda i: (i, 0)),
                   pl.BlockSpec((1, TOP_K, tb), lambda i: (i, 0, 0)),
                   pl.BlockSpec((1, TOP_K, tb), lambda i: (i, 0, 0)),
                   pl.BlockSpec((1, TOP_K, tb), lambda i: (i, 0, 0)),
                   pl.BlockSpec((e, V7X_LANES), lambda i: (0, 0))],
        out_shape=[jax.ShapeDtypeStruct((t, d), jnp.bfloat16),
                   jax.ShapeDtypeStruct((t * slab, V7X_LANES), jnp.int32),
                   jax.ShapeDtypeStruct((nt, TOP_K, tb), jnp.int32),
                   jax.ShapeDtypeStruct((nt, TOP_K, tb), jnp.int32),
                   jax.ShapeDtypeStruct((nt, TOP_K, tb), jnp.float32),
                   jax.ShapeDtypeStruct((e, V7X_LANES), jnp.int32)],
        scratch_shapes=[pltpu.VMEM((e, 1), jnp.float32)],
        compiler_params=_params(("arbitrary",)),
        name="router",
    )(x1, g.reshape(1, d), mod3, mod3, w_router_t, b_router.reshape(e, 1))


def _slots_kernel(pstart_ref, idx_ref, pos_ref, o_ref):
    idx = idx_ref[...]
    base = jnp.zeros(idx.shape, jnp.int32)
    for ex in range(N_EXPERTS):
        base = jnp.where(idx == ex, pstart_ref[ex], base)
    o_ref[...] = base + pos_ref[...]


def _slots(pad_start, idx3, pos3):
    nt = idx3.shape[0]
    blk = (1,) + idx3.shape[1:]
    spec = pl.BlockSpec(blk, lambda i, ps: (i, 0, 0))
    return pl.pallas_call(
        _slots_kernel,
        grid_spec=pltpu.PrefetchScalarGridSpec(num_scalar_prefetch=1, grid=(nt,),
                                               in_specs=[spec, spec], out_specs=spec),
        out_shape=jax.ShapeDtypeStruct(idx3.shape, jnp.int32),
        compiler_params=_params(("parallel",)),
        name="slots",
    )(pad_start, idx3, pos3)


def _dispatch_kernel(pstart_ref, pend_ref, hp_ref, slot_hbm, xs_hbm,
                     slot_s, zero_v, sem_i, sem_z, sem_r, *, slab):
    i = pl.program_id(0)
    tb = hp_ref.shape[0] // slab

    def slot_copy():
        return pltpu.make_async_copy(slot_hbm.at[i], slot_s, sem_i)

    slot_copy().start()

    def zero_copy(ex):
        first = pl.multiple_of((pend_ref[ex] - tb) * slab, tb * slab)
        return pltpu.make_async_copy(zero_v, xs_hbm.at[pl.ds(first, tb * slab)], sem_z)

    @pl.when(i == 0)
    def _():
        zero_v[...] = jnp.zeros_like(zero_v)

        def start(ex, carry):
            @pl.when(pend_ref[ex] > pstart_ref[ex])
            def _():
                zero_copy(ex).start()
            return carry

        def wait(ex, carry):
            @pl.when(pend_ref[ex] > pstart_ref[ex])
            def _():
                zero_copy(ex).wait()
            return carry

        lax.fori_loop(0, N_EXPERTS, start, 0)
        lax.fori_loop(0, N_EXPERTS, wait, 0)

    slot_copy().wait()

    def row_copy(t, k):
        return pltpu.make_async_copy(hp_ref.at[_slab_rows(t, slab)],
                                     xs_hbm.at[_slab_rows(slot_s[k, t], slab)], sem_r)

    def start_rows(t, carry):
        for k in range(TOP_K):
            row_copy(t, k).start(priority=k % 2)
        return carry

    lax.fori_loop(0, tb, start_rows, 0, unroll=2)
    for k in range(TOP_K):
        pltpu.make_async_copy(hp_ref, xs_hbm.at[pl.ds(0, tb * slab)], sem_r).wait()


def _dispatch(pad_start, pad_end, h2p, slot3, n_rows, slab):
    tb = MOE_TILE
    t = h2p.shape[0] // slab
    return pl.pallas_call(
        functools.partial(_dispatch_kernel, slab=slab),
        grid_spec=pltpu.PrefetchScalarGridSpec(
            num_scalar_prefetch=2,
            grid=(t // tb,),
            in_specs=[pl.BlockSpec((tb * slab, V7X_LANES), lambda i, ps, pe: (i, 0)),
                      pl.BlockSpec(memory_space=pl.ANY)],
            out_specs=pl.BlockSpec(memory_space=pl.ANY),
            scratch_shapes=[pltpu.SMEM((TOP_K, tb), jnp.int32),
                            pltpu.VMEM((tb * slab, V7X_LANES), jnp.int32),
                            pltpu.SemaphoreType.DMA,
                            pltpu.SemaphoreType.DMA,
                            pltpu.SemaphoreType.DMA],
        ),
        out_shape=jax.ShapeDtypeStruct((n_rows * slab, V7X_LANES), jnp.int32),
        compiler_params=_params(("arbitrary",)),
        name="dispatch",
    )(pad_start, pad_end, h2p, slot3)


def _expert_kernel(blk_e_ref, nused_ref, x_ref, wg_ref, wu_ref, wd_ref, o_ref, *, slab):
    i = pl.program_id(0)

    @pl.when(i < nused_ref[0])
    def _():
        tb = x_ref.shape[0] // slab
        half = slab * V7X_LANES
        lo, hi = _unpack_pair(_slab_load(x_ref, tb, slab))
        lo = lo.astype(jnp.bfloat16)
        hi = hi.astype(jnp.bfloat16)
        g = (jnp.dot(lo, wg_ref[0, :half, :], preferred_element_type=jnp.float32)
             + jnp.dot(hi, wg_ref[0, half:, :], preferred_element_type=jnp.float32))
        u = (jnp.dot(lo, wu_ref[0, :half, :], preferred_element_type=jnp.float32)
             + jnp.dot(hi, wu_ref[0, half:, :], preferred_element_type=jnp.float32))
        a = (_silu(g) * u).astype(jnp.bfloat16)
        y = jnp.dot(a, wd_ref[0], preferred_element_type=jnp.float32)
        _slab_store(o_ref, _pack_pair(y[:, :half], y[:, half:]), tb, slab)


def _experts(blk_e, n_used, xs, wg, wu, wd, slab):
    tb = MOE_TILE
    p = xs.shape[0] // slab
    d = 2 * slab * V7X_LANES
    f = wg.shape[2]
    nblk = p // tb

    def row_map(i, be, nu):
        return (jnp.minimum(i, nu[0] - 1), 0)

    def w_map(i, be, nu):
        return (be[jnp.minimum(i, nu[0] - 1)], 0, 0)

    return pl.pallas_call(
        functools.partial(_expert_kernel, slab=slab),
        grid_spec=pltpu.PrefetchScalarGridSpec(
            num_scalar_prefetch=2,
            grid=(nblk,),
            in_specs=[pl.BlockSpec((tb * slab, V7X_LANES), row_map),
                      pl.BlockSpec((1, d, f), w_map),
                      pl.BlockSpec((1, d, f), w_map),
                      pl.BlockSpec((1, f, d), w_map)],
            out_specs=pl.BlockSpec((tb * slab, V7X_LANES), row_map),
        ),
        out_shape=jax.ShapeDtypeStruct(xs.shape, jnp.int32),
        compiler_params=_params(("arbitrary",)),
        name="experts",
    )(blk_e, n_used, xs, wg, wu, wd)


def _combine_kernel(x_ref, h_ref, wt_ref, gt_ref, gf_ref, wsg_ref, wsu_ref, wsd_ref,
                    slot_hbm, ys_hbm, o_ref, slot_s, rows_v, sem_i, sem_r):
    i = pl.program_id(0)
    n = pl.num_programs(0)
    tb = x_ref.shape[0]
    half = x_ref.shape[1] // 2
    slab = half // V7X_LANES
    per_tile = MOE_TILE // tb
    cur = i % 2

    def slot_copy(step):
        win = pl.ds((step % per_tile) * tb, tb)
        return pltpu.make_async_copy(slot_hbm.at[step // per_tile, :, win], slot_s.at[step % 2],
                                     sem_i.at[step % 2])

    def start_rows(step):
        buf = step % 2

        def body(t, carry):
            for k in range(TOP_K):
                pltpu.make_async_copy(ys_hbm.at[_slab_rows(slot_s[buf, k, t], slab)],
                                      rows_v.at[buf, k, _slab_rows(t, slab)],
                                      sem_r.at[buf]).start(priority=k % 2)
            return carry

        lax.fori_loop(0, tb, body, 0, unroll=2)

    def wait_rows(buf):
        for k in range(TOP_K):
            pltpu.make_async_copy(ys_hbm.at[pl.ds(0, tb * slab)], rows_v.at[buf, k], sem_r.at[buf]).wait()

    @pl.when(i == 0)
    def _():
        slot_copy(i).start()
        slot_copy(i).wait()
        start_rows(i)

    @pl.when(i + 1 < n)
    def _():
        slot_copy(i + 1).start()

    h = h_ref[...]
    g = jnp.dot(h, wsg_ref[...], preferred_element_type=jnp.float32)
    u = jnp.dot(h, wsu_ref[...], preferred_element_type=jnp.float32)
    a = (_silu(g) * u).astype(jnp.bfloat16)
    y = jnp.dot(a, wsd_ref[...], preferred_element_type=jnp.float32)
    y_lo = y[:, :half]
    y_hi = y[:, half:]

    @pl.when(i + 1 < n)
    def _():
        slot_copy(i + 1).wait()
        start_rows(i + 1)

    wait_rows(cur)

    wt = wt_ref[...]
    for k in range(TOP_K):
        lo, hi = _unpack_pair(_slab_load(rows_v.at[cur, k], tb, slab))
        wk = wt[:, k:k + 1]
        y_lo = y_lo + wk * lo
        y_hi = y_hi + wk * hi
    yy = jnp.concatenate([y_lo, y_hi], axis=1)
    xo = x_ref[...] + gt_ref[0] * yy
    ms = jnp.mean(xo * xo, axis=-1, keepdims=True)
    o_ref[...] = xo * lax.rsqrt(ms + EPS) * gf_ref[...]


def _combine(x1, h2, w_t, mod3, gate_idx, g_final, wsg, wsu, wsd, slot3, ys, seq):
    t, d = x1.shape
    tb = COMBINE_TILE
    assert MOE_TILE % tb == 0 and seq % tb == 0
    per_b = seq // tb
    half = d // 2
    f = wsg.shape[1]
    once = pl.Buffered(1)
    return pl.pallas_call(
        _combine_kernel,
        grid=(t // tb,),
        in_specs=[pl.BlockSpec((tb, d), lambda i: (i, 0)),
                  pl.BlockSpec((tb, d), lambda i: (i, 0)),
                  pl.BlockSpec((tb, TOP_K), lambda i: (i, 0)),
                  pl.BlockSpec((1, 1, d), lambda i: ((i // per_b) * 6 + gate_idx, 0, 0)),
                  pl.BlockSpec((1, d), lambda i: (0, 0)),
                  pl.BlockSpec((d, f), lambda i: (0, 0), pipeline_mode=once),
                  pl.BlockSpec((d, f), lambda i: (0, 0), pipeline_mode=once),
                  pl.BlockSpec((f, d), lambda i: (0, 0), pipeline_mode=once),
                  pl.BlockSpec(memory_space=pl.ANY),
                  pl.BlockSpec(memory_space=pl.ANY)],
        out_specs=pl.BlockSpec((tb, d), lambda i: (i, 0)),
        scratch_shapes=[pltpu.SMEM((2, TOP_K, tb), jnp.int32),
                        pltpu.VMEM((2, TOP_K, tb * half // V7X_LANES, V7X_LANES), jnp.int32),
                        pltpu.SemaphoreType.DMA((2,)),
                        pltpu.SemaphoreType.DMA((2,))],
        out_shape=jax.ShapeDtypeStruct((t, d), jnp.float32),
        compiler_params=_params(("arbitrary",)),
        name="combine",
    )(x1, h2, w_t, mod3, g_final.reshape(1, d), wsg, wsu, wsd, slot3, ys)


def _in_layout(d):
    qw = ATT_HEADS * ATT_HEAD_DIM
    kvw = ATT_KV_HEADS * ATT_HEAD_DIM
    rqk = RET_HEADS * RET_QK_DIM
    rv = RET_HEADS * RET_V_DIM
    src_order = [("qa", qw), ("ka", kvw), ("va", kvw), ("qr", rqk), ("kr", rqk),
                 ("vr", rv), ("gr", rv), ("ga", d), ("gb", d)]
    src = {}
    off = 0
    for name, width in src_order:
        src[name] = (off, width)
        off += width
    dst_order = ["ga", "gb", "gr", "vr", "qr", "kr", "qa", "ka", "va"]
    dst = {}
    off = 0
    for name in dst_order:
        dst[name] = off
        off += src[name][1]
    return src, dst_order, dst


def kernel(x, c, w_ada, b_ada, g_norm_mix, w_in, attn_sinks, w_attn_out, w_ret_out, w_o, g_norm_ffn,
           w_router, b_router, w_gate, w_up, w_down, w_sh_gate, w_sh_up, w_sh_down, g_norm_final):
    b, s, d = x.shape
    t = b * s
    depth = w_ada.shape[0]
    bf = jnp.bfloat16
    src, dst_order, dst = _in_layout(d)

    c_pad = jnp.zeros((V7X_SUBLANES, d), jnp.float32).at[:b].set(c)
    x2 = x.reshape(t, d)
    for l in range(depth):
        mod = _ada(c_pad, w_ada[l], b_ada[l])
        mod3 = mod[:b].reshape(b * 6, 1, d)

        h = _norm_mod(x2.reshape(b, s, d), g_norm_mix[l], mod3, 0, 1)
        w_in_p = jnp.concatenate(
            [w_in[l][:, src[n][0]:src[n][0] + src[n][1]] for n in dst_order], axis=1).astype(bf)
        proj, (wg_b, wu_b, wd_b) = _in_proj(h.reshape(t, d), w_in_p, (w_gate[l], w_up[l], w_down[l]), 1024, 768)
        proj3 = proj.reshape(b, s, proj.shape[1])
        attn = _attention(proj3, attn_sinks[l], dst["qa"], dst["ka"], dst["va"])
        ret = _retention(proj3, dst["qr"], dst["kr"], dst["vr"], dst["gr"])
        mix = _mix(attn.reshape(t, -1), ret.reshape(t, -1), w_attn_out[l].astype(bf),
                   w_ret_out[l].astype(bf), proj, dst["ga"], dst["gb"])
        x1 = _out_resid(mix, w_o[l].astype(bf), x2, mod3, 2, s)

        h2, h2p, idx3, pos3, w3, cnt = _router(x1, g_norm_ffn[l], mod3, 3, 4,
                                               w_router[l].T.astype(bf), b_router[l], s)
        counts = cnt[:, 0]
        tile = MOE_TILE
        padded = (counts + tile - 1) // tile * tile
        pad_end = jnp.cumsum(padded).astype(jnp.int32)
        pad_start = (pad_end - padded).astype(jnp.int32)
        n_blocks = (t * TOP_K) // tile + N_EXPERTS
        n_used = (pad_end[-1] // tile).reshape(1).astype(jnp.int32)
        blk_first = jnp.arange(n_blocks, dtype=jnp.int32) * tile
        blk_e = jnp.minimum(jnp.sum((pad_end[None, :] <= blk_first[:, None]).astype(jnp.int32), axis=1),
                            N_EXPERTS - 1)
        slab = d // 2 // V7X_LANES
        slot3 = _slots(pad_start, idx3, pos3)
        xs = _dispatch(pad_start, pad_end, h2p, slot3, n_blocks * tile, slab)
        ys = _experts(blk_e, n_used, xs, wg_b, wu_b, wd_b, slab)
        w_t = w3.transpose(0, 2, 1).reshape(t, TOP_K)
        is_last = l == depth - 1
        assert is_last, "the final norm is fused into the last layer's combine"
        x2 = _combine(x1, h2, w_t, mod3, 5, g_norm_final, w_sh_gate[l].astype(bf),
                      w_sh_up[l].astype(bf), w_sh_down[l].astype(bf), slot3, ys, s)
    return x2.reshape(b, s, d)
```

```python
import functools
import math

import jax
import jax.numpy as jnp
from jax import lax
from jax.experimental import pallas as pl
from jax.experimental.pallas import tpu as pltpu

ATT_HEADS = 32
ATT_KV_HEADS = 4
ATT_HEAD_DIM = 64
WINDOW = 128
ATT_BLOCK = 128
RET_HEADS = 8
RET_QK_DIM = 256
RET_V_DIM = 512
RET_CHUNK = 128
N_EXPERTS = 64
N_GROUPS = 8
TOPK_GROUPS = 4
TOP_K = 8
ROUTED_SCALE = 2.5
EPS = 1e-6

V7X_LANES = 128
V7X_SUBLANES = 8
V7X_VMEM_LIMIT_BYTES = 60000 * 1024

MOE_TILE = 256
COMBINE_TILE = 128
NEG_BIG = -1e30


def _div_block(n, target, align):
    best = None
    b = align
    while b <= min(n, target):
        if n % b == 0:
            best = b
        b += align
    assert best is not None, (n, target, align)
    return best


def _params(semantics):
    return pltpu.CompilerParams(dimension_semantics=semantics,
                                vmem_limit_bytes=V7X_VMEM_LIMIT_BYTES)


def _sigmoid(v):
    return 1.0 / (1.0 + jnp.exp(-v))


def _silu(v):
    return v * _sigmoid(v)


def _pack_pair(lo, hi):
    return pltpu.pack_elementwise([lo, hi], packed_dtype=jnp.bfloat16)


def _unpack_pair(p):
    lo = pltpu.unpack_elementwise(p, index=0, packed_dtype=jnp.bfloat16, unpacked_dtype=jnp.float32)
    hi = pltpu.unpack_elementwise(p, index=1, packed_dtype=jnp.bfloat16, unpacked_dtype=jnp.float32)
    return lo, hi


def _slab_load(ref, n_rows, slab):
    return jnp.concatenate([ref[pl.ds(s, n_rows, stride=slab), :] for s in range(slab)], axis=1)


def _slab_store(ref, val, n_rows, slab):
    for s in range(slab):
        ref[pl.ds(s, n_rows, stride=slab), :] = val[:, s * V7X_LANES:(s + 1) * V7X_LANES]


def _slab_rows(r, slab):
    return pl.ds(pl.multiple_of(r * slab, slab), slab)


def _ada_kernel(c_ref, w_ref, b_ref, o_ref):
    cs = _silu(c_ref[...]).astype(jnp.bfloat16)
    o_ref[...] = jnp.dot(cs, w_ref[...].astype(jnp.bfloat16),
                         preferred_element_type=jnp.float32) + b_ref[...]


def _ada(c_pad, w, b):
    m, d = c_pad.shape
    n = w.shape[1]
    tn = _div_block(n, 512, V7X_LANES)
    return pl.pallas_call(
        _ada_kernel,
        grid=(n // tn,),
        in_specs=[pl.BlockSpec((m, d), lambda j: (0, 0)),
                  pl.BlockSpec((d, tn), lambda j: (0, j)),
                  pl.BlockSpec((1, tn), lambda j: (0, j))],
        out_specs=pl.BlockSpec((m, tn), lambda j: (0, j)),
        out_shape=jax.ShapeDtypeStruct((m, n), jnp.float32),
        compiler_params=_params(("parallel",)),
        name="ada",
    )(c_pad, w, b.reshape(1, n))


def _norm_mod_kernel(x_ref, g_ref, sh_ref, sc_ref, o_ref):
    x = x_ref[0]
    ms = jnp.mean(x * x, axis=-1, keepdims=True)
    y = x * lax.rsqrt(ms + EPS) * g_ref[...]
    o_ref[0] = (y * (1.0 + sc_ref[0]) + sh_ref[0]).astype(o_ref.dtype)


def _norm_mod(x3, g, mod3, shift_idx, scale_idx):
    b, s, d = x3.shape
    ts = _div_block(s, 256, V7X_SUBLANES)
    return pl.pallas_call(
        _norm_mod_kernel,
        grid=(b, s // ts),
        in_specs=[pl.BlockSpec((1, ts, d), lambda bi, i: (bi, i, 0)),
                  pl.BlockSpec((1, d), lambda bi, i: (0, 0)),
                  pl.BlockSpec((1, 1, d), lambda bi, i: (bi * 6 + shift_idx, 0, 0)),
                  pl.BlockSpec((1, 1, d), lambda bi, i: (bi * 6 + scale_idx, 0, 0))],
        out_specs=pl.BlockSpec((1, ts, d), lambda bi, i: (bi, i, 0)),
        out_shape=jax.ShapeDtypeStruct((b, s, d), jnp.bfloat16),
        compiler_params=_params(("parallel", "parallel")),
        name="norm_mod",
    )(x3, g.reshape(1, d), mod3, mod3)


def _in_proj_kernel(a_ref, b_ref, *rest, n_side, n_cast, n_j):
    src = rest[:n_side]
    o_ref = rest[n_side]
    dst = rest[n_side + 1:]
    o_ref[...] = jnp.dot(a_ref[...], b_ref[...],
                         preferred_element_type=jnp.float32).astype(o_ref.dtype)
    step = pl.program_id(0) * n_j + pl.program_id(1)

    @pl.when(step < n_cast)
    def _():
        for s_ref, d_ref in zip(src, dst):
            d_ref[...] = s_ref[...].astype(d_ref.dtype)


def _in_proj(a, b, side, tm_target, tn_target):
    m, k = a.shape
    n = b.shape[1]
    tm = _div_block(m, tm_target, V7X_SUBLANES)
    tn = _div_block(n, tn_target, V7X_LANES)
    n_i, n_j = m // tm, n // tn
    n_cast = 1 << ((n_i * n_j).bit_length() - 1)
    chunked = []
    for w in side:
        rows = w.size // w.shape[-1]
        assert rows % (n_cast * V7X_SUBLANES * 2) == 0, (w.shape, n_cast)
        chunked.append(w.reshape(n_cast, rows // n_cast, w.shape[-1]))

    def side_map(i, j):
        return (jnp.minimum(i * n_j + j, n_cast - 1), 0, 0)

    side_specs = [pl.BlockSpec((1,) + c.shape[1:], side_map) for c in chunked]
    outs = pl.pallas_call(
        functools.partial(_in_proj_kernel, n_side=len(side), n_cast=n_cast, n_j=n_j),
        grid=(n_i, n_j),
        in_specs=[pl.BlockSpec((tm, k), lambda i, j: (i, 0)),
                  pl.BlockSpec((k, tn), lambda i, j: (0, j))] + side_specs,
        out_specs=[pl.BlockSpec((tm, tn), lambda i, j: (i, j))] + side_specs,
        out_shape=[jax.ShapeDtypeStruct((m, n), jnp.bfloat16)]
        + [jax.ShapeDtypeStruct(c.shape, jnp.bfloat16) for c in chunked],
        compiler_params=_params(("arbitrary", "arbitrary")),
        name="in_proj",
    )(a, b, *chunked)
    return outs[0], [o.reshape(w.shape) for o, w in zip(outs[1:], side)]


def _attn_kernel(sink_ref, q_ref, kc_ref, kp_ref, vc_ref, vp_ref, bias_ref, o_ref):
    i = pl.program_id(1)
    blk = ATT_BLOCK
    hd = ATT_HEAD_DIM
    group = ATT_HEADS // ATT_KV_HEADS
    pairs = group // 2
    nt = (((1,), (1,)), ((), ()))
    zpad = jnp.zeros((2 * blk, hd), jnp.bfloat16)

    def scores(h, par):
        sl = slice(h * hd, (h + 1) * hd)
        k2 = jnp.concatenate([kp_ref[0, :, sl], kc_ref[0, :, sl]], axis=0) * (hd ** -0.5)
        qp = jnp.concatenate([q_ref[0, :, (h * pairs + p) * 2 * hd:(h * pairs + p + 1) * 2 * hd]
                              for p in range(pairs)], axis=0)
        kz = jnp.concatenate([k2, zpad] if par == 0 else [zpad, k2], axis=1)
        return lax.dot_general(qp, kz, nt, preferred_element_type=jnp.float32)

    def softmax(s, h, par):
        s = s + bias_ref[h * 2 + par]
        s = jnp.concatenate([jnp.where(i == 0, NEG_BIG, s[:, :blk]), s[:, blk:]], axis=1)
        sink = jnp.concatenate([jnp.full((blk, 1), sink_ref[h * group + 2 * p + par], jnp.float32)
                                for p in range(pairs)], axis=0)
        m = jnp.maximum(s.max(-1, keepdims=True), sink)
        pr = jnp.exp(s - m)
        denom = pr.sum(-1, keepdims=True) + jnp.exp(sink - m)
        return pr.astype(jnp.bfloat16), 1.0 / denom

    def values(pr, inv, h, par):
        sl = slice(h * hd, (h + 1) * hd)
        v2 = jnp.concatenate([vp_ref[0, :, sl], vc_ref[0, :, sl]], axis=0)
        vz = jnp.concatenate([v2, zpad] if par == 0 else [zpad, v2], axis=1)
        return jnp.dot(pr, vz, preferred_element_type=jnp.float32) * inv

    items = [(h, par) for h in range(ATT_KV_HEADS) for par in range(2)]
    s_of, p_of, acc = {}, {}, {}
    for n in range(len(items) + 2):
        if n < len(items):
            s_of[n] = scores(*items[n])
        if 1 <= n <= len(items):
            p_of[n - 1] = softmax(s_of.pop(n - 1), *items[n - 1])
        if n >= 2:
            h, par = items[n - 2]
            o = values(*p_of.pop(n - 2), h, par)
            acc[h] = o if par == 0 else acc[h] + o
            if par == 1:
                out = acc.pop(h)
                for p in range(pairs):
                    o_ref[0, :, (h * pairs + p) * 2 * hd:(h * pairs + p + 1) * 2 * hd] = (
                        out[p * blk:(p + 1) * blk].astype(o_ref.dtype))


def _attn_bias():
    blk = ATT_BLOCK
    group = ATT_HEADS // ATT_KV_HEADS
    pairs = group // 2
    qi = jnp.arange(blk)[:, None]
    kj = jnp.arange(2 * blk)[None, :]
    dist = qi + blk - kj
    valid = (dist >= 0) & (dist < WINDOW)
    slopes = jnp.exp2(-8.0 * jnp.arange(1, ATT_HEADS + 1, dtype=jnp.float32) / ATT_HEADS)
    slopes = slopes.reshape(ATT_KV_HEADS, pairs, 2)
    bias = jnp.where(valid, -slopes[..., None, None] * dist.astype(jnp.float32), NEG_BIG)
    return bias.transpose(0, 2, 1, 3, 4).reshape(ATT_KV_HEADS * 2, pairs * blk, 2 * blk)


def _attention(proj3, sinks, q_off, k_off, v_off):
    b, s, _ = proj3.shape
    qw = ATT_HEADS * ATT_HEAD_DIM
    kvw = ATT_KV_HEADS * ATT_HEAD_DIM
    nb = s // ATT_BLOCK
    group = ATT_HEADS // ATT_KV_HEADS
    assert q_off % qw == 0 and k_off % kvw == 0 and v_off % kvw == 0
    assert group % 2 == 0 and 2 * ATT_HEAD_DIM == V7X_LANES and WINDOW == ATT_BLOCK
    assert 4 ** round(math.log(ATT_HEAD_DIM, 4)) == ATT_HEAD_DIM, "score scale must be a power of two"
    qb, kb, vb = q_off // qw, k_off // kvw, v_off // kvw
    bias = _attn_bias()
    return pl.pallas_call(
        _attn_kernel,
        grid_spec=pltpu.PrefetchScalarGridSpec(
            num_scalar_prefetch=1,
            grid=(b, nb),
            in_specs=[pl.BlockSpec((1, ATT_BLOCK, qw), lambda bi, i, sk: (bi, i, qb)),
                      pl.BlockSpec((1, ATT_BLOCK, kvw), lambda bi, i, sk: (bi, i, kb)),
                      pl.BlockSpec((1, ATT_BLOCK, kvw), lambda bi, i, sk: (bi, jnp.maximum(i - 1, 0), kb)),
                      pl.BlockSpec((1, ATT_BLOCK, kvw), lambda bi, i, sk: (bi, i, vb)),
                      pl.BlockSpec((1, ATT_BLOCK, kvw), lambda bi, i, sk: (bi, jnp.maximum(i - 1, 0), vb)),
                      pl.BlockSpec(bias.shape, lambda bi, i, sk: (0, 0, 0), pipeline_mode=pl.Buffered(1))],
            out_specs=pl.BlockSpec((1, ATT_BLOCK, qw), lambda bi, i, sk: (bi, i, 0)),
        ),
        out_shape=jax.ShapeDtypeStruct((b, s, qw), jnp.bfloat16),
        compiler_params=_params(("parallel", "parallel")),
        name="attention",
    )(sinks, proj3, proj3, proj3, proj3, proj3, bias)


def _ret_kernel(q_ref, k_ref, v_ref, gr_ref, mask_ref, qd_ref, kd_ref, cd_ref, o_ref, state_ref):
    c = pl.program_id(1)

    @pl.when(c == 0)
    def _():
        state_ref[...] = jnp.zeros_like(state_ref)

    nt = (((1,), (1,)), ((), ()))
    tn = (((0,), (0,)), ((), ()))
    def decayed(bi):
        q = q_ref[bi]
        k = k_ref[bi]
        attn = lax.dot_general(q, k, nt, preferred_element_type=jnp.float32) * mask_ref[0]
        kd = (k.astype(jnp.float32) * kd_ref[0]).astype(k.dtype)
        return attn.astype(jnp.bfloat16), kd

    def recur(bi, attn, kd):
        q = q_ref[bi]
        v = v_ref[bi]
        intra = jnp.dot(attn, v, preferred_element_type=jnp.float32)
        state = state_ref[bi]
        inter = jnp.dot(q, state.astype(q.dtype), preferred_element_type=jnp.float32) * qd_ref[0]
        state_ref[bi] = state * cd_ref[0] + lax.dot_general(kd, v, tn, preferred_element_type=jnp.float32)
        return intra + inter

    def finish(bi, o):
        mu = jnp.mean(o, axis=-1, keepdims=True)
        oc = o - mu
        var = jnp.mean(oc * oc, axis=-1, keepdims=True)
        y = oc * lax.rsqrt(var + EPS)
        o_ref[bi] = (_silu(gr_ref[bi].astype(jnp.float32)) * y).astype(o_ref.dtype)

    nb = q_ref.shape[0]
    a_of, o_of = {}, {}
    for n in range(nb + 2):
        if n < nb:
            a_of[n] = decayed(n)
        if 1 <= n <= nb:
            o_of[n - 1] = recur(n - 1, *a_of.pop(n - 1))
        if n >= 2:
            finish(n - 2, o_of.pop(n - 2))


def _retention(proj3, q_off, k_off, v_off, g_off):
    b, s, _ = proj3.shape
    dk, dv, ch = RET_QK_DIM, RET_V_DIM, RET_CHUNK
    assert q_off % dk == 0 and k_off % dk == 0 and v_off % dv == 0 and g_off % dv == 0
    qb, kb, vb, gb = q_off // dk, k_off // dk, v_off // dv, g_off // dv
    n = s // ch
    log_g = jnp.log1p(-jnp.exp2(-5.0 - jnp.arange(RET_HEADS, dtype=jnp.float32)))
    pos = jnp.arange(ch, dtype=jnp.float32)
    rel = pos[:, None] - pos[None, :]
    scale = dk ** -0.5
    mask = jnp.where(rel[None] >= 0, jnp.exp(rel[None] * log_g[:, None, None]), 0.0) * scale
    q_decay = jnp.exp((pos[None, :, None] + 1.0) * log_g[:, None, None])
    k_decay = jnp.exp((ch - 1.0 - pos[None, :, None]) * log_g[:, None, None]) * scale
    c_decay = jnp.exp(ch * log_g)[:, None, None]
    return pl.pallas_call(
        _ret_kernel,
        grid=(RET_HEADS, n),
        in_specs=[pl.BlockSpec((b, ch, dk), lambda h, c: (0, c, qb + h)),
                  pl.BlockSpec((b, ch, dk), lambda h, c: (0, c, kb + h)),
                  pl.BlockSpec((b, ch, dv), lambda h, c: (0, c, vb + h)),
                  pl.BlockSpec((b, ch, dv), lambda h, c: (0, c, gb + h)),
                  pl.BlockSpec((1, ch, ch), lambda h, c: (h, 0, 0)),
                  pl.BlockSpec((1, ch, 1), lambda h, c: (h, 0, 0)),
                  pl.BlockSpec((1, ch, 1), lambda h, c: (h, 0, 0)),
                  pl.BlockSpec((1, 1, 1), lambda h, c: (h, 0, 0))],
        out_specs=pl.BlockSpec((b, ch, dv), lambda h, c: (0, c, h)),
        out_shape=jax.ShapeDtypeStruct((b, s, RET_HEADS * dv), jnp.bfloat16),
        scratch_shapes=[pltpu.VMEM((b, dk, dv), jnp.float32)],
        compiler_params=_params(("parallel", "arbitrary")),
        name="retention",
    )(proj3, proj3, proj3, proj3, mask, q_decay, k_decay, c_decay)


def _mix_kernel(a_ref, r_ref, wa_ref, wr_ref, ga_ref, gb_ref, o_ref):
    ya = jnp.dot(a_ref[...], wa_ref[...], preferred_element_type=jnp.float32)
    yr = jnp.dot(r_ref[...], wr_ref[...], preferred_element_type=jnp.float32)
    ga = _sigmoid(ga_ref[...].astype(jnp.float32))
    gb = _sigmoid(gb_ref[...].astype(jnp.float32))
    o_ref[...] = (ga * ya + gb * yr).astype(o_ref.dtype)


def _mix(attn2, ret2, wa, wr, proj2, ga_off, gb_off):
    m, ka = attn2.shape
    kr = ret2.shape[1]
    d = wa.shape[1]
    tm = _div_block(m, 512, V7X_SUBLANES)
    tn = _div_block(d, 512, V7X_LANES)
    assert ga_off % tn == 0 and gb_off % tn == 0
    gab, gbb = ga_off // tn, gb_off // tn
    return pl.pallas_call(
        _mix_kernel,
        grid=(m // tm, d // tn),
        in_specs=[pl.BlockSpec((tm, ka), lambda i, j: (i, 0)),
                  pl.BlockSpec((tm, kr), lambda i, j: (i, 0)),
                  pl.BlockSpec((ka, tn), lambda i, j: (0, j)),
                  pl.BlockSpec((kr, tn), lambda i, j: (0, j)),
                  pl.BlockSpec((tm, tn), lambda i, j: (i, gab + j)),
                  pl.BlockSpec((tm, tn), lambda i, j: (i, gbb + j))],
        out_specs=pl.BlockSpec((tm, tn), lambda i, j: (i, j)),
        out_shape=jax.ShapeDtypeStruct((m, d), jnp.bfloat16),
        compiler_params=_params(("parallel", "parallel")),
        name="mix",
    )(attn2, ret2, wa, wr, proj2, proj2)


def _resid_kernel(a_ref, w_ref, x_ref, gt_ref, o_ref):
    y = jnp.dot(a_ref[...], w_ref[...], preferred_element_type=jnp.float32)
    o_ref[...] = x_ref[...] + gt_ref[0] * y


def _out_resid(mix2, w, x2, mod3, gate_idx, seq):
    m, k = mix2.shape
    d = w.shape[1]
    tm = _div_block(seq, 1024, V7X_SUBLANES)
    tn = _div_block(d, 512, V7X_LANES)
    per_b = seq // tm
    return pl.pallas_call(
        _resid_kernel,
        grid=(m // tm, d // tn),
        in_specs=[pl.BlockSpec((tm, k), lambda i, j: (i, 0)),
                  pl.BlockSpec((k, tn), lambda i, j: (0, j)),
                  pl.BlockSpec((tm, tn), lambda i, j: (i, j)),
                  pl.BlockSpec((1, 1, tn), lambda i, j: ((i // per_b) * 6 + gate_idx, 0, j))],
        out_specs=pl.BlockSpec((tm, tn), lambda i, j: (i, j)),
        out_shape=jax.ShapeDtypeStruct((m, d), jnp.float32),
        compiler_params=_params(("parallel", "parallel")),
        name="out_resid",
    )(mix2, w, x2, mod3)


def _router_kernel(x_ref, g_ref, sh_ref, sc_ref, wr_ref, br_ref,
                   h_ref, hp_ref, idx_ref, pos_ref, w_ref, cnt_ref, carry_ref):
    i = pl.program_id(0)
    e = N_EXPERTS
    per_g = e // N_GROUPS
    tb = x_ref.shape[0]

    @pl.when(i == 0)
    def _():
        carry_ref[...] = jnp.zeros_like(carry_ref)

    x = x_ref[...]
    ms = jnp.mean(x * x, axis=-1, keepdims=True)
    h = x * lax.rsqrt(ms + EPS) * g_ref[...]
    h = h * (1.0 + sc_ref[0]) + sh_ref[0]
    hb = h.astype(jnp.bfloat16)
    h_ref[...] = hb
    half = h.shape[1] // 2
    _slab_store(hp_ref, _pack_pair(h[:, :half], h[:, half:]), tb, half // V7X_LANES)

    nt = (((1,), (1,)), ((), ()))
    logits = lax.dot_general(wr_ref[...], hb, nt, preferred_element_type=jnp.float32)
    scores = _sigmoid(logits)
    choice = scores + br_ref[...]

    c3 = choice.reshape(N_GROUPS, per_g, tb)
    j_iota = lax.broadcasted_iota(jnp.int32, c3.shape, 1).astype(jnp.float32)
    m1 = c3.max(axis=1, keepdims=True)
    first = jnp.min(jnp.where(c3 == m1, j_iota, float(per_g)), axis=1, keepdims=True)
    m2 = jnp.where(j_iota == first, -jnp.inf, c3).max(axis=1, keepdims=True)
    gs = (m1 + m2).reshape(N_GROUPS, tb)

    g_iota = lax.broadcasted_iota(jnp.int32, gs.shape, 0)
    grank = jnp.zeros(gs.shape, jnp.int32)
    for gp in range(N_GROUPS):
        row = gs[gp:gp + 1, :]
        ahead = (row > gs) | ((row == gs) & (gp < g_iota))
        grank = grank + ahead.astype(jnp.int32)
    gmask = grank < TOPK_GROUPS
    emask = jnp.broadcast_to(gmask.reshape(N_GROUPS, 1, tb), c3.shape).reshape(e, tb)
    masked = jnp.where(emask, choice, -jnp.inf)

    e_iota = lax.broadcasted_iota(jnp.int32, masked.shape, 0)
    erank = jnp.zeros(masked.shape, jnp.int32)
    for ep in range(e):
        row = masked[ep:ep + 1, :]
        ahead = (row > masked) | ((row == masked) & (ep < e_iota))
        erank = erank + ahead.astype(jnp.int32)
    sel = (erank < TOP_K) & emask
    self32 = sel.astype(jnp.float32)

    wsel = scores * self32
    wn = wsel / jnp.sum(wsel, axis=0, keepdims=True) * ROUTED_SCALE

    selb = self32.astype(jnp.bfloat16)
    t_r = lax.broadcasted_iota(jnp.int32, (tb, tb), 0)
    t_c = lax.broadcasted_iota(jnp.int32, (tb, tb), 1)
    upper = (t_r <= t_c).astype(jnp.bfloat16)
    incl = jnp.dot(selb, upper, preferred_element_type=jnp.float32)
    carry = carry_ref[...]
    rank_in_e = carry + incl - 1.0
    carry_new = carry + jnp.sum(self32, axis=1, keepdims=True)
    carry_ref[...] = carry_new
    cnt_ref[...] = jnp.broadcast_to(carry_new, cnt_ref.shape).astype(jnp.int32)

    e_r = lax.broadcasted_iota(jnp.int32, (e, e), 0)
    e_c = lax.broadcasted_iota(jnp.int32, (e, e), 1)
    lower = (e_c < e_r).astype(jnp.bfloat16)
    before = jnp.dot(lower, selb, preferred_element_type=jnp.float32)
    e_f = e_iota.astype(jnp.float32)
    idx_rows, pos_rows, w_rows = [], [], []
    for k in range(TOP_K):
        hit = jnp.where(sel & (before == float(k)), 1.0, 0.0)
        idx_rows.append(jnp.sum(hit * e_f, axis=0, keepdims=True))
        pos_rows.append(jnp.sum(hit * rank_in_e, axis=0, keepdims=True))
        w_rows.append(jnp.sum(hit * wn, axis=0, keepdims=True))
    idx_ref[0] = jnp.concatenate(idx_rows, axis=0).astype(jnp.int32)
    pos_ref[0] = jnp.concatenate(pos_rows, axis=0).astype(jnp.int32)
    w_ref[0] = jnp.concatenate(w_rows, axis=0)


def _router(x1, g, mod3, shift_idx, scale_idx, w_router_t, b_router, seq):
    t, d = x1.shape
    e = N_EXPERTS
    tb = MOE_TILE
    assert seq % tb == 0 and d % (2 * V7X_LANES * V7X_SUBLANES) == 0
    slab = d // 2 // V7X_LANES
    per_b = seq // tb
    nt = t // tb
    return pl.pallas_call(
        _router_kernel,
        grid=(nt,),
        in_specs=[pl.BlockSpec((tb, d), lambda i: (i, 0)),
                  pl.BlockSpec((1, d), lambda i: (0, 0)),
                  pl.BlockSpec((1, 1, d), lambda i: ((i // per_b) * 6 + shift_idx, 0, 0)),
                  pl.BlockSpec((1, 1, d), lambda i: ((i // per_b) * 6 + scale_idx, 0, 0)),
                  pl.BlockSpec((e, d), lambda i: (0, 0)),
                  pl.BlockSpec((e, 1), lambda i: (0, 0))],
        out_specs=[pl.BlockSpec((tb, d), lambda i: (i, 0)),
                   pl.BlockSpec((tb * slab, V7X_LANES), lambda i: (i, 0)),
                   pl.BlockSpec((1, TOP_K, tb), lambda i: (i, 0, 0)),
                   pl.BlockSpec((1, TOP_K, tb), lambda i: (i, 0, 0)),
                   pl.BlockSpec((1, TOP_K, tb), lambda i: (i, 0, 0)),
                   pl.BlockSpec((e, V7X_LANES), lambda i: (0, 0))],
        out_shape=[jax.ShapeDtypeStruct((t, d), jnp.bfloat16),
                   jax.ShapeDtypeStruct((t * slab, V7X_LANES), jnp.int32),
                   jax.ShapeDtypeStruct((nt, TOP_K, tb), jnp.int32),
                   jax.ShapeDtypeStruct((nt, TOP_K, tb), jnp.int32),
                   jax.ShapeDtypeStruct((nt, TOP_K, tb), jnp.float32),
                   jax.ShapeDtypeStruct((e, V7X_LANES), jnp.int32)],
        scratch_shapes=[pltpu.VMEM((e, 1), jnp.float32)],
        compiler_params=_params(("arbitrary",)),
        name="router",
    )(x1, g.reshape(1, d), mod3, mod3, w_router_t, b_router.reshape(e, 1))


def _slots_kernel(pstart_ref, idx_ref, pos_ref, o_ref):
    idx = idx_ref[...]
    base = jnp.zeros(idx.shape, jnp.int32)
    for ex in range(N_EXPERTS):
        base = jnp.where(idx == ex, pstart_ref[ex], base)
    o_ref[...] = base + pos_ref[...]


def _slots(pad_start, idx3, pos3):
    nt = idx3.shape[0]
    blk = (1,) + idx3.shape[1:]
    spec = pl.BlockSpec(blk, lambda i, ps: (i, 0, 0))
    return pl.pallas_call(
        _slots_kernel,
        grid_spec=pltpu.PrefetchScalarGridSpec(num_scalar_prefetch=1, grid=(nt,),
                                               in_specs=[spec, spec], out_specs=spec),
        out_shape=jax.ShapeDtypeStruct(idx3.shape, jnp.int32),
        compiler_params=_params(("parallel",)),
        name="slots",
    )(pad_start, idx3, pos3)


def _dispatch_kernel(pstart_ref, pend_ref, hp_ref, slot_hbm, xs_hbm,
                     slot_s, zero_v, sem_i, sem_z, sem_r, *, slab):
    i = pl.program_id(0)
    tb = hp_ref.shape[0] // slab

    def slot_copy():
        return pltpu.make_async_copy(slot_hbm.at[i], slot_s, sem_i)

    slot_copy().start()

    def zero_copy(ex):
        first = pl.multiple_of((pend_ref[ex] - tb) * slab, tb * slab)
        return pltpu.make_async_copy(zero_v, xs_hbm.at[pl.ds(first, tb * slab)], sem_z)

    @pl.when(i == 0)
    def _():
        zero_v[...] = jnp.zeros_like(zero_v)

        def start(ex, carry):
            @pl.when(pend_ref[ex] > pstart_ref[ex])
            def _():
                zero_copy(ex).start()
            return carry

        def wait(ex, carry):
            @pl.when(pend_ref[ex] > pstart_ref[ex])
            def _():
                zero_copy(ex).wait()
            return carry

        lax.fori_loop(0, N_EXPERTS, start, 0)
        lax.fori_loop(0, N_EXPERTS, wait, 0)

    slot_copy().wait()

    def row_copy(t, k):
        return pltpu.make_async_copy(hp_ref.at[_slab_rows(t, slab)],
                                     xs_hbm.at[_slab_rows(slot_s[k, t], slab)], sem_r)

    def start_rows(t, carry):
        for k in range(TOP_K):
            row_copy(t, k).start(priority=k % 2)
        return carry

    lax.fori_loop(0, tb, start_rows, 0, unroll=2)
    for k in range(TOP_K):
        pltpu.make_async_copy(hp_ref, xs_hbm.at[pl.ds(0, tb * slab)], sem_r).wait()


def _dispatch(pad_start, pad_end, h2p, slot3, n_rows, slab):
    tb = MOE_TILE
    t = h2p.shape[0] // slab
    return pl.pallas_call(
        functools.partial(_dispatch_kernel, slab=slab),
        grid_spec=pltpu.PrefetchScalarGridSpec(
            num_scalar_prefetch=2,
            grid=(t // tb,),
            in_specs=[pl.BlockSpec((tb * slab, V7X_LANES), lambda i, ps, pe: (i, 0)),
                      pl.BlockSpec(memory_space=pl.ANY)],
            out_specs=pl.BlockSpec(memory_space=pl.ANY),
            scratch_shapes=[pltpu.SMEM((TOP_K, tb), jnp.int32),
                            pltpu.VMEM((tb * slab, V7X_LANES), jnp.int32),
                            pltpu.SemaphoreType.DMA,
                            pltpu.SemaphoreType.DMA,
                            pltpu.SemaphoreType.DMA],
        ),
        out_shape=jax.ShapeDtypeStruct((n_rows * slab, V7X_LANES), jnp.int32),
        compiler_params=_params(("arbitrary",)),
        name="dispatch",
    )(pad_start, pad_end, h2p, slot3)


def _expert_kernel(blk_e_ref, nused_ref, x_ref, wg_ref, wu_ref, wd_ref, o_ref, *, slab):
    i = pl.program_id(0)

    @pl.when(i < nused_ref[0])
    def _():
        tb = x_ref.shape[0] // slab
        half = slab * V7X_LANES
        lo, hi = _unpack_pair(_slab_load(x_ref, tb, slab))
        lo = lo.astype(jnp.bfloat16)
        hi = hi.astype(jnp.bfloat16)
        g = (jnp.dot(lo, wg_ref[0, :half, :], preferred_element_type=jnp.float32)
             + jnp.dot(hi, wg_ref[0, half:, :], preferred_element_type=jnp.float32))
        u = (jnp.dot(lo, wu_ref[0, :half, :], preferred_element_type=jnp.float32)
             + jnp.dot(hi, wu_ref[0, half:, :], preferred_element_type=jnp.float32))
        a = (_silu(g) * u).astype(jnp.bfloat16)
        y = jnp.dot(a, wd_ref[0], preferred_element_type=jnp.float32)
        _slab_store(o_ref, _pack_pair(y[:, :half], y[:, half:]), tb, slab)


def _experts(blk_e, n_used, xs, wg, wu, wd, slab):
    tb = MOE_TILE
    p = xs.shape[0] // slab
    d = 2 * slab * V7X_LANES
    f = wg.shape[2]
    nblk = p // tb

    def row_map(i, be, nu):
        return (jnp.minimum(i, nu[0] - 1), 0)

    def w_map(i, be, nu):
        return (be[jnp.minimum(i, nu[0] - 1)], 0, 0)

    return pl.pallas_call(
        functools.partial(_expert_kernel, slab=slab),
        grid_spec=pltpu.PrefetchScalarGridSpec(
            num_scalar_prefetch=2,
            grid=(nblk,),
            in_specs=[pl.BlockSpec((tb * slab, V7X_LANES), row_map),
                      pl.BlockSpec((1, d, f), w_map),
                      pl.BlockSpec((1, d, f), w_map),
                      pl.BlockSpec((1, f, d), w_map)],
            out_specs=pl.BlockSpec((tb * slab, V7X_LANES), row_map),
        ),
        out_shape=jax.ShapeDtypeStruct(xs.shape, jnp.int32),
        compiler_params=_params(("arbitrary",)),
        name="experts",
    )(blk_e, n_used, xs, wg, wu, wd)


def _combine_kernel(x_ref, h_ref, wt_ref, gt_ref, gf_ref, wsg_ref, wsu_ref, wsd_ref,
                    slot_hbm, ys_hbm, o_ref, slot_s, rows_v, wrep, acc_lo, acc_hi, sem_i, sem_r):
    i = pl.program_id(0)
    n = pl.num_programs(0)
    tb = x_ref.shape[0]
    half = x_ref.shape[1] // 2
    slab = half // V7X_LANES
    per_tile = MOE_TILE // tb
    cur = i % 2
    nxt = 1 - cur

    def slot_copy(step):
        win = pl.ds((step % per_tile) * tb, tb)
        return pltpu.make_async_copy(slot_hbm.at[step // per_tile, :, win], slot_s.at[step % 2],
                                     sem_i.at[step % 2])

    def request_token(buf, t):
        for k in range(TOP_K):
            pltpu.make_async_copy(ys_hbm.at[_slab_rows(slot_s[buf, k, t], slab)],
                                  rows_v.at[buf, k, _slab_rows(t, slab)],
                                  sem_r.at[buf]).start(priority=k % 2)

    def wait_rows(buf):
        for k in range(TOP_K):
            pltpu.make_async_copy(ys_hbm.at[pl.ds(0, tb * slab)], rows_v.at[buf, k], sem_r.at[buf]).wait()

    @pl.when(i == 0)
    def _():
        slot_copy(0).start()
        slot_copy(0).wait()

        def first(t, carry):
            request_token(0, t)
            return carry

        lax.fori_loop(0, tb, first, 0, unroll=2)

        @pl.when(n > 1)
        def _():
            slot_copy(1).start()
            slot_copy(1).wait()

    @pl.when(i + 2 < n)
    def _():
        slot_copy(i + 2).start()

    h = h_ref[...]
    g = jnp.dot(h, wsg_ref[...], preferred_element_type=jnp.float32)
    u = jnp.dot(h, wsu_ref[...], preferred_element_type=jnp.float32)
    a = (_silu(g) * u).astype(jnp.bfloat16)
    y = jnp.dot(a, wsd_ref[...], preferred_element_type=jnp.float32)

    wt = wt_ref[...]
    for k in range(TOP_K):
        wrep[k] = jnp.broadcast_to(wt[:, k:k + 1], (tb, V7X_LANES))

    def token(t, carry, buf, request_next):
        rows = _slab_rows(t, slab)
        lo_acc = jnp.zeros((slab, V7X_LANES), jnp.float32)
        hi_acc = jnp.zeros((slab, V7X_LANES), jnp.float32)
        for k in range(TOP_K):
            wv = jnp.broadcast_to(wrep[k, pl.ds(t, 1), :], (slab, V7X_LANES))
            lo, hi = _unpack_pair(rows_v[buf, k, rows, :])
            lo_acc = lo_acc + wv * lo
            hi_acc = hi_acc + wv * hi
        acc_lo[rows, :] = lo_acc
        acc_hi[rows, :] = hi_acc
        if request_next:
            request_token(1 - buf, t)
        return carry

    for buf in range(2):
        for request_next in (True, False):
            @pl.when((cur == buf) & ((i + 1 < n) == request_next))
            def _():
                wait_rows(buf)
                lax.fori_loop(0, tb, functools.partial(token, buf=buf, request_next=request_next),
                              0, unroll=4)

    @pl.when(i + 2 < n)
    def _():
        slot_copy(i + 2).wait()

    yy = y + jnp.concatenate([_slab_load(acc_lo, tb, slab), _slab_load(acc_hi, tb, slab)], axis=1)
    xo = x_ref[...] + gt_ref[0] * yy
    ms = jnp.mean(xo * xo, axis=-1, keepdims=True)
    o_ref[...] = xo * lax.rsqrt(ms + EPS) * gf_ref[...]


def _combine(x1, h2, w_t, mod3, gate_idx, g_final, wsg, wsu, wsd, slot3, ys, seq):
    t, d = x1.shape
    tb = COMBINE_TILE
    assert MOE_TILE % tb == 0 and seq % tb == 0
    per_b = seq // tb
    half = d // 2
    f = wsg.shape[1]
    once = pl.Buffered(1)
    return pl.pallas_call(
        _combine_kernel,
        grid=(t // tb,),
        in_specs=[pl.BlockSpec((tb, d), lambda i: (i, 0)),
                  pl.BlockSpec((tb, d), lambda i: (i, 0)),
                  pl.BlockSpec((tb, TOP_K), lambda i: (i, 0)),
                  pl.BlockSpec((1, 1, d), lambda i: ((i // per_b) * 6 + gate_idx, 0, 0)),
                  pl.BlockSpec((1, d), lambda i: (0, 0)),
                  pl.BlockSpec((d, f), lambda i: (0, 0), pipeline_mode=once),
                  pl.BlockSpec((d, f), lambda i: (0, 0), pipeline_mode=once),
                  pl.BlockSpec((f, d), lambda i: (0, 0), pipeline_mode=once),
                  pl.BlockSpec(memory_space=pl.ANY),
                  pl.BlockSpec(memory_space=pl.ANY)],
        out_specs=pl.BlockSpec((tb, d), lambda i: (i, 0)),
        scratch_shapes=[pltpu.SMEM((2, TOP_K, tb), jnp.int32),
                        pltpu.VMEM((2, TOP_K, tb * half // V7X_LANES, V7X_LANES), jnp.int32),
                        pltpu.VMEM((TOP_K, tb, V7X_LANES), jnp.float32),
                        pltpu.VMEM((tb * half // V7X_LANES, V7X_LANES), jnp.float32),
                        pltpu.VMEM((tb * half // V7X_LANES, V7X_LANES), jnp.float32),
                        pltpu.SemaphoreType.DMA((2,)),
                        pltpu.SemaphoreType.DMA((2,))],
        out_shape=jax.ShapeDtypeStruct((t, d), jnp.float32),
        compiler_params=_params(("arbitrary",)),
        name="combine",
    )(x1, h2, w_t, mod3, g_final.reshape(1, d), wsg, wsu, wsd, slot3, ys)


def _in_layout(d):
    qw = ATT_HEADS * ATT_HEAD_DIM
    kvw = ATT_KV_HEADS * ATT_HEAD_DIM
    rqk = RET_HEADS * RET_QK_DIM
    rv = RET_HEADS * RET_V_DIM
    order = [("qa", qw), ("ka", kvw), ("va", kvw), ("qr", rqk), ("kr", rqk),
             ("vr", rv), ("gr", rv), ("ga", d), ("gb", d)]
    dst = {}
    off = 0
    for name, width in order:
        dst[name] = off
        off += width
    return dst


def kernel(x, c, w_ada, b_ada, g_norm_mix, w_in, attn_sinks, w_attn_out, w_ret_out, w_o, g_norm_ffn,
           w_router, b_router, w_gate, w_up, w_down, w_sh_gate, w_sh_up, w_sh_down, g_norm_final):
    b, s, d = x.shape
    t = b * s
    depth = w_ada.shape[0]
    bf = jnp.bfloat16
    dst = _in_layout(d)

    c_pad = jnp.zeros((V7X_SUBLANES, d), jnp.float32).at[:b].set(c)
    x2 = x.reshape(t, d)
    for l in range(depth):
        mod = _ada(c_pad, w_ada[l], b_ada[l])
        mod3 = mod[:b].reshape(b * 6, 1, d)

        h = _norm_mod(x2.reshape(b, s, d), g_norm_mix[l], mod3, 0, 1)
        proj, (wg_b, wu_b, wd_b) = _in_proj(h.reshape(t, d), w_in[l].astype(bf),
                                            (w_gate[l], w_up[l], w_down[l]), 1024, 768)
        proj3 = proj.reshape(b, s, proj.shape[1])
        attn = _attention(proj3, attn_sinks[l], dst["qa"], dst["ka"], dst["va"])
        ret = _retention(proj3, dst["qr"], dst["kr"], dst["vr"], dst["gr"])
        mix = _mix(attn.reshape(t, -1), ret.reshape(t, -1), w_attn_out[l].astype(bf),
                   w_ret_out[l].astype(bf), proj, dst["ga"], dst["gb"])
        x1 = _out_resid(mix, w_o[l].astype(bf), x2, mod3, 2, s)

        h2, h2p, idx3, pos3, w3, cnt = _router(x1, g_norm_ffn[l], mod3, 3, 4,
                                               w_router[l].T.astype(bf), b_router[l], s)
        counts = cnt[:, 0]
        tile = MOE_TILE
        padded = (counts + tile - 1) // tile * tile
        pad_end = jnp.cumsum(padded).astype(jnp.int32)
        pad_start = (pad_end - padded).astype(jnp.int32)
        n_blocks = (t * TOP_K) // tile + N_EXPERTS
        n_used = (pad_end[-1] // tile).reshape(1).astype(jnp.int32)
        blk_first = jnp.arange(n_blocks, dtype=jnp.int32) * tile
        blk_e = jnp.minimum(jnp.sum((pad_end[None, :] <= blk_first[:, None]).astype(jnp.int32), axis=1),
                            N_EXPERTS - 1)
        slab = d // 2 // V7X_LANES
        slot3 = _slots(pad_start, idx3, pos3)
        xs = _dispatch(pad_start, pad_end, h2p, slot3, n_blocks * tile, slab)
        ys = _experts(blk_e, n_used, xs, wg_b, wu_b, wd_b, slab)
        w_t = w3.transpose(0, 2, 1).reshape(t, TOP_K)
        is_last = l == depth - 1
        assert is_last, "the final norm is fused into the last layer's combine"
        x2 = _combine(x1, h2, w_t, mod3, 5, g_norm_final, w_sh_gate[l].astype(bf),
                      w_sh_up[l].astype(bf), w_sh_down[l].astype(bf), slot3, ys, s)
    return x2.reshape(b, s, d)
```

```python
import functools
import math

import jax
import jax.numpy as jnp
from jax import lax
from jax.experimental import pallas as pl
from jax.experimental.pallas import tpu as pltpu

ATT_HEADS = 32
ATT_KV_HEADS = 4
ATT_HEAD_DIM = 64
WINDOW = 128
ATT_BLOCK = 128
RET_HEADS = 8
RET_QK_DIM = 256
RET_V_DIM = 512
RET_CHUNK = 128
N_EXPERTS = 64
N_GROUPS = 8
TOPK_GROUPS = 4
TOP_K = 8
ROUTED_SCALE = 2.5
EPS = 1e-6

V7X_LANES = 128
V7X_SUBLANES = 8
V7X_VMEM_LIMIT_BYTES = 60000 * 1024

MOE_TILE = 256
COMBINE_TILE = 128
NEG_BIG = -1e30


def _div_block(n, target, align):
    best = None
    b = align
    while b <= min(n, target):
        if n % b == 0:
            best = b
        b += align
    assert best is not None, (n, target, align)
    return best


def _params(semantics):
    return pltpu.CompilerParams(dimension_semantics=semantics,
                                vmem_limit_bytes=V7X_VMEM_LIMIT_BYTES)


def _sigmoid(v):
    return 1.0 / (1.0 + jnp.exp(-v))


def _silu(v):
    return v * _sigmoid(v)


def _pack_pair(lo, hi):
    return pltpu.pack_elementwise([lo, hi], packed_dtype=jnp.bfloat16)


def _unpack_pair(p):
    lo = pltpu.unpack_elementwise(p, index=0, packed_dtype=jnp.bfloat16, unpacked_dtype=jnp.float32)
    hi = pltpu.unpack_elementwise(p, index=1, packed_dtype=jnp.bfloat16, unpacked_dtype=jnp.float32)
    return lo, hi


def _slab_load(ref, n_rows, slab):
    return jnp.concatenate([ref[pl.ds(s, n_rows, stride=slab), :] for s in range(slab)], axis=1)


def _slab_store(ref, val, n_rows, slab):
    for s in range(slab):
        ref[pl.ds(s, n_rows, stride=slab), :] = val[:, s * V7X_LANES:(s + 1) * V7X_LANES]


def _slab_rows(r, slab):
    return pl.ds(pl.multiple_of(r * slab, slab), slab)


def _ada_kernel(c_ref, w_ref, b_ref, o_ref):
    cs = _silu(c_ref[...]).astype(jnp.bfloat16)
    o_ref[...] = jnp.dot(cs, w_ref[...].astype(jnp.bfloat16),
                         preferred_element_type=jnp.float32) + b_ref[...]


def _ada(c_pad, w, b):
    m, d = c_pad.shape
    n = w.shape[1]
    tn = _div_block(n, 512, V7X_LANES)
    return pl.pallas_call(
        _ada_kernel,
        grid=(n // tn,),
        in_specs=[pl.BlockSpec((m, d), lambda j: (0, 0)),
                  pl.BlockSpec((d, tn), lambda j: (0, j)),
                  pl.BlockSpec((1, tn), lambda j: (0, j))],
        out_specs=pl.BlockSpec((m, tn), lambda j: (0, j)),
        out_shape=jax.ShapeDtypeStruct((m, n), jnp.float32),
        compiler_params=_params(("parallel",)),
        name="ada",
    )(c_pad, w, b.reshape(1, n))


def _norm_mod_kernel(x_ref, g_ref, sh_ref, sc_ref, o_ref):
    x = x_ref[0]
    ms = jnp.mean(x * x, axis=-1, keepdims=True)
    y = x * lax.rsqrt(ms + EPS) * g_ref[...]
    o_ref[0] = (y * (1.0 + sc_ref[0]) + sh_ref[0]).astype(o_ref.dtype)


def _norm_mod(x3, g, mod3, shift_idx, scale_idx):
    b, s, d = x3.shape
    ts = _div_block(s, 256, V7X_SUBLANES)
    return pl.pallas_call(
        _norm_mod_kernel,
        grid=(b, s // ts),
        in_specs=[pl.BlockSpec((1, ts, d), lambda bi, i: (bi, i, 0)),
                  pl.BlockSpec((1, d), lambda bi, i: (0, 0)),
                  pl.BlockSpec((1, 1, d), lambda bi, i: (bi * 6 + shift_idx, 0, 0)),
                  pl.BlockSpec((1, 1, d), lambda bi, i: (bi * 6 + scale_idx, 0, 0))],
        out_specs=pl.BlockSpec((1, ts, d), lambda bi, i: (bi, i, 0)),
        out_shape=jax.ShapeDtypeStruct((b, s, d), jnp.bfloat16),
        compiler_params=_params(("parallel", "parallel")),
        name="norm_mod",
    )(x3, g.reshape(1, d), mod3, mod3)


def _in_proj_kernel(a_ref, b_ref, *rest, n_side, n_cast, n_j):
    src = rest[:n_side]
    o_ref = rest[n_side]
    dst = rest[n_side + 1:]
    o_ref[...] = jnp.dot(a_ref[...], b_ref[...],
                         preferred_element_type=jnp.float32).astype(o_ref.dtype)
    step = pl.program_id(0) * n_j + pl.program_id(1)

    @pl.when(step < n_cast)
    def _():
        for s_ref, d_ref in zip(src, dst):
            d_ref[...] = s_ref[...].astype(d_ref.dtype)


def _in_proj(a, b, side, tm_target, tn_target):
    m, k = a.shape
    n = b.shape[1]
    tm = _div_block(m, tm_target, V7X_SUBLANES)
    tn = _div_block(n, tn_target, V7X_LANES)
    n_i, n_j = m // tm, n // tn
    n_cast = 1 << ((n_i * n_j).bit_length() - 1)
    chunked = []
    for w in side:
        rows = w.size // w.shape[-1]
        assert rows % (n_cast * V7X_SUBLANES * 2) == 0, (w.shape, n_cast)
        chunked.append(w.reshape(n_cast, rows // n_cast, w.shape[-1]))

    def side_map(i, j):
        return (jnp.minimum(i * n_j + j, n_cast - 1), 0, 0)

    side_specs = [pl.BlockSpec((1,) + c.shape[1:], side_map) for c in chunked]
    outs = pl.pallas_call(
        functools.partial(_in_proj_kernel, n_side=len(side), n_cast=n_cast, n_j=n_j),
        grid=(n_i, n_j),
        in_specs=[pl.BlockSpec((tm, k), lambda i, j: (i, 0)),
                  pl.BlockSpec((k, tn), lambda i, j: (0, j))] + side_specs,
        out_specs=[pl.BlockSpec((tm, tn), lambda i, j: (i, j))] + side_specs,
        out_shape=[jax.ShapeDtypeStruct((m, n), jnp.bfloat16)]
        + [jax.ShapeDtypeStruct(c.shape, jnp.bfloat16) for c in chunked],
        compiler_params=_params(("arbitrary", "arbitrary")),
        name="in_proj",
    )(a, b, *chunked)
    return outs[0], [o.reshape(w.shape) for o, w in zip(outs[1:], side)]


def _attn_kernel(sink_ref, q_ref, kc_ref, kp_ref, vc_ref, vp_ref, bias_ref, o_ref):
    i = pl.program_id(1)
    blk = ATT_BLOCK
    hd = ATT_HEAD_DIM
    group = ATT_HEADS // ATT_KV_HEADS
    pairs = group // 2
    nt = (((1,), (1,)), ((), ()))
    zpad = jnp.zeros((2 * blk, hd), jnp.bfloat16)

    def scores(h, par):
        sl = slice(h * hd, (h + 1) * hd)
        k2 = jnp.concatenate([kp_ref[0, :, sl], kc_ref[0, :, sl]], axis=0) * (hd ** -0.5)
        qp = jnp.concatenate([q_ref[0, :, (h * pairs + p) * 2 * hd:(h * pairs + p + 1) * 2 * hd]
                              for p in range(pairs)], axis=0)
        kz = jnp.concatenate([k2, zpad] if par == 0 else [zpad, k2], axis=1)
        return lax.dot_general(qp, kz, nt, preferred_element_type=jnp.float32)

    def softmax(s, h, par):
        s = s + bias_ref[h * 2 + par]
        s = jnp.concatenate([jnp.where(i == 0, NEG_BIG, s[:, :blk]), s[:, blk:]], axis=1)
        sink = jnp.concatenate([jnp.full((blk, 1), sink_ref[h * group + 2 * p + par], jnp.float32)
                                for p in range(pairs)], axis=0)
        m = jnp.maximum(s.max(-1, keepdims=True), sink)
        pr = jnp.exp(s - m)
        denom = pr.sum(-1, keepdims=True) + jnp.exp(sink - m)
        return pr.astype(jnp.bfloat16), 1.0 / denom

    def values(pr, inv, h, par):
        sl = slice(h * hd, (h + 1) * hd)
        v2 = jnp.concatenate([vp_ref[0, :, sl], vc_ref[0, :, sl]], axis=0)
        vz = jnp.concatenate([v2, zpad] if par == 0 else [zpad, v2], axis=1)
        return jnp.dot(pr, vz, preferred_element_type=jnp.float32) * inv

    items = [(h, par) for h in range(ATT_KV_HEADS) for par in range(2)]
    s_of, p_of, acc = {}, {}, {}
    for n in range(len(items) + 2):
        if n < len(items):
            s_of[n] = scores(*items[n])
        if 1 <= n <= len(items):
            p_of[n - 1] = softmax(s_of.pop(n - 1), *items[n - 1])
        if n >= 2:
            h, par = items[n - 2]
            o = values(*p_of.pop(n - 2), h, par)
            acc[h] = o if par == 0 else acc[h] + o
            if par == 1:
                out = acc.pop(h)
                for p in range(pairs):
                    o_ref[0, :, (h * pairs + p) * 2 * hd:(h * pairs + p + 1) * 2 * hd] = (
                        out[p * blk:(p + 1) * blk].astype(o_ref.dtype))


def _attn_bias():
    blk = ATT_BLOCK
    group = ATT_HEADS // ATT_KV_HEADS
    pairs = group // 2
    qi = jnp.arange(blk)[:, None]
    kj = jnp.arange(2 * blk)[None, :]
    dist = qi + blk - kj
    valid = (dist >= 0) & (dist < WINDOW)
    slopes = jnp.exp2(-8.0 * jnp.arange(1, ATT_HEADS + 1, dtype=jnp.float32) / ATT_HEADS)
    slopes = slopes.reshape(ATT_KV_HEADS, pairs, 2)
    bias = jnp.where(valid, -slopes[..., None, None] * dist.astype(jnp.float32), NEG_BIG)
    return bias.transpose(0, 2, 1, 3, 4).reshape(ATT_KV_HEADS * 2, pairs * blk, 2 * blk)


def _attention(proj3, sinks, q_off, k_off, v_off):
    b, s, _ = proj3.shape
    qw = ATT_HEADS * ATT_HEAD_DIM
    kvw = ATT_KV_HEADS * ATT_HEAD_DIM
    nb = s // ATT_BLOCK
    group = ATT_HEADS // ATT_KV_HEADS
    assert q_off % qw == 0 and k_off % kvw == 0 and v_off % kvw == 0
    assert group % 2 == 0 and 2 * ATT_HEAD_DIM == V7X_LANES and WINDOW == ATT_BLOCK
    assert 4 ** round(math.log(ATT_HEAD_DIM, 4)) == ATT_HEAD_DIM, "score scale must be a power of two"
    qb, kb, vb = q_off // qw, k_off // kvw, v_off // kvw
    bias = _attn_bias()
    return pl.pallas_call(
        _attn_kernel,
        grid_spec=pltpu.PrefetchScalarGridSpec(
            num_scalar_prefetch=1,
            grid=(b, nb),
            in_specs=[pl.BlockSpec((1, ATT_BLOCK, qw), lambda bi, i, sk: (bi, i, qb)),
                      pl.BlockSpec((1, ATT_BLOCK, kvw), lambda bi, i, sk: (bi, i, kb)),
                      pl.BlockSpec((1, ATT_BLOCK, kvw), lambda bi, i, sk: (bi, jnp.maximum(i - 1, 0), kb)),
                      pl.BlockSpec((1, ATT_BLOCK, kvw), lambda bi, i, sk: (bi, i, vb)),
                      pl.BlockSpec((1, ATT_BLOCK, kvw), lambda bi, i, sk: (bi, jnp.maximum(i - 1, 0), vb)),
                      pl.BlockSpec(bias.shape, lambda bi, i, sk: (0, 0, 0), pipeline_mode=pl.Buffered(1))],
            out_specs=pl.BlockSpec((1, ATT_BLOCK, qw), lambda bi, i, sk: (bi, i, 0)),
        ),
        out_shape=jax.ShapeDtypeStruct((b, s, qw), jnp.bfloat16),
        compiler_params=_params(("parallel", "parallel")),
        name="attention",
    )(sinks, proj3, proj3, proj3, proj3, proj3, bias)


def _ret_kernel(q_ref, k_ref, v_ref, gr_ref, mask_ref, qd_ref, kd_ref, cd_ref, o_ref, state_ref):
    c = pl.program_id(1)

    @pl.when(c == 0)
    def _():
        state_ref[...] = jnp.zeros_like(state_ref)

    nt = (((1,), (1,)), ((), ()))
    tn = (((0,), (0,)), ((), ()))
    def decayed(bi):
        q = q_ref[bi]
        k = k_ref[bi]
        attn = lax.dot_general(q, k, nt, preferred_element_type=jnp.float32) * mask_ref[0]
        kd = (k.astype(jnp.float32) * kd_ref[0]).astype(k.dtype)
        return attn.astype(jnp.bfloat16), kd

    def recur(bi, attn, kd):
        q = q_ref[bi]
        v = v_ref[bi]
        intra = jnp.dot(attn, v, preferred_element_type=jnp.float32)
        state = state_ref[bi]
        inter = jnp.dot(q, state.astype(q.dtype), preferred_element_type=jnp.float32) * qd_ref[0]
        state_ref[bi] = state * cd_ref[0] + lax.dot_general(kd, v, tn, preferred_element_type=jnp.float32)
        return intra + inter

    def finish(bi, o):
        mu = jnp.mean(o, axis=-1, keepdims=True)
        oc = o - mu
        var = jnp.mean(oc * oc, axis=-1, keepdims=True)
        y = oc * lax.rsqrt(var + EPS)
        o_ref[bi] = (_silu(gr_ref[bi].astype(jnp.float32)) * y).astype(o_ref.dtype)

    nb = q_ref.shape[0]
    a_of, o_of = {}, {}
    for n in range(nb + 2):
        if n < nb:
            a_of[n] = decayed(n)
        if 1 <= n <= nb:
            o_of[n - 1] = recur(n - 1, *a_of.pop(n - 1))
        if n >= 2:
            finish(n - 2, o_of.pop(n - 2))


def _retention(proj3, q_off, k_off, v_off, g_off):
    b, s, _ = proj3.shape
    dk, dv, ch = RET_QK_DIM, RET_V_DIM, RET_CHUNK
    assert q_off % dk == 0 and k_off % dk == 0 and v_off % dv == 0 and g_off % dv == 0
    qb, kb, vb, gb = q_off // dk, k_off // dk, v_off // dv, g_off // dv
    n = s // ch
    log_g = jnp.log1p(-jnp.exp2(-5.0 - jnp.arange(RET_HEADS, dtype=jnp.float32)))
    pos = jnp.arange(ch, dtype=jnp.float32)
    rel = pos[:, None] - pos[None, :]
    scale = dk ** -0.5
    mask = jnp.where(rel[None] >= 0, jnp.exp(rel[None] * log_g[:, None, None]), 0.0) * scale
    q_decay = jnp.exp((pos[None, :, None] + 1.0) * log_g[:, None, None])
    k_decay = jnp.exp((ch - 1.0 - pos[None, :, None]) * log_g[:, None, None]) * scale
    c_decay = jnp.exp(ch * log_g)[:, None, None]
    return pl.pallas_call(
        _ret_kernel,
        grid=(RET_HEADS, n),
        in_specs=[pl.BlockSpec((b, ch, dk), lambda h, c: (0, c, qb + h)),
                  pl.BlockSpec((b, ch, dk), lambda h, c: (0, c, kb + h)),
                  pl.BlockSpec((b, ch, dv), lambda h, c: (0, c, vb + h)),
                  pl.BlockSpec((b, ch, dv), lambda h, c: (0, c, gb + h)),
                  pl.BlockSpec((1, ch, ch), lambda h, c: (h, 0, 0)),
                  pl.BlockSpec((1, ch, 1), lambda h, c: (h, 0, 0)),
                  pl.BlockSpec((1, ch, 1), lambda h, c: (h, 0, 0)),
                  pl.BlockSpec((1, 1, 1), lambda h, c: (h, 0, 0))],
        out_specs=pl.BlockSpec((b, ch, dv), lambda h, c: (0, c, h)),
        out_shape=jax.ShapeDtypeStruct((b, s, RET_HEADS * dv), jnp.bfloat16),
        scratch_shapes=[pltpu.VMEM((b, dk, dv), jnp.float32)],
        compiler_params=_params(("parallel", "arbitrary")),
        name="retention",
    )(proj3, proj3, proj3, proj3, mask, q_decay, k_decay, c_decay)


def _mix_kernel(a_ref, r_ref, wa_ref, wr_ref, ga_ref, gb_ref, o_ref):
    ya = jnp.dot(a_ref[...], wa_ref[...], preferred_element_type=jnp.float32)
    yr = jnp.dot(r_ref[...], wr_ref[...], preferred_element_type=jnp.float32)
    ga = _sigmoid(ga_ref[...].astype(jnp.float32))
    gb = _sigmoid(gb_ref[...].astype(jnp.float32))
    o_ref[...] = (ga * ya + gb * yr).astype(o_ref.dtype)


def _mix(attn2, ret2, wa, wr, proj2, ga_off, gb_off):
    m, ka = attn2.shape
    kr = ret2.shape[1]
    d = wa.shape[1]
    tm = _div_block(m, 512, V7X_SUBLANES)
    tn = _div_block(d, 512, V7X_LANES)
    assert ga_off % tn == 0 and gb_off % tn == 0
    gab, gbb = ga_off // tn, gb_off // tn
    return pl.pallas_call(
        _mix_kernel,
        grid=(m // tm, d // tn),
        in_specs=[pl.BlockSpec((tm, ka), lambda i, j: (i, 0)),
                  pl.BlockSpec((tm, kr), lambda i, j: (i, 0)),
                  pl.BlockSpec((ka, tn), lambda i, j: (0, j)),
                  pl.BlockSpec((kr, tn), lambda i, j: (0, j)),
                  pl.BlockSpec((tm, tn), lambda i, j: (i, gab + j)),
                  pl.BlockSpec((tm, tn), lambda i, j: (i, gbb + j))],
        out_specs=pl.BlockSpec((tm, tn), lambda i, j: (i, j)),
        out_shape=jax.ShapeDtypeStruct((m, d), jnp.bfloat16),
        compiler_params=_params(("parallel", "parallel")),
        name="mix",
    )(attn2, ret2, wa, wr, proj2, proj2)


def _resid_kernel(a_ref, w_ref, x_ref, gt_ref, o_ref):
    y = jnp.dot(a_ref[...], w_ref[...], preferred_element_type=jnp.float32)
    o_ref[...] = x_ref[...] + gt_ref[0] * y


def _out_resid(mix2, w, x2, mod3, gate_idx, seq):
    m, k = mix2.shape
    d = w.shape[1]
    tm = _div_block(seq, 1024, V7X_SUBLANES)
    tn = _div_block(d, 512, V7X_LANES)
    per_b = seq // tm
    return pl.pallas_call(
        _resid_kernel,
        grid=(m // tm, d // tn),
        in_specs=[pl.BlockSpec((tm, k), lambda i, j: (i, 0)),
                  pl.BlockSpec((k, tn), lambda i, j: (0, j)),
                  pl.BlockSpec((tm, tn), lambda i, j: (i, j)),
                  pl.BlockSpec((1, 1, tn), lambda i, j: ((i // per_b) * 6 + gate_idx, 0, j))],
        out_specs=pl.BlockSpec((tm, tn), lambda i, j: (i, j)),
        out_shape=jax.ShapeDtypeStruct((m, d), jnp.float32),
        compiler_params=_params(("parallel", "parallel")),
        name="out_resid",
    )(mix2, w, x2, mod3)


def _router_kernel(x_ref, g_ref, sh_ref, sc_ref, wr_ref, br_ref,
                   h_ref, hp_ref, idx_ref, pos_ref, w_ref, cnt_ref, carry_ref):
    i = pl.program_id(0)
    e = N_EXPERTS
    per_g = e // N_GROUPS
    tb = x_ref.shape[0]

    @pl.when(i == 0)
    def _():
        carry_ref[...] = jnp.zeros_like(carry_ref)

    x = x_ref[...]
    ms = jnp.mean(x * x, axis=-1, keepdims=True)
    h = x * lax.rsqrt(ms + EPS) * g_ref[...]
    h = h * (1.0 + sc_ref[0]) + sh_ref[0]
    hb = h.astype(jnp.bfloat16)
    h_ref[...] = hb
    half = h.shape[1] // 2
    _slab_store(hp_ref, _pack_pair(h[:, :half], h[:, half:]), tb, half // V7X_LANES)

    nt = (((1,), (1,)), ((), ()))
    logits = lax.dot_general(wr_ref[...], hb, nt, preferred_element_type=jnp.float32)
    scores = _sigmoid(logits)
    choice = scores + br_ref[...]

    c3 = choice.reshape(N_GROUPS, per_g, tb)
    j_iota = lax.broadcasted_iota(jnp.int32, c3.shape, 1).astype(jnp.float32)
    m1 = c3.max(axis=1, keepdims=True)
    first = jnp.min(jnp.where(c3 == m1, j_iota, float(per_g)), axis=1, keepdims=True)
    m2 = jnp.where(j_iota == first, -jnp.inf, c3).max(axis=1, keepdims=True)
    gs = (m1 + m2).reshape(N_GROUPS, tb)

    g_iota = lax.broadcasted_iota(jnp.int32, gs.shape, 0)
    grank = jnp.zeros(gs.shape, jnp.int32)
    for gp in range(N_GROUPS):
        row = gs[gp:gp + 1, :]
        ahead = (row > gs) | ((row == gs) & (gp < g_iota))
        grank = grank + ahead.astype(jnp.int32)
    gmask = grank < TOPK_GROUPS
    emask = jnp.broadcast_to(gmask.reshape(N_GROUPS, 1, tb), c3.shape).reshape(e, tb)
    masked = jnp.where(emask, choice, -jnp.inf)

    e_iota = lax.broadcasted_iota(jnp.int32, masked.shape, 0)
    erank = jnp.zeros(masked.shape, jnp.int32)
    for ep in range(e):
        row = masked[ep:ep + 1, :]
        ahead = (row > masked) | ((row == masked) & (ep < e_iota))
        erank = erank + ahead.astype(jnp.int32)
    sel = (erank < TOP_K) & emask
    self32 = sel.astype(jnp.float32)

    wsel = scores * self32
    wn = wsel / jnp.sum(wsel, axis=0, keepdims=True) * ROUTED_SCALE

    selb = self32.astype(jnp.bfloat16)
    t_r = lax.broadcasted_iota(jnp.int32, (tb, tb), 0)
    t_c = lax.broadcasted_iota(jnp.int32, (tb, tb), 1)
    upper = (t_r <= t_c).astype(jnp.bfloat16)
    incl = jnp.dot(selb, upper, preferred_element_type=jnp.float32)
    carry = carry_ref[...]
    rank_in_e = carry + incl - 1.0
    carry_new = carry + jnp.sum(self32, axis=1, keepdims=True)
    carry_ref[...] = carry_new
    cnt_ref[...] = jnp.broadcast_to(carry_new, cnt_ref.shape).astype(jnp.int32)

    e_r = lax.broadcasted_iota(jnp.int32, (e, e), 0)
    e_c = lax.broadcasted_iota(jnp.int32, (e, e), 1)
    lower = (e_c < e_r).astype(jnp.bfloat16)
    before = jnp.dot(lower, selb, preferred_element_type=jnp.float32)
    e_f = e_iota.astype(jnp.float32)
    idx_rows, pos_rows, w_rows = [], [], []
    for k in range(TOP_K):
        hit = jnp.where(sel & (before == float(k)), 1.0, 0.0)
        idx_rows.append(jnp.sum(hit * e_f, axis=0, keepdims=True))
        pos_rows.append(jnp.sum(hit * rank_in_e, axis=0, keepdims=True))
        w_rows.append(jnp.sum(hit * wn, axis=0, keepdims=True))
    idx_ref[0] = jnp.concatenate(idx_rows, axis=0).astype(jnp.int32)
    pos_ref[0] = jnp.concatenate(pos_rows, axis=0).astype(jnp.int32)
    w_ref[0] = jnp.concatenate(w_rows, axis=0)


def _router(x1, g, mod3, shift_idx, scale_idx, w_router_t, b_router, seq):
    t, d = x1.shape
    e = N_EXPERTS
    tb = MOE_TILE
    assert seq % tb == 0 and d % (2 * V7X_LANES * V7X_SUBLANES) == 0
    slab = d // 2 // V7X_LANES
    per_b = seq // tb
    nt = t // tb
    return pl.pallas_call(
        _router_kernel,
        grid=(nt,),
        in_specs=[pl.BlockSpec((tb, d), lambda i: (i, 0)),
                  pl.BlockSpec((1, d), lambda i: (0, 0)),
                  pl.BlockSpec((1, 1, d), lambda i: ((i // per_b) * 6 + shift_idx, 0, 0)),
                  pl.BlockSpec((1, 1, d), lambda i: ((i // per_b) * 6 + scale_idx, 0, 0)),
                  pl.BlockSpec((e, d), lambda i: (0, 0)),
                  pl.BlockSpec((e, 1), lambda i: (0, 0))],
        out_specs=[pl.BlockSpec((tb, d), lambda i: (i, 0)),
                   pl.BlockSpec((tb * slab, V7X_LANES), lambda i: (i, 0)),
                   pl.BlockSpec((1, TOP_K, tb), lambda i: (i, 0, 0)),
                   pl.BlockSpec((1, TOP_K, tb), lambda i: (i, 0, 0)),
                   pl.BlockSpec((1, TOP_K, tb), lambda i: (i, 0, 0)),
                   pl.BlockSpec((e, V7X_LANES), lambda i: (0, 0))],
        out_shape=[jax.ShapeDtypeStruct((t, d), jnp.bfloat16),
                   jax.ShapeDtypeStruct((t * slab, V7X_LANES), jnp.int32),
                   jax.ShapeDtypeStruct((nt, TOP_K, tb), jnp.int32),
                   jax.ShapeDtypeStruct((nt, TOP_K, tb), jnp.int32),
                   jax.ShapeDtypeStruct((nt, TOP_K, tb), jnp.float32),
                   jax.ShapeDtypeStruct((e, V7X_LANES), jnp.int32)],
        scratch_shapes=[pltpu.VMEM((e, 1), jnp.float32)],
        compiler_params=_params(("arbitrary",)),
        name="router",
    )(x1, g.reshape(1, d), mod3, mod3, w_router_t, b_router.reshape(e, 1))


def _slots_kernel(pstart_ref, idx_ref, pos_ref, o_ref):
    idx = idx_ref[...]
    base = jnp.zeros(idx.shape, jnp.int32)
    for ex in range(N_EXPERTS):
        base = jnp.where(idx == ex, pstart_ref[ex], base)
    o_ref[...] = base + pos_ref[...]


def _slots(pad_start, idx3, pos3):
    nt = idx3.shape[0]
    blk = (1,) + idx3.shape[1:]
    spec = pl.BlockSpec(blk, lambda i, ps: (i, 0, 0))
    return pl.pallas_call(
        _slots_kernel,
        grid_spec=pltpu.PrefetchScalarGridSpec(num_scalar_prefetch=1, grid=(nt,),
                                               in_specs=[spec, spec], out_specs=spec),
        out_shape=jax.ShapeDtypeStruct(idx3.shape, jnp.int32),
        compiler_params=_params(("parallel",)),
        name="slots",
    )(pad_start, idx3, pos3)


def _dispatch_kernel(pstart_ref, pend_ref, hp_ref, slot_hbm, xs_hbm,
                     slot_s, zero_v, sem_i, sem_z, sem_r, *, slab):
    i = pl.program_id(0)
    tb = hp_ref.shape[0] // slab

    def slot_copy():
        return pltpu.make_async_copy(slot_hbm.at[i], slot_s, sem_i)

    slot_copy().start()

    def zero_copy(ex):
        first = pl.multiple_of((pend_ref[ex] - tb) * slab, tb * slab)
        return pltpu.make_async_copy(zero_v, xs_hbm.at[pl.ds(first, tb * slab)], sem_z)

    @pl.when(i == 0)
    def _():
        zero_v[...] = jnp.zeros_like(zero_v)

        def start(ex, carry):
            @pl.when(pend_ref[ex] > pstart_ref[ex])
            def _():
                zero_copy(ex).start()
            return carry

        def wait(ex, carry):
            @pl.when(pend_ref[ex] > pstart_ref[ex])
            def _():
                zero_copy(ex).wait()
            return carry

        lax.fori_loop(0, N_EXPERTS, start, 0)
        lax.fori_loop(0, N_EXPERTS, wait, 0)

    slot_copy().wait()

    def row_copy(t, k):
        return pltpu.make_async_copy(hp_ref.at[_slab_rows(t, slab)],
                                     xs_hbm.at[_slab_rows(slot_s[k, t], slab)], sem_r)

    def start_rows(t, carry):
        for k in range(TOP_K):
            row_copy(t, k).start(priority=k % 2)
        return carry

    lax.fori_loop(0, tb, start_rows, 0, unroll=2)
    for k in range(TOP_K):
        pltpu.make_async_copy(hp_ref, xs_hbm.at[pl.ds(0, tb * slab)], sem_r).wait()


def _dispatch(pad_start, pad_end, h2p, slot3, n_rows, slab):
    tb = MOE_TILE
    t = h2p.shape[0] // slab
    return pl.pallas_call(
        functools.partial(_dispatch_kernel, slab=slab),
        grid_spec=pltpu.PrefetchScalarGridSpec(
            num_scalar_prefetch=2,
            grid=(t // tb,),
            in_specs=[pl.BlockSpec((tb * slab, V7X_LANES), lambda i, ps, pe: (i, 0)),
                      pl.BlockSpec(memory_space=pl.ANY)],
            out_specs=pl.BlockSpec(memory_space=pl.ANY),
            scratch_shapes=[pltpu.SMEM((TOP_K, tb), jnp.int32),
                            pltpu.VMEM((tb * slab, V7X_LANES), jnp.int32),
                            pltpu.SemaphoreType.DMA,
                            pltpu.SemaphoreType.DMA,
                            pltpu.SemaphoreType.DMA],
        ),
        out_shape=jax.ShapeDtypeStruct((n_rows * slab, V7X_LANES), jnp.int32),
        compiler_params=_params(("arbitrary",)),
        name="dispatch",
    )(pad_start, pad_end, h2p, slot3)


def _expert_kernel(blk_e_ref, nused_ref, xs_hbm, wg_ref, wu_ref, wd_ref, ys_hbm,
                   xt, yt, sem_in, sem_out, *, slab):
    j = pl.program_id(0)
    n_used = nused_ref[0]
    tb = xt.shape[1]
    lanes = V7X_LANES
    half = slab * lanes

    def in_copies(tile, buf):
        rows = pl.ds(pl.multiple_of(tile * tb, tb), tb)
        return [pltpu.make_async_copy(xs_hbm.at[rows, s], xt.at[buf, :, pl.ds(s * lanes, lanes)],
                                      sem_in.at[buf]) for s in range(slab)]

    def out_copies(tile, buf):
        rows = pl.ds(pl.multiple_of(tile * tb, tb), tb)
        return [pltpu.make_async_copy(yt.at[buf, :, pl.ds(s * lanes, lanes)], ys_hbm.at[rows, s],
                                      sem_out.at[buf]) for s in range(slab)]

    @pl.when(j == 0)
    def _():
        for c in in_copies(0, 0):
            c.start()

    @pl.when(j < n_used)
    def _():
        cur = j % 2

        @pl.when(j + 1 < n_used)
        def _():
            for c in in_copies(j + 1, 1 - cur):
                c.start()

        for c in in_copies(j, cur):
            c.wait()

        @pl.when(j >= 2)
        def _():
            for c in out_copies(j - 2, cur):
                c.wait()

        lo, hi = _unpack_pair(xt[cur])
        lo = lo.astype(jnp.bfloat16)
        hi = hi.astype(jnp.bfloat16)
        g = (jnp.dot(lo, wg_ref[0, :half, :], preferred_element_type=jnp.float32)
             + jnp.dot(hi, wg_ref[0, half:, :], preferred_element_type=jnp.float32))
        u = (jnp.dot(lo, wu_ref[0, :half, :], preferred_element_type=jnp.float32)
             + jnp.dot(hi, wu_ref[0, half:, :], preferred_element_type=jnp.float32))
        a = (_silu(g) * u).astype(jnp.bfloat16)
        y = jnp.dot(a, wd_ref[0], preferred_element_type=jnp.float32)
        yt[cur] = _pack_pair(y[:, :half], y[:, half:])
        for c in out_copies(j, cur):
            c.start()

        @pl.when(j == n_used - 1)
        def _():
            @pl.when(j >= 1)
            def _():
                for c in out_copies(j - 1, 1 - cur):
                    c.wait()

            for c in out_copies(j, cur):
                c.wait()


def _experts(blk_e, n_used, xs, wg, wu, wd, slab):
    tb = MOE_TILE
    p = xs.shape[0] // slab
    half = slab * V7X_LANES
    d = 2 * half
    f = wg.shape[2]
    nblk = p // tb

    def w_map(i, be, nu):
        return (be[jnp.minimum(i, nu[0] - 1)], 0, 0)

    ys = pl.pallas_call(
        functools.partial(_expert_kernel, slab=slab),
        grid_spec=pltpu.PrefetchScalarGridSpec(
            num_scalar_prefetch=2,
            grid=(nblk,),
            in_specs=[pl.BlockSpec(memory_space=pl.ANY),
                      pl.BlockSpec((1, d, f), w_map),
                      pl.BlockSpec((1, d, f), w_map),
                      pl.BlockSpec((1, f, d), w_map)],
            out_specs=pl.BlockSpec(memory_space=pl.ANY),
            scratch_shapes=[pltpu.VMEM((2, tb, half), jnp.int32),
                            pltpu.VMEM((2, tb, half), jnp.int32),
                            pltpu.SemaphoreType.DMA((2,)),
                            pltpu.SemaphoreType.DMA((2,))],
        ),
        out_shape=jax.ShapeDtypeStruct((p, slab, V7X_LANES), jnp.int32),
        compiler_params=_params(("arbitrary",)),
        name="experts",
    )(blk_e, n_used, xs.reshape(p, slab, V7X_LANES), wg, wu, wd)
    return ys.reshape(xs.shape)


def _combine_kernel(x_ref, h_ref, wt_ref, gt_ref, gf_ref, wsg_ref, wsu_ref, wsd_ref,
                    slot_hbm, ys_hbm, o_ref, slot_s, rows_v, wrep, acc_lo, acc_hi, sem_i, sem_r):
    i = pl.program_id(0)
    n = pl.num_programs(0)
    tb = x_ref.shape[0]
    half = x_ref.shape[1] // 2
    slab = half // V7X_LANES
    per_tile = MOE_TILE // tb
    cur = i % 2
    nxt = 1 - cur

    def slot_copy(step):
        win = pl.ds((step % per_tile) * tb, tb)
        return pltpu.make_async_copy(slot_hbm.at[step // per_tile, :, win], slot_s.at[step % 2],
                                     sem_i.at[step % 2])

    def request_token(buf, t):
        for k in range(TOP_K):
            pltpu.make_async_copy(ys_hbm.at[_slab_rows(slot_s[buf, k, t], slab)],
                                  rows_v.at[buf, k, _slab_rows(t, slab)],
                                  sem_r.at[buf]).start(priority=k % 2)

    def wait_rows(buf):
        for k in range(TOP_K):
            pltpu.make_async_copy(ys_hbm.at[pl.ds(0, tb * slab)], rows_v.at[buf, k], sem_r.at[buf]).wait()

    @pl.when(i == 0)
    def _():
        slot_copy(0).start()
        slot_copy(0).wait()

        def first(t, carry):
            request_token(0, t)
            return carry

        lax.fori_loop(0, tb, first, 0, unroll=2)

        @pl.when(n > 1)
        def _():
            slot_copy(1).start()
            slot_copy(1).wait()

    @pl.when(i + 2 < n)
    def _():
        slot_copy(i + 2).start()

    h = h_ref[...]
    g = jnp.dot(h, wsg_ref[...], preferred_element_type=jnp.float32)
    u = jnp.dot(h, wsu_ref[...], preferred_element_type=jnp.float32)
    a = (_silu(g) * u).astype(jnp.bfloat16)
    y = jnp.dot(a, wsd_ref[...], preferred_element_type=jnp.float32)

    wt = wt_ref[...]
    for k in range(TOP_K):
        wrep[k] = jnp.broadcast_to(wt[:, k:k + 1], (tb, V7X_LANES))

    def token(t, carry, buf, request_next):
        rows = _slab_rows(t, slab)
        lo_acc = jnp.zeros((slab, V7X_LANES), jnp.float32)
        hi_acc = jnp.zeros((slab, V7X_LANES), jnp.float32)
        for k in range(TOP_K):
            wv = jnp.broadcast_to(wrep[k, pl.ds(t, 1), :], (slab, V7X_LANES))
            lo, hi = _unpack_pair(rows_v[buf, k, rows, :])
            lo_acc = lo_acc + wv * lo
            hi_acc = hi_acc + wv * hi
        acc_lo[rows, :] = lo_acc
        acc_hi[rows, :] = hi_acc
        if request_next:
            request_token(1 - buf, t)
        return carry

    for buf in range(2):
        for request_next in (True, False):
            @pl.when((cur == buf) & ((i + 1 < n) == request_next))
            def _():
                wait_rows(buf)
                lax.fori_loop(0, tb, functools.partial(token, buf=buf, request_next=request_next),
                              0, unroll=4)

    @pl.when(i + 2 < n)
    def _():
        slot_copy(i + 2).wait()

    yy = y + jnp.concatenate([_slab_load(acc_lo, tb, slab), _slab_load(acc_hi, tb, slab)], axis=1)
    xo = x_ref[...] + gt_ref[0] * yy
    ms = jnp.mean(xo * xo, axis=-1, keepdims=True)
    o_ref[...] = xo * lax.rsqrt(ms + EPS) * gf_ref[...]


def _combine(x1, h2, w_t, mod3, gate_idx, g_final, wsg, wsu, wsd, slot3, ys, seq):
    t, d = x1.shape
    tb = COMBINE_TILE
    assert MOE_TILE % tb == 0 and seq % tb == 0
    per_b = seq // tb
    half = d // 2
    f = wsg.shape[1]
    once = pl.Buffered(1)
    return pl.pallas_call(
        _combine_kernel,
        grid=(t // tb,),
        in_specs=[pl.BlockSpec((tb, d), lambda i: (i, 0)),
                  pl.BlockSpec((tb, d), lambda i: (i, 0)),
                  pl.BlockSpec((tb, TOP_K), lambda i: (i, 0)),
                  pl.BlockSpec((1, 1, d), lambda i: ((i // per_b) * 6 + gate_idx, 0, 0)),
                  pl.BlockSpec((1, d), lambda i: (0, 0)),
                  pl.BlockSpec((d, f), lambda i: (0, 0), pipeline_mode=once),
                  pl.BlockSpec((d, f), lambda i: (0, 0), pipeline_mode=once),
                  pl.BlockSpec((f, d), lambda i: (0, 0), pipeline_mode=once),
                  pl.BlockSpec(memory_space=pl.ANY),
                  pl.BlockSpec(memory_space=pl.ANY)],
        out_specs=pl.BlockSpec((tb, d), lambda i: (i, 0)),
        scratch_shapes=[pltpu.SMEM((2, TOP_K, tb), jnp.int32),
                        pltpu.VMEM((2, TOP_K, tb * half // V7X_LANES, V7X_LANES), jnp.int32),
                        pltpu.VMEM((TOP_K, tb, V7X_LANES), jnp.float32),
                        pltpu.VMEM((tb * half // V7X_LANES, V7X_LANES), jnp.float32),
                        pltpu.VMEM((tb * half // V7X_LANES, V7X_LANES), jnp.float32),
                        pltpu.SemaphoreType.DMA((2,)),
                        pltpu.SemaphoreType.DMA((2,))],
        out_shape=jax.ShapeDtypeStruct((t, d), jnp.float32),
        compiler_params=_params(("arbitrary",)),
        name="combine",
    )(x1, h2, w_t, mod3, g_final.reshape(1, d), wsg, wsu, wsd, slot3, ys)


def _in_layout(d):
    qw = ATT_HEADS * ATT_HEAD_DIM
    kvw = ATT_KV_HEADS * ATT_HEAD_DIM
    rqk = RET_HEADS * RET_QK_DIM
    rv = RET_HEADS * RET_V_DIM
    order = [("qa", qw), ("ka", kvw), ("va", kvw), ("qr", rqk), ("kr", rqk),
             ("vr", rv), ("gr", rv), ("ga", d), ("gb", d)]
    dst = {}
    off = 0
    for name, width in order:
        dst[name] = off
        off += width
    return dst


def kernel(x, c, w_ada, b_ada, g_norm_mix, w_in, attn_sinks, w_attn_out, w_ret_out, w_o, g_norm_ffn,
           w_router, b_router, w_gate, w_up, w_down, w_sh_gate, w_sh_up, w_sh_down, g_norm_final):
    b, s, d = x.shape
    t = b * s
    depth = w_ada.shape[0]
    bf = jnp.bfloat16
    dst = _in_layout(d)

    c_pad = jnp.zeros((V7X_SUBLANES, d), jnp.float32).at[:b].set(c)
    x2 = x.reshape(t, d)
    for l in range(depth):
        mod = _ada(c_pad, w_ada[l], b_ada[l])
        mod3 = mod[:b].reshape(b * 6, 1, d)

        h = _norm_mod(x2.reshape(b, s, d), g_norm_mix[l], mod3, 0, 1)
        proj, (wg_b, wu_b, wd_b) = _in_proj(h.reshape(t, d), w_in[l].astype(bf),
                                            (w_gate[l], w_up[l], w_down[l]), 1024, 768)
        proj3 = proj.reshape(b, s, proj.shape[1])
        attn = _attention(proj3, attn_sinks[l], dst["qa"], dst["ka"], dst["va"])
        ret = _retention(proj3, dst["qr"], dst["kr"], dst["vr"], dst["gr"])
        mix = _mix(attn.reshape(t, -1), ret.reshape(t, -1), w_attn_out[l].astype(bf),
                   w_ret_out[l].astype(bf), proj, dst["ga"], dst["gb"])
        x1 = _out_resid(mix, w_o[l].astype(bf), x2, mod3, 2, s)

        h2, h2p, idx3, pos3, w3, cnt = _router(x1, g_norm_ffn[l], mod3, 3, 4,
                                               w_router[l].T.astype(bf), b_router[l], s)
        counts = cnt[:, 0]
        tile = MOE_TILE
        padded = (counts + tile - 1) // tile * tile
        pad_end = jnp.cumsum(padded).astype(jnp.int32)
        pad_start = (pad_end - padded).astype(jnp.int32)
        n_blocks = (t * TOP_K) // tile + N_EXPERTS
        n_used = (pad_end[-1] // tile).reshape(1).astype(jnp.int32)
        blk_first = jnp.arange(n_blocks, dtype=jnp.int32) * tile
        blk_e = jnp.minimum(jnp.sum((pad_end[None, :] <= blk_first[:, None]).astype(jnp.int32), axis=1),
                            N_EXPERTS - 1)
        slab = d // 2 // V7X_LANES
        slot3 = _slots(pad_start, idx3, pos3)
        xs = _dispatch(pad_start, pad_end, h2p, slot3, n_blocks * tile, slab)
        ys = _experts(blk_e, n_used, xs, wg_b, wu_b, wd_b, slab)
        w_t = w3.transpose(0, 2, 1).reshape(t, TOP_K)
        is_last = l == depth - 1
        assert is_last, "the final norm is fused into the last layer's combine"
        x2 = _combine(x1, h2, w_t, mod3, 5, g_norm_final, w_sh_gate[l].astype(bf),
                      w_sh_up[l].astype(bf), w_sh_down[l].astype(bf), slot3, ys, s)
    return x2.reshape(b, s, d)
```

```python
import functools
import math

import jax
import jax.numpy as jnp
from jax import lax
from jax.experimental import pallas as pl
from jax.experimental.pallas import tpu as pltpu

ATT_HEADS = 32
ATT_KV_HEADS = 4
ATT_HEAD_DIM = 64
WINDOW = 128
ATT_BLOCK = 128
RET_HEADS = 8
RET_QK_DIM = 256
RET_V_DIM = 512
RET_CHUNK = 128
N_EXPERTS = 64
N_GROUPS = 8
TOPK_GROUPS = 4
TOP_K = 8
ROUTED_SCALE = 2.5
EPS = 1e-6

V7X_LANES = 128
V7X_SUBLANES = 8
V7X_VMEM_LIMIT_BYTES = 60000 * 1024

MOE_TILE = 256
COMBINE_TILE = 128
NEG_BIG = -1e30


def _div_block(n, target, align):
    best = None
    b = align
    while b <= min(n, target):
        if n % b == 0:
            best = b
        b += align
    assert best is not None, (n, target, align)
    return best


def _params(semantics):
    return pltpu.CompilerParams(dimension_semantics=semantics,
                                vmem_limit_bytes=V7X_VMEM_LIMIT_BYTES)


def _sigmoid(v):
    return 1.0 / (1.0 + jnp.exp(-v))


def _silu(v):
    return v * _sigmoid(v)


def _pack_pair(lo, hi):
    return pltpu.pack_elementwise([lo, hi], packed_dtype=jnp.bfloat16)


def _unpack_pair(p):
    lo = pltpu.unpack_elementwise(p, index=0, packed_dtype=jnp.bfloat16, unpacked_dtype=jnp.float32)
    hi = pltpu.unpack_elementwise(p, index=1, packed_dtype=jnp.bfloat16, unpacked_dtype=jnp.float32)
    return lo, hi


def _slab_load(ref, n_rows, slab):
    return jnp.concatenate([ref[pl.ds(s, n_rows, stride=slab), :] for s in range(slab)], axis=1)


def _slab_store(ref, val, n_rows, slab):
    for s in range(slab):
        ref[pl.ds(s, n_rows, stride=slab), :] = val[:, s * V7X_LANES:(s + 1) * V7X_LANES]


def _slab_rows(r, slab):
    return pl.ds(pl.multiple_of(r * slab, slab), slab)


def _ada_kernel(c_ref, w_ref, b_ref, o_ref):
    cs = _silu(c_ref[...]).astype(jnp.bfloat16)
    o_ref[...] = jnp.dot(cs, w_ref[...].astype(jnp.bfloat16),
                         preferred_element_type=jnp.float32) + b_ref[...]


def _ada(c_pad, w, b):
    m, d = c_pad.shape
    n = w.shape[1]
    tn = _div_block(n, 512, V7X_LANES)
    return pl.pallas_call(
        _ada_kernel,
        grid=(n // tn,),
        in_specs=[pl.BlockSpec((m, d), lambda j: (0, 0)),
                  pl.BlockSpec((d, tn), lambda j: (0, j)),
                  pl.BlockSpec((1, tn), lambda j: (0, j))],
        out_specs=pl.BlockSpec((m, tn), lambda j: (0, j)),
        out_shape=jax.ShapeDtypeStruct((m, n), jnp.float32),
        compiler_params=_params(("parallel",)),
        name="ada",
    )(c_pad, w, b.reshape(1, n))


def _norm_mod_kernel(x_ref, g_ref, sh_ref, sc_ref, o_ref):
    x = x_ref[0]
    ms = jnp.mean(x * x, axis=-1, keepdims=True)
    y = x * lax.rsqrt(ms + EPS) * g_ref[...]
    o_ref[0] = (y * (1.0 + sc_ref[0]) + sh_ref[0]).astype(o_ref.dtype)


def _norm_mod(x3, g, mod3, shift_idx, scale_idx):
    b, s, d = x3.shape
    ts = _div_block(s, 256, V7X_SUBLANES)
    return pl.pallas_call(
        _norm_mod_kernel,
        grid=(b, s // ts),
        in_specs=[pl.BlockSpec((1, ts, d), lambda bi, i: (bi, i, 0)),
                  pl.BlockSpec((1, d), lambda bi, i: (0, 0)),
                  pl.BlockSpec((1, 1, d), lambda bi, i: (bi * 6 + shift_idx, 0, 0)),
                  pl.BlockSpec((1, 1, d), lambda bi, i: (bi * 6 + scale_idx, 0, 0))],
        out_specs=pl.BlockSpec((1, ts, d), lambda bi, i: (bi, i, 0)),
        out_shape=jax.ShapeDtypeStruct((b, s, d), jnp.bfloat16),
        compiler_params=_params(("parallel", "parallel")),
        name="norm_mod",
    )(x3, g.reshape(1, d), mod3, mod3)


def _in_proj_kernel(a_ref, b_ref, *rest, n_side, n_cast, n_j):
    src = rest[:n_side]
    o_ref = rest[n_side]
    dst = rest[n_side + 1:]
    o_ref[...] = jnp.dot(a_ref[...], b_ref[...],
                         preferred_element_type=jnp.float32).astype(o_ref.dtype)
    step = pl.program_id(0) * n_j + pl.program_id(1)

    @pl.when(step < n_cast)
    def _():
        for s_ref, d_ref in zip(src, dst):
            d_ref[...] = s_ref[...].astype(d_ref.dtype)


def _in_proj(a, b, side, tm_target, tn_target):
    m, k = a.shape
    n = b.shape[1]
    tm = _div_block(m, tm_target, V7X_SUBLANES)
    tn = _div_block(n, tn_target, V7X_LANES)
    n_i, n_j = m // tm, n // tn
    n_cast = 1 << ((n_i * n_j).bit_length() - 1)
    chunked = []
    for w in side:
        rows = w.size // w.shape[-1]
        assert rows % (n_cast * V7X_SUBLANES * 2) == 0, (w.shape, n_cast)
        chunked.append(w.reshape(n_cast, rows // n_cast, w.shape[-1]))

    def side_map(i, j):
        return (jnp.minimum(i * n_j + j, n_cast - 1), 0, 0)

    side_specs = [pl.BlockSpec((1,) + c.shape[1:], side_map) for c in chunked]
    outs = pl.pallas_call(
        functools.partial(_in_proj_kernel, n_side=len(side), n_cast=n_cast, n_j=n_j),
        grid=(n_i, n_j),
        in_specs=[pl.BlockSpec((tm, k), lambda i, j: (i, 0)),
                  pl.BlockSpec((k, tn), lambda i, j: (0, j))] + side_specs,
        out_specs=[pl.BlockSpec((tm, tn), lambda i, j: (i, j))] + side_specs,
        out_shape=[jax.ShapeDtypeStruct((m, n), jnp.bfloat16)]
        + [jax.ShapeDtypeStruct(c.shape, jnp.bfloat16) for c in chunked],
        compiler_params=_params(("arbitrary", "arbitrary")),
        name="in_proj",
    )(a, b, *chunked)
    return outs[0], [o.reshape(w.shape) for o, w in zip(outs[1:], side)]


def _attn_kernel(sink_ref, q_ref, kc_ref, kp_ref, vc_ref, vp_ref, bias_ref, o_ref):
    i = pl.program_id(1)
    blk = ATT_BLOCK
    hd = ATT_HEAD_DIM
    group = ATT_HEADS // ATT_KV_HEADS
    pairs = group // 2
    nt = (((1,), (1,)), ((), ()))
    tn = (((0,), (0,)), ((), ()))
    zpad = jnp.zeros((2 * blk, hd), jnp.bfloat16)

    def scores(h, par):
        sl = slice(h * hd, (h + 1) * hd)
        k2 = jnp.concatenate([kp_ref[0, :, sl], kc_ref[0, :, sl]], axis=0) * (hd ** -0.5)
        qp = jnp.concatenate([q_ref[0, :, (h * pairs + p) * 2 * hd:(h * pairs + p + 1) * 2 * hd]
                              for p in range(pairs)], axis=0)
        kz = jnp.concatenate([k2, zpad] if par == 0 else [zpad, k2], axis=1)
        return lax.dot_general(kz, qp, nt, preferred_element_type=jnp.float32)

    def softmax(s, h, par):
        s = s + bias_ref[h * 2 + par]
        s = jnp.concatenate([jnp.where(i == 0, NEG_BIG, s[:blk]), s[blk:]], axis=0)
        sink = jnp.concatenate([jnp.full((1, blk), sink_ref[h * group + 2 * p + par], jnp.float32)
                                for p in range(pairs)], axis=1)
        m = jnp.maximum(s.max(0, keepdims=True), sink)
        pr = jnp.exp(s - m)
        denom = pr.sum(0, keepdims=True) + jnp.exp(sink - m)
        return pr.astype(jnp.bfloat16), 1.0 / denom

    def values(pr, inv, h, par):
        sl = slice(h * hd, (h + 1) * hd)
        v2 = jnp.concatenate([vp_ref[0, :, sl], vc_ref[0, :, sl]], axis=0)
        vz = jnp.concatenate([v2, zpad] if par == 0 else [zpad, v2], axis=1)
        return lax.dot_general(vz, pr, tn, preferred_element_type=jnp.float32) * inv

    items = [(h, par) for h in range(ATT_KV_HEADS) for par in range(2)]
    s_of, p_of, acc = {}, {}, {}
    for n in range(len(items) + 2):
        if n < len(items):
            s_of[n] = scores(*items[n])
        if 1 <= n <= len(items):
            p_of[n - 1] = softmax(s_of.pop(n - 1), *items[n - 1])
        if n >= 2:
            h, par = items[n - 2]
            o = values(*p_of.pop(n - 2), h, par)
            acc[h] = o if par == 0 else acc[h] + o
            if par == 1:
                out = acc.pop(h)
                for p in range(pairs):
                    o_ref[0, :, (h * pairs + p) * 2 * hd:(h * pairs + p + 1) * 2 * hd] = (
                        out[:, p * blk:(p + 1) * blk].T.astype(o_ref.dtype))


def _attn_bias():
    blk = ATT_BLOCK
    group = ATT_HEADS // ATT_KV_HEADS
    pairs = group // 2
    qi = jnp.arange(blk)[:, None]
    kj = jnp.arange(2 * blk)[None, :]
    dist = qi + blk - kj
    valid = (dist >= 0) & (dist < WINDOW)
    slopes = jnp.exp2(-8.0 * jnp.arange(1, ATT_HEADS + 1, dtype=jnp.float32) / ATT_HEADS)
    slopes = slopes.reshape(ATT_KV_HEADS, pairs, 2)
    bias = jnp.where(valid, -slopes[..., None, None] * dist.astype(jnp.float32), NEG_BIG)
    return bias.transpose(0, 2, 4, 1, 3).reshape(ATT_KV_HEADS * 2, 2 * blk, pairs * blk)


def _attention(proj3, sinks, q_off, k_off, v_off):
    b, s, _ = proj3.shape
    qw = ATT_HEADS * ATT_HEAD_DIM
    kvw = ATT_KV_HEADS * ATT_HEAD_DIM
    nb = s // ATT_BLOCK
    group = ATT_HEADS // ATT_KV_HEADS
    assert q_off % qw == 0 and k_off % kvw == 0 and v_off % kvw == 0
    assert group % 2 == 0 and 2 * ATT_HEAD_DIM == V7X_LANES and WINDOW == ATT_BLOCK
    assert 4 ** round(math.log(ATT_HEAD_DIM, 4)) == ATT_HEAD_DIM, "score scale must be a power of two"
    qb, kb, vb = q_off // qw, k_off // kvw, v_off // kvw
    bias = _attn_bias()
    return pl.pallas_call(
        _attn_kernel,
        grid_spec=pltpu.PrefetchScalarGridSpec(
            num_scalar_prefetch=1,
            grid=(b, nb),
            in_specs=[pl.BlockSpec((1, ATT_BLOCK, qw), lambda bi, i, sk: (bi, i, qb)),
                      pl.BlockSpec((1, ATT_BLOCK, kvw), lambda bi, i, sk: (bi, i, kb)),
                      pl.BlockSpec((1, ATT_BLOCK, kvw), lambda bi, i, sk: (bi, jnp.maximum(i - 1, 0), kb)),
                      pl.BlockSpec((1, ATT_BLOCK, kvw), lambda bi, i, sk: (bi, i, vb)),
                      pl.BlockSpec((1, ATT_BLOCK, kvw), lambda bi, i, sk: (bi, jnp.maximum(i - 1, 0), vb)),
                      pl.BlockSpec(bias.shape, lambda bi, i, sk: (0, 0, 0), pipeline_mode=pl.Buffered(1))],
            out_specs=pl.BlockSpec((1, ATT_BLOCK, qw), lambda bi, i, sk: (bi, i, 0)),
        ),
        out_shape=jax.ShapeDtypeStruct((b, s, qw), jnp.bfloat16),
        compiler_params=_params(("parallel", "parallel")),
        name="attention",
    )(sinks, proj3, proj3, proj3, proj3, proj3, bias)


def _ret_kernel(q_ref, k_ref, v_ref, gr_ref, mask_ref, qd_ref, kd_ref, cd_ref, o_ref, state_ref):
    c = pl.program_id(1)

    @pl.when(c == 0)
    def _():
        state_ref[...] = jnp.zeros_like(state_ref)

    nt = (((1,), (1,)), ((), ()))
    tn = (((0,), (0,)), ((), ()))
    def decayed(bi):
        q = q_ref[bi]
        k = k_ref[bi]
        attn = lax.dot_general(q, k, nt, preferred_element_type=jnp.float32) * mask_ref[0]
        kd = (k.astype(jnp.float32) * kd_ref[0]).astype(k.dtype)
        return attn.astype(jnp.bfloat16), kd

    def recur(bi, attn, kd):
        q = q_ref[bi]
        v = v_ref[bi]
        intra = jnp.dot(attn, v, preferred_element_type=jnp.float32)
        state = state_ref[bi]
        inter = jnp.dot(q, state.astype(q.dtype), preferred_element_type=jnp.float32) * qd_ref[0]
        state_ref[bi] = state * cd_ref[0] + lax.dot_general(kd, v, tn, preferred_element_type=jnp.float32)
        return intra + inter

    def finish(bi, o):
        mu = jnp.mean(o, axis=-1, keepdims=True)
        oc = o - mu
        var = jnp.mean(oc * oc, axis=-1, keepdims=True)
        y = oc * lax.rsqrt(var + EPS)
        o_ref[bi] = (_silu(gr_ref[bi].astype(jnp.float32)) * y).astype(o_ref.dtype)

    nb = q_ref.shape[0]
    a_of, o_of = {}, {}
    for n in range(nb + 2):
        if n < nb:
            a_of[n] = decayed(n)
        if 1 <= n <= nb:
            o_of[n - 1] = recur(n - 1, *a_of.pop(n - 1))
        if n >= 2:
            finish(n - 2, o_of.pop(n - 2))


def _retention(proj3, q_off, k_off, v_off, g_off):
    b, s, _ = proj3.shape
    dk, dv, ch = RET_QK_DIM, RET_V_DIM, RET_CHUNK
    assert q_off % dk == 0 and k_off % dk == 0 and v_off % dv == 0 and g_off % dv == 0
    qb, kb, vb, gb = q_off // dk, k_off // dk, v_off // dv, g_off // dv
    n = s // ch
    log_g = jnp.log1p(-jnp.exp2(-5.0 - jnp.arange(RET_HEADS, dtype=jnp.float32)))
    pos = jnp.arange(ch, dtype=jnp.float32)
    rel = pos[:, None] - pos[None, :]
    scale = dk ** -0.5
    mask = jnp.where(rel[None] >= 0, jnp.exp(rel[None] * log_g[:, None, None]), 0.0) * scale
    q_decay = jnp.exp((pos[None, :, None] + 1.0) * log_g[:, None, None])
    k_decay = jnp.exp((ch - 1.0 - pos[None, :, None]) * log_g[:, None, None]) * scale
    c_decay = jnp.exp(ch * log_g)[:, None, None]
    return pl.pallas_call(
        _ret_kernel,
        grid=(RET_HEADS, n),
        in_specs=[pl.BlockSpec((b, ch, dk), lambda h, c: (0, c, qb + h)),
                  pl.BlockSpec((b, ch, dk), lambda h, c: (0, c, kb + h)),
                  pl.BlockSpec((b, ch, dv), lambda h, c: (0, c, vb + h)),
                  pl.BlockSpec((b, ch, dv), lambda h, c: (0, c, gb + h)),
                  pl.BlockSpec((1, ch, ch), lambda h, c: (h, 0, 0)),
                  pl.BlockSpec((1, ch, 1), lambda h, c: (h, 0, 0)),
                  pl.BlockSpec((1, ch, 1), lambda h, c: (h, 0, 0)),
                  pl.BlockSpec((1, 1, 1), lambda h, c: (h, 0, 0))],
        out_specs=pl.BlockSpec((b, ch, dv), lambda h, c: (0, c, h)),
        out_shape=jax.ShapeDtypeStruct((b, s, RET_HEADS * dv), jnp.bfloat16),
        scratch_shapes=[pltpu.VMEM((b, dk, dv), jnp.float32)],
        compiler_params=_params(("parallel", "arbitrary")),
        name="retention",
    )(proj3, proj3, proj3, proj3, mask, q_decay, k_decay, c_decay)


def _mix_kernel(a_ref, r_ref, wa_ref, wr_ref, ga_ref, gb_ref, o_ref):
    ya = jnp.dot(a_ref[...], wa_ref[...], preferred_element_type=jnp.float32)
    yr = jnp.dot(r_ref[...], wr_ref[...], preferred_element_type=jnp.float32)
    ga = _sigmoid(ga_ref[...].astype(jnp.float32))
    gb = _sigmoid(gb_ref[...].astype(jnp.float32))
    o_ref[...] = (ga * ya + gb * yr).astype(o_ref.dtype)


def _mix(attn2, ret2, wa, wr, proj2, ga_off, gb_off):
    m, ka = attn2.shape
    kr = ret2.shape[1]
    d = wa.shape[1]
    tm = _div_block(m, 512, V7X_SUBLANES)
    tn = _div_block(d, 512, V7X_LANES)
    assert ga_off % tn == 0 and gb_off % tn == 0
    gab, gbb = ga_off // tn, gb_off // tn
    return pl.pallas_call(
        _mix_kernel,
        grid=(m // tm, d // tn),
        in_specs=[pl.BlockSpec((tm, ka), lambda i, j: (i, 0)),
                  pl.BlockSpec((tm, kr), lambda i, j: (i, 0)),
                  pl.BlockSpec((ka, tn), lambda i, j: (0, j)),
                  pl.BlockSpec((kr, tn), lambda i, j: (0, j)),
                  pl.BlockSpec((tm, tn), lambda i, j: (i, gab + j)),
                  pl.BlockSpec((tm, tn), lambda i, j: (i, gbb + j))],
        out_specs=pl.BlockSpec((tm, tn), lambda i, j: (i, j)),
        out_shape=jax.ShapeDtypeStruct((m, d), jnp.bfloat16),
        compiler_params=_params(("parallel", "parallel")),
        name="mix",
    )(attn2, ret2, wa, wr, proj2, proj2)


def _resid_kernel(a_ref, w_ref, x_ref, gt_ref, o_ref):
    y = jnp.dot(a_ref[...], w_ref[...], preferred_element_type=jnp.float32)
    o_ref[...] = x_ref[...] + gt_ref[0] * y


def _out_resid(mix2, w, x2, mod3, gate_idx, seq):
    m, k = mix2.shape
    d = w.shape[1]
    tm = _div_block(seq, 1024, V7X_SUBLANES)
    tn = _div_block(d, 512, V7X_LANES)
    per_b = seq // tm
    return pl.pallas_call(
        _resid_kernel,
        grid=(m // tm, d // tn),
        in_specs=[pl.BlockSpec((tm, k), lambda i, j: (i, 0)),
                  pl.BlockSpec((k, tn), lambda i, j: (0, j)),
                  pl.BlockSpec((tm, tn), lambda i, j: (i, j)),
                  pl.BlockSpec((1, 1, tn), lambda i, j: ((i // per_b) * 6 + gate_idx, 0, j))],
        out_specs=pl.BlockSpec((tm, tn), lambda i, j: (i, j)),
        out_shape=jax.ShapeDtypeStruct((m, d), jnp.float32),
        compiler_params=_params(("parallel", "parallel")),
        name="out_resid",
    )(mix2, w, x2, mod3)


def _router_kernel(x_ref, g_ref, sh_ref, sc_ref, wr_ref, br_ref,
                   h_ref, hp_ref, idx_ref, pos_ref, w_ref, cnt_ref, carry_ref):
    i = pl.program_id(0)
    e = N_EXPERTS
    per_g = e // N_GROUPS
    tb = x_ref.shape[0]

    @pl.when(i == 0)
    def _():
        carry_ref[...] = jnp.zeros_like(carry_ref)

    x = x_ref[...]
    ms = jnp.mean(x * x, axis=-1, keepdims=True)
    h = x * lax.rsqrt(ms + EPS) * g_ref[...]
    h = h * (1.0 + sc_ref[0]) + sh_ref[0]
    hb = h.astype(jnp.bfloat16)
    h_ref[...] = hb
    half = h.shape[1] // 2
    _slab_store(hp_ref, _pack_pair(h[:, :half], h[:, half:]), tb, half // V7X_LANES)

    nt = (((1,), (1,)), ((), ()))
    logits = lax.dot_general(wr_ref[...], hb, nt, preferred_element_type=jnp.float32)
    scores = _sigmoid(logits)
    choice = scores + br_ref[...]

    c3 = choice.reshape(N_GROUPS, per_g, tb)
    j_iota = lax.broadcasted_iota(jnp.int32, c3.shape, 1).astype(jnp.float32)
    m1 = c3.max(axis=1, keepdims=True)
    first = jnp.min(jnp.where(c3 == m1, j_iota, float(per_g)), axis=1, keepdims=True)
    m2 = jnp.where(j_iota == first, -jnp.inf, c3).max(axis=1, keepdims=True)
    gs = (m1 + m2).reshape(N_GROUPS, tb)

    g_iota = lax.broadcasted_iota(jnp.int32, gs.shape, 0)
    grank = jnp.zeros(gs.shape, jnp.int32)
    for gp in range(N_GROUPS):
        row = gs[gp:gp + 1, :]
        ahead = (row > gs) | ((row == gs) & (gp < g_iota))
        grank = grank + ahead.astype(jnp.int32)
    gmask = grank < TOPK_GROUPS
    emask = jnp.broadcast_to(gmask.reshape(N_GROUPS, 1, tb), c3.shape).reshape(e, tb)
    masked = jnp.where(emask, choice, -jnp.inf)

    e_iota = lax.broadcasted_iota(jnp.int32, masked.shape, 0)
    erank = jnp.zeros(masked.shape, jnp.int32)
    for ep in range(e):
        row = masked[ep:ep + 1, :]
        ahead = (row > masked) | ((row == masked) & (ep < e_iota))
        erank = erank + ahead.astype(jnp.int32)
    sel = (erank < TOP_K) & emask
    self32 = sel.astype(jnp.float32)

    wsel = scores * self32
    wn = wsel / jnp.sum(wsel, axis=0, keepdims=True) * ROUTED_SCALE

    selb = self32.astype(jnp.bfloat16)
    t_r = lax.broadcasted_iota(jnp.int32, (tb, tb), 0)
    t_c = lax.broadcasted_iota(jnp.int32, (tb, tb), 1)
    upper = (t_r <= t_c).astype(jnp.bfloat16)
    incl = jnp.dot(selb, upper, preferred_element_type=jnp.float32)
    carry = carry_ref[...]
    rank_in_e = carry + incl - 1.0
    carry_new = carry + jnp.sum(self32, axis=1, keepdims=True)
    carry_ref[...] = carry_new
    cnt_ref[...] = jnp.broadcast_to(carry_new, cnt_ref.shape).astype(jnp.int32)

    e_r = lax.broadcasted_iota(jnp.int32, (e, e), 0)
    e_c = lax.broadcasted_iota(jnp.int32, (e, e), 1)
    lower = (e_c < e_r).astype(jnp.bfloat16)
    before = jnp.dot(lower, selb, preferred_element_type=jnp.float32)
    e_f = e_iota.astype(jnp.float32)
    idx_rows, pos_rows, w_rows = [], [], []
    for k in range(TOP_K):
        hit = jnp.where(sel & (before == float(k)), 1.0, 0.0)
        idx_rows.append(jnp.sum(hit * e_f, axis=0, keepdims=True))
        pos_rows.append(jnp.sum(hit * rank_in_e, axis=0, keepdims=True))
        w_rows.append(jnp.sum(hit * wn, axis=0, keepdims=True))
    idx_ref[0] = jnp.concatenate(idx_rows, axis=0).astype(jnp.int32)
    pos_ref[0] = jnp.concatenate(pos_rows, axis=0).astype(jnp.int32)
    w_ref[0] = jnp.concatenate(w_rows, axis=0)


def _router(x1, g, mod3, shift_idx, scale_idx, w_router_t, b_router, seq):
    t, d = x1.shape
    e = N_EXPERTS
    tb = MOE_TILE
    assert seq % tb == 0 and d % (2 * V7X_LANES * V7X_SUBLANES) == 0
    slab = d // 2 // V7X_LANES
    per_b = seq // tb
    nt = t // tb
    return pl.pallas_call(
        _router_kernel,
        grid=(nt,),
        in_specs=[pl.BlockSpec((tb, d), lambda i: (i, 0)),
                  pl.BlockSpec((1, d), lambda i: (0, 0)),
                  pl.BlockSpec((1, 1, d), lambda i: ((i // per_b) * 6 + shift_idx, 0, 0)),
                  pl.BlockSpec((1, 1, d), lambda i: ((i // per_b) * 6 + scale_idx, 0, 0)),
                  pl.BlockSpec((e, d), lambda i: (0, 0)),
                  pl.BlockSpec((e, 1), lambda i: (0, 0))],
        out_specs=[pl.BlockSpec((tb, d), lambda i: (i, 0)),
                   pl.BlockSpec((tb * slab, V7X_LANES), lambda i: (i, 0)),
                   pl.BlockSpec((1, TOP_K, tb), lambda i: (i, 0, 0)),
                   pl.BlockSpec((1, TOP_K, tb), lambda i: (i, 0, 0)),
                   pl.BlockSpec((1, TOP_K, tb), lambda i: (i, 0, 0)),
                   pl.BlockSpec((e, V7X_LANES), lambda i: (0, 0))],
        out_shape=[jax.ShapeDtypeStruct((t, d), jnp.bfloat16),
                   jax.ShapeDtypeStruct((t * slab, V7X_LANES), jnp.int32),
                   jax.ShapeDtypeStruct((nt, TOP_K, tb), jnp.int32),
                   jax.ShapeDtypeStruct((nt, TOP_K, tb), jnp.int32),
                   jax.ShapeDtypeStruct((nt, TOP_K, tb), jnp.float32),
                   jax.ShapeDtypeStruct((e, V7X_LANES), jnp.int32)],
        scratch_shapes=[pltpu.VMEM((e, 1), jnp.float32)],
        compiler_params=_params(("arbitrary",)),
        name="router",
    )(x1, g.reshape(1, d), mod3, mod3, w_router_t, b_router.reshape(e, 1))


def _slots_kernel(pstart_ref, idx_ref, pos_ref, o_ref):
    idx = idx_ref[...]
    base = jnp.zeros(idx.shape, jnp.int32)
    for ex in range(N_EXPERTS):
        base = jnp.where(idx == ex, pstart_ref[ex], base)
    o_ref[...] = base + pos_ref[...]


def _slots(pad_start, idx3, pos3):
    nt = idx3.shape[0]
    blk = (1,) + idx3.shape[1:]
    spec = pl.BlockSpec(blk, lambda i, ps: (i, 0, 0))
    return pl.pallas_call(
        _slots_kernel,
        grid_spec=pltpu.PrefetchScalarGridSpec(num_scalar_prefetch=1, grid=(nt,),
                                               in_specs=[spec, spec], out_specs=spec),
        out_shape=jax.ShapeDtypeStruct(idx3.shape, jnp.int32),
        compiler_params=_params(("parallel",)),
        name="slots",
    )(pad_start, idx3, pos3)


def _dispatch_kernel(pstart_ref, pend_ref, hp_ref, slot_hbm, xs_hbm,
                     slot_s, zero_v, sem_i, sem_z, sem_r, *, slab):
    i = pl.program_id(0)
    tb = hp_ref.shape[0] // slab

    def slot_copy():
        return pltpu.make_async_copy(slot_hbm.at[i], slot_s, sem_i)

    slot_copy().start()

    def zero_copy(ex):
        first = pl.multiple_of((pend_ref[ex] - tb) * slab, tb * slab)
        return pltpu.make_async_copy(zero_v, xs_hbm.at[pl.ds(first, tb * slab)], sem_z)

    @pl.when(i == 0)
    def _():
        zero_v[...] = jnp.zeros_like(zero_v)

        def start(ex, carry):
            @pl.when(pend_ref[ex] > pstart_ref[ex])
            def _():
                zero_copy(ex).start()
            return carry

        def wait(ex, carry):
            @pl.when(pend_ref[ex] > pstart_ref[ex])
            def _():
                zero_copy(ex).wait()
            return carry

        lax.fori_loop(0, N_EXPERTS, start, 0)
        lax.fori_loop(0, N_EXPERTS, wait, 0)

    slot_copy().wait()

    def row_copy(t, k):
        return pltpu.make_async_copy(hp_ref.at[_slab_rows(t, slab)],
                                     xs_hbm.at[_slab_rows(slot_s[k, t], slab)], sem_r)

    def start_rows(t, carry):
        for k in range(TOP_K):
            row_copy(t, k).start(priority=k % 2)
        return carry

    lax.fori_loop(0, tb, start_rows, 0, unroll=2)
    for k in range(TOP_K):
        pltpu.make_async_copy(hp_ref, xs_hbm.at[pl.ds(0, tb * slab)], sem_r).wait()


def _dispatch(pad_start, pad_end, h2p, slot3, n_rows, slab):
    tb = MOE_TILE
    t = h2p.shape[0] // slab
    return pl.pallas_call(
        functools.partial(_dispatch_kernel, slab=slab),
        grid_spec=pltpu.PrefetchScalarGridSpec(
            num_scalar_prefetch=2,
            grid=(t // tb,),
            in_specs=[pl.BlockSpec((tb * slab, V7X_LANES), lambda i, ps, pe: (i, 0)),
                      pl.BlockSpec(memory_space=pl.ANY)],
            out_specs=pl.BlockSpec(memory_space=pl.ANY),
            scratch_shapes=[pltpu.SMEM((TOP_K, tb), jnp.int32),
                            pltpu.VMEM((tb * slab, V7X_LANES), jnp.int32),
                            pltpu.SemaphoreType.DMA,
                            pltpu.SemaphoreType.DMA,
                            pltpu.SemaphoreType.DMA],
        ),
        out_shape=jax.ShapeDtypeStruct((n_rows * slab, V7X_LANES), jnp.int32),
        compiler_params=_params(("arbitrary",)),
        name="dispatch",
    )(pad_start, pad_end, h2p, slot3)


def _expert_kernel(blk_e_ref, nused_ref, xs_hbm, wg_ref, wu_ref, wd_ref, ys_hbm,
                   xt, yt, sem_in, sem_out, *, slab):
    j = pl.program_id(0)
    n_used = nused_ref[0]
    tb = xt.shape[1]
    lanes = V7X_LANES
    half = slab * lanes

    def in_copies(tile, buf):
        rows = pl.ds(pl.multiple_of(tile * tb, tb), tb)
        return [pltpu.make_async_copy(xs_hbm.at[rows, s], xt.at[buf, :, pl.ds(s * lanes, lanes)],
                                      sem_in.at[buf]) for s in range(slab)]

    def out_copies(tile, buf):
        rows = pl.ds(pl.multiple_of(tile * tb, tb), tb)
        return [pltpu.make_async_copy(yt.at[buf, :, pl.ds(s * lanes, lanes)], ys_hbm.at[rows, s],
                                      sem_out.at[buf]) for s in range(slab)]

    @pl.when(j == 0)
    def _():
        for c in in_copies(0, 0):
            c.start()

    @pl.when(j < n_used)
    def _():
        cur = j % 2

        @pl.when(j + 1 < n_used)
        def _():
            for c in in_copies(j + 1, 1 - cur):
                c.start()

        for c in in_copies(j, cur):
            c.wait()

        @pl.when(j >= 2)
        def _():
            for c in out_copies(j - 2, cur):
                c.wait()

        lo, hi = _unpack_pair(xt[cur])
        lo = lo.astype(jnp.bfloat16)
        hi = hi.astype(jnp.bfloat16)
        g = (jnp.dot(lo, wg_ref[0, :half, :], preferred_element_type=jnp.float32)
             + jnp.dot(hi, wg_ref[0, half:, :], preferred_element_type=jnp.float32))
        u = (jnp.dot(lo, wu_ref[0, :half, :], preferred_element_type=jnp.float32)
             + jnp.dot(hi, wu_ref[0, half:, :], preferred_element_type=jnp.float32))
        a = (_silu(g) * u).astype(jnp.bfloat16)
        y = jnp.dot(a, wd_ref[0], preferred_element_type=jnp.float32)
        yt[cur] = _pack_pair(y[:, :half], y[:, half:])
        for c in out_copies(j, cur):
            c.start()

        @pl.when(j == n_used - 1)
        def _():
            @pl.when(j >= 1)
            def _():
                for c in out_copies(j - 1, 1 - cur):
                    c.wait()

            for c in out_copies(j, cur):
                c.wait()


def _experts(blk_e, n_used, xs, wg, wu, wd, slab):
    tb = MOE_TILE
    p = xs.shape[0] // slab
    half = slab * V7X_LANES
    d = 2 * half
    f = wg.shape[2]
    nblk = p // tb

    def w_map(i, be, nu):
        return (be[jnp.minimum(i, nu[0] - 1)], 0, 0)

    ys = pl.pallas_call(
        functools.partial(_expert_kernel, slab=slab),
        grid_spec=pltpu.PrefetchScalarGridSpec(
            num_scalar_prefetch=2,
            grid=(nblk,),
            in_specs=[pl.BlockSpec(memory_space=pl.ANY),
                      pl.BlockSpec((1, d, f), w_map),
                      pl.BlockSpec((1, d, f), w_map),
                      pl.BlockSpec((1, f, d), w_map)],
            out_specs=pl.BlockSpec(memory_space=pl.ANY),
            scratch_shapes=[pltpu.VMEM((2, tb, half), jnp.int32),
                            pltpu.VMEM((2, tb, half), jnp.int32),
                            pltpu.SemaphoreType.DMA((2,)),
                            pltpu.SemaphoreType.DMA((2,))],
        ),
        out_shape=jax.ShapeDtypeStruct((p, slab, V7X_LANES), jnp.int32),
        compiler_params=_params(("arbitrary",)),
        name="experts",
    )(blk_e, n_used, xs.reshape(p, slab, V7X_LANES), wg, wu, wd)
    return ys.reshape(xs.shape)


def _combine_kernel(x_ref, h_ref, wt_ref, gt_ref, gf_ref, wsg_ref, wsu_ref, wsd_ref,
                    slot_hbm, ys_hbm, o_ref, slot_s, rows_v, wrep, acc_lo, acc_hi, sem_i, sem_r):
    i = pl.program_id(0)
    n = pl.num_programs(0)
    tb = x_ref.shape[0]
    half = x_ref.shape[1] // 2
    slab = half // V7X_LANES
    per_tile = MOE_TILE // tb
    cur = i % 2
    nxt = 1 - cur

    def slot_copy(step):
        win = pl.ds((step % per_tile) * tb, tb)
        return pltpu.make_async_copy(slot_hbm.at[step // per_tile, :, win], slot_s.at[step % 2],
                                     sem_i.at[step % 2])

    def request_token(buf, t):
        for k in range(TOP_K):
            pltpu.make_async_copy(ys_hbm.at[_slab_rows(slot_s[buf, k, t], slab)],
                                  rows_v.at[buf, k, _slab_rows(t, slab)],
                                  sem_r.at[buf]).start(priority=k % 2)

    def wait_rows(buf):
        for k in range(TOP_K):
            pltpu.make_async_copy(ys_hbm.at[pl.ds(0, tb * slab)], rows_v.at[buf, k], sem_r.at[buf]).wait()

    @pl.when(i == 0)
    def _():
        slot_copy(0).start()
        slot_copy(0).wait()

        def first(t, carry):
            request_token(0, t)
            return carry

        lax.fori_loop(0, tb, first, 0, unroll=2)

        @pl.when(n > 1)
        def _():
            slot_copy(1).start()
            slot_copy(1).wait()

    @pl.when(i + 2 < n)
    def _():
        slot_copy(i + 2).start()

    h = h_ref[...]
    g = jnp.dot(h, wsg_ref[...], preferred_element_type=jnp.float32)
    u = jnp.dot(h, wsu_ref[...], preferred_element_type=jnp.float32)
    a = (_silu(g) * u).astype(jnp.bfloat16)
    y = jnp.dot(a, wsd_ref[...], preferred_element_type=jnp.float32)

    wt = wt_ref[...]
    for k in range(TOP_K):
        wrep[k] = jnp.broadcast_to(wt[:, k:k + 1], (tb, V7X_LANES))

    def token(t, carry, buf, request_next):
        rows = _slab_rows(t, slab)
        lo_acc = jnp.zeros((slab, V7X_LANES), jnp.float32)
        hi_acc = jnp.zeros((slab, V7X_LANES), jnp.float32)
        for k in range(TOP_K):
            wv = jnp.broadcast_to(wrep[k, pl.ds(t, 1), :], (slab, V7X_LANES))
            lo, hi = _unpack_pair(rows_v[buf, k, rows, :])
            lo_acc = lo_acc + wv * lo
            hi_acc = hi_acc + wv * hi
        acc_lo[rows, :] = lo_acc
        acc_hi[rows, :] = hi_acc
        if request_next:
            request_token(1 - buf, t)
        return carry

    for buf in range(2):
        for request_next in (True, False):
            @pl.when((cur == buf) & ((i + 1 < n) == request_next))
            def _():
                wait_rows(buf)
                lax.fori_loop(0, tb, functools.partial(token, buf=buf, request_next=request_next),
                              0, unroll=4)

    @pl.when(i + 2 < n)
    def _():
        slot_copy(i + 2).wait()

    yy = y + jnp.concatenate([_slab_load(acc_lo, tb, slab), _slab_load(acc_hi, tb, slab)], axis=1)
    xo = x_ref[...] + gt_ref[0] * yy
    ms = jnp.mean(xo * xo, axis=-1, keepdims=True)
    o_ref[...] = xo * lax.rsqrt(ms + EPS) * gf_ref[...]


def _combine(x1, h2, w_t, mod3, gate_idx, g_final, wsg, wsu, wsd, slot3, ys, seq):
    t, d = x1.shape
    tb = COMBINE_TILE
    assert MOE_TILE % tb == 0 and seq % tb == 0
    per_b = seq // tb
    half = d // 2
    f = wsg.shape[1]
    once = pl.Buffered(1)
    return pl.pallas_call(
        _combine_kernel,
        grid=(t // tb,),
        in_specs=[pl.BlockSpec((tb, d), lambda i: (i, 0)),
                  pl.BlockSpec((tb, d), lambda i: (i, 0)),
                  pl.BlockSpec((tb, TOP_K), lambda i: (i, 0)),
                  pl.BlockSpec((1, 1, d), lambda i: ((i // per_b) * 6 + gate_idx, 0, 0)),
                  pl.BlockSpec((1, d), lambda i: (0, 0)),
                  pl.BlockSpec((d, f), lambda i: (0, 0), pipeline_mode=once),
                  pl.BlockSpec((d, f), lambda i: (0, 0), pipeline_mode=once),
                  pl.BlockSpec((f, d), lambda i: (0, 0), pipeline_mode=once),
                  pl.BlockSpec(memory_space=pl.ANY),
                  pl.BlockSpec(memory_space=pl.ANY)],
        out_specs=pl.BlockSpec((tb, d), lambda i: (i, 0)),
        scratch_shapes=[pltpu.SMEM((2, TOP_K, tb), jnp.int32),
                        pltpu.VMEM((2, TOP_K, tb * half // V7X_LANES, V7X_LANES), jnp.int32),
                        pltpu.VMEM((TOP_K, tb, V7X_LANES), jnp.float32),
                        pltpu.VMEM((tb * half // V7X_LANES, V7X_LANES), jnp.float32),
                        pltpu.VMEM((tb * half // V7X_LANES, V7X_LANES), jnp.float32),
                        pltpu.SemaphoreType.DMA((2,)),
                        pltpu.SemaphoreType.DMA((2,))],
        out_shape=jax.ShapeDtypeStruct((t, d), jnp.float32),
        compiler_params=_params(("arbitrary",)),
        name="combine",
    )(x1, h2, w_t, mod3, g_final.reshape(1, d), wsg, wsu, wsd, slot3, ys)


def _in_layout(d):
    qw = ATT_HEADS * ATT_HEAD_DIM
    kvw = ATT_KV_HEADS * ATT_HEAD_DIM
    rqk = RET_HEADS * RET_QK_DIM
    rv = RET_HEADS * RET_V_DIM
    order = [("qa", qw), ("ka", kvw), ("va", kvw), ("qr", rqk), ("kr", rqk),
             ("vr", rv), ("gr", rv), ("ga", d), ("gb", d)]
    dst = {}
    off = 0
    for name, width in order:
        dst[name] = off
        off += width
    return dst


def kernel(x, c, w_ada, b_ada, g_norm_mix, w_in, attn_sinks, w_attn_out, w_ret_out, w_o, g_norm_ffn,
           w_router, b_router, w_gate, w_up, w_down, w_sh_gate, w_sh_up, w_sh_down, g_norm_final):
    b, s, d = x.shape
    t = b * s
    depth = w_ada.shape[0]
    bf = jnp.bfloat16
    dst = _in_layout(d)

    c_pad = jnp.zeros((V7X_SUBLANES, d), jnp.float32).at[:b].set(c)
    x2 = x.reshape(t, d)
    for l in range(depth):
        mod = _ada(c_pad, w_ada[l], b_ada[l])
        mod3 = mod[:b].reshape(b * 6, 1, d)

        h = _norm_mod(x2.reshape(b, s, d), g_norm_mix[l], mod3, 0, 1)
        proj, (wg_b, wu_b, wd_b) = _in_proj(h.reshape(t, d), w_in[l].astype(bf),
                                            (w_gate[l], w_up[l], w_down[l]), 1024, 768)
        proj3 = proj.reshape(b, s, proj.shape[1])
        attn = _attention(proj3, attn_sinks[l], dst["qa"], dst["ka"], dst["va"])
        ret = _retention(proj3, dst["qr"], dst["kr"], dst["vr"], dst["gr"])
        mix = _mix(attn.reshape(t, -1), ret.reshape(t, -1), w_attn_out[l].astype(bf),
                   w_ret_out[l].astype(bf), proj, dst["ga"], dst["gb"])
        x1 = _out_resid(mix, w_o[l].astype(bf), x2, mod3, 2, s)

        h2, h2p, idx3, pos3, w3, cnt = _router(x1, g_norm_ffn[l], mod3, 3, 4,
                                               w_router[l].T.astype(bf), b_router[l], s)
        counts = cnt[:, 0]
        tile = MOE_TILE
        padded = (counts + tile - 1) // tile * tile
        pad_end = jnp.cumsum(padded).astype(jnp.int32)
        pad_start = (pad_end - padded).astype(jnp.int32)
        n_blocks = (t * TOP_K) // tile + N_EXPERTS
        n_used = (pad_end[-1] // tile).reshape(1).astype(jnp.int32)
        blk_first = jnp.arange(n_blocks, dtype=jnp.int32) * tile
        blk_e = jnp.minimum(jnp.sum((pad_end[None, :] <= blk_first[:, None]).astype(jnp.int32), axis=1),
                            N_EXPERTS - 1)
        slab = d // 2 // V7X_LANES
        slot3 = _slots(pad_start, idx3, pos3)
        xs = _dispatch(pad_start, pad_end, h2p, slot3, n_blocks * tile, slab)
        ys = _experts(blk_e, n_used, xs, wg_b, wu_b, wd_b, slab)
        w_t = w3.transpose(0, 2, 1).reshape(t, TOP_K)
        is_last = l == depth - 1
        assert is_last, "the final norm is fused into the last layer's combine"
        x2 = _combine(x1, h2, w_t, mod3, 5, g_norm_final, w_sh_gate[l].astype(bf),
                      w_sh_up[l].astype(bf), w_sh_down[l].astype(bf), slot3, ys, s)
    return x2.reshape(b, s, d)
```

```python
import functools
import math

import jax
import jax.numpy as jnp
from jax import lax
from jax.experimental import pallas as pl
from jax.experimental.pallas import tpu as pltpu

ATT_HEADS = 32
ATT_KV_HEADS = 4
ATT_HEAD_DIM = 64
WINDOW = 128
ATT_BLOCK = 128
RET_HEADS = 8
RET_QK_DIM = 256
RET_V_DIM = 512
RET_CHUNK = 128
N_EXPERTS = 64
N_GROUPS = 8
TOPK_GROUPS = 4
TOP_K = 8
ROUTED_SCALE = 2.5
EPS = 1e-6

V7X_LANES = 128
V7X_SUBLANES = 8
V7X_VMEM_LIMIT_BYTES = 60000 * 1024

MOE_TILE = 256
COMBINE_TILE = 128
NEG_BIG = -1e30


def _div_block(n, target, align):
    best = None
    b = align
    while b <= min(n, target):
        if n % b == 0:
            best = b
        b += align
    assert best is not None, (n, target, align)
    return best


def _params(semantics):
    return pltpu.CompilerParams(dimension_semantics=semantics,
                                vmem_limit_bytes=V7X_VMEM_LIMIT_BYTES)


def _sigmoid(v):
    return 1.0 / (1.0 + jnp.exp(-v))


def _silu(v):
    return v * _sigmoid(v)


def _pack_pair(lo, hi):
    return pltpu.pack_elementwise([lo, hi], packed_dtype=jnp.bfloat16)


def _unpack_pair(p):
    lo = pltpu.unpack_elementwise(p, index=0, packed_dtype=jnp.bfloat16, unpacked_dtype=jnp.float32)
    hi = pltpu.unpack_elementwise(p, index=1, packed_dtype=jnp.bfloat16, unpacked_dtype=jnp.float32)
    return lo, hi


def _slab_load(ref, n_rows, slab):
    return jnp.concatenate([ref[pl.ds(s, n_rows, stride=slab), :] for s in range(slab)], axis=1)


def _slab_store(ref, val, n_rows, slab):
    for s in range(slab):
        ref[pl.ds(s, n_rows, stride=slab), :] = val[:, s * V7X_LANES:(s + 1) * V7X_LANES]


def _slab_rows(r, slab):
    return pl.ds(pl.multiple_of(r * slab, slab), slab)


def _ada_kernel(c_ref, w_ref, b_ref, o_ref):
    cs = _silu(c_ref[...]).astype(jnp.bfloat16)
    o_ref[...] = jnp.dot(cs, w_ref[...].astype(jnp.bfloat16),
                         preferred_element_type=jnp.float32) + b_ref[...]


def _ada(c_pad, w, b):
    m, d = c_pad.shape
    n = w.shape[1]
    tn = _div_block(n, 512, V7X_LANES)
    return pl.pallas_call(
        _ada_kernel,
        grid=(n // tn,),
        in_specs=[pl.BlockSpec((m, d), lambda j: (0, 0)),
                  pl.BlockSpec((d, tn), lambda j: (0, j)),
                  pl.BlockSpec((1, tn), lambda j: (0, j))],
        out_specs=pl.BlockSpec((m, tn), lambda j: (0, j)),
        out_shape=jax.ShapeDtypeStruct((m, n), jnp.float32),
        compiler_params=_params(("parallel",)),
        name="ada",
    )(c_pad, w, b.reshape(1, n))


def _norm_mod_kernel(x_ref, g_ref, sh_ref, sc_ref, o_ref):
    x = x_ref[0]
    ms = jnp.mean(x * x, axis=-1, keepdims=True)
    y = x * lax.rsqrt(ms + EPS) * g_ref[...]
    o_ref[0] = (y * (1.0 + sc_ref[0]) + sh_ref[0]).astype(o_ref.dtype)


def _norm_mod(x3, g, mod3, shift_idx, scale_idx):
    b, s, d = x3.shape
    ts = _div_block(s, 256, V7X_SUBLANES)
    return pl.pallas_call(
        _norm_mod_kernel,
        grid=(b, s // ts),
        in_specs=[pl.BlockSpec((1, ts, d), lambda bi, i: (bi, i, 0)),
                  pl.BlockSpec((1, d), lambda bi, i: (0, 0)),
                  pl.BlockSpec((1, 1, d), lambda bi, i: (bi * 6 + shift_idx, 0, 0)),
                  pl.BlockSpec((1, 1, d), lambda bi, i: (bi * 6 + scale_idx, 0, 0))],
        out_specs=pl.BlockSpec((1, ts, d), lambda bi, i: (bi, i, 0)),
        out_shape=jax.ShapeDtypeStruct((b, s, d), jnp.bfloat16),
        compiler_params=_params(("parallel", "parallel")),
        name="norm_mod",
    )(x3, g.reshape(1, d), mod3, mod3)


def _in_proj_kernel(a_ref, b_ref, *rest, windows, n_j):
    n_side = len(windows)
    src = rest[:n_side]
    o_ref = rest[n_side]
    dst = rest[n_side + 1:]
    o_ref[...] = jnp.dot(a_ref[...], b_ref[...],
                         preferred_element_type=jnp.float32).astype(o_ref.dtype)
    step = pl.program_id(0) * n_j + pl.program_id(1)

    for s_ref, d_ref, (first, count) in zip(src, dst, windows):
        @pl.when((step >= first) & (step < first + count))
        def _():
            d_ref[...] = s_ref[...].astype(d_ref.dtype)


def _in_proj(a, b, side_groups, tm_target, tn_target):
    m, k = a.shape
    n = b.shape[1]
    tm = _div_block(m, tm_target, V7X_SUBLANES)
    tn = _div_block(n, tn_target, V7X_LANES)
    n_i, n_j = m // tm, n // tn
    steps_left = n_i * n_j
    first = 0
    side, chunked, windows = [], [], []
    for group in side_groups:
        n_max = 1 << (steps_left.bit_length() - 1)
        used = 0
        for w in group:
            rows = w.size // w.shape[-1]
            count = n_max
            while rows % (count * 2 * V7X_SUBLANES):
                count //= 2
            side.append(w)
            chunked.append(w.reshape(count, rows // count, w.shape[-1]))
            windows.append((first, count))
            used = max(used, count)
        first += used
        steps_left -= used

    def side_spec(c, window):
        w_first, count = window
        return pl.BlockSpec((1,) + c.shape[1:],
                            lambda i, j: (jnp.clip(i * n_j + j - w_first, 0, count - 1), 0, 0))

    side_specs = [side_spec(c, w) for c, w in zip(chunked, windows)]
    outs = pl.pallas_call(
        functools.partial(_in_proj_kernel, windows=tuple(windows), n_j=n_j),
        grid=(n_i, n_j),
        in_specs=[pl.BlockSpec((tm, k), lambda i, j: (i, 0)),
                  pl.BlockSpec((k, tn), lambda i, j: (0, j))] + side_specs,
        out_specs=[pl.BlockSpec((tm, tn), lambda i, j: (i, j))] + side_specs,
        out_shape=[jax.ShapeDtypeStruct((m, n), jnp.bfloat16)]
        + [jax.ShapeDtypeStruct(c.shape, jnp.bfloat16) for c in chunked],
        compiler_params=_params(("arbitrary", "arbitrary")),
        name="in_proj",
    )(a, b, *chunked)
    return outs[0], [o.reshape(w.shape) for o, w in zip(outs[1:], side)]


def _attn_kernel(sink_ref, q_ref, kc_ref, kp_ref, vc_ref, vp_ref, bias_ref, o_ref):
    i = pl.program_id(1)
    blk = ATT_BLOCK
    hd = ATT_HEAD_DIM
    group = ATT_HEADS // ATT_KV_HEADS
    pairs = group // 2
    nt = (((1,), (1,)), ((), ()))
    tn = (((0,), (0,)), ((), ()))
    zpad = jnp.zeros((2 * blk, hd), jnp.bfloat16)

    def scores(h, par):
        sl = slice(h * hd, (h + 1) * hd)
        k2 = jnp.concatenate([kp_ref[0, :, sl], kc_ref[0, :, sl]], axis=0) * (hd ** -0.5)
        qp = jnp.concatenate([q_ref[0, :, (h * pairs + p) * 2 * hd:(h * pairs + p + 1) * 2 * hd]
                              for p in range(pairs)], axis=0)
        kz = jnp.concatenate([k2, zpad] if par == 0 else [zpad, k2], axis=1)
        return lax.dot_general(kz, qp, nt, preferred_element_type=jnp.float32)

    def softmax(s, h, par):
        s = s + bias_ref[h * 2 + par]
        s = jnp.concatenate([jnp.where(i == 0, NEG_BIG, s[:blk]), s[blk:]], axis=0)
        sink = jnp.concatenate([jnp.full((1, blk), sink_ref[h * group + 2 * p + par], jnp.float32)
                                for p in range(pairs)], axis=1)
        m = jnp.maximum(s.max(0, keepdims=True), sink)
        pr = jnp.exp(s - m)
        denom = pr.sum(0, keepdims=True) + jnp.exp(sink - m)
        return pr.astype(jnp.bfloat16), 1.0 / denom

    def values(pr, inv, h, par):
        sl = slice(h * hd, (h + 1) * hd)
        v2 = jnp.concatenate([vp_ref[0, :, sl], vc_ref[0, :, sl]], axis=0)
        vz = jnp.concatenate([v2, zpad] if par == 0 else [zpad, v2], axis=1)
        return lax.dot_general(vz, pr, tn, preferred_element_type=jnp.float32) * inv

    items = [(h, par) for h in range(ATT_KV_HEADS) for par in range(2)]
    s_of, p_of, acc = {}, {}, {}
    for n in range(len(items) + 2):
        if n < len(items):
            s_of[n] = scores(*items[n])
        if 1 <= n <= len(items):
            p_of[n - 1] = softmax(s_of.pop(n - 1), *items[n - 1])
        if n >= 2:
            h, par = items[n - 2]
            o = values(*p_of.pop(n - 2), h, par)
            acc[h] = o if par == 0 else acc[h] + o
            if par == 1:
                out = acc.pop(h)
                for p in range(pairs):
                    o_ref[0, :, (h * pairs + p) * 2 * hd:(h * pairs + p + 1) * 2 * hd] = (
                        out[:, p * blk:(p + 1) * blk].T.astype(o_ref.dtype))


def _attn_bias():
    blk = ATT_BLOCK
    group = ATT_HEADS // ATT_KV_HEADS
    pairs = group // 2
    qi = jnp.arange(blk)[:, None]
    kj = jnp.arange(2 * blk)[None, :]
    dist = qi + blk - kj
    valid = (dist >= 0) & (dist < WINDOW)
    slopes = jnp.exp2(-8.0 * jnp.arange(1, ATT_HEADS + 1, dtype=jnp.float32) / ATT_HEADS)
    slopes = slopes.reshape(ATT_KV_HEADS, pairs, 2)
    bias = jnp.where(valid, -slopes[..., None, None] * dist.astype(jnp.float32), NEG_BIG)
    return bias.transpose(0, 2, 4, 1, 3).reshape(ATT_KV_HEADS * 2, 2 * blk, pairs * blk)


def _attention(proj3, sinks, q_off, k_off, v_off):
    b, s, _ = proj3.shape
    qw = ATT_HEADS * ATT_HEAD_DIM
    kvw = ATT_KV_HEADS * ATT_HEAD_DIM
    nb = s // ATT_BLOCK
    group = ATT_HEADS // ATT_KV_HEADS
    assert q_off % qw == 0 and k_off % kvw == 0 and v_off % kvw == 0
    assert group % 2 == 0 and 2 * ATT_HEAD_DIM == V7X_LANES and WINDOW == ATT_BLOCK
    assert 4 ** round(math.log(ATT_HEAD_DIM, 4)) == ATT_HEAD_DIM, "score scale must be a power of two"
    qb, kb, vb = q_off // qw, k_off // kvw, v_off // kvw
    bias = _attn_bias()
    return pl.pallas_call(
        _attn_kernel,
        grid_spec=pltpu.PrefetchScalarGridSpec(
            num_scalar_prefetch=1,
            grid=(b, nb),
            in_specs=[pl.BlockSpec((1, ATT_BLOCK, qw), lambda bi, i, sk: (bi, i, qb)),
                      pl.BlockSpec((1, ATT_BLOCK, kvw), lambda bi, i, sk: (bi, i, kb)),
                      pl.BlockSpec((1, ATT_BLOCK, kvw), lambda bi, i, sk: (bi, jnp.maximum(i - 1, 0), kb)),
                      pl.BlockSpec((1, ATT_BLOCK, kvw), lambda bi, i, sk: (bi, i, vb)),
                      pl.BlockSpec((1, ATT_BLOCK, kvw), lambda bi, i, sk: (bi, jnp.maximum(i - 1, 0), vb)),
                      pl.BlockSpec(bias.shape, lambda bi, i, sk: (0, 0, 0), pipeline_mode=pl.Buffered(1))],
            out_specs=pl.BlockSpec((1, ATT_BLOCK, qw), lambda bi, i, sk: (bi, i, 0)),
        ),
        out_shape=jax.ShapeDtypeStruct((b, s, qw), jnp.bfloat16),
        compiler_params=_params(("parallel", "parallel")),
        name="attention",
    )(sinks, proj3, proj3, proj3, proj3, proj3, bias)


def _ret_kernel(q_ref, k_ref, v_ref, gr_ref, mask_ref, qd_ref, kd_ref, cd_ref, o_ref, state_ref):
    c = pl.program_id(1)

    @pl.when(c == 0)
    def _():
        state_ref[...] = jnp.zeros_like(state_ref)

    nt = (((1,), (1,)), ((), ()))
    tn = (((0,), (0,)), ((), ()))
    def decayed(bi):
        q = q_ref[bi]
        k = k_ref[bi]
        attn = lax.dot_general(q, k, nt, preferred_element_type=jnp.float32) * mask_ref[0]
        kd = (k.astype(jnp.float32) * kd_ref[0]).astype(k.dtype)
        return attn.astype(jnp.bfloat16), kd

    def recur(bi, attn, kd):
        q = q_ref[bi]
        v = v_ref[bi]
        intra = jnp.dot(attn, v, preferred_element_type=jnp.float32)
        state = state_ref[bi]
        inter = jnp.dot(q, state.astype(q.dtype), preferred_element_type=jnp.float32) * qd_ref[0]
        state_ref[bi] = state * cd_ref[0] + lax.dot_general(kd, v, tn, preferred_element_type=jnp.float32)
        return intra + inter

    def finish(bi, o):
        mu = jnp.mean(o, axis=-1, keepdims=True)
        oc = o - mu
        var = jnp.mean(oc * oc, axis=-1, keepdims=True)
        y = oc * lax.rsqrt(var + EPS)
        o_ref[bi] = (_silu(gr_ref[bi].astype(jnp.float32)) * y).astype(o_ref.dtype)

    nb = q_ref.shape[0]
    a_of, o_of = {}, {}
    for n in range(nb + 2):
        if n < nb:
            a_of[n] = decayed(n)
        if 1 <= n <= nb:
            o_of[n - 1] = recur(n - 1, *a_of.pop(n - 1))
        if n >= 2:
            finish(n - 2, o_of.pop(n - 2))


def _retention(proj3, q_off, k_off, v_off, g_off):
    b, s, _ = proj3.shape
    dk, dv, ch = RET_QK_DIM, RET_V_DIM, RET_CHUNK
    assert q_off % dk == 0 and k_off % dk == 0 and v_off % dv == 0 and g_off % dv == 0
    qb, kb, vb, gb = q_off // dk, k_off // dk, v_off // dv, g_off // dv
    n = s // ch
    log_g = jnp.log1p(-jnp.exp2(-5.0 - jnp.arange(RET_HEADS, dtype=jnp.float32)))
    pos = jnp.arange(ch, dtype=jnp.float32)
    rel = pos[:, None] - pos[None, :]
    scale = dk ** -0.5
    mask = jnp.where(rel[None] >= 0, jnp.exp(rel[None] * log_g[:, None, None]), 0.0) * scale
    q_decay = jnp.exp((pos[None, :, None] + 1.0) * log_g[:, None, None])
    k_decay = jnp.exp((ch - 1.0 - pos[None, :, None]) * log_g[:, None, None]) * scale
    c_decay = jnp.exp(ch * log_g)[:, None, None]
    return pl.pallas_call(
        _ret_kernel,
        grid=(RET_HEADS, n),
        in_specs=[pl.BlockSpec((b, ch, dk), lambda h, c: (0, c, qb + h)),
                  pl.BlockSpec((b, ch, dk), lambda h, c: (0, c, kb + h)),
                  pl.BlockSpec((b, ch, dv), lambda h, c: (0, c, vb + h)),
                  pl.BlockSpec((b, ch, dv), lambda h, c: (0, c, gb + h)),
                  pl.BlockSpec((1, ch, ch), lambda h, c: (h, 0, 0)),
                  pl.BlockSpec((1, ch, 1), lambda h, c: (h, 0, 0)),
                  pl.BlockSpec((1, ch, 1), lambda h, c: (h, 0, 0)),
                  pl.BlockSpec((1, 1, 1), lambda h, c: (h, 0, 0))],
        out_specs=pl.BlockSpec((b, ch, dv), lambda h, c: (0, c, h)),
        out_shape=jax.ShapeDtypeStruct((b, s, RET_HEADS * dv), jnp.bfloat16),
        scratch_shapes=[pltpu.VMEM((b, dk, dv), jnp.float32)],
        compiler_params=_params(("parallel", "arbitrary")),
        name="retention",
    )(proj3, proj3, proj3, proj3, mask, q_decay, k_decay, c_decay)


def _mix_kernel(a_ref, r_ref, wa_ref, wr_ref, ga_ref, gb_ref, o_ref):
    ya = jnp.dot(a_ref[...], wa_ref[...], preferred_element_type=jnp.float32)
    yr = jnp.dot(r_ref[...], wr_ref[...], preferred_element_type=jnp.float32)
    ga = _sigmoid(ga_ref[...].astype(jnp.float32))
    gb = _sigmoid(gb_ref[...].astype(jnp.float32))
    o_ref[...] = (ga * ya + gb * yr).astype(o_ref.dtype)


def _mix(attn2, ret2, wa, wr, proj2, ga_off, gb_off):
    m, ka = attn2.shape
    kr = ret2.shape[1]
    d = wa.shape[1]
    tm = _div_block(m, 1024, V7X_SUBLANES)
    tn = _div_block(d, 512, V7X_LANES)
    assert ga_off % tn == 0 and gb_off % tn == 0
    gab, gbb = ga_off // tn, gb_off // tn
    return pl.pallas_call(
        _mix_kernel,
        grid=(m // tm, d // tn),
        in_specs=[pl.BlockSpec((tm, ka), lambda i, j: (i, 0)),
                  pl.BlockSpec((tm, kr), lambda i, j: (i, 0)),
                  pl.BlockSpec((ka, tn), lambda i, j: (0, j)),
                  pl.BlockSpec((kr, tn), lambda i, j: (0, j)),
                  pl.BlockSpec((tm, tn), lambda i, j: (i, gab + j)),
                  pl.BlockSpec((tm, tn), lambda i, j: (i, gbb + j))],
        out_specs=pl.BlockSpec((tm, tn), lambda i, j: (i, j)),
        out_shape=jax.ShapeDtypeStruct((m, d), jnp.bfloat16),
        compiler_params=_params(("parallel", "parallel")),
        name="mix",
    )(attn2, ret2, wa, wr, proj2, proj2)


def _resid_kernel(a_ref, w_ref, x_ref, gt_ref, o_ref):
    y = jnp.dot(a_ref[...], w_ref[...], preferred_element_type=jnp.float32)
    o_ref[...] = x_ref[...] + gt_ref[0] * y


def _out_resid(mix2, w, x2, mod3, gate_idx, seq):
    m, k = mix2.shape
    d = w.shape[1]
    tm = _div_block(seq, 1024, V7X_SUBLANES)
    tn = _div_block(d, 1024, V7X_LANES)
    per_b = seq // tm
    return pl.pallas_call(
        _resid_kernel,
        grid=(m // tm, d // tn),
        in_specs=[pl.BlockSpec((tm, k), lambda i, j: (i, 0)),
                  pl.BlockSpec((k, tn), lambda i, j: (0, j)),
                  pl.BlockSpec((tm, tn), lambda i, j: (i, j)),
                  pl.BlockSpec((1, 1, tn), lambda i, j: ((i // per_b) * 6 + gate_idx, 0, j))],
        out_specs=pl.BlockSpec((tm, tn), lambda i, j: (i, j)),
        out_shape=jax.ShapeDtypeStruct((m, d), jnp.float32),
        compiler_params=_params(("parallel", "parallel")),
        name="out_resid",
    )(mix2, w, x2, mod3)


def _router_kernel(x_ref, g_ref, sh_ref, sc_ref, wr_ref, br_ref,
                   h_ref, hp_ref, idx_ref, pos_ref, w_ref, cnt_ref, carry_ref):
    i = pl.program_id(0)
    e = N_EXPERTS
    per_g = e // N_GROUPS
    tb = x_ref.shape[0]

    @pl.when(i == 0)
    def _():
        carry_ref[...] = jnp.zeros_like(carry_ref)

    x = x_ref[...]
    ms = jnp.mean(x * x, axis=-1, keepdims=True)
    h = x * lax.rsqrt(ms + EPS) * g_ref[...]
    h = h * (1.0 + sc_ref[0]) + sh_ref[0]
    hb = h.astype(jnp.bfloat16)
    h_ref[...] = hb
    half = h.shape[1] // 2
    _slab_store(hp_ref, _pack_pair(h[:, :half], h[:, half:]), tb, half // V7X_LANES)

    nt = (((1,), (1,)), ((), ()))
    logits = lax.dot_general(wr_ref[...], hb, nt, preferred_element_type=jnp.float32)
    scores = _sigmoid(logits)
    choice = scores + br_ref[...]

    c3 = choice.reshape(N_GROUPS, per_g, tb)
    j_iota = lax.broadcasted_iota(jnp.int32, c3.shape, 1).astype(jnp.float32)
    m1 = c3.max(axis=1, keepdims=True)
    first = jnp.min(jnp.where(c3 == m1, j_iota, float(per_g)), axis=1, keepdims=True)
    m2 = jnp.where(j_iota == first, -jnp.inf, c3).max(axis=1, keepdims=True)
    gs = (m1 + m2).reshape(N_GROUPS, tb)

    g_iota = lax.broadcasted_iota(jnp.int32, gs.shape, 0)
    grank = jnp.zeros(gs.shape, jnp.int32)
    for gp in range(N_GROUPS):
        row = gs[gp:gp + 1, :]
        ahead = (row > gs) | ((row == gs) & (gp < g_iota))
        grank = grank + ahead.astype(jnp.int32)
    gmask = grank < TOPK_GROUPS
    emask = jnp.broadcast_to(gmask.reshape(N_GROUPS, 1, tb), c3.shape).reshape(e, tb)
    masked = jnp.where(emask, choice, -jnp.inf)

    e_iota = lax.broadcasted_iota(jnp.int32, masked.shape, 0)
    erank = jnp.zeros(masked.shape, jnp.int32)
    for ep in range(e):
        row = masked[ep:ep + 1, :]
        ahead = (row > masked) | ((row == masked) & (ep < e_iota))
        erank = erank + ahead.astype(jnp.int32)
    sel = (erank < TOP_K) & emask
    self32 = sel.astype(jnp.float32)

    wsel = scores * self32
    wn = wsel / jnp.sum(wsel, axis=0, keepdims=True) * ROUTED_SCALE

    selb = self32.astype(jnp.bfloat16)
    t_r = lax.broadcasted_iota(jnp.int32, (tb, tb), 0)
    t_c = lax.broadcasted_iota(jnp.int32, (tb, tb), 1)
    upper = (t_r <= t_c).astype(jnp.bfloat16)
    incl = jnp.dot(selb, upper, preferred_element_type=jnp.float32)
    carry = carry_ref[...]
    rank_in_e = carry + incl - 1.0
    carry_new = carry + jnp.sum(self32, axis=1, keepdims=True)
    carry_ref[...] = carry_new
    cnt_ref[...] = jnp.broadcast_to(carry_new, cnt_ref.shape).astype(jnp.int32)

    e_r = lax.broadcasted_iota(jnp.int32, (e, e), 0)
    e_c = lax.broadcasted_iota(jnp.int32, (e, e), 1)
    lower = (e_c < e_r).astype(jnp.bfloat16)
    before = jnp.dot(lower, selb, preferred_element_type=jnp.float32)
    e_f = e_iota.astype(jnp.float32)
    idx_rows, pos_rows, w_rows = [], [], []
    for k in range(TOP_K):
        hit = jnp.where(sel & (before == float(k)), 1.0, 0.0)
        idx_rows.append(jnp.sum(hit * e_f, axis=0, keepdims=True))
        pos_rows.append(jnp.sum(hit * rank_in_e, axis=0, keepdims=True))
        w_rows.append(jnp.sum(hit * wn, axis=0, keepdims=True))
    idx_ref[0] = jnp.concatenate(idx_rows, axis=0).astype(jnp.int32)
    pos_ref[0] = jnp.concatenate(pos_rows, axis=0).astype(jnp.int32)
    w_ref[0] = jnp.concatenate(w_rows, axis=0)


def _router(x1, g, mod3, shift_idx, scale_idx, w_router_t, b_router, seq):
    t, d = x1.shape
    e = N_EXPERTS
    tb = MOE_TILE
    assert seq % tb == 0 and d % (2 * V7X_LANES * V7X_SUBLANES) == 0
    slab = d // 2 // V7X_LANES
    per_b = seq // tb
    nt = t // tb
    return pl.pallas_call(
        _router_kernel,
        grid=(nt,),
        in_specs=[pl.BlockSpec((tb, d), lambda i: (i, 0)),
                  pl.BlockSpec((1, d), lambda i: (0, 0)),
                  pl.BlockSpec((1, 1, d), lambda i: ((i // per_b) * 6 + shift_idx, 0, 0)),
                  pl.BlockSpec((1, 1, d), lambda i: ((i // per_b) * 6 + scale_idx, 0, 0)),
                  pl.BlockSpec((e, d), lambda i: (0, 0)),
                  pl.BlockSpec((e, 1), lambda i: (0, 0))],
        out_specs=[pl.BlockSpec((tb, d), lambda i: (i, 0)),
                   pl.BlockSpec((tb * slab, V7X_LANES), lambda i: (i, 0)),
                   pl.BlockSpec((1, TOP_K, tb), lambda i: (i, 0, 0)),
                   pl.BlockSpec((1, TOP_K, tb), lambda i: (i, 0, 0)),
                   pl.BlockSpec((1, TOP_K, tb), lambda i: (i, 0, 0)),
                   pl.BlockSpec((e, V7X_LANES), lambda i: (0, 0))],
        out_shape=[jax.ShapeDtypeStruct((t, d), jnp.bfloat16),
                   jax.ShapeDtypeStruct((t * slab, V7X_LANES), jnp.int32),
                   jax.ShapeDtypeStruct((nt, TOP_K, tb), jnp.int32),
                   jax.ShapeDtypeStruct((nt, TOP_K, tb), jnp.int32),
                   jax.ShapeDtypeStruct((nt, TOP_K, tb), jnp.float32),
                   jax.ShapeDtypeStruct((e, V7X_LANES), jnp.int32)],
        scratch_shapes=[pltpu.VMEM((e, 1), jnp.float32)],
        compiler_params=_params(("arbitrary",)),
        name="router",
    )(x1, g.reshape(1, d), mod3, mod3, w_router_t, b_router.reshape(e, 1))


def _slots_kernel(pstart_ref, idx_ref, pos_ref, o_ref):
    idx = idx_ref[...]
    base = jnp.zeros(idx.shape, jnp.int32)
    for ex in range(N_EXPERTS):
        base = jnp.where(idx == ex, pstart_ref[ex], base)
    o_ref[...] = base + pos_ref[...]


def _slots(pad_start, idx3, pos3):
    nt = idx3.shape[0]
    blk = (1,) + idx3.shape[1:]
    spec = pl.BlockSpec(blk, lambda i, ps: (i, 0, 0))
    return pl.pallas_call(
        _slots_kernel,
        grid_spec=pltpu.PrefetchScalarGridSpec(num_scalar_prefetch=1, grid=(nt,),
                                               in_specs=[spec, spec], out_specs=spec),
        out_shape=jax.ShapeDtypeStruct(idx3.shape, jnp.int32),
        compiler_params=_params(("parallel",)),
        name="slots",
    )(pad_start, idx3, pos3)


def _dispatch_kernel(pstart_ref, pend_ref, hp_ref, slot_hbm, xs_hbm,
                     slot_s, zero_v, sem_i, sem_z, sem_r, *, slab):
    i = pl.program_id(0)
    tb = hp_ref.shape[0] // slab

    def slot_copy():
        return pltpu.make_async_copy(slot_hbm.at[i], slot_s, sem_i)

    slot_copy().start()

    def zero_copy(ex):
        first = pl.multiple_of((pend_ref[ex] - tb) * slab, tb * slab)
        return pltpu.make_async_copy(zero_v, xs_hbm.at[pl.ds(first, tb * slab)], sem_z)

    @pl.when(i == 0)
    def _():
        zero_v[...] = jnp.zeros_like(zero_v)

        def start(ex, carry):
            @pl.when(pend_ref[ex] > pstart_ref[ex])
            def _():
                zero_copy(ex).start()
            return carry

        def wait(ex, carry):
            @pl.when(pend_ref[ex] > pstart_ref[ex])
            def _():
                zero_copy(ex).wait()
            return carry

        lax.fori_loop(0, N_EXPERTS, start, 0)
        lax.fori_loop(0, N_EXPERTS, wait, 0)

    slot_copy().wait()

    def row_copy(t, k):
        return pltpu.make_async_copy(hp_ref.at[_slab_rows(t, slab)],
                                     xs_hbm.at[_slab_rows(slot_s[k, t], slab)], sem_r)

    def start_rows(t, carry):
        for k in range(TOP_K):
            row_copy(t, k).start(priority=k % 2)
        return carry

    lax.fori_loop(0, tb, start_rows, 0, unroll=2)
    for k in range(TOP_K):
        pltpu.make_async_copy(hp_ref, xs_hbm.at[pl.ds(0, tb * slab)], sem_r).wait()


def _dispatch(pad_start, pad_end, h2p, slot3, n_rows, slab):
    tb = MOE_TILE
    t = h2p.shape[0] // slab
    return pl.pallas_call(
        functools.partial(_dispatch_kernel, slab=slab),
        grid_spec=pltpu.PrefetchScalarGridSpec(
            num_scalar_prefetch=2,
            grid=(t // tb,),
            in_specs=[pl.BlockSpec((tb * slab, V7X_LANES), lambda i, ps, pe: (i, 0)),
                      pl.BlockSpec(memory_space=pl.ANY)],
            out_specs=pl.BlockSpec(memory_space=pl.ANY),
            scratch_shapes=[pltpu.SMEM((TOP_K, tb), jnp.int32),
                            pltpu.VMEM((tb * slab, V7X_LANES), jnp.int32),
                            pltpu.SemaphoreType.DMA,
                            pltpu.SemaphoreType.DMA,
                            pltpu.SemaphoreType.DMA],
        ),
        out_shape=jax.ShapeDtypeStruct((n_rows * slab, V7X_LANES), jnp.int32),
        compiler_params=_params(("arbitrary",)),
        name="dispatch",
    )(pad_start, pad_end, h2p, slot3)


def _expert_kernel(blk_e_ref, nused_ref, xs_hbm, wg_ref, wu_ref, wd_ref, ys_hbm,
                   xt, yt, sem_in, sem_out, *, slab):
    j = pl.program_id(0)
    n_used = nused_ref[0]
    tb = xt.shape[1]
    lanes = V7X_LANES
    half = slab * lanes

    def in_copies(tile, buf):
        rows = pl.ds(pl.multiple_of(tile * tb, tb), tb)
        return [pltpu.make_async_copy(xs_hbm.at[rows, s], xt.at[buf, :, pl.ds(s * lanes, lanes)],
                                      sem_in.at[buf]) for s in range(slab)]

    def out_copies(tile, buf):
        rows = pl.ds(pl.multiple_of(tile * tb, tb), tb)
        return [pltpu.make_async_copy(yt.at[buf, :, pl.ds(s * lanes, lanes)], ys_hbm.at[rows, s],
                                      sem_out.at[buf]) for s in range(slab)]

    @pl.when(j == 0)
    def _():
        for c in in_copies(0, 0):
            c.start()

    @pl.when(j < n_used)
    def _():
        cur = j % 2

        @pl.when(j + 1 < n_used)
        def _():
            for c in in_copies(j + 1, 1 - cur):
                c.start()

        for c in in_copies(j, cur):
            c.wait()

        @pl.when(j >= 2)
        def _():
            for c in out_copies(j - 2, cur):
                c.wait()

        lo, hi = _unpack_pair(xt[cur])
        lo = lo.astype(jnp.bfloat16)
        hi = hi.astype(jnp.bfloat16)
        g = (jnp.dot(lo, wg_ref[0, :half, :], preferred_element_type=jnp.float32)
             + jnp.dot(hi, wg_ref[0, half:, :], preferred_element_type=jnp.float32))
        u = (jnp.dot(lo, wu_ref[0, :half, :], preferred_element_type=jnp.float32)
             + jnp.dot(hi, wu_ref[0, half:, :], preferred_element_type=jnp.float32))
        a = (_silu(g) * u).astype(jnp.bfloat16)
        y = jnp.dot(a, wd_ref[0], preferred_element_type=jnp.float32)
        yt[cur] = _pack_pair(y[:, :half], y[:, half:])
        for c in out_copies(j, cur):
            c.start()

        @pl.when(j == n_used - 1)
        def _():
            @pl.when(j >= 1)
            def _():
                for c in out_copies(j - 1, 1 - cur):
                    c.wait()

            for c in out_copies(j, cur):
                c.wait()


def _experts(blk_e, n_used, xs, wg, wu, wd, slab):
    tb = MOE_TILE
    p = xs.shape[0] // slab
    half = slab * V7X_LANES
    d = 2 * half
    f = wg.shape[2]
    nblk = p // tb

    def w_map(i, be, nu):
        return (be[jnp.minimum(i, nu[0] - 1)], 0, 0)

    ys = pl.pallas_call(
        functools.partial(_expert_kernel, slab=slab),
        grid_spec=pltpu.PrefetchScalarGridSpec(
            num_scalar_prefetch=2,
            grid=(nblk,),
            in_specs=[pl.BlockSpec(memory_space=pl.ANY),
                      pl.BlockSpec((1, d, f), w_map),
                      pl.BlockSpec((1, d, f), w_map),
                      pl.BlockSpec((1, f, d), w_map)],
            out_specs=pl.BlockSpec(memory_space=pl.ANY),
            scratch_shapes=[pltpu.VMEM((2, tb, half), jnp.int32),
                            pltpu.VMEM((2, tb, half), jnp.int32),
                            pltpu.SemaphoreType.DMA((2,)),
                            pltpu.SemaphoreType.DMA((2,))],
        ),
        out_shape=jax.ShapeDtypeStruct((p, slab, V7X_LANES), jnp.int32),
        compiler_params=_params(("arbitrary",)),
        name="experts",
    )(blk_e, n_used, xs.reshape(p, slab, V7X_LANES), wg, wu, wd)
    return ys.reshape(xs.shape)


def _combine_kernel(x_ref, h_ref, wt_ref, gt_ref, gf_ref, wsg_ref, wsu_ref, wsd_ref,
                    slot_hbm, ys_hbm, o_ref, slot_s, rows_v, wrep, acc_lo, acc_hi, sem_i, sem_r):
    i = pl.program_id(0)
    n = pl.num_programs(0)
    tb = x_ref.shape[0]
    half = x_ref.shape[1] // 2
    slab = half // V7X_LANES
    per_tile = MOE_TILE // tb
    cur = i % 2
    nxt = 1 - cur

    def slot_copy(step):
        win = pl.ds((step % per_tile) * tb, tb)
        return pltpu.make_async_copy(slot_hbm.at[step // per_tile, :, win], slot_s.at[step % 2],
                                     sem_i.at[step % 2])

    def request_token(buf, t):
        for k in range(TOP_K):
            pltpu.make_async_copy(ys_hbm.at[_slab_rows(slot_s[buf, k, t], slab)],
                                  rows_v.at[buf, k, _slab_rows(t, slab)],
                                  sem_r.at[buf]).start(priority=k % 2)

    def wait_rows(buf):
        for k in range(TOP_K):
            pltpu.make_async_copy(ys_hbm.at[pl.ds(0, tb * slab)], rows_v.at[buf, k], sem_r.at[buf]).wait()

    @pl.when(i == 0)
    def _():
        slot_copy(0).start()
        slot_copy(0).wait()

        def first(t, carry):
            request_token(0, t)
            return carry

        lax.fori_loop(0, tb, first, 0, unroll=2)

        @pl.when(n > 1)
        def _():
            slot_copy(1).start()
            slot_copy(1).wait()

    @pl.when(i + 2 < n)
    def _():
        slot_copy(i + 2).start()

    h = h_ref[...]
    g = jnp.dot(h, wsg_ref[...], preferred_element_type=jnp.float32)
    u = jnp.dot(h, wsu_ref[...], preferred_element_type=jnp.float32)
    a = (_silu(g) * u).astype(jnp.bfloat16)
    y = jnp.dot(a, wsd_ref[...], preferred_element_type=jnp.float32)

    wt = wt_ref[...]
    for k in range(TOP_K):
        wrep[k] = jnp.broadcast_to(wt[:, k:k + 1], (tb, V7X_LANES))

    def token(t, carry, buf, request_next):
        rows = _slab_rows(t, slab)
        lo_acc = jnp.zeros((slab, V7X_LANES), jnp.float32)
        hi_acc = jnp.zeros((slab, V7X_LANES), jnp.float32)
        for k in range(TOP_K):
            wv = jnp.broadcast_to(wrep[k, pl.ds(t, 1), :], (slab, V7X_LANES))
            lo, hi = _unpack_pair(rows_v[buf, k, rows, :])
            lo_acc = lo_acc + wv * lo
            hi_acc = hi_acc + wv * hi
        acc_lo[rows, :] = lo_acc
        acc_hi[rows, :] = hi_acc
        if request_next:
            request_token(1 - buf, t)
        return carry

    for buf in range(2):
        for request_next in (True, False):
            @pl.when((cur == buf) & ((i + 1 < n) == request_next))
            def _():
                wait_rows(buf)
                lax.fori_loop(0, tb, functools.partial(token, buf=buf, request_next=request_next),
                              0, unroll=4)

    @pl.when(i + 2 < n)
    def _():
        slot_copy(i + 2).wait()

    yy = y + jnp.concatenate([_slab_load(acc_lo, tb, slab), _slab_load(acc_hi, tb, slab)], axis=1)
    xo = x_ref[...] + gt_ref[0] * yy
    ms = jnp.mean(xo * xo, axis=-1, keepdims=True)
    o_ref[...] = xo * lax.rsqrt(ms + EPS) * gf_ref[...]


def _combine(x1, h2, w_t, mod3, gate_idx, g_final, wsg, wsu, wsd, slot3, ys, seq):
    t, d = x1.shape
    tb = COMBINE_TILE
    assert MOE_TILE % tb == 0 and seq % tb == 0
    per_b = seq // tb
    half = d // 2
    f = wsg.shape[1]
    once = pl.Buffered(1)
    return pl.pallas_call(
        _combine_kernel,
        grid=(t // tb,),
        in_specs=[pl.BlockSpec((tb, d), lambda i: (i, 0)),
                  pl.BlockSpec((tb, d), lambda i: (i, 0)),
                  pl.BlockSpec((tb, TOP_K), lambda i: (i, 0)),
                  pl.BlockSpec((1, 1, d), lambda i: ((i // per_b) * 6 + gate_idx, 0, 0)),
                  pl.BlockSpec((1, d), lambda i: (0, 0)),
                  pl.BlockSpec((d, f), lambda i: (0, 0), pipeline_mode=once),
                  pl.BlockSpec((d, f), lambda i: (0, 0), pipeline_mode=once),
                  pl.BlockSpec((f, d), lambda i: (0, 0), pipeline_mode=once),
                  pl.BlockSpec(memory_space=pl.ANY),
                  pl.BlockSpec(memory_space=pl.ANY)],
        out_specs=pl.BlockSpec((tb, d), lambda i: (i, 0)),
        scratch_shapes=[pltpu.SMEM((2, TOP_K, tb), jnp.int32),
                        pltpu.VMEM((2, TOP_K, tb * half // V7X_LANES, V7X_LANES), jnp.int32),
                        pltpu.VMEM((TOP_K, tb, V7X_LANES), jnp.float32),
                        pltpu.VMEM((tb * half // V7X_LANES, V7X_LANES), jnp.float32),
                        pltpu.VMEM((tb * half // V7X_LANES, V7X_LANES), jnp.float32),
                        pltpu.SemaphoreType.DMA((2,)),
                        pltpu.SemaphoreType.DMA((2,))],
        out_shape=jax.ShapeDtypeStruct((t, d), jnp.float32),
        compiler_params=_params(("arbitrary",)),
        name="combine",
    )(x1, h2, w_t, mod3, g_final.reshape(1, d), wsg, wsu, wsd, slot3, ys)


def _in_layout(d):
    qw = ATT_HEADS * ATT_HEAD_DIM
    kvw = ATT_KV_HEADS * ATT_HEAD_DIM
    rqk = RET_HEADS * RET_QK_DIM
    rv = RET_HEADS * RET_V_DIM
    order = [("qa", qw), ("ka", kvw), ("va", kvw), ("qr", rqk), ("kr", rqk),
             ("vr", rv), ("gr", rv), ("ga", d), ("gb", d)]
    dst = {}
    off = 0
    for name, width in order:
        dst[name] = off
        off += width
    return dst


def kernel(x, c, w_ada, b_ada, g_norm_mix, w_in, attn_sinks, w_attn_out, w_ret_out, w_o, g_norm_ffn,
           w_router, b_router, w_gate, w_up, w_down, w_sh_gate, w_sh_up, w_sh_down, g_norm_final):
    b, s, d = x.shape
    t = b * s
    depth = w_ada.shape[0]
    bf = jnp.bfloat16
    dst = _in_layout(d)

    c_pad = jnp.zeros((V7X_SUBLANES, d), jnp.float32).at[:b].set(c)
    x2 = x.reshape(t, d)
    for l in range(depth):
        mod = _ada(c_pad, w_ada[l], b_ada[l])
        mod3 = mod[:b].reshape(b * 6, 1, d)

        h = _norm_mod(x2.reshape(b, s, d), g_norm_mix[l], mod3, 0, 1)
        later_weights = ((w_gate[l], w_up[l], w_down[l]),
                         (w_attn_out[l], w_ret_out[l], w_o[l], w_sh_gate[l], w_sh_up[l], w_sh_down[l]))
        proj, (wg_b, wu_b, wd_b, wao_b, wro_b, wo_b, wsg_b, wsu_b, wsd_b) = _in_proj(
            h.reshape(t, d), w_in[l].astype(bf), later_weights, 1024, 512)
        proj3 = proj.reshape(b, s, proj.shape[1])
        attn = _attention(proj3, attn_sinks[l], dst["qa"], dst["ka"], dst["va"])
        ret = _retention(proj3, dst["qr"], dst["kr"], dst["vr"], dst["gr"])
        mix = _mix(attn.reshape(t, -1), ret.reshape(t, -1), wao_b, wro_b, proj, dst["ga"], dst["gb"])
        x1 = _out_resid(mix, wo_b, x2, mod3, 2, s)

        h2, h2p, idx3, pos3, w3, cnt = _router(x1, g_norm_ffn[l], mod3, 3, 4,
                                               w_router[l].T.astype(bf), b_router[l], s)
        counts = cnt[:, 0]
        tile = MOE_TILE
        padded = (counts + tile - 1) // tile * tile
        pad_end = jnp.cumsum(padded).astype(jnp.int32)
        pad_start = (pad_end - padded).astype(jnp.int32)
        n_blocks = (t * TOP_K) // tile + N_EXPERTS
        n_used = (pad_end[-1] // tile).reshape(1).astype(jnp.int32)
        blk_first = jnp.arange(n_blocks, dtype=jnp.int32) * tile
        blk_e = jnp.minimum(jnp.sum((pad_end[None, :] <= blk_first[:, None]).astype(jnp.int32), axis=1),
                            N_EXPERTS - 1)
        slab = d // 2 // V7X_LANES
        slot3 = _slots(pad_start, idx3, pos3)
        xs = _dispatch(pad_start, pad_end, h2p, slot3, n_blocks * tile, slab)
        ys = _experts(blk_e, n_used, xs, wg_b, wu_b, wd_b, slab)
        w_t = w3.transpose(0, 2, 1).reshape(t, TOP_K)
        is_last = l == depth - 1
        assert is_last, "the final norm is fused into the last layer's combine"
        x2 = _combine(x1, h2, w_t, mod3, 5, g_norm_final, wsg_b, wsu_b, wsd_b, slot3, ys, s)
    return x2.reshape(b, s, d)
```

```python
import functools
import math

import jax
import jax.numpy as jnp
from jax import lax
from jax.experimental import pallas as pl
from jax.experimental.pallas import tpu as pltpu

ATT_HEADS = 32
ATT_KV_HEADS = 4
ATT_HEAD_DIM = 64
WINDOW = 128
ATT_BLOCK = 128
RET_HEADS = 8
RET_QK_DIM = 256
RET_V_DIM = 512
RET_CHUNK = 128
N_EXPERTS = 64
N_GROUPS = 8
TOPK_GROUPS = 4
TOP_K = 8
ROUTED_SCALE = 2.5
EPS = 1e-6

V7X_LANES = 128
V7X_SUBLANES = 8
V7X_VMEM_LIMIT_BYTES = 60000 * 1024

MOE_TILE = 256
COMBINE_TILE = 128
NEG_BIG = -1e30


def _div_block(n, target, align):
    best = None
    b = align
    while b <= min(n, target):
        if n % b == 0:
            best = b
        b += align
    assert best is not None, (n, target, align)
    return best


def _params(semantics):
    return pltpu.CompilerParams(dimension_semantics=semantics,
                                vmem_limit_bytes=V7X_VMEM_LIMIT_BYTES)


def _sigmoid(v):
    return 1.0 / (1.0 + jnp.exp(-v))


def _silu(v):
    return v * _sigmoid(v)


def _pack_pair(lo, hi):
    return pltpu.pack_elementwise([lo, hi], packed_dtype=jnp.bfloat16)


def _unpack_pair(p):
    lo = pltpu.unpack_elementwise(p, index=0, packed_dtype=jnp.bfloat16, unpacked_dtype=jnp.float32)
    hi = pltpu.unpack_elementwise(p, index=1, packed_dtype=jnp.bfloat16, unpacked_dtype=jnp.float32)
    return lo, hi


def _slab_load(ref, n_rows, slab):
    return jnp.concatenate([ref[pl.ds(s, n_rows, stride=slab), :] for s in range(slab)], axis=1)


def _slab_store(ref, val, n_rows, slab):
    for s in range(slab):
        ref[pl.ds(s, n_rows, stride=slab), :] = val[:, s * V7X_LANES:(s + 1) * V7X_LANES]


def _slab_rows(r, slab):
    return pl.ds(pl.multiple_of(r * slab, slab), slab)


def _ada_kernel(c_ref, w_ref, b_ref, o_ref):
    cs = _silu(c_ref[...]).astype(jnp.bfloat16)
    o_ref[...] = jnp.dot(cs, w_ref[...].astype(jnp.bfloat16),
                         preferred_element_type=jnp.float32) + b_ref[...]


def _ada(c_pad, w, b):
    m, d = c_pad.shape
    n = w.shape[1]
    tn = _div_block(n, 512, V7X_LANES)
    return pl.pallas_call(
        _ada_kernel,
        grid=(n // tn,),
        in_specs=[pl.BlockSpec((m, d), lambda j: (0, 0)),
                  pl.BlockSpec((d, tn), lambda j: (0, j)),
                  pl.BlockSpec((1, tn), lambda j: (0, j))],
        out_specs=pl.BlockSpec((m, tn), lambda j: (0, j)),
        out_shape=jax.ShapeDtypeStruct((m, n), jnp.float32),
        compiler_params=_params(("parallel",)),
        name="ada",
    )(c_pad, w, b.reshape(1, n))


def _norm_mod_kernel(x_ref, g_ref, sh_ref, sc_ref, o_ref):
    x = x_ref[0]
    ms = jnp.mean(x * x, axis=-1, keepdims=True)
    y = x * lax.rsqrt(ms + EPS) * g_ref[...]
    o_ref[0] = (y * (1.0 + sc_ref[0]) + sh_ref[0]).astype(o_ref.dtype)


def _norm_mod(x3, g, mod3, shift_idx, scale_idx):
    b, s, d = x3.shape
    ts = _div_block(s, 256, V7X_SUBLANES)
    return pl.pallas_call(
        _norm_mod_kernel,
        grid=(b, s // ts),
        in_specs=[pl.BlockSpec((1, ts, d), lambda bi, i: (bi, i, 0)),
                  pl.BlockSpec((1, d), lambda bi, i: (0, 0)),
                  pl.BlockSpec((1, 1, d), lambda bi, i: (bi * 6 + shift_idx, 0, 0)),
                  pl.BlockSpec((1, 1, d), lambda bi, i: (bi * 6 + scale_idx, 0, 0))],
        out_specs=pl.BlockSpec((1, ts, d), lambda bi, i: (bi, i, 0)),
        out_shape=jax.ShapeDtypeStruct((b, s, d), jnp.bfloat16),
        compiler_params=_params(("parallel", "parallel")),
        name="norm_mod",
    )(x3, g.reshape(1, d), mod3, mod3)


def _in_proj_kernel(a_ref, b_ref, *rest, windows, n_j):
    n_side = len(windows)
    src = rest[:n_side]
    o_ref = rest[n_side]
    dst = rest[n_side + 1:]
    o_ref[...] = jnp.dot(a_ref[...], b_ref[...],
                         preferred_element_type=jnp.float32).astype(o_ref.dtype)
    step = pl.program_id(0) * n_j + pl.program_id(1)

    for s_ref, d_ref, (first, count) in zip(src, dst, windows):
        @pl.when((step >= first) & (step < first + count))
        def _():
            d_ref[...] = s_ref[...].astype(d_ref.dtype)


def _in_proj(a, b, side_groups, tm_target, tn_target):
    m, k = a.shape
    n = b.shape[1]
    tm = _div_block(m, tm_target, V7X_SUBLANES)
    tn = _div_block(n, tn_target, V7X_LANES)
    n_i, n_j = m // tm, n // tn
    steps_left = n_i * n_j
    first = 0
    side, chunked, windows = [], [], []
    for group in side_groups:
        n_max = 1 << (steps_left.bit_length() - 1)
        used = 0
        for w in group:
            rows = w.size // w.shape[-1]
            count = n_max
            while rows % (count * 2 * V7X_SUBLANES):
                count //= 2
            side.append(w)
            chunked.append(w.reshape(count, rows // count, w.shape[-1]))
            windows.append((first, count))
            used = max(used, count)
        first += used
        steps_left -= used

    def side_spec(c, window):
        w_first, count = window
        return pl.BlockSpec((1,) + c.shape[1:],
                            lambda i, j: (jnp.clip(i * n_j + j - w_first, 0, count - 1), 0, 0))

    side_specs = [side_spec(c, w) for c, w in zip(chunked, windows)]
    outs = pl.pallas_call(
        functools.partial(_in_proj_kernel, windows=tuple(windows), n_j=n_j),
        grid=(n_i, n_j),
        in_specs=[pl.BlockSpec((tm, k), lambda i, j: (i, 0), pipeline_mode=pl.Buffered(1)),
                  pl.BlockSpec((k, tn), lambda i, j: (0, j))] + side_specs,
        out_specs=[pl.BlockSpec((tm, tn), lambda i, j: (i, j))] + side_specs,
        out_shape=[jax.ShapeDtypeStruct((m, n), jnp.bfloat16)]
        + [jax.ShapeDtypeStruct(c.shape, jnp.bfloat16) for c in chunked],
        compiler_params=_params(("arbitrary", "arbitrary")),
        name="in_proj",
    )(a, b, *chunked)
    return outs[0], [o.reshape(w.shape) for o, w in zip(outs[1:], side)]


def _attn_kernel(sink_ref, q_ref, kc_ref, kp_ref, vc_ref, vp_ref, bias_ref, o_ref):
    i = pl.program_id(1)
    blk = ATT_BLOCK
    hd = ATT_HEAD_DIM
    group = ATT_HEADS // ATT_KV_HEADS
    pairs = group // 2
    nt = (((1,), (1,)), ((), ()))
    tn = (((0,), (0,)), ((), ()))
    zpad = jnp.zeros((2 * blk, hd), jnp.bfloat16)

    def scores(h, par):
        sl = slice(h * hd, (h + 1) * hd)
        k2 = jnp.concatenate([kp_ref[0, :, sl], kc_ref[0, :, sl]], axis=0) * (hd ** -0.5)
        qp = jnp.concatenate([q_ref[0, :, (h * pairs + p) * 2 * hd:(h * pairs + p + 1) * 2 * hd]
                              for p in range(pairs)], axis=0)
        kz = jnp.concatenate([k2, zpad] if par == 0 else [zpad, k2], axis=1)
        return lax.dot_general(kz, qp, nt, preferred_element_type=jnp.float32)

    def softmax(s, h, par):
        s = s + bias_ref[h * 2 + par]
        s = jnp.concatenate([jnp.where(i == 0, NEG_BIG, s[:blk]), s[blk:]], axis=0)
        sink = jnp.concatenate([jnp.full((1, blk), sink_ref[h * group + 2 * p + par], jnp.float32)
                                for p in range(pairs)], axis=1)
        m = jnp.maximum(s.max(0, keepdims=True), sink)
        pr = jnp.exp(s - m)
        denom = pr.sum(0, keepdims=True) + jnp.exp(sink - m)
        return pr.astype(jnp.bfloat16), 1.0 / denom

    def values(pr, inv, h, par):
        sl = slice(h * hd, (h + 1) * hd)
        v2 = jnp.concatenate([vp_ref[0, :, sl], vc_ref[0, :, sl]], axis=0)
        vz = jnp.concatenate([v2, zpad] if par == 0 else [zpad, v2], axis=1)
        return lax.dot_general(vz, pr, tn, preferred_element_type=jnp.float32) * inv

    items = [(h, par) for h in range(ATT_KV_HEADS) for par in range(2)]
    s_of, p_of, acc = {}, {}, {}
    for n in range(len(items) + 2):
        if n < len(items):
            s_of[n] = scores(*items[n])
        if 1 <= n <= len(items):
            p_of[n - 1] = softmax(s_of.pop(n - 1), *items[n - 1])
        if n >= 2:
            h, par = items[n - 2]
            o = values(*p_of.pop(n - 2), h, par)
            acc[h] = o if par == 0 else acc[h] + o
            if par == 1:
                out = acc.pop(h)
                for p in range(pairs):
                    o_ref[0, :, (h * pairs + p) * 2 * hd:(h * pairs + p + 1) * 2 * hd] = (
                        out[:, p * blk:(p + 1) * blk].T.astype(o_ref.dtype))


def _attn_bias():
    blk = ATT_BLOCK
    group = ATT_HEADS // ATT_KV_HEADS
    pairs = group // 2
    qi = jnp.arange(blk)[:, None]
    kj = jnp.arange(2 * blk)[None, :]
    dist = qi + blk - kj
    valid = (dist >= 0) & (dist < WINDOW)
    slopes = jnp.exp2(-8.0 * jnp.arange(1, ATT_HEADS + 1, dtype=jnp.float32) / ATT_HEADS)
    slopes = slopes.reshape(ATT_KV_HEADS, pairs, 2)
    bias = jnp.where(valid, -slopes[..., None, None] * dist.astype(jnp.float32), NEG_BIG)
    return bias.transpose(0, 2, 4, 1, 3).reshape(ATT_KV_HEADS * 2, 2 * blk, pairs * blk)


def _attention(proj3, sinks, q_off, k_off, v_off):
    b, s, _ = proj3.shape
    qw = ATT_HEADS * ATT_HEAD_DIM
    kvw = ATT_KV_HEADS * ATT_HEAD_DIM
    nb = s // ATT_BLOCK
    group = ATT_HEADS // ATT_KV_HEADS
    assert q_off % qw == 0 and k_off % kvw == 0 and v_off % kvw == 0
    assert group % 2 == 0 and 2 * ATT_HEAD_DIM == V7X_LANES and WINDOW == ATT_BLOCK
    assert 4 ** round(math.log(ATT_HEAD_DIM, 4)) == ATT_HEAD_DIM, "score scale must be a power of two"
    qb, kb, vb = q_off // qw, k_off // kvw, v_off // kvw
    bias = _attn_bias()
    return pl.pallas_call(
        _attn_kernel,
        grid_spec=pltpu.PrefetchScalarGridSpec(
            num_scalar_prefetch=1,
            grid=(b, nb),
            in_specs=[pl.BlockSpec((1, ATT_BLOCK, qw), lambda bi, i, sk: (bi, i, qb)),
                      pl.BlockSpec((1, ATT_BLOCK, kvw), lambda bi, i, sk: (bi, i, kb)),
                      pl.BlockSpec((1, ATT_BLOCK, kvw), lambda bi, i, sk: (bi, jnp.maximum(i - 1, 0), kb)),
                      pl.BlockSpec((1, ATT_BLOCK, kvw), lambda bi, i, sk: (bi, i, vb)),
                      pl.BlockSpec((1, ATT_BLOCK, kvw), lambda bi, i, sk: (bi, jnp.maximum(i - 1, 0), vb)),
                      pl.BlockSpec(bias.shape, lambda bi, i, sk: (0, 0, 0), pipeline_mode=pl.Buffered(1))],
            out_specs=pl.BlockSpec((1, ATT_BLOCK, qw), lambda bi, i, sk: (bi, i, 0)),
        ),
        out_shape=jax.ShapeDtypeStruct((b, s, qw), jnp.bfloat16),
        compiler_params=_params(("parallel", "parallel")),
        name="attention",
    )(sinks, proj3, proj3, proj3, proj3, proj3, bias)


def _ret_kernel(q_ref, k_ref, v_ref, gr_ref, mask_ref, qd_ref, kd_ref, cd_ref, o_ref, state_ref):
    c = pl.program_id(1)

    @pl.when(c == 0)
    def _():
        state_ref[...] = jnp.zeros_like(state_ref)

    nt = (((1,), (1,)), ((), ()))
    tn = (((0,), (0,)), ((), ()))
    def decayed(bi):
        q = q_ref[bi]
        k = k_ref[bi]
        attn = lax.dot_general(q, k, nt, preferred_element_type=jnp.float32) * mask_ref[0]
        kd = (k.astype(jnp.float32) * kd_ref[0]).astype(k.dtype)
        return attn.astype(jnp.bfloat16), kd

    def recur(bi, attn, kd):
        q = q_ref[bi]
        v = v_ref[bi]
        intra = jnp.dot(attn, v, preferred_element_type=jnp.float32)
        state = state_ref[bi]
        inter = jnp.dot(q, state.astype(q.dtype), preferred_element_type=jnp.float32) * qd_ref[0]
        state_ref[bi] = state * cd_ref[0] + lax.dot_general(kd, v, tn, preferred_element_type=jnp.float32)
        return intra + inter

    def finish(bi, o):
        mu = jnp.mean(o, axis=-1, keepdims=True)
        oc = o - mu
        var = jnp.mean(oc * oc, axis=-1, keepdims=True)
        y = oc * lax.rsqrt(var + EPS)
        o_ref[bi] = (_silu(gr_ref[bi].astype(jnp.float32)) * y).astype(o_ref.dtype)

    nb = q_ref.shape[0]
    a_of, o_of = {}, {}
    for n in range(nb + 2):
        if n < nb:
            a_of[n] = decayed(n)
        if 1 <= n <= nb:
            o_of[n - 1] = recur(n - 1, *a_of.pop(n - 1))
        if n >= 2:
            finish(n - 2, o_of.pop(n - 2))


def _retention(proj3, q_off, k_off, v_off, g_off):
    b, s, _ = proj3.shape
    dk, dv, ch = RET_QK_DIM, RET_V_DIM, RET_CHUNK
    assert q_off % dk == 0 and k_off % dk == 0 and v_off % dv == 0 and g_off % dv == 0
    qb, kb, vb, gb = q_off // dk, k_off // dk, v_off // dv, g_off // dv
    n = s // ch
    log_g = jnp.log1p(-jnp.exp2(-5.0 - jnp.arange(RET_HEADS, dtype=jnp.float32)))
    pos = jnp.arange(ch, dtype=jnp.float32)
    rel = pos[:, None] - pos[None, :]
    scale = dk ** -0.5
    mask = jnp.where(rel[None] >= 0, jnp.exp(rel[None] * log_g[:, None, None]), 0.0) * scale
    q_decay = jnp.exp((pos[None, :, None] + 1.0) * log_g[:, None, None])
    k_decay = jnp.exp((ch - 1.0 - pos[None, :, None]) * log_g[:, None, None]) * scale
    c_decay = jnp.exp(ch * log_g)[:, None, None]
    return pl.pallas_call(
        _ret_kernel,
        grid=(RET_HEADS, n),
        in_specs=[pl.BlockSpec((b, ch, dk), lambda h, c: (0, c, qb + h)),
                  pl.BlockSpec((b, ch, dk), lambda h, c: (0, c, kb + h)),
                  pl.BlockSpec((b, ch, dv), lambda h, c: (0, c, vb + h)),
                  pl.BlockSpec((b, ch, dv), lambda h, c: (0, c, gb + h)),
                  pl.BlockSpec((1, ch, ch), lambda h, c: (h, 0, 0)),
                  pl.BlockSpec((1, ch, 1), lambda h, c: (h, 0, 0)),
                  pl.BlockSpec((1, ch, 1), lambda h, c: (h, 0, 0)),
                  pl.BlockSpec((1, 1, 1), lambda h, c: (h, 0, 0))],
        out_specs=pl.BlockSpec((b, ch, dv), lambda h, c: (0, c, h)),
        out_shape=jax.ShapeDtypeStruct((b, s, RET_HEADS * dv), jnp.bfloat16),
        scratch_shapes=[pltpu.VMEM((b, dk, dv), jnp.float32)],
        compiler_params=_params(("parallel", "arbitrary")),
        name="retention",
    )(proj3, proj3, proj3, proj3, mask, q_decay, k_decay, c_decay)


def _mix_kernel(a_ref, r_ref, wa_ref, wr_ref, ga_ref, gb_ref, o_ref):
    ya = jnp.dot(a_ref[...], wa_ref[...], preferred_element_type=jnp.float32)
    yr = jnp.dot(r_ref[...], wr_ref[...], preferred_element_type=jnp.float32)
    ga = _sigmoid(ga_ref[...].astype(jnp.float32))
    gb = _sigmoid(gb_ref[...].astype(jnp.float32))
    o_ref[...] = (ga * ya + gb * yr).astype(o_ref.dtype)


def _mix(attn2, ret2, wa, wr, proj2, ga_off, gb_off):
    m, ka = attn2.shape
    kr = ret2.shape[1]
    d = wa.shape[1]
    tm = _div_block(m, 1024, V7X_SUBLANES)
    tn = _div_block(d, 512, V7X_LANES)
    assert ga_off % tn == 0 and gb_off % tn == 0
    gab, gbb = ga_off // tn, gb_off // tn
    return pl.pallas_call(
        _mix_kernel,
        grid=(m // tm, d // tn),
        in_specs=[pl.BlockSpec((tm, ka), lambda i, j: (i, 0)),
                  pl.BlockSpec((tm, kr), lambda i, j: (i, 0)),
                  pl.BlockSpec((ka, tn), lambda i, j: (0, j)),
                  pl.BlockSpec((kr, tn), lambda i, j: (0, j)),
                  pl.BlockSpec((tm, tn), lambda i, j: (i, gab + j)),
                  pl.BlockSpec((tm, tn), lambda i, j: (i, gbb + j))],
        out_specs=pl.BlockSpec((tm, tn), lambda i, j: (i, j)),
        out_shape=jax.ShapeDtypeStruct((m, d), jnp.bfloat16),
        compiler_params=_params(("parallel", "parallel")),
        name="mix",
    )(attn2, ret2, wa, wr, proj2, proj2)


def _resid_kernel(a_ref, w_ref, x_ref, gt_ref, o_ref):
    y = jnp.dot(a_ref[...], w_ref[...], preferred_element_type=jnp.float32)
    o_ref[...] = x_ref[...] + gt_ref[0] * y


def _out_resid(mix2, w, x2, mod3, gate_idx, seq):
    m, k = mix2.shape
    d = w.shape[1]
    tm = _div_block(seq, 1024, V7X_SUBLANES)
    tn = _div_block(d, 1024, V7X_LANES)
    per_b = seq // tm
    return pl.pallas_call(
        _resid_kernel,
        grid=(m // tm, d // tn),
        in_specs=[pl.BlockSpec((tm, k), lambda i, j: (i, 0)),
                  pl.BlockSpec((k, tn), lambda i, j: (0, j)),
                  pl.BlockSpec((tm, tn), lambda i, j: (i, j)),
                  pl.BlockSpec((1, 1, tn), lambda i, j: ((i // per_b) * 6 + gate_idx, 0, j))],
        out_specs=pl.BlockSpec((tm, tn), lambda i, j: (i, j)),
        out_shape=jax.ShapeDtypeStruct((m, d), jnp.float32),
        compiler_params=_params(("parallel", "parallel")),
        name="out_resid",
    )(mix2, w, x2, mod3)


def _router_kernel(x_ref, g_ref, sh_ref, sc_ref, wr_ref, br_ref,
                   h_ref, hp_ref, idx_ref, pos_ref, w_ref, cnt_ref, carry_ref):
    i = pl.program_id(0)
    e = N_EXPERTS
    per_g = e // N_GROUPS
    tb = x_ref.shape[0]

    @pl.when(i == 0)
    def _():
        carry_ref[...] = jnp.zeros_like(carry_ref)

    x = x_ref[...]
    ms = jnp.mean(x * x, axis=-1, keepdims=True)
    h = x * lax.rsqrt(ms + EPS) * g_ref[...]
    h = h * (1.0 + sc_ref[0]) + sh_ref[0]
    hb = h.astype(jnp.bfloat16)
    h_ref[...] = hb
    half = h.shape[1] // 2
    _slab_store(hp_ref, _pack_pair(h[:, :half], h[:, half:]), tb, half // V7X_LANES)

    nt = (((1,), (1,)), ((), ()))
    logits = lax.dot_general(wr_ref[...], hb, nt, preferred_element_type=jnp.float32)
    scores = _sigmoid(logits)
    choice = scores + br_ref[...]

    c3 = choice.reshape(N_GROUPS, per_g, tb)
    j_iota = lax.broadcasted_iota(jnp.int32, c3.shape, 1).astype(jnp.float32)
    m1 = c3.max(axis=1, keepdims=True)
    first = jnp.min(jnp.where(c3 == m1, j_iota, float(per_g)), axis=1, keepdims=True)
    m2 = jnp.where(j_iota == first, -jnp.inf, c3).max(axis=1, keepdims=True)
    gs = (m1 + m2).reshape(N_GROUPS, tb)

    g_iota = lax.broadcasted_iota(jnp.int32, gs.shape, 0)
    grank = jnp.zeros(gs.shape, jnp.int32)
    for gp in range(N_GROUPS):
        row = gs[gp:gp + 1, :]
        ahead = (row > gs) | ((row == gs) & (gp < g_iota))
        grank = grank + ahead.astype(jnp.int32)
    gmask = grank < TOPK_GROUPS
    emask = jnp.broadcast_to(gmask.reshape(N_GROUPS, 1, tb), c3.shape).reshape(e, tb)
    masked = jnp.where(emask, choice, -jnp.inf)

    e_iota = lax.broadcasted_iota(jnp.int32, masked.shape, 0)
    erank = jnp.zeros(masked.shape, jnp.int32)
    for ep in range(e):
        row = masked[ep:ep + 1, :]
        ahead = (row > masked) | ((row == masked) & (ep < e_iota))
        erank = erank + ahead.astype(jnp.int32)
    sel = (erank < TOP_K) & emask
    self32 = sel.astype(jnp.float32)

    wsel = scores * self32
    wn = wsel / jnp.sum(wsel, axis=0, keepdims=True) * ROUTED_SCALE

    selb = self32.astype(jnp.bfloat16)
    t_r = lax.broadcasted_iota(jnp.int32, (tb, tb), 0)
    t_c = lax.broadcasted_iota(jnp.int32, (tb, tb), 1)
    upper = (t_r <= t_c).astype(jnp.bfloat16)
    incl = jnp.dot(selb, upper, preferred_element_type=jnp.float32)
    carry = carry_ref[...]
    rank_in_e = carry + incl - 1.0
    carry_new = carry + jnp.sum(self32, axis=1, keepdims=True)
    carry_ref[...] = carry_new
    cnt_ref[...] = jnp.broadcast_to(carry_new, cnt_ref.shape).astype(jnp.int32)

    e_r = lax.broadcasted_iota(jnp.int32, (e, e), 0)
    e_c = lax.broadcasted_iota(jnp.int32, (e, e), 1)
    lower = (e_c < e_r).astype(jnp.bfloat16)
    before = jnp.dot(lower, selb, preferred_element_type=jnp.float32)
    e_f = e_iota.astype(jnp.float32)
    idx_rows, pos_rows, w_rows = [], [], []
    for k in range(TOP_K):
        hit = jnp.where(sel & (before == float(k)), 1.0, 0.0)
        idx_rows.append(jnp.sum(hit * e_f, axis=0, keepdims=True))
        pos_rows.append(jnp.sum(hit * rank_in_e, axis=0, keepdims=True))
        w_rows.append(jnp.sum(hit * wn, axis=0, keepdims=True))
    idx_ref[0] = jnp.concatenate(idx_rows, axis=0).astype(jnp.int32)
    pos_ref[0] = jnp.concatenate(pos_rows, axis=0).astype(jnp.int32)
    w_ref[0] = jnp.concatenate(w_rows, axis=0)


def _router(x1, g, mod3, shift_idx, scale_idx, w_router_t, b_router, seq):
    t, d = x1.shape
    e = N_EXPERTS
    tb = MOE_TILE
    assert seq % tb == 0 and d % (2 * V7X_LANES * V7X_SUBLANES) == 0
    slab = d // 2 // V7X_LANES
    per_b = seq // tb
    nt = t // tb
    return pl.pallas_call(
        _router_kernel,
        grid=(nt,),
        in_specs=[pl.BlockSpec((tb, d), lambda i: (i, 0)),
                  pl.BlockSpec((1, d), lambda i: (0, 0)),
                  pl.BlockSpec((1, 1, d), lambda i: ((i // per_b) * 6 + shift_idx, 0, 0)),
                  pl.BlockSpec((1, 1, d), lambda i: ((i // per_b) * 6 + scale_idx, 0, 0)),
                  pl.BlockSpec((e, d), lambda i: (0, 0)),
                  pl.BlockSpec((e, 1), lambda i: (0, 0))],
        out_specs=[pl.BlockSpec((tb, d), lambda i: (i, 0)),
                   pl.BlockSpec((tb * slab, V7X_LANES), lambda i: (i, 0)),
                   pl.BlockSpec((1, TOP_K, tb), lambda i: (i, 0, 0)),
                   pl.BlockSpec((1, TOP_K, tb), lambda i: (i, 0, 0)),
                   pl.BlockSpec((1, TOP_K, tb), lambda i: (i, 0, 0)),
                   pl.BlockSpec((e, V7X_LANES), lambda i: (0, 0))],
        out_shape=[jax.ShapeDtypeStruct((t, d), jnp.bfloat16),
                   jax.ShapeDtypeStruct((t * slab, V7X_LANES), jnp.int32),
                   jax.ShapeDtypeStruct((nt, TOP_K, tb), jnp.int32),
                   jax.ShapeDtypeStruct((nt, TOP_K, tb), jnp.int32),
                   jax.ShapeDtypeStruct((nt, TOP_K, tb), jnp.float32),
                   jax.ShapeDtypeStruct((e, V7X_LANES), jnp.int32)],
        scratch_shapes=[pltpu.VMEM((e, 1), jnp.float32)],
        compiler_params=_params(("arbitrary",)),
        name="router",
    )(x1, g.reshape(1, d), mod3, mod3, w_router_t, b_router.reshape(e, 1))


def _slots_kernel(pstart_ref, idx_ref, pos_ref, o_ref):
    idx = idx_ref[...]
    base = jnp.zeros(idx.shape, jnp.int32)
    for ex in range(N_EXPERTS):
        base = jnp.where(idx == ex, pstart_ref[ex], base)
    o_ref[...] = base + pos_ref[...]


def _slots(pad_start, idx3, pos3):
    nt = idx3.shape[0]
    blk = (1,) + idx3.shape[1:]
    spec = pl.BlockSpec(blk, lambda i, ps: (i, 0, 0))
    return pl.pallas_call(
        _slots_kernel,
        grid_spec=pltpu.PrefetchScalarGridSpec(num_scalar_prefetch=1, grid=(nt,),
                                               in_specs=[spec, spec], out_specs=spec),
        out_shape=jax.ShapeDtypeStruct(idx3.shape, jnp.int32),
        compiler_params=_params(("parallel",)),
        name="slots",
    )(pad_start, idx3, pos3)


def _dispatch_kernel(pstart_ref, pend_ref, hp_ref, slot_hbm, xs_hbm,
                     slot_s, zero_v, sem_i, sem_z, sem_r, *, slab):
    i = pl.program_id(0)
    tb = hp_ref.shape[0] // slab

    def slot_copy():
        return pltpu.make_async_copy(slot_hbm.at[i], slot_s, sem_i)

    slot_copy().start()

    def zero_copy(ex):
        first = pl.multiple_of((pend_ref[ex] - tb) * slab, tb * slab)
        return pltpu.make_async_copy(zero_v, xs_hbm.at[pl.ds(first, tb * slab)], sem_z)

    @pl.when(i == 0)
    def _():
        zero_v[...] = jnp.zeros_like(zero_v)

        def start(ex, carry):
            @pl.when(pend_ref[ex] > pstart_ref[ex])
            def _():
                zero_copy(ex).start()
            return carry

        def wait(ex, carry):
            @pl.when(pend_ref[ex] > pstart_ref[ex])
            def _():
                zero_copy(ex).wait()
            return carry

        lax.fori_loop(0, N_EXPERTS, start, 0)
        lax.fori_loop(0, N_EXPERTS, wait, 0)

    slot_copy().wait()

    def row_copy(t, k):
        return pltpu.make_async_copy(hp_ref.at[_slab_rows(t, slab)],
                                     xs_hbm.at[_slab_rows(slot_s[k, t], slab)], sem_r)

    def start_rows(t, carry):
        for k in range(TOP_K):
            row_copy(t, k).start(priority=k % 2)
        return carry

    lax.fori_loop(0, tb, start_rows, 0, unroll=2)
    for k in range(TOP_K):
        pltpu.make_async_copy(hp_ref, xs_hbm.at[pl.ds(0, tb * slab)], sem_r).wait()


def _dispatch(pad_start, pad_end, h2p, slot3, n_rows, slab):
    tb = MOE_TILE
    t = h2p.shape[0] // slab
    return pl.pallas_call(
        functools.partial(_dispatch_kernel, slab=slab),
        grid_spec=pltpu.PrefetchScalarGridSpec(
            num_scalar_prefetch=2,
            grid=(t // tb,),
            in_specs=[pl.BlockSpec((tb * slab, V7X_LANES), lambda i, ps, pe: (i, 0)),
                      pl.BlockSpec(memory_space=pl.ANY)],
            out_specs=pl.BlockSpec(memory_space=pl.ANY),
            scratch_shapes=[pltpu.SMEM((TOP_K, tb), jnp.int32),
                            pltpu.VMEM((tb * slab, V7X_LANES), jnp.int32),
                            pltpu.SemaphoreType.DMA,
                            pltpu.SemaphoreType.DMA,
                            pltpu.SemaphoreType.DMA],
        ),
        out_shape=jax.ShapeDtypeStruct((n_rows * slab, V7X_LANES), jnp.int32),
        compiler_params=_params(("arbitrary",)),
        name="dispatch",
    )(pad_start, pad_end, h2p, slot3)


def _expert_kernel(blk_e_ref, nused_ref, first_ref, ord_ref, next_e_ref, xs_hbm, wg_hbm, wu_hbm, wd_hbm,
                   ys_hbm, xt, yt, wg_v, wu_v, wd_v, sem_in, sem_out, sem_w, *, slab):
    j = pl.program_id(0)
    n_used = nused_ref[0]
    tb = xt.shape[1]
    lanes = V7X_LANES
    half = slab * lanes

    def w_copies(e, slot):
        return [pltpu.make_async_copy(src.at[e], dst.at[slot], sem_w.at[slot])
                for src, dst in ((wg_hbm, wg_v), (wu_hbm, wu_v), (wd_hbm, wd_v))]

    def in_copies(tile, buf):
        rows = pl.ds(pl.multiple_of(tile * tb, tb), tb)
        return [pltpu.make_async_copy(xs_hbm.at[rows, s], xt.at[buf, :, pl.ds(s * lanes, lanes)],
                                      sem_in.at[buf]) for s in range(slab)]

    def out_copies(tile, buf):
        rows = pl.ds(pl.multiple_of(tile * tb, tb), tb)
        return [pltpu.make_async_copy(yt.at[buf, :, pl.ds(s * lanes, lanes)], ys_hbm.at[rows, s],
                                      sem_out.at[buf]) for s in range(slab)]

    @pl.when(j == 0)
    def _():
        for c in in_copies(0, 0) + w_copies(blk_e_ref[0], 0):
            c.start()

    @pl.when(j < n_used)
    def _():
        cur = j % 2
        slot = ord_ref[j] % 2

        @pl.when(first_ref[j] == 1)
        def _():
            for c in w_copies(blk_e_ref[j], slot):
                c.wait()

            @pl.when(next_e_ref[j] >= 0)
            def _():
                for c in w_copies(next_e_ref[j], 1 - slot):
                    c.start()

        @pl.when(j + 1 < n_used)
        def _():
            for c in in_copies(j + 1, 1 - cur):
                c.start()

        for c in in_copies(j, cur):
            c.wait()

        @pl.when(j >= 2)
        def _():
            for c in out_copies(j - 2, cur):
                c.wait()

        lo, hi = _unpack_pair(xt[cur])
        lo = lo.astype(jnp.bfloat16)
        hi = hi.astype(jnp.bfloat16)
        g = (jnp.dot(lo, wg_v[slot, :half, :], preferred_element_type=jnp.float32)
             + jnp.dot(hi, wg_v[slot, half:, :], preferred_element_type=jnp.float32))
        u = (jnp.dot(lo, wu_v[slot, :half, :], preferred_element_type=jnp.float32)
             + jnp.dot(hi, wu_v[slot, half:, :], preferred_element_type=jnp.float32))
        a = (_silu(g) * u).astype(jnp.bfloat16)
        y = jnp.dot(a, wd_v[slot], preferred_element_type=jnp.float32)
        yt[cur] = _pack_pair(y[:, :half], y[:, half:])
        for c in out_copies(j, cur):
            c.start()

        @pl.when(j == n_used - 1)
        def _():
            @pl.when(j >= 1)
            def _():
                for c in out_copies(j - 1, 1 - cur):
                    c.wait()

            for c in out_copies(j, cur):
                c.wait()


def _experts(blk_e, n_used, xs, wg, wu, wd, slab):
    tb = MOE_TILE
    p = xs.shape[0] // slab
    half = slab * V7X_LANES
    d = 2 * half
    f = wg.shape[2]
    nblk = p // tb

    tiles = jnp.arange(nblk, dtype=jnp.int32)
    used = tiles < n_used[0]
    first = used & ((tiles == 0) | (blk_e != jnp.roll(blk_e, 1)))
    ordinal = jnp.cumsum(first.astype(jnp.int32)) - 1
    first_pos = jnp.where(first, tiles, nblk)
    next_first = lax.cummin(jnp.roll(first_pos, -1).at[-1].set(nblk), reverse=True)
    next_e = jnp.where(next_first < nblk, blk_e[jnp.minimum(next_first, nblk - 1)], -1)

    any_spec = pl.BlockSpec(memory_space=pl.ANY)
    ys = pl.pallas_call(
        functools.partial(_expert_kernel, slab=slab),
        grid_spec=pltpu.PrefetchScalarGridSpec(
            num_scalar_prefetch=5,
            grid=(nblk,),
            in_specs=[any_spec, any_spec, any_spec, any_spec],
            out_specs=any_spec,
            scratch_shapes=[pltpu.VMEM((2, tb, half), jnp.int32),
                            pltpu.VMEM((2, tb, half), jnp.int32),
                            pltpu.VMEM((2, d, f), jnp.bfloat16),
                            pltpu.VMEM((2, d, f), jnp.bfloat16),
                            pltpu.VMEM((2, f, d), jnp.bfloat16),
                            pltpu.SemaphoreType.DMA((2,)),
                            pltpu.SemaphoreType.DMA((2,)),
                            pltpu.SemaphoreType.DMA((2,))],
        ),
        out_shape=jax.ShapeDtypeStruct((p, slab, V7X_LANES), jnp.int32),
        compiler_params=_params(("arbitrary",)),
        name="experts",
    )(blk_e, n_used, first.astype(jnp.int32), ordinal.astype(jnp.int32), next_e.astype(jnp.int32),
      xs.reshape(p, slab, V7X_LANES), wg, wu, wd)
    return ys.reshape(xs.shape)


def _combine_kernel(x_ref, h_ref, wt_ref, gt_ref, gf_ref, wsg_ref, wsu_ref, wsd_ref,
                    slot_hbm, ys_hbm, o_ref, slot_s, rows_v, wrep, acc_lo, acc_hi, sem_i, sem_r):
    i = pl.program_id(0)
    n = pl.num_programs(0)
    tb = x_ref.shape[0]
    half = x_ref.shape[1] // 2
    slab = half // V7X_LANES
    per_tile = MOE_TILE // tb
    cur = i % 2
    nxt = 1 - cur

    def slot_copy(step):
        win = pl.ds((step % per_tile) * tb, tb)
        return pltpu.make_async_copy(slot_hbm.at[step // per_tile, :, win], slot_s.at[step % 2],
                                     sem_i.at[step % 2])

    def request_token(buf, t):
        for k in range(TOP_K):
            pltpu.make_async_copy(ys_hbm.at[_slab_rows(slot_s[buf, k, t], slab)],
                                  rows_v.at[buf, k, _slab_rows(t, slab)],
                                  sem_r.at[buf]).start(priority=k % 2)

    def wait_rows(buf):
        for k in range(TOP_K):
            pltpu.make_async_copy(ys_hbm.at[pl.ds(0, tb * slab)], rows_v.at[buf, k], sem_r.at[buf]).wait()

    @pl.when(i == 0)
    def _():
        slot_copy(0).start()
        slot_copy(0).wait()

        def first(t, carry):
            request_token(0, t)
            return carry

        lax.fori_loop(0, tb, first, 0, unroll=2)

        @pl.when(n > 1)
        def _():
            slot_copy(1).start()
            slot_copy(1).wait()

    @pl.when(i + 2 < n)
    def _():
        slot_copy(i + 2).start()

    h = h_ref[...]
    g = jnp.dot(h, wsg_ref[...], preferred_element_type=jnp.float32)
    u = jnp.dot(h, wsu_ref[...], preferred_element_type=jnp.float32)
    a = (_silu(g) * u).astype(jnp.bfloat16)
    y = jnp.dot(a, wsd_ref[...], preferred_element_type=jnp.float32)

    wt = wt_ref[...]
    for k in range(TOP_K):
        wrep[k] = jnp.broadcast_to(wt[:, k:k + 1], (tb, V7X_LANES))

    def token(t, carry, buf, request_next):
        rows = _slab_rows(t, slab)
        lo_acc = jnp.zeros((slab, V7X_LANES), jnp.float32)
        hi_acc = jnp.zeros((slab, V7X_LANES), jnp.float32)
        for k in range(TOP_K):
            wv = jnp.broadcast_to(wrep[k, pl.ds(t, 1), :], (slab, V7X_LANES))
            lo, hi = _unpack_pair(rows_v[buf, k, rows, :])
            lo_acc = lo_acc + wv * lo
            hi_acc = hi_acc + wv * hi
        acc_lo[rows, :] = lo_acc
        acc_hi[rows, :] = hi_acc
        if request_next:
            request_token(1 - buf, t)
        return carry

    for buf in range(2):
        for request_next in (True, False):
            @pl.when((cur == buf) & ((i + 1 < n) == request_next))
            def _():
                wait_rows(buf)
                lax.fori_loop(0, tb, functools.partial(token, buf=buf, request_next=request_next),
                              0, unroll=4)

    @pl.when(i + 2 < n)
    def _():
        slot_copy(i + 2).wait()

    yy = y + jnp.concatenate([_slab_load(acc_lo, tb, slab), _slab_load(acc_hi, tb, slab)], axis=1)
    xo = x_ref[...] + gt_ref[0] * yy
    ms = jnp.mean(xo * xo, axis=-1, keepdims=True)
    o_ref[...] = xo * lax.rsqrt(ms + EPS) * gf_ref[...]


def _combine(x1, h2, w_t, mod3, gate_idx, g_final, wsg, wsu, wsd, slot3, ys, seq):
    t, d = x1.shape
    tb = COMBINE_TILE
    assert MOE_TILE % tb == 0 and seq % tb == 0
    per_b = seq // tb
    half = d // 2
    f = wsg.shape[1]
    once = pl.Buffered(1)
    return pl.pallas_call(
        _combine_kernel,
        grid=(t // tb,),
        in_specs=[pl.BlockSpec((tb, d), lambda i: (i, 0)),
                  pl.BlockSpec((tb, d), lambda i: (i, 0)),
                  pl.BlockSpec((tb, TOP_K), lambda i: (i, 0)),
                  pl.BlockSpec((1, 1, d), lambda i: ((i // per_b) * 6 + gate_idx, 0, 0)),
                  pl.BlockSpec((1, d), lambda i: (0, 0)),
                  pl.BlockSpec((d, f), lambda i: (0, 0), pipeline_mode=once),
                  pl.BlockSpec((d, f), lambda i: (0, 0), pipeline_mode=once),
                  pl.BlockSpec((f, d), lambda i: (0, 0), pipeline_mode=once),
                  pl.BlockSpec(memory_space=pl.ANY),
                  pl.BlockSpec(memory_space=pl.ANY)],
        out_specs=pl.BlockSpec((tb, d), lambda i: (i, 0)),
        scratch_shapes=[pltpu.SMEM((2, TOP_K, tb), jnp.int32),
                        pltpu.VMEM((2, TOP_K, tb * half // V7X_LANES, V7X_LANES), jnp.int32),
                        pltpu.VMEM((TOP_K, tb, V7X_LANES), jnp.float32),
                        pltpu.VMEM((tb * half // V7X_LANES, V7X_LANES), jnp.float32),
                        pltpu.VMEM((tb * half // V7X_LANES, V7X_LANES), jnp.float32),
                        pltpu.SemaphoreType.DMA((2,)),
                        pltpu.SemaphoreType.DMA((2,))],
        out_shape=jax.ShapeDtypeStruct((t, d), jnp.float32),
        compiler_params=_params(("arbitrary",)),
        name="combine",
    )(x1, h2, w_t, mod3, g_final.reshape(1, d), wsg, wsu, wsd, slot3, ys)


def _in_layout(d):
    qw = ATT_HEADS * ATT_HEAD_DIM
    kvw = ATT_KV_HEADS * ATT_HEAD_DIM
    rqk = RET_HEADS * RET_QK_DIM
    rv = RET_HEADS * RET_V_DIM
    order = [("qa", qw), ("ka", kvw), ("va", kvw), ("qr", rqk), ("kr", rqk),
             ("vr", rv), ("gr", rv), ("ga", d), ("gb", d)]
    dst = {}
    off = 0
    for name, width in order:
        dst[name] = off
        off += width
    return dst


def kernel(x, c, w_ada, b_ada, g_norm_mix, w_in, attn_sinks, w_attn_out, w_ret_out, w_o, g_norm_ffn,
           w_router, b_router, w_gate, w_up, w_down, w_sh_gate, w_sh_up, w_sh_down, g_norm_final):
    b, s, d = x.shape
    t = b * s
    depth = w_ada.shape[0]
    bf = jnp.bfloat16
    dst = _in_layout(d)

    c_pad = jnp.zeros((V7X_SUBLANES, d), jnp.float32).at[:b].set(c)
    x2 = x.reshape(t, d)
    for l in range(depth):
        mod = _ada(c_pad, w_ada[l], b_ada[l])
        mod3 = mod[:b].reshape(b * 6, 1, d)

        h = _norm_mod(x2.reshape(b, s, d), g_norm_mix[l], mod3, 0, 1)
        later_weights = ((w_gate[l], w_up[l], w_down[l]),
                         (w_attn_out[l], w_ret_out[l], w_o[l], w_sh_gate[l], w_sh_up[l], w_sh_down[l]))
        proj, (wg_b, wu_b, wd_b, wao_b, wro_b, wo_b, wsg_b, wsu_b, wsd_b) = _in_proj(
            h.reshape(t, d), w_in[l].astype(bf), later_weights, 1024, 768)
        proj3 = proj.reshape(b, s, proj.shape[1])
        attn = _attention(proj3, attn_sinks[l], dst["qa"], dst["ka"], dst["va"])
        ret = _retention(proj3, dst["qr"], dst["kr"], dst["vr"], dst["gr"])
        mix = _mix(attn.reshape(t, -1), ret.reshape(t, -1), wao_b, wro_b, proj, dst["ga"], dst["gb"])
        x1 = _out_resid(mix, wo_b, x2, mod3, 2, s)

        h2, h2p, idx3, pos3, w3, cnt = _router(x1, g_norm_ffn[l], mod3, 3, 4,
                                               w_router[l].T.astype(bf), b_router[l], s)
        counts = cnt[:, 0]
        tile = MOE_TILE
        padded = (counts + tile - 1) // tile * tile
        pad_end = jnp.cumsum(padded).astype(jnp.int32)
        pad_start = (pad_end - padded).astype(jnp.int32)
        n_blocks = (t * TOP_K) // tile + N_EXPERTS
        n_used = (pad_end[-1] // tile).reshape(1).astype(jnp.int32)
        blk_first = jnp.arange(n_blocks, dtype=jnp.int32) * tile
        blk_e = jnp.minimum(jnp.sum((pad_end[None, :] <= blk_first[:, None]).astype(jnp.int32), axis=1),
                            N_EXPERTS - 1)
        slab = d // 2 // V7X_LANES
        slot3 = _slots(pad_start, idx3, pos3)
        xs = _dispatch(pad_start, pad_end, h2p, slot3, n_blocks * tile, slab)
        ys = _experts(blk_e, n_used, xs, wg_b, wu_b, wd_b, slab)
        w_t = w3.transpose(0, 2, 1).reshape(t, TOP_K)
        is_last = l == depth - 1
        assert is_last, "the final norm is fused into the last layer's combine"
        x2 = _combine(x1, h2, w_t, mod3, 5, g_norm_final, wsg_b, wsu_b, wsd_b, slot3, ys, s)
    return x2.reshape(b, s, d)
```

```python
import functools
import math

import jax
import jax.numpy as jnp
from jax import lax
from jax.experimental import pallas as pl
from jax.experimental.pallas import tpu as pltpu

ATT_HEADS = 32
ATT_KV_HEADS = 4
ATT_HEAD_DIM = 64
WINDOW = 128
ATT_BLOCK = 128
RET_HEADS = 8
RET_QK_DIM = 256
RET_V_DIM = 512
RET_CHUNK = 128
N_EXPERTS = 64
N_GROUPS = 8
TOPK_GROUPS = 4
TOP_K = 8
ROUTED_SCALE = 2.5
EPS = 1e-6

V7X_LANES = 128
V7X_SUBLANES = 8
V7X_VMEM_LIMIT_BYTES = 60000 * 1024

MOE_TILE = 256
COMBINE_TILE = 128
NEG_BIG = -1e30


def _div_block(n, target, align):
    best = None
    b = align
    while b <= min(n, target):
        if n % b == 0:
            best = b
        b += align
    assert best is not None, (n, target, align)
    return best


def _params(semantics):
    return pltpu.CompilerParams(dimension_semantics=semantics,
                                vmem_limit_bytes=V7X_VMEM_LIMIT_BYTES)


def _sigmoid(v):
    return 1.0 / (1.0 + jnp.exp(-v))


def _silu(v):
    return v * _sigmoid(v)


def _pack_pair(lo, hi):
    return pltpu.pack_elementwise([lo, hi], packed_dtype=jnp.bfloat16)


def _unpack_pair(p):
    lo = pltpu.unpack_elementwise(p, index=0, packed_dtype=jnp.bfloat16, unpacked_dtype=jnp.float32)
    hi = pltpu.unpack_elementwise(p, index=1, packed_dtype=jnp.bfloat16, unpacked_dtype=jnp.float32)
    return lo, hi


def _slab_load(ref, n_rows, slab):
    return jnp.concatenate([ref[pl.ds(s, n_rows, stride=slab), :] for s in range(slab)], axis=1)


def _slab_store(ref, val, n_rows, slab):
    for s in range(slab):
        ref[pl.ds(s, n_rows, stride=slab), :] = val[:, s * V7X_LANES:(s + 1) * V7X_LANES]


def _slab_rows(r, slab):
    return pl.ds(pl.multiple_of(r * slab, slab), slab)


def _ada_kernel(c_ref, w_ref, b_ref, o_ref):
    cs = _silu(c_ref[...]).astype(jnp.bfloat16)
    o_ref[...] = jnp.dot(cs, w_ref[...].astype(jnp.bfloat16),
                         preferred_element_type=jnp.float32) + b_ref[...]


def _ada(c_pad, w, b):
    m, d = c_pad.shape
    n = w.shape[1]
    tn = _div_block(n, 512, V7X_LANES)
    return pl.pallas_call(
        _ada_kernel,
        grid=(n // tn,),
        in_specs=[pl.BlockSpec((m, d), lambda j: (0, 0)),
                  pl.BlockSpec((d, tn), lambda j: (0, j)),
                  pl.BlockSpec((1, tn), lambda j: (0, j))],
        out_specs=pl.BlockSpec((m, tn), lambda j: (0, j)),
        out_shape=jax.ShapeDtypeStruct((m, n), jnp.float32),
        compiler_params=_params(("parallel",)),
        name="ada",
    )(c_pad, w, b.reshape(1, n))


def _norm_mod_kernel(x_ref, g_ref, sh_ref, sc_ref, o_ref):
    x = x_ref[0]
    ms = jnp.mean(x * x, axis=-1, keepdims=True)
    y = x * lax.rsqrt(ms + EPS) * g_ref[...]
    o_ref[0] = (y * (1.0 + sc_ref[0]) + sh_ref[0]).astype(o_ref.dtype)


def _norm_mod(x3, g, mod3, shift_idx, scale_idx):
    b, s, d = x3.shape
    ts = _div_block(s, 256, V7X_SUBLANES)
    return pl.pallas_call(
        _norm_mod_kernel,
        grid=(b, s // ts),
        in_specs=[pl.BlockSpec((1, ts, d), lambda bi, i: (bi, i, 0)),
                  pl.BlockSpec((1, d), lambda bi, i: (0, 0)),
                  pl.BlockSpec((1, 1, d), lambda bi, i: (bi * 6 + shift_idx, 0, 0)),
                  pl.BlockSpec((1, 1, d), lambda bi, i: (bi * 6 + scale_idx, 0, 0))],
        out_specs=pl.BlockSpec((1, ts, d), lambda bi, i: (bi, i, 0)),
        out_shape=jax.ShapeDtypeStruct((b, s, d), jnp.bfloat16),
        compiler_params=_params(("parallel", "parallel")),
        name="norm_mod",
    )(x3, g.reshape(1, d), mod3, mod3)


def _in_proj_kernel(a_ref, b_ref, *rest, windows, n_j):
    n_side = len(windows)
    src = rest[:n_side]
    o_ref = rest[n_side]
    dst = rest[n_side + 1:]
    o_ref[...] = jnp.dot(a_ref[...], b_ref[...],
                         preferred_element_type=jnp.float32).astype(o_ref.dtype)
    step = pl.program_id(0) * n_j + pl.program_id(1)

    for s_ref, d_ref, (first, count) in zip(src, dst, windows):
        @pl.when((step >= first) & (step < first + count))
        def _():
            d_ref[...] = s_ref[...].astype(d_ref.dtype)


def _in_proj(a, b, side_groups, tm_target, tn_target):
    m, k = a.shape
    n = b.shape[1]
    tm = _div_block(m, tm_target, V7X_SUBLANES)
    tn = _div_block(n, tn_target, V7X_LANES)
    n_i, n_j = m // tm, n // tn
    steps_left = n_i * n_j
    first = 0
    side, chunked, windows = [], [], []
    for group in side_groups:
        n_max = 1 << (steps_left.bit_length() - 1)
        used = 0
        for w in group:
            rows = w.size // w.shape[-1]
            count = n_max
            while rows % (count * 2 * V7X_SUBLANES):
                count //= 2
            side.append(w)
            chunked.append(w.reshape(count, rows // count, w.shape[-1]))
            windows.append((first, count))
            used = max(used, count)
        first += used
        steps_left -= used

    def side_spec(c, window):
        w_first, count = window
        return pl.BlockSpec((1,) + c.shape[1:],
                            lambda i, j: (jnp.clip(i * n_j + j - w_first, 0, count - 1), 0, 0))

    side_specs = [side_spec(c, w) for c, w in zip(chunked, windows)]
    outs = pl.pallas_call(
        functools.partial(_in_proj_kernel, windows=tuple(windows), n_j=n_j),
        grid=(n_i, n_j),
        in_specs=[pl.BlockSpec((tm, k), lambda i, j: (i, 0)),
                  pl.BlockSpec((k, tn), lambda i, j: (0, j))] + side_specs,
        out_specs=[pl.BlockSpec((tm, tn), lambda i, j: (i, j))] + side_specs,
        out_shape=[jax.ShapeDtypeStruct((m, n), jnp.bfloat16)]
        + [jax.ShapeDtypeStruct(c.shape, jnp.bfloat16) for c in chunked],
        compiler_params=_params(("arbitrary", "arbitrary")),
        name="in_proj",
    )(a, b, *chunked)
    return outs[0], [o.reshape(w.shape) for o, w in zip(outs[1:], side)]


def _attn_kernel(sink_ref, q_ref, kc_ref, kp_ref, vc_ref, vp_ref, bias_ref, o_ref):
    i = pl.program_id(1)
    blk = ATT_BLOCK
    hd = ATT_HEAD_DIM
    group = ATT_HEADS // ATT_KV_HEADS
    pairs = group // 2
    nt = (((1,), (1,)), ((), ()))
    tn = (((0,), (0,)), ((), ()))
    zpad = jnp.zeros((2 * blk, hd), jnp.bfloat16)

    def scores(h, par):
        sl = slice(h * hd, (h + 1) * hd)
        k2 = jnp.concatenate([kp_ref[0, :, sl], kc_ref[0, :, sl]], axis=0) * (hd ** -0.5)
        qp = jnp.concatenate([q_ref[0, :, (h * pairs + p) * 2 * hd:(h * pairs + p + 1) * 2 * hd]
                              for p in range(pairs)], axis=0)
        kz = jnp.concatenate([k2, zpad] if par == 0 else [zpad, k2], axis=1)
        return lax.dot_general(kz, qp, nt, preferred_element_type=jnp.float32)

    def softmax(s, h, par):
        s = s + bias_ref[h * 2 + par]
        s = jnp.concatenate([jnp.where(i == 0, NEG_BIG, s[:blk]), s[blk:]], axis=0)
        sink = jnp.concatenate([jnp.full((1, blk), sink_ref[h * group + 2 * p + par], jnp.float32)
                                for p in range(pairs)], axis=1)
        m = jnp.maximum(s.max(0, keepdims=True), sink)
        pr = jnp.exp(s - m)
        denom = pr.sum(0, keepdims=True) + jnp.exp(sink - m)
        return pr.astype(jnp.bfloat16), 1.0 / denom

    def values(pr, inv, h, par):
        sl = slice(h * hd, (h + 1) * hd)
        v2 = jnp.concatenate([vp_ref[0, :, sl], vc_ref[0, :, sl]], axis=0)
        vz = jnp.concatenate([v2, zpad] if par == 0 else [zpad, v2], axis=1)
        return lax.dot_general(vz, pr, tn, preferred_element_type=jnp.float32) * inv

    items = [(h, par) for h in range(ATT_KV_HEADS) for par in range(2)]
    s_of, p_of, acc = {}, {}, {}
    for n in range(len(items) + 2):
        if n < len(items):
            s_of[n] = scores(*items[n])
        if 1 <= n <= len(items):
            p_of[n - 1] = softmax(s_of.pop(n - 1), *items[n - 1])
        if n >= 2:
            h, par = items[n - 2]
            o = values(*p_of.pop(n - 2), h, par)
            acc[h] = o if par == 0 else acc[h] + o
            if par == 1:
                out = acc.pop(h)
                for p in range(pairs):
                    o_ref[0, :, (h * pairs + p) * 2 * hd:(h * pairs + p + 1) * 2 * hd] = (
                        out[:, p * blk:(p + 1) * blk].T.astype(o_ref.dtype))


def _attn_bias():
    blk = ATT_BLOCK
    group = ATT_HEADS // ATT_KV_HEADS
    pairs = group // 2
    qi = jnp.arange(blk)[:, None]
    kj = jnp.arange(2 * blk)[None, :]
    dist = qi + blk - kj
    valid = (dist >= 0) & (dist < WINDOW)
    slopes = jnp.exp2(-8.0 * jnp.arange(1, ATT_HEADS + 1, dtype=jnp.float32) / ATT_HEADS)
    slopes = slopes.reshape(ATT_KV_HEADS, pairs, 2)
    bias = jnp.where(valid, -slopes[..., None, None] * dist.astype(jnp.float32), NEG_BIG)
    return bias.transpose(0, 2, 4, 1, 3).reshape(ATT_KV_HEADS * 2, 2 * blk, pairs * blk)


def _attention(proj3, sinks, q_off, k_off, v_off):
    b, s, _ = proj3.shape
    qw = ATT_HEADS * ATT_HEAD_DIM
    kvw = ATT_KV_HEADS * ATT_HEAD_DIM
    nb = s // ATT_BLOCK
    group = ATT_HEADS // ATT_KV_HEADS
    assert q_off % qw == 0 and k_off % kvw == 0 and v_off % kvw == 0
    assert group % 2 == 0 and 2 * ATT_HEAD_DIM == V7X_LANES and WINDOW == ATT_BLOCK
    assert 4 ** round(math.log(ATT_HEAD_DIM, 4)) == ATT_HEAD_DIM, "score scale must be a power of two"
    qb, kb, vb = q_off // qw, k_off // kvw, v_off // kvw
    bias = _attn_bias()
    return pl.pallas_call(
        _attn_kernel,
        grid_spec=pltpu.PrefetchScalarGridSpec(
            num_scalar_prefetch=1,
            grid=(b, nb),
            in_specs=[pl.BlockSpec((1, ATT_BLOCK, qw), lambda bi, i, sk: (bi, i, qb)),
                      pl.BlockSpec((1, ATT_BLOCK, kvw), lambda bi, i, sk: (bi, i, kb)),
                      pl.BlockSpec((1, ATT_BLOCK, kvw), lambda bi, i, sk: (bi, jnp.maximum(i - 1, 0), kb)),
                      pl.BlockSpec((1, ATT_BLOCK, kvw), lambda bi, i, sk: (bi, i, vb)),
                      pl.BlockSpec((1, ATT_BLOCK, kvw), lambda bi, i, sk: (bi, jnp.maximum(i - 1, 0), vb)),
                      pl.BlockSpec(bias.shape, lambda bi, i, sk: (0, 0, 0), pipeline_mode=pl.Buffered(1))],
            out_specs=pl.BlockSpec((1, ATT_BLOCK, qw), lambda bi, i, sk: (bi, i, 0)),
        ),
        out_shape=jax.ShapeDtypeStruct((b, s, qw), jnp.bfloat16),
        compiler_params=_params(("parallel", "parallel")),
        name="attention",
    )(sinks, proj3, proj3, proj3, proj3, proj3, bias)


def _ret_kernel(q_ref, k_ref, v_ref, gr_ref, mask_ref, qd_ref, kd_ref, cd_ref, o_ref, state_ref):
    c = pl.program_id(1)

    @pl.when(c == 0)
    def _():
        state_ref[...] = jnp.zeros_like(state_ref)

    nt = (((1,), (1,)), ((), ()))
    tn = (((0,), (0,)), ((), ()))
    def decayed(bi):
        q = q_ref[bi]
        k = k_ref[bi]
        attn = lax.dot_general(q, k, nt, preferred_element_type=jnp.float32) * mask_ref[0]
        kd = (k.astype(jnp.float32) * kd_ref[0]).astype(k.dtype)
        return attn.astype(jnp.bfloat16), kd

    def recur(bi, attn, kd):
        q = q_ref[bi]
        v = v_ref[bi]
        intra = jnp.dot(attn, v, preferred_element_type=jnp.float32)
        state = state_ref[bi]
        inter = jnp.dot(q, state.astype(q.dtype), preferred_element_type=jnp.float32) * qd_ref[0]
        state_ref[bi] = state * cd_ref[0] + lax.dot_general(kd, v, tn, preferred_element_type=jnp.float32)
        return intra + inter

    def finish(bi, o):
        mu = jnp.mean(o, axis=-1, keepdims=True)
        oc = o - mu
        var = jnp.mean(oc * oc, axis=-1, keepdims=True)
        y = oc * lax.rsqrt(var + EPS)
        o_ref[bi] = (_silu(gr_ref[bi].astype(jnp.float32)) * y).astype(o_ref.dtype)

    nb = q_ref.shape[0]
    a_of, o_of = {}, {}
    for n in range(nb + 2):
        if n < nb:
            a_of[n] = decayed(n)
        if 1 <= n <= nb:
            o_of[n - 1] = recur(n - 1, *a_of.pop(n - 1))
        if n >= 2:
            finish(n - 2, o_of.pop(n - 2))


def _retention(proj3, q_off, k_off, v_off, g_off):
    b, s, _ = proj3.shape
    dk, dv, ch = RET_QK_DIM, RET_V_DIM, RET_CHUNK
    assert q_off % dk == 0 and k_off % dk == 0 and v_off % dv == 0 and g_off % dv == 0
    qb, kb, vb, gb = q_off // dk, k_off // dk, v_off // dv, g_off // dv
    n = s // ch
    log_g = jnp.log1p(-jnp.exp2(-5.0 - jnp.arange(RET_HEADS, dtype=jnp.float32)))
    pos = jnp.arange(ch, dtype=jnp.float32)
    rel = pos[:, None] - pos[None, :]
    scale = dk ** -0.5
    mask = jnp.where(rel[None] >= 0, jnp.exp(rel[None] * log_g[:, None, None]), 0.0) * scale
    q_decay = jnp.exp((pos[None, :, None] + 1.0) * log_g[:, None, None])
    k_decay = jnp.exp((ch - 1.0 - pos[None, :, None]) * log_g[:, None, None]) * scale
    c_decay = jnp.exp(ch * log_g)[:, None, None]
    return pl.pallas_call(
        _ret_kernel,
        grid=(RET_HEADS, n),
        in_specs=[pl.BlockSpec((b, ch, dk), lambda h, c: (0, c, qb + h)),
                  pl.BlockSpec((b, ch, dk), lambda h, c: (0, c, kb + h)),
                  pl.BlockSpec((b, ch, dv), lambda h, c: (0, c, vb + h)),
                  pl.BlockSpec((b, ch, dv), lambda h, c: (0, c, gb + h)),
                  pl.BlockSpec((1, ch, ch), lambda h, c: (h, 0, 0)),
                  pl.BlockSpec((1, ch, 1), lambda h, c: (h, 0, 0)),
                  pl.BlockSpec((1, ch, 1), lambda h, c: (h, 0, 0)),
                  pl.BlockSpec((1, 1, 1), lambda h, c: (h, 0, 0))],
        out_specs=pl.BlockSpec((b, ch, dv), lambda h, c: (0, c, h)),
        out_shape=jax.ShapeDtypeStruct((b, s, RET_HEADS * dv), jnp.bfloat16),
        scratch_shapes=[pltpu.VMEM((b, dk, dv), jnp.float32)],
        compiler_params=_params(("parallel", "arbitrary")),
        name="retention",
    )(proj3, proj3, proj3, proj3, mask, q_decay, k_decay, c_decay)


def _mix_kernel(a_ref, r_ref, wa_ref, wr_ref, ga_ref, gb_ref, o_ref):
    ya = jnp.dot(a_ref[...], wa_ref[...], preferred_element_type=jnp.float32)
    yr = jnp.dot(r_ref[...], wr_ref[...], preferred_element_type=jnp.float32)
    ga = _sigmoid(ga_ref[...].astype(jnp.float32))
    gb = _sigmoid(gb_ref[...].astype(jnp.float32))
    o_ref[...] = (ga * ya + gb * yr).astype(o_ref.dtype)


def _mix(attn2, ret2, wa, wr, proj2, ga_off, gb_off):
    m, ka = attn2.shape
    kr = ret2.shape[1]
    d = wa.shape[1]
    tm = _div_block(m, 1024, V7X_SUBLANES)
    tn = _div_block(d, 512, V7X_LANES)
    assert ga_off % tn == 0 and gb_off % tn == 0
    gab, gbb = ga_off // tn, gb_off // tn
    return pl.pallas_call(
        _mix_kernel,
        grid=(m // tm, d // tn),
        in_specs=[pl.BlockSpec((tm, ka), lambda i, j: (i, 0)),
                  pl.BlockSpec((tm, kr), lambda i, j: (i, 0)),
                  pl.BlockSpec((ka, tn), lambda i, j: (0, j)),
                  pl.BlockSpec((kr, tn), lambda i, j: (0, j)),
                  pl.BlockSpec((tm, tn), lambda i, j: (i, gab + j)),
                  pl.BlockSpec((tm, tn), lambda i, j: (i, gbb + j))],
        out_specs=pl.BlockSpec((tm, tn), lambda i, j: (i, j)),
        out_shape=jax.ShapeDtypeStruct((m, d), jnp.bfloat16),
        compiler_params=_params(("parallel", "parallel")),
        name="mix",
    )(attn2, ret2, wa, wr, proj2, proj2)


def _resid_kernel(a_ref, w_ref, x_ref, gt_ref, o_ref):
    y = jnp.dot(a_ref[...], w_ref[...], preferred_element_type=jnp.float32)
    o_ref[...] = x_ref[...] + gt_ref[0] * y


def _out_resid(mix2, w, x2, mod3, gate_idx, seq):
    m, k = mix2.shape
    d = w.shape[1]
    tm = _div_block(seq, 1024, V7X_SUBLANES)
    tn = _div_block(d, 1024, V7X_LANES)
    per_b = seq // tm
    return pl.pallas_call(
        _resid_kernel,
        grid=(m // tm, d // tn),
        in_specs=[pl.BlockSpec((tm, k), lambda i, j: (i, 0)),
                  pl.BlockSpec((k, tn), lambda i, j: (0, j)),
                  pl.BlockSpec((tm, tn), lambda i, j: (i, j)),
                  pl.BlockSpec((1, 1, tn), lambda i, j: ((i // per_b) * 6 + gate_idx, 0, j))],
        out_specs=pl.BlockSpec((tm, tn), lambda i, j: (i, j)),
        out_shape=jax.ShapeDtypeStruct((m, d), jnp.float32),
        compiler_params=_params(("parallel", "parallel")),
        name="out_resid",
    )(mix2, w, x2, mod3)


def _router_kernel(x_ref, g_ref, sh_ref, sc_ref, wr_ref, br_ref,
                   h_ref, hp_ref, idx_ref, pos_ref, w_ref, cnt_ref, carry_ref):
    i = pl.program_id(0)
    e = N_EXPERTS
    per_g = e // N_GROUPS
    tb = x_ref.shape[0]

    @pl.when(i == 0)
    def _():
        carry_ref[...] = jnp.zeros_like(carry_ref)

    x = x_ref[...]
    ms = jnp.mean(x * x, axis=-1, keepdims=True)
    h = x * lax.rsqrt(ms + EPS) * g_ref[...]
    h = h * (1.0 + sc_ref[0]) + sh_ref[0]
    hb = h.astype(jnp.bfloat16)
    h_ref[...] = hb
    half = h.shape[1] // 2
    _slab_store(hp_ref, _pack_pair(h[:, :half], h[:, half:]), tb, half // V7X_LANES)

    nt = (((1,), (1,)), ((), ()))
    logits = lax.dot_general(wr_ref[...], hb, nt, preferred_element_type=jnp.float32)
    scores = _sigmoid(logits)
    choice = scores + br_ref[...]

    c3 = choice.reshape(N_GROUPS, per_g, tb)
    j_iota = lax.broadcasted_iota(jnp.int32, c3.shape, 1).astype(jnp.float32)
    m1 = c3.max(axis=1, keepdims=True)
    first = jnp.min(jnp.where(c3 == m1, j_iota, float(per_g)), axis=1, keepdims=True)
    m2 = jnp.where(j_iota == first, -jnp.inf, c3).max(axis=1, keepdims=True)
    gs = (m1 + m2).reshape(N_GROUPS, tb)

    g_iota = lax.broadcasted_iota(jnp.int32, gs.shape, 0)
    grank = jnp.zeros(gs.shape, jnp.int32)
    for gp in range(N_GROUPS):
        row = gs[gp:gp + 1, :]
        ahead = (row > gs) | ((row == gs) & (gp < g_iota))
        grank = grank + ahead.astype(jnp.int32)
    gmask = grank < TOPK_GROUPS
    emask = jnp.broadcast_to(gmask.reshape(N_GROUPS, 1, tb), c3.shape).reshape(e, tb)
    masked = jnp.where(emask, choice, -jnp.inf)

    e_iota = lax.broadcasted_iota(jnp.int32, masked.shape, 0)
    erank = jnp.zeros(masked.shape, jnp.int32)
    for ep in range(e):
        row = masked[ep:ep + 1, :]
        ahead = (row > masked) | ((row == masked) & (ep < e_iota))
        erank = erank + ahead.astype(jnp.int32)
    sel = (erank < TOP_K) & emask
    self32 = sel.astype(jnp.float32)

    wsel = scores * self32
    wn = wsel / jnp.sum(wsel, axis=0, keepdims=True) * ROUTED_SCALE

    selb = self32.astype(jnp.bfloat16)
    t_r = lax.broadcasted_iota(jnp.int32, (tb, tb), 0)
    t_c = lax.broadcasted_iota(jnp.int32, (tb, tb), 1)
    upper = (t_r <= t_c).astype(jnp.bfloat16)
    incl = jnp.dot(selb, upper, preferred_element_type=jnp.float32)
    carry = carry_ref[...]
    rank_in_e = carry + incl - 1.0
    carry_new = carry + jnp.sum(self32, axis=1, keepdims=True)
    carry_ref[...] = carry_new
    cnt_ref[...] = jnp.broadcast_to(carry_new, cnt_ref.shape).astype(jnp.int32)

    e_r = lax.broadcasted_iota(jnp.int32, (e, e), 0)
    e_c = lax.broadcasted_iota(jnp.int32, (e, e), 1)
    lower = (e_c < e_r).astype(jnp.bfloat16)
    before = jnp.dot(lower, selb, preferred_element_type=jnp.float32)
    e_f = e_iota.astype(jnp.float32)
    idx_rows, pos_rows, w_rows = [], [], []
    for k in range(TOP_K):
        hit = jnp.where(sel & (before == float(k)), 1.0, 0.0)
        idx_rows.append(jnp.sum(hit * e_f, axis=0, keepdims=True))
        pos_rows.append(jnp.sum(hit * rank_in_e, axis=0, keepdims=True))
        w_rows.append(jnp.sum(hit * wn, axis=0, keepdims=True))
    idx_ref[0] = jnp.concatenate(idx_rows, axis=0).astype(jnp.int32)
    pos_ref[0] = jnp.concatenate(pos_rows, axis=0).astype(jnp.int32)
    w_ref[0] = jnp.concatenate(w_rows, axis=0)


def _router(x1, g, mod3, shift_idx, scale_idx, w_router_t, b_router, seq):
    t, d = x1.shape
    e = N_EXPERTS
    tb = MOE_TILE
    assert seq % tb == 0 and d % (2 * V7X_LANES * V7X_SUBLANES) == 0
    slab = d // 2 // V7X_LANES
    per_b = seq // tb
    nt = t // tb
    return pl.pallas_call(
        _router_kernel,
        grid=(nt,),
        in_specs=[pl.BlockSpec((tb, d), lambda i: (i, 0)),
                  pl.BlockSpec((1, d), lambda i: (0, 0)),
                  pl.BlockSpec((1, 1, d), lambda i: ((i // per_b) * 6 + shift_idx, 0, 0)),
                  pl.BlockSpec((1, 1, d), lambda i: ((i // per_b) * 6 + scale_idx, 0, 0)),
                  pl.BlockSpec((e, d), lambda i: (0, 0)),
                  pl.BlockSpec((e, 1), lambda i: (0, 0))],
        out_specs=[pl.BlockSpec((tb, d), lambda i: (i, 0)),
                   pl.BlockSpec((tb * slab, V7X_LANES), lambda i: (i, 0)),
                   pl.BlockSpec((1, TOP_K, tb), lambda i: (i, 0, 0)),
                   pl.BlockSpec((1, TOP_K, tb), lambda i: (i, 0, 0)),
                   pl.BlockSpec((1, TOP_K, tb), lambda i: (i, 0, 0)),
                   pl.BlockSpec((e, V7X_LANES), lambda i: (0, 0))],
        out_shape=[jax.ShapeDtypeStruct((t, d), jnp.bfloat16),
                   jax.ShapeDtypeStruct((t * slab, V7X_LANES), jnp.int32),
                   jax.ShapeDtypeStruct((nt, TOP_K, tb), jnp.int32),
                   jax.ShapeDtypeStruct((nt, TOP_K, tb), jnp.int32),
                   jax.ShapeDtypeStruct((nt, TOP_K, tb), jnp.float32),
                   jax.ShapeDtypeStruct((e, V7X_LANES), jnp.int32)],
        scratch_shapes=[pltpu.VMEM((e, 1), jnp.float32)],
        compiler_params=_params(("arbitrary",)),
        name="router",
    )(x1, g.reshape(1, d), mod3, mod3, w_router_t, b_router.reshape(e, 1))


def _slots_kernel(pstart_ref, idx_ref, pos_ref, o_ref):
    idx = idx_ref[...]
    base = jnp.zeros(idx.shape, jnp.int32)
    for ex in range(N_EXPERTS):
        base = jnp.where(idx == ex, pstart_ref[ex], base)
    o_ref[...] = base + pos_ref[...]


def _slots(pad_start, idx3, pos3):
    nt = idx3.shape[0]
    blk = (1,) + idx3.shape[1:]
    spec = pl.BlockSpec(blk, lambda i, ps: (i, 0, 0))
    return pl.pallas_call(
        _slots_kernel,
        grid_spec=pltpu.PrefetchScalarGridSpec(num_scalar_prefetch=1, grid=(nt,),
                                               in_specs=[spec, spec], out_specs=spec),
        out_shape=jax.ShapeDtypeStruct(idx3.shape, jnp.int32),
        compiler_params=_params(("parallel",)),
        name="slots",
    )(pad_start, idx3, pos3)


def _dispatch_kernel(pstart_ref, pend_ref, hp_ref, h_ref, wsg_ref, wsu_ref, wsd_ref, slot_hbm,
                     xs_hbm, ysh_ref, slot_s, zero_v, sem_i, sem_z, sem_r, *, slab):
    i = pl.program_id(0)
    tb = hp_ref.shape[0] // slab

    def slot_copy():
        return pltpu.make_async_copy(slot_hbm.at[i], slot_s, sem_i)

    slot_copy().start()

    def zero_copy(ex):
        first = pl.multiple_of((pend_ref[ex] - tb) * slab, tb * slab)
        return pltpu.make_async_copy(zero_v, xs_hbm.at[pl.ds(first, tb * slab)], sem_z)

    @pl.when(i == 0)
    def _():
        zero_v[...] = jnp.zeros_like(zero_v)

        def start(ex, carry):
            @pl.when(pend_ref[ex] > pstart_ref[ex])
            def _():
                zero_copy(ex).start()
            return carry

        def wait(ex, carry):
            @pl.when(pend_ref[ex] > pstart_ref[ex])
            def _():
                zero_copy(ex).wait()
            return carry

        lax.fori_loop(0, N_EXPERTS, start, 0)
        lax.fori_loop(0, N_EXPERTS, wait, 0)

    slot_copy().wait()

    def row_copy(t, k):
        return pltpu.make_async_copy(hp_ref.at[_slab_rows(t, slab)],
                                     xs_hbm.at[_slab_rows(slot_s[k, t], slab)], sem_r)

    def start_rows(t, carry):
        for k in range(TOP_K):
            row_copy(t, k).start(priority=k % 2)
        return carry

    lax.fori_loop(0, tb, start_rows, 0, unroll=2)

    h = h_ref[...]
    g = jnp.dot(h, wsg_ref[...], preferred_element_type=jnp.float32)
    u = jnp.dot(h, wsu_ref[...], preferred_element_type=jnp.float32)
    a = (_silu(g) * u).astype(jnp.bfloat16)
    ysh_ref[...] = jnp.dot(a, wsd_ref[...], preferred_element_type=jnp.float32).astype(ysh_ref.dtype)

    for k in range(TOP_K):
        pltpu.make_async_copy(hp_ref, xs_hbm.at[pl.ds(0, tb * slab)], sem_r).wait()


def _dispatch(pad_start, pad_end, h2p, h2, wsg, wsu, wsd, slot3, n_rows, slab):
    tb = MOE_TILE
    t, d = h2.shape
    f = wsg.shape[1]
    once = pl.Buffered(1)
    return pl.pallas_call(
        functools.partial(_dispatch_kernel, slab=slab),
        grid_spec=pltpu.PrefetchScalarGridSpec(
            num_scalar_prefetch=2,
            grid=(t // tb,),
            in_specs=[pl.BlockSpec((tb * slab, V7X_LANES), lambda i, ps, pe: (i, 0)),
                      pl.BlockSpec((tb, d), lambda i, ps, pe: (i, 0)),
                      pl.BlockSpec((d, f), lambda i, ps, pe: (0, 0), pipeline_mode=once),
                      pl.BlockSpec((d, f), lambda i, ps, pe: (0, 0), pipeline_mode=once),
                      pl.BlockSpec((f, d), lambda i, ps, pe: (0, 0), pipeline_mode=once),
                      pl.BlockSpec(memory_space=pl.ANY)],
            out_specs=[pl.BlockSpec(memory_space=pl.ANY),
                       pl.BlockSpec((tb, d), lambda i, ps, pe: (i, 0))],
            scratch_shapes=[pltpu.SMEM((TOP_K, tb), jnp.int32),
                            pltpu.VMEM((tb * slab, V7X_LANES), jnp.int32),
                            pltpu.SemaphoreType.DMA,
                            pltpu.SemaphoreType.DMA,
                            pltpu.SemaphoreType.DMA],
        ),
        out_shape=[jax.ShapeDtypeStruct((n_rows * slab, V7X_LANES), jnp.int32),
                   jax.ShapeDtypeStruct((t, d), jnp.bfloat16)],
        compiler_params=_params(("arbitrary",)),
        name="dispatch",
    )(pad_start, pad_end, h2p, h2, wsg, wsu, wsd, slot3)


def _expert_kernel(blk_e_ref, nused_ref, first_ref, ord_ref, next_e_ref, xs_hbm, wg_hbm, wu_hbm, wd_hbm,
                   ys_hbm, xt, yt, wg_v, wu_v, wd_v, sem_in, sem_out, sem_w, *, slab):
    j = pl.program_id(0)
    n_used = nused_ref[0]
    tb = xt.shape[1]
    lanes = V7X_LANES
    half = slab * lanes

    def w_copies(e, slot):
        return [pltpu.make_async_copy(src.at[e], dst.at[slot], sem_w.at[slot])
                for src, dst in ((wg_hbm, wg_v), (wu_hbm, wu_v), (wd_hbm, wd_v))]

    def in_copies(tile, buf):
        rows = pl.ds(pl.multiple_of(tile * tb, tb), tb)
        return [pltpu.make_async_copy(xs_hbm.at[rows, s], xt.at[buf, :, pl.ds(s * lanes, lanes)],
                                      sem_in.at[buf]) for s in range(slab)]

    def out_copies(tile, buf):
        rows = pl.ds(pl.multiple_of(tile * tb, tb), tb)
        return [pltpu.make_async_copy(yt.at[buf, :, pl.ds(s * lanes, lanes)], ys_hbm.at[rows, s],
                                      sem_out.at[buf]) for s in range(slab)]

    @pl.when(j == 0)
    def _():
        for c in in_copies(0, 0) + w_copies(blk_e_ref[0], 0):
            c.start()

    @pl.when(j < n_used)
    def _():
        cur = j % 2
        slot = ord_ref[j] % 2

        @pl.when(first_ref[j] == 1)
        def _():
            for c in w_copies(blk_e_ref[j], slot):
                c.wait()

            @pl.when(next_e_ref[j] >= 0)
            def _():
                for c in w_copies(next_e_ref[j], 1 - slot):
                    c.start()

        @pl.when(j + 1 < n_used)
        def _():
            for c in in_copies(j + 1, 1 - cur):
                c.start()

        for c in in_copies(j, cur):
            c.wait()

        @pl.when(j >= 2)
        def _():
            for c in out_copies(j - 2, cur):
                c.wait()

        lo, hi = _unpack_pair(xt[cur])
        lo = lo.astype(jnp.bfloat16)
        hi = hi.astype(jnp.bfloat16)
        g = (jnp.dot(lo, wg_v[slot, :half, :], preferred_element_type=jnp.float32)
             + jnp.dot(hi, wg_v[slot, half:, :], preferred_element_type=jnp.float32))
        u = (jnp.dot(lo, wu_v[slot, :half, :], preferred_element_type=jnp.float32)
             + jnp.dot(hi, wu_v[slot, half:, :], preferred_element_type=jnp.float32))
        a = (_silu(g) * u).astype(jnp.bfloat16)
        y = jnp.dot(a, wd_v[slot], preferred_element_type=jnp.float32)
        yt[cur] = _pack_pair(y[:, :half], y[:, half:])
        for c in out_copies(j, cur):
            c.start(priority=1)

        @pl.when(j == n_used - 1)
        def _():
            @pl.when(j >= 1)
            def _():
                for c in out_copies(j - 1, 1 - cur):
                    c.wait()

            for c in out_copies(j, cur):
                c.wait()


def _experts(blk_e, n_used, xs, wg, wu, wd, slab):
    tb = MOE_TILE
    p = xs.shape[0] // slab
    half = slab * V7X_LANES
    d = 2 * half
    f = wg.shape[2]
    nblk = p // tb

    tiles = jnp.arange(nblk, dtype=jnp.int32)
    used = tiles < n_used[0]
    first = used & ((tiles == 0) | (blk_e != jnp.roll(blk_e, 1)))
    ordinal = jnp.cumsum(first.astype(jnp.int32)) - 1
    first_pos = jnp.where(first, tiles, nblk)
    next_first = lax.cummin(jnp.roll(first_pos, -1).at[-1].set(nblk), reverse=True)
    next_e = jnp.where(next_first < nblk, blk_e[jnp.minimum(next_first, nblk - 1)], -1)

    any_spec = pl.BlockSpec(memory_space=pl.ANY)
    ys = pl.pallas_call(
        functools.partial(_expert_kernel, slab=slab),
        grid_spec=pltpu.PrefetchScalarGridSpec(
            num_scalar_prefetch=5,
            grid=(nblk,),
            in_specs=[any_spec, any_spec, any_spec, any_spec],
            out_specs=any_spec,
            scratch_shapes=[pltpu.VMEM((2, tb, half), jnp.int32),
                            pltpu.VMEM((2, tb, half), jnp.int32),
                            pltpu.VMEM((2, d, f), jnp.bfloat16),
                            pltpu.VMEM((2, d, f), jnp.bfloat16),
                            pltpu.VMEM((2, f, d), jnp.bfloat16),
                            pltpu.SemaphoreType.DMA((2,)),
                            pltpu.SemaphoreType.DMA((2,)),
                            pltpu.SemaphoreType.DMA((2,))],
        ),
        out_shape=jax.ShapeDtypeStruct((p, slab, V7X_LANES), jnp.int32),
        compiler_params=_params(("arbitrary",)),
        name="experts",
    )(blk_e, n_used, first.astype(jnp.int32), ordinal.astype(jnp.int32), next_e.astype(jnp.int32),
      xs.reshape(p, slab, V7X_LANES), wg, wu, wd)
    return ys.reshape(xs.shape)


def _combine_kernel(x_ref, ysh_ref, wt_ref, gt_ref, gf_ref,
                    slot_hbm, ys_hbm, o_ref, slot_s, rows_v, wrep, acc_lo, acc_hi, sem_i, sem_r):
    i = pl.program_id(0)
    n = pl.num_programs(0)
    tb = x_ref.shape[0]
    half = x_ref.shape[1] // 2
    slab = half // V7X_LANES
    per_tile = MOE_TILE // tb
    cur = i % 2
    nxt = 1 - cur

    def slot_copy(step):
        win = pl.ds((step % per_tile) * tb, tb)
        return pltpu.make_async_copy(slot_hbm.at[step // per_tile, :, win], slot_s.at[step % 2],
                                     sem_i.at[step % 2])

    def request_token(buf, t):
        for k in range(TOP_K):
            pltpu.make_async_copy(ys_hbm.at[_slab_rows(slot_s[buf, k, t], slab)],
                                  rows_v.at[buf, k, _slab_rows(t, slab)],
                                  sem_r.at[buf]).start(priority=k % 2)

    def wait_rows(buf):
        for k in range(TOP_K):
            pltpu.make_async_copy(ys_hbm.at[pl.ds(0, tb * slab)], rows_v.at[buf, k], sem_r.at[buf]).wait()

    @pl.when(i == 0)
    def _():
        slot_copy(0).start()
        slot_copy(0).wait()

        def first(t, carry):
            request_token(0, t)
            return carry

        lax.fori_loop(0, tb, first, 0, unroll=2)

        @pl.when(n > 1)
        def _():
            slot_copy(1).start()
            slot_copy(1).wait()

    @pl.when(i + 2 < n)
    def _():
        slot_copy(i + 2).start()

    wt = wt_ref[...]
    for k in range(TOP_K):
        wrep[k] = jnp.broadcast_to(wt[:, k:k + 1], (tb, V7X_LANES))

    def token(t, carry, buf, request_next):
        rows = _slab_rows(t, slab)
        lo_acc = jnp.zeros((slab, V7X_LANES), jnp.float32)
        hi_acc = jnp.zeros((slab, V7X_LANES), jnp.float32)
        for k in range(TOP_K):
            wv = jnp.broadcast_to(wrep[k, pl.ds(t, 1), :], (slab, V7X_LANES))
            lo, hi = _unpack_pair(rows_v[buf, k, rows, :])
            lo_acc = lo_acc + wv * lo
            hi_acc = hi_acc + wv * hi
        acc_lo[rows, :] = lo_acc
        acc_hi[rows, :] = hi_acc
        if request_next:
            request_token(1 - buf, t)
        return carry

    for buf in range(2):
        for request_next in (True, False):
            @pl.when((cur == buf) & ((i + 1 < n) == request_next))
            def _():
                wait_rows(buf)
                lax.fori_loop(0, tb, functools.partial(token, buf=buf, request_next=request_next),
                              0, unroll=4)

    @pl.when(i + 2 < n)
    def _():
        slot_copy(i + 2).wait()

    yy = ysh_ref[...].astype(jnp.float32) + jnp.concatenate(
        [_slab_load(acc_lo, tb, slab), _slab_load(acc_hi, tb, slab)], axis=1)
    xo = x_ref[...] + gt_ref[0] * yy
    ms = jnp.mean(xo * xo, axis=-1, keepdims=True)
    o_ref[...] = xo * lax.rsqrt(ms + EPS) * gf_ref[...]


def _combine(x1, y_shared, w_t, mod3, gate_idx, g_final, slot3, ys, seq):
    t, d = x1.shape
    tb = COMBINE_TILE
    assert MOE_TILE % tb == 0 and seq % tb == 0
    per_b = seq // tb
    half = d // 2
    return pl.pallas_call(
        _combine_kernel,
        grid=(t // tb,),
        in_specs=[pl.BlockSpec((tb, d), lambda i: (i, 0)),
                  pl.BlockSpec((tb, d), lambda i: (i, 0)),
                  pl.BlockSpec((tb, TOP_K), lambda i: (i, 0)),
                  pl.BlockSpec((1, 1, d), lambda i: ((i // per_b) * 6 + gate_idx, 0, 0)),
                  pl.BlockSpec((1, d), lambda i: (0, 0)),
                  pl.BlockSpec(memory_space=pl.ANY),
                  pl.BlockSpec(memory_space=pl.ANY)],
        out_specs=pl.BlockSpec((tb, d), lambda i: (i, 0)),
        scratch_shapes=[pltpu.SMEM((2, TOP_K, tb), jnp.int32),
                        pltpu.VMEM((2, TOP_K, tb * half // V7X_LANES, V7X_LANES), jnp.int32),
                        pltpu.VMEM((TOP_K, tb, V7X_LANES), jnp.float32),
                        pltpu.VMEM((tb * half // V7X_LANES, V7X_LANES), jnp.float32),
                        pltpu.VMEM((tb * half // V7X_LANES, V7X_LANES), jnp.float32),
                        pltpu.SemaphoreType.DMA((2,)),
                        pltpu.SemaphoreType.DMA((2,))],
        out_shape=jax.ShapeDtypeStruct((t, d), jnp.float32),
        compiler_params=_params(("arbitrary",)),
        name="combine",
    )(x1, y_shared, w_t, mod3, g_final.reshape(1, d), slot3, ys)


def _in_layout(d):
    qw = ATT_HEADS * ATT_HEAD_DIM
    kvw = ATT_KV_HEADS * ATT_HEAD_DIM
    rqk = RET_HEADS * RET_QK_DIM
    rv = RET_HEADS * RET_V_DIM
    order = [("qa", qw), ("ka", kvw), ("va", kvw), ("qr", rqk), ("kr", rqk),
             ("vr", rv), ("gr", rv), ("ga", d), ("gb", d)]
    dst = {}
    off = 0
    for name, width in order:
        dst[name] = off
        off += width
    return dst


def kernel(x, c, w_ada, b_ada, g_norm_mix, w_in, attn_sinks, w_attn_out, w_ret_out, w_o, g_norm_ffn,
           w_router, b_router, w_gate, w_up, w_down, w_sh_gate, w_sh_up, w_sh_down, g_norm_final):
    b, s, d = x.shape
    t = b * s
    depth = w_ada.shape[0]
    bf = jnp.bfloat16
    dst = _in_layout(d)

    c_pad = jnp.zeros((V7X_SUBLANES, d), jnp.float32).at[:b].set(c)
    x2 = x.reshape(t, d)
    for l in range(depth):
        mod = _ada(c_pad, w_ada[l], b_ada[l])
        mod3 = mod[:b].reshape(b * 6, 1, d)

        h = _norm_mod(x2.reshape(b, s, d), g_norm_mix[l], mod3, 0, 1)
        proj, (wg_b, wu_b, wd_b) = _in_proj(h.reshape(t, d), w_in[l].astype(bf),
                                            ((w_gate[l], w_up[l], w_down[l]),), 1024, 768)
        wao_b, wro_b, wo_b = w_attn_out[l].astype(bf), w_ret_out[l].astype(bf), w_o[l].astype(bf)
        wsg_b, wsu_b, wsd_b = w_sh_gate[l].astype(bf), w_sh_up[l].astype(bf), w_sh_down[l].astype(bf)
        proj3 = proj.reshape(b, s, proj.shape[1])
        attn = _attention(proj3, attn_sinks[l], dst["qa"], dst["ka"], dst["va"])
        ret = _retention(proj3, dst["qr"], dst["kr"], dst["vr"], dst["gr"])
        mix = _mix(attn.reshape(t, -1), ret.reshape(t, -1), wao_b, wro_b, proj, dst["ga"], dst["gb"])
        x1 = _out_resid(mix, wo_b, x2, mod3, 2, s)

        h2, h2p, idx3, pos3, w3, cnt = _router(x1, g_norm_ffn[l], mod3, 3, 4,
                                               w_router[l].T.astype(bf), b_router[l], s)
        counts = cnt[:, 0]
        tile = MOE_TILE
        padded = (counts + tile - 1) // tile * tile
        pad_end = jnp.cumsum(padded).astype(jnp.int32)
        pad_start = (pad_end - padded).astype(jnp.int32)
        n_blocks = (t * TOP_K) // tile + N_EXPERTS
        n_used = (pad_end[-1] // tile).reshape(1).astype(jnp.int32)
        blk_first = jnp.arange(n_blocks, dtype=jnp.int32) * tile
        blk_e = jnp.minimum(jnp.sum((pad_end[None, :] <= blk_first[:, None]).astype(jnp.int32), axis=1),
                            N_EXPERTS - 1)
        slab = d // 2 // V7X_LANES
        slot3 = _slots(pad_start, idx3, pos3)
        xs, y_shared = _dispatch(pad_start, pad_end, h2p, h2, wsg_b, wsu_b, wsd_b, slot3,
                                 n_blocks * tile, slab)
        ys = _experts(blk_e, n_used, xs, wg_b, wu_b, wd_b, slab)
        w_t = w3.transpose(0, 2, 1).reshape(t, TOP_K)
        is_last = l == depth - 1
        assert is_last, "the final norm is fused into the last layer's combine"
        x2 = _combine(x1, y_shared, w_t, mod3, 5, g_norm_final, slot3, ys, s)
    return x2.reshape(b, s, d)
```

```python
import functools
import math

import jax
import jax.numpy as jnp
from jax import lax
from jax.experimental import pallas as pl
from jax.experimental.pallas import tpu as pltpu

ATT_HEADS = 32
ATT_KV_HEADS = 4
ATT_HEAD_DIM = 64
WINDOW = 128
ATT_BLOCK = 128
RET_HEADS = 8
RET_QK_DIM = 256
RET_V_DIM = 512
RET_CHUNK = 128
N_EXPERTS = 64
N_GROUPS = 8
TOPK_GROUPS = 4
TOP_K = 8
ROUTED_SCALE = 2.5
EPS = 1e-6

V7X_LANES = 128
V7X_SUBLANES = 8
V7X_VMEM_LIMIT_BYTES = 60000 * 1024

MOE_TILE = 256
COMBINE_TILE = 128
N_IN_TILES = 3
NEG_BIG = -1e30


def _div_block(n, target, align):
    best = None
    b = align
    while b <= min(n, target):
        if n % b == 0:
            best = b
        b += align
    assert best is not None, (n, target, align)
    return best


def _params(semantics):
    return pltpu.CompilerParams(dimension_semantics=semantics,
                                vmem_limit_bytes=V7X_VMEM_LIMIT_BYTES)


def _sigmoid(v):
    return 1.0 / (1.0 + jnp.exp(-v))


def _silu(v):
    return v * _sigmoid(v)


def _pack_pair(lo, hi):
    return pltpu.pack_elementwise([lo, hi], packed_dtype=jnp.bfloat16)


def _unpack_pair(p):
    lo = pltpu.unpack_elementwise(p, index=0, packed_dtype=jnp.bfloat16, unpacked_dtype=jnp.float32)
    hi = pltpu.unpack_elementwise(p, index=1, packed_dtype=jnp.bfloat16, unpacked_dtype=jnp.float32)
    return lo, hi


def _slab_load(ref, n_rows, slab):
    return jnp.concatenate([ref[pl.ds(s, n_rows, stride=slab), :] for s in range(slab)], axis=1)


def _slab_store(ref, val, n_rows, slab):
    for s in range(slab):
        ref[pl.ds(s, n_rows, stride=slab), :] = val[:, s * V7X_LANES:(s + 1) * V7X_LANES]


def _slab_rows(r, slab):
    return pl.ds(pl.multiple_of(r * slab, slab), slab)


def _ada_kernel(c_ref, w_ref, b_ref, o_ref):
    cs = _silu(c_ref[...]).astype(jnp.bfloat16)
    o_ref[...] = jnp.dot(cs, w_ref[...].astype(jnp.bfloat16),
                         preferred_element_type=jnp.float32) + b_ref[...]


def _ada(c_pad, w, b):
    m, d = c_pad.shape
    n = w.shape[1]
    tn = _div_block(n, 512, V7X_LANES)
    return pl.pallas_call(
        _ada_kernel,
        grid=(n // tn,),
        in_specs=[pl.BlockSpec((m, d), lambda j: (0, 0)),
                  pl.BlockSpec((d, tn), lambda j: (0, j)),
                  pl.BlockSpec((1, tn), lambda j: (0, j))],
        out_specs=pl.BlockSpec((m, tn), lambda j: (0, j)),
        out_shape=jax.ShapeDtypeStruct((m, n), jnp.float32),
        compiler_params=_params(("parallel",)),
        name="ada",
    )(c_pad, w, b.reshape(1, n))


def _norm_mod_kernel(x_ref, g_ref, sh_ref, sc_ref, o_ref):
    x = x_ref[0]
    ms = jnp.mean(x * x, axis=-1, keepdims=True)
    y = x * lax.rsqrt(ms + EPS) * g_ref[...]
    o_ref[0] = (y * (1.0 + sc_ref[0]) + sh_ref[0]).astype(o_ref.dtype)


def _norm_mod(x3, g, mod3, shift_idx, scale_idx):
    b, s, d = x3.shape
    ts = _div_block(s, 256, V7X_SUBLANES)
    return pl.pallas_call(
        _norm_mod_kernel,
        grid=(b, s // ts),
        in_specs=[pl.BlockSpec((1, ts, d), lambda bi, i: (bi, i, 0)),
                  pl.BlockSpec((1, d), lambda bi, i: (0, 0)),
                  pl.BlockSpec((1, 1, d), lambda bi, i: (bi * 6 + shift_idx, 0, 0)),
                  pl.BlockSpec((1, 1, d), lambda bi, i: (bi * 6 + scale_idx, 0, 0))],
        out_specs=pl.BlockSpec((1, ts, d), lambda bi, i: (bi, i, 0)),
        out_shape=jax.ShapeDtypeStruct((b, s, d), jnp.bfloat16),
        compiler_params=_params(("parallel", "parallel")),
        name="norm_mod",
    )(x3, g.reshape(1, d), mod3, mod3)


def _in_proj_kernel(a_ref, b_ref, *rest, windows, n_j):
    n_side = len(windows)
    src = rest[:n_side]
    o_ref = rest[n_side]
    dst = rest[n_side + 1:]
    o_ref[...] = jnp.dot(a_ref[...], b_ref[...],
                         preferred_element_type=jnp.float32).astype(o_ref.dtype)
    step = pl.program_id(0) * n_j + pl.program_id(1)

    for s_ref, d_ref, (first, count) in zip(src, dst, windows):
        @pl.when((step >= first) & (step < first + count))
        def _():
            d_ref[...] = s_ref[...].astype(d_ref.dtype)


def _in_proj(a, b, side_groups, tm_target, tn_target):
    m, k = a.shape
    n = b.shape[1]
    tm = _div_block(m, tm_target, V7X_SUBLANES)
    tn = _div_block(n, tn_target, V7X_LANES)
    n_i, n_j = m // tm, n // tn
    steps_left = n_i * n_j
    first = 0
    side, chunked, windows = [], [], []
    for group in side_groups:
        n_max = 1 << (steps_left.bit_length() - 1)
        used = 0
        for w in group:
            rows = w.size // w.shape[-1]
            count = n_max
            while rows % (count * 2 * V7X_SUBLANES):
                count //= 2
            side.append(w)
            chunked.append(w.reshape(count, rows // count, w.shape[-1]))
            windows.append((first, count))
            used = max(used, count)
        first += used
        steps_left -= used

    def side_spec(c, window):
        w_first, count = window
        return pl.BlockSpec((1,) + c.shape[1:],
                            lambda i, j: (jnp.clip(i * n_j + j - w_first, 0, count - 1), 0, 0))

    side_specs = [side_spec(c, w) for c, w in zip(chunked, windows)]
    outs = pl.pallas_call(
        functools.partial(_in_proj_kernel, windows=tuple(windows), n_j=n_j),
        grid=(n_i, n_j),
        in_specs=[pl.BlockSpec((tm, k), lambda i, j: (i, 0)),
                  pl.BlockSpec((k, tn), lambda i, j: (0, j))] + side_specs,
        out_specs=[pl.BlockSpec((tm, tn), lambda i, j: (i, j))] + side_specs,
        out_shape=[jax.ShapeDtypeStruct((m, n), jnp.bfloat16)]
        + [jax.ShapeDtypeStruct(c.shape, jnp.bfloat16) for c in chunked],
        compiler_params=_params(("arbitrary", "arbitrary")),
        name="in_proj",
    )(a, b, *chunked)
    return outs[0], [o.reshape(w.shape) for o, w in zip(outs[1:], side)]


def _attn_kernel(sink_ref, q_ref, kc_ref, kp_ref, vc_ref, vp_ref, bias_ref, o_ref):
    i = pl.program_id(1)
    blk = ATT_BLOCK
    hd = ATT_HEAD_DIM
    group = ATT_HEADS // ATT_KV_HEADS
    pairs = group // 2
    nt = (((1,), (1,)), ((), ()))
    tn = (((0,), (0,)), ((), ()))
    zpad = jnp.zeros((2 * blk, hd), jnp.bfloat16)

    def scores(h, par):
        sl = slice(h * hd, (h + 1) * hd)
        k2 = jnp.concatenate([kp_ref[0, :, sl], kc_ref[0, :, sl]], axis=0) * (hd ** -0.5)
        qp = jnp.concatenate([q_ref[0, :, (h * pairs + p) * 2 * hd:(h * pairs + p + 1) * 2 * hd]
                              for p in range(pairs)], axis=0)
        kz = jnp.concatenate([k2, zpad] if par == 0 else [zpad, k2], axis=1)
        return lax.dot_general(kz, qp, nt, preferred_element_type=jnp.float32)

    def softmax(s, h, par):
        s = s + bias_ref[h * 2 + par]
        s = jnp.concatenate([jnp.where(i == 0, NEG_BIG, s[:blk]), s[blk:]], axis=0)
        sink = jnp.concatenate([jnp.full((1, blk), sink_ref[h * group + 2 * p + par], jnp.float32)
                                for p in range(pairs)], axis=1)
        m = jnp.maximum(s.max(0, keepdims=True), sink)
        pr = jnp.exp(s - m)
        denom = pr.sum(0, keepdims=True) + jnp.exp(sink - m)
        return pr.astype(jnp.bfloat16), 1.0 / denom

    def values(pr, inv, h, par):
        sl = slice(h * hd, (h + 1) * hd)
        v2 = jnp.concatenate([vp_ref[0, :, sl], vc_ref[0, :, sl]], axis=0)
        vz = jnp.concatenate([v2, zpad] if par == 0 else [zpad, v2], axis=1)
        return lax.dot_general(vz, pr, tn, preferred_element_type=jnp.float32) * inv

    items = [(h, par) for h in range(ATT_KV_HEADS) for par in range(2)]
    s_of, p_of, acc = {}, {}, {}
    for n in range(len(items) + 2):
        if n < len(items):
            s_of[n] = scores(*items[n])
        if 1 <= n <= len(items):
            p_of[n - 1] = softmax(s_of.pop(n - 1), *items[n - 1])
        if n >= 2:
            h, par = items[n - 2]
            o = values(*p_of.pop(n - 2), h, par)
            acc[h] = o if par == 0 else acc[h] + o
            if par == 1:
                out = acc.pop(h)
                for p in range(pairs):
                    o_ref[0, :, (h * pairs + p) * 2 * hd:(h * pairs + p + 1) * 2 * hd] = (
                        out[:, p * blk:(p + 1) * blk].T.astype(o_ref.dtype))


def _attn_bias():
    blk = ATT_BLOCK
    group = ATT_HEADS // ATT_KV_HEADS
    pairs = group // 2
    qi = jnp.arange(blk)[:, None]
    kj = jnp.arange(2 * blk)[None, :]
    dist = qi + blk - kj
    valid = (dist >= 0) & (dist < WINDOW)
    slopes = jnp.exp2(-8.0 * jnp.arange(1, ATT_HEADS + 1, dtype=jnp.float32) / ATT_HEADS)
    slopes = slopes.reshape(ATT_KV_HEADS, pairs, 2)
    bias = jnp.where(valid, -slopes[..., None, None] * dist.astype(jnp.float32), NEG_BIG)
    return bias.transpose(0, 2, 4, 1, 3).reshape(ATT_KV_HEADS * 2, 2 * blk, pairs * blk)


def _attention(proj3, sinks, q_off, k_off, v_off):
    b, s, _ = proj3.shape
    qw = ATT_HEADS * ATT_HEAD_DIM
    kvw = ATT_KV_HEADS * ATT_HEAD_DIM
    nb = s // ATT_BLOCK
    group = ATT_HEADS // ATT_KV_HEADS
    assert q_off % qw == 0 and k_off % kvw == 0 and v_off % kvw == 0
    assert group % 2 == 0 and 2 * ATT_HEAD_DIM == V7X_LANES and WINDOW == ATT_BLOCK
    assert 4 ** round(math.log(ATT_HEAD_DIM, 4)) == ATT_HEAD_DIM, "score scale must be a power of two"
    qb, kb, vb = q_off // qw, k_off // kvw, v_off // kvw
    bias = _attn_bias()
    return pl.pallas_call(
        _attn_kernel,
        grid_spec=pltpu.PrefetchScalarGridSpec(
            num_scalar_prefetch=1,
            grid=(b, nb),
            in_specs=[pl.BlockSpec((1, ATT_BLOCK, qw), lambda bi, i, sk: (bi, i, qb)),
                      pl.BlockSpec((1, ATT_BLOCK, kvw), lambda bi, i, sk: (bi, i, kb)),
                      pl.BlockSpec((1, ATT_BLOCK, kvw), lambda bi, i, sk: (bi, jnp.maximum(i - 1, 0), kb)),
                      pl.BlockSpec((1, ATT_BLOCK, kvw), lambda bi, i, sk: (bi, i, vb)),
                      pl.BlockSpec((1, ATT_BLOCK, kvw), lambda bi, i, sk: (bi, jnp.maximum(i - 1, 0), vb)),
                      pl.BlockSpec(bias.shape, lambda bi, i, sk: (0, 0, 0), pipeline_mode=pl.Buffered(1))],
            out_specs=pl.BlockSpec((1, ATT_BLOCK, qw), lambda bi, i, sk: (bi, i, 0)),
        ),
        out_shape=jax.ShapeDtypeStruct((b, s, qw), jnp.bfloat16),
        compiler_params=_params(("parallel", "parallel")),
        name="attention",
    )(sinks, proj3, proj3, proj3, proj3, proj3, bias)


def _ret_kernel(q_ref, k_ref, v_ref, gr_ref, mask_ref, qd_ref, kd_ref, cd_ref, o_ref, state_ref):
    c = pl.program_id(1)

    @pl.when(c == 0)
    def _():
        state_ref[...] = jnp.zeros_like(state_ref)

    nt = (((1,), (1,)), ((), ()))
    tn = (((0,), (0,)), ((), ()))
    def decayed(bi):
        q = q_ref[bi]
        k = k_ref[bi]
        attn = lax.dot_general(q, k, nt, preferred_element_type=jnp.float32) * mask_ref[0]
        kd = (k.astype(jnp.float32) * kd_ref[0]).astype(k.dtype)
        return attn.astype(jnp.bfloat16), kd

    def recur(bi, attn, kd):
        q = q_ref[bi]
        v = v_ref[bi]
        intra = jnp.dot(attn, v, preferred_element_type=jnp.float32)
        state = state_ref[bi]
        inter = jnp.dot(q, state.astype(q.dtype), preferred_element_type=jnp.float32) * qd_ref[0]
        state_ref[bi] = state * cd_ref[0] + lax.dot_general(kd, v, tn, preferred_element_type=jnp.float32)
        return intra + inter

    def finish(bi, o):
        mu = jnp.mean(o, axis=-1, keepdims=True)
        oc = o - mu
        var = jnp.mean(oc * oc, axis=-1, keepdims=True)
        y = oc * lax.rsqrt(var + EPS)
        o_ref[bi] = (_silu(gr_ref[bi].astype(jnp.float32)) * y).astype(o_ref.dtype)

    nb = q_ref.shape[0]
    a_of, o_of = {}, {}
    for n in range(nb + 2):
        if n < nb:
            a_of[n] = decayed(n)
        if 1 <= n <= nb:
            o_of[n - 1] = recur(n - 1, *a_of.pop(n - 1))
        if n >= 2:
            finish(n - 2, o_of.pop(n - 2))


def _retention(proj3, q_off, k_off, v_off, g_off):
    b, s, _ = proj3.shape
    dk, dv, ch = RET_QK_DIM, RET_V_DIM, RET_CHUNK
    assert q_off % dk == 0 and k_off % dk == 0 and v_off % dv == 0 and g_off % dv == 0
    qb, kb, vb, gb = q_off // dk, k_off // dk, v_off // dv, g_off // dv
    n = s // ch
    log_g = jnp.log1p(-jnp.exp2(-5.0 - jnp.arange(RET_HEADS, dtype=jnp.float32)))
    pos = jnp.arange(ch, dtype=jnp.float32)
    rel = pos[:, None] - pos[None, :]
    scale = dk ** -0.5
    mask = jnp.where(rel[None] >= 0, jnp.exp(rel[None] * log_g[:, None, None]), 0.0) * scale
    q_decay = jnp.exp((pos[None, :, None] + 1.0) * log_g[:, None, None])
    k_decay = jnp.exp((ch - 1.0 - pos[None, :, None]) * log_g[:, None, None]) * scale
    c_decay = jnp.exp(ch * log_g)[:, None, None]
    return pl.pallas_call(
        _ret_kernel,
        grid=(RET_HEADS, n),
        in_specs=[pl.BlockSpec((b, ch, dk), lambda h, c: (0, c, qb + h)),
                  pl.BlockSpec((b, ch, dk), lambda h, c: (0, c, kb + h)),
                  pl.BlockSpec((b, ch, dv), lambda h, c: (0, c, vb + h)),
                  pl.BlockSpec((b, ch, dv), lambda h, c: (0, c, gb + h)),
                  pl.BlockSpec((1, ch, ch), lambda h, c: (h, 0, 0)),
                  pl.BlockSpec((1, ch, 1), lambda h, c: (h, 0, 0)),
                  pl.BlockSpec((1, ch, 1), lambda h, c: (h, 0, 0)),
                  pl.BlockSpec((1, 1, 1), lambda h, c: (h, 0, 0))],
        out_specs=pl.BlockSpec((b, ch, dv), lambda h, c: (0, c, h)),
        out_shape=jax.ShapeDtypeStruct((b, s, RET_HEADS * dv), jnp.bfloat16),
        scratch_shapes=[pltpu.VMEM((b, dk, dv), jnp.float32)],
        compiler_params=_params(("parallel", "arbitrary")),
        name="retention",
    )(proj3, proj3, proj3, proj3, mask, q_decay, k_decay, c_decay)


def _mix_kernel(a_ref, r_ref, wa_ref, wr_ref, ga_ref, gb_ref, o_ref):
    ya = jnp.dot(a_ref[...], wa_ref[...], preferred_element_type=jnp.float32)
    yr = jnp.dot(r_ref[...], wr_ref[...], preferred_element_type=jnp.float32)
    ga = _sigmoid(ga_ref[...].astype(jnp.float32))
    gb = _sigmoid(gb_ref[...].astype(jnp.float32))
    o_ref[...] = (ga * ya + gb * yr).astype(o_ref.dtype)


def _mix(attn2, ret2, wa, wr, proj2, ga_off, gb_off):
    m, ka = attn2.shape
    kr = ret2.shape[1]
    d = wa.shape[1]
    tm = _div_block(m, 1024, V7X_SUBLANES)
    tn = _div_block(d, 512, V7X_LANES)
    assert ga_off % tn == 0 and gb_off % tn == 0
    gab, gbb = ga_off // tn, gb_off // tn
    return pl.pallas_call(
        _mix_kernel,
        grid=(m // tm, d // tn),
        in_specs=[pl.BlockSpec((tm, ka), lambda i, j: (i, 0)),
                  pl.BlockSpec((tm, kr), lambda i, j: (i, 0)),
                  pl.BlockSpec((ka, tn), lambda i, j: (0, j)),
                  pl.BlockSpec((kr, tn), lambda i, j: (0, j)),
                  pl.BlockSpec((tm, tn), lambda i, j: (i, gab + j)),
                  pl.BlockSpec((tm, tn), lambda i, j: (i, gbb + j))],
        out_specs=pl.BlockSpec((tm, tn), lambda i, j: (i, j)),
        out_shape=jax.ShapeDtypeStruct((m, d), jnp.bfloat16),
        compiler_params=_params(("parallel", "parallel")),
        name="mix",
    )(attn2, ret2, wa, wr, proj2, proj2)


def _resid_kernel(a_ref, w_ref, x_ref, gt_ref, o_ref):
    y = jnp.dot(a_ref[...], w_ref[...], preferred_element_type=jnp.float32)
    o_ref[...] = x_ref[...] + gt_ref[0] * y


def _out_resid(mix2, w, x2, mod3, gate_idx, seq):
    m, k = mix2.shape
    d = w.shape[1]
    tm = _div_block(seq, 1024, V7X_SUBLANES)
    tn = _div_block(d, 1024, V7X_LANES)
    per_b = seq // tm
    return pl.pallas_call(
        _resid_kernel,
        grid=(m // tm, d // tn),
        in_specs=[pl.BlockSpec((tm, k), lambda i, j: (i, 0)),
                  pl.BlockSpec((k, tn), lambda i, j: (0, j)),
                  pl.BlockSpec((tm, tn), lambda i, j: (i, j)),
                  pl.BlockSpec((1, 1, tn), lambda i, j: ((i // per_b) * 6 + gate_idx, 0, j))],
        out_specs=pl.BlockSpec((tm, tn), lambda i, j: (i, j)),
        out_shape=jax.ShapeDtypeStruct((m, d), jnp.float32),
        compiler_params=_params(("parallel", "parallel")),
        name="out_resid",
    )(mix2, w, x2, mod3)


def _router_kernel(x_ref, g_ref, sh_ref, sc_ref, wr_ref, br_ref,
                   h_ref, hp_ref, idx_ref, pos_ref, w_ref, cnt_ref, carry_ref):
    i = pl.program_id(0)
    e = N_EXPERTS
    per_g = e // N_GROUPS
    tb = x_ref.shape[0]

    @pl.when(i == 0)
    def _():
        carry_ref[...] = jnp.zeros_like(carry_ref)

    x = x_ref[...]
    ms = jnp.mean(x * x, axis=-1, keepdims=True)
    h = x * lax.rsqrt(ms + EPS) * g_ref[...]
    h = h * (1.0 + sc_ref[0]) + sh_ref[0]
    hb = h.astype(jnp.bfloat16)
    h_ref[...] = hb
    half = h.shape[1] // 2
    _slab_store(hp_ref, _pack_pair(h[:, :half], h[:, half:]), tb, half // V7X_LANES)

    nt = (((1,), (1,)), ((), ()))
    logits = lax.dot_general(wr_ref[...], hb, nt, preferred_element_type=jnp.float32)
    scores = _sigmoid(logits)
    choice = scores + br_ref[...]

    c3 = choice.reshape(N_GROUPS, per_g, tb)
    j_iota = lax.broadcasted_iota(jnp.int32, c3.shape, 1).astype(jnp.float32)
    m1 = c3.max(axis=1, keepdims=True)
    first = jnp.min(jnp.where(c3 == m1, j_iota, float(per_g)), axis=1, keepdims=True)
    m2 = jnp.where(j_iota == first, -jnp.inf, c3).max(axis=1, keepdims=True)
    gs = (m1 + m2).reshape(N_GROUPS, tb)

    g_iota = lax.broadcasted_iota(jnp.int32, gs.shape, 0)
    grank = jnp.zeros(gs.shape, jnp.int32)
    for gp in range(N_GROUPS):
        row = gs[gp:gp + 1, :]
        ahead = (row > gs) | ((row == gs) & (gp < g_iota))
        grank = grank + ahead.astype(jnp.int32)
    gmask = grank < TOPK_GROUPS
    emask = jnp.broadcast_to(gmask.reshape(N_GROUPS, 1, tb), c3.shape).reshape(e, tb)
    masked = jnp.where(emask, choice, -jnp.inf)

    e_iota = lax.broadcasted_iota(jnp.int32, masked.shape, 0)
    erank = jnp.zeros(masked.shape, jnp.int32)
    for ep in range(e):
        row = masked[ep:ep + 1, :]
        ahead = (row > masked) | ((row == masked) & (ep < e_iota))
        erank = erank + ahead.astype(jnp.int32)
    sel = (erank < TOP_K) & emask
    self32 = sel.astype(jnp.float32)

    wsel = scores * self32
    wn = wsel / jnp.sum(wsel, axis=0, keepdims=True) * ROUTED_SCALE

    selb = self32.astype(jnp.bfloat16)
    t_r = lax.broadcasted_iota(jnp.int32, (tb, tb), 0)
    t_c = lax.broadcasted_iota(jnp.int32, (tb, tb), 1)
    upper = (t_r <= t_c).astype(jnp.bfloat16)
    incl = jnp.dot(selb, upper, preferred_element_type=jnp.float32)
    carry = carry_ref[...]
    rank_in_e = carry + incl - 1.0
    carry_new = carry + jnp.sum(self32, axis=1, keepdims=True)
    carry_ref[...] = carry_new
    cnt_ref[...] = jnp.broadcast_to(carry_new, cnt_ref.shape).astype(jnp.int32)

    e_r = lax.broadcasted_iota(jnp.int32, (e, e), 0)
    e_c = lax.broadcasted_iota(jnp.int32, (e, e), 1)
    lower = (e_c < e_r).astype(jnp.bfloat16)
    before = jnp.dot(lower, selb, preferred_element_type=jnp.float32)
    e_f = e_iota.astype(jnp.float32)
    idx_rows, pos_rows, w_rows = [], [], []
    for k in range(TOP_K):
        hit = jnp.where(sel & (before == float(k)), 1.0, 0.0)
        idx_rows.append(jnp.sum(hit * e_f, axis=0, keepdims=True))
        pos_rows.append(jnp.sum(hit * rank_in_e, axis=0, keepdims=True))
        w_rows.append(jnp.sum(hit * wn, axis=0, keepdims=True))
    idx_ref[0] = jnp.concatenate(idx_rows, axis=0).astype(jnp.int32)
    pos_ref[0] = jnp.concatenate(pos_rows, axis=0).astype(jnp.int32)
    w_ref[0] = jnp.concatenate(w_rows, axis=0)


def _router(x1, g, mod3, shift_idx, scale_idx, w_router_t, b_router, seq):
    t, d = x1.shape
    e = N_EXPERTS
    tb = MOE_TILE
    assert seq % tb == 0 and d % (2 * V7X_LANES * V7X_SUBLANES) == 0
    slab = d // 2 // V7X_LANES
    per_b = seq // tb
    nt = t // tb
    return pl.pallas_call(
        _router_kernel,
        grid=(nt,),
        in_specs=[pl.BlockSpec((tb, d), lambda i: (i, 0)),
                  pl.BlockSpec((1, d), lambda i: (0, 0)),
                  pl.BlockSpec((1, 1, d), lambda i: ((i // per_b) * 6 + shift_idx, 0, 0)),
                  pl.BlockSpec((1, 1, d), lambda i: ((i // per_b) * 6 + scale_idx, 0, 0)),
                  pl.BlockSpec((e, d), lambda i: (0, 0)),
                  pl.BlockSpec((e, 1), lambda i: (0, 0))],
        out_specs=[pl.BlockSpec((tb, d), lambda i: (i, 0)),
                   pl.BlockSpec((tb * slab, V7X_LANES), lambda i: (i, 0)),
                   pl.BlockSpec((1, TOP_K, tb), lambda i: (i, 0, 0)),
                   pl.BlockSpec((1, TOP_K, tb), lambda i: (i, 0, 0)),
                   pl.BlockSpec((1, TOP_K, tb), lambda i: (i, 0, 0)),
                   pl.BlockSpec((e, V7X_LANES), lambda i: (0, 0))],
        out_shape=[jax.ShapeDtypeStruct((t, d), jnp.bfloat16),
                   jax.ShapeDtypeStruct((t * slab, V7X_LANES), jnp.int32),
                   jax.ShapeDtypeStruct((nt, TOP_K, tb), jnp.int32),
                   jax.ShapeDtypeStruct((nt, TOP_K, tb), jnp.int32),
                   jax.ShapeDtypeStruct((nt, TOP_K, tb), jnp.float32),
                   jax.ShapeDtypeStruct((e, V7X_LANES), jnp.int32)],
        scratch_shapes=[pltpu.VMEM((e, 1), jnp.float32)],
        compiler_params=_params(("arbitrary",)),
        name="router",
    )(x1, g.reshape(1, d), mod3, mod3, w_router_t, b_router.reshape(e, 1))


def _slots_kernel(pstart_ref, idx_ref, pos_ref, o_ref):
    idx = idx_ref[...]
    base = jnp.zeros(idx.shape, jnp.int32)
    for ex in range(N_EXPERTS):
        base = jnp.where(idx == ex, pstart_ref[ex], base)
    o_ref[...] = base + pos_ref[...]


def _slots(pad_start, idx3, pos3):
    nt = idx3.shape[0]
    blk = (1,) + idx3.shape[1:]
    spec = pl.BlockSpec(blk, lambda i, ps: (i, 0, 0))
    return pl.pallas_call(
        _slots_kernel,
        grid_spec=pltpu.PrefetchScalarGridSpec(num_scalar_prefetch=1, grid=(nt,),
                                               in_specs=[spec, spec], out_specs=spec),
        out_shape=jax.ShapeDtypeStruct(idx3.shape, jnp.int32),
        compiler_params=_params(("parallel",)),
        name="slots",
    )(pad_start, idx3, pos3)


def _dispatch_kernel(pstart_ref, pend_ref, hp_ref, h_ref, wsg_ref, wsu_ref, wsd_ref, slot_hbm,
                     xs_hbm, ysh_ref, slot_s, zero_v, sem_i, sem_z, sem_r, *, slab):
    i = pl.program_id(0)
    tb = hp_ref.shape[0] // slab

    def slot_copy():
        return pltpu.make_async_copy(slot_hbm.at[i], slot_s, sem_i)

    slot_copy().start()

    def zero_copy(ex):
        first = pl.multiple_of((pend_ref[ex] - tb) * slab, tb * slab)
        return pltpu.make_async_copy(zero_v, xs_hbm.at[pl.ds(first, tb * slab)], sem_z)

    @pl.when(i == 0)
    def _():
        zero_v[...] = jnp.zeros_like(zero_v)

        def start(ex, carry):
            @pl.when(pend_ref[ex] > pstart_ref[ex])
            def _():
                zero_copy(ex).start()
            return carry

        def wait(ex, carry):
            @pl.when(pend_ref[ex] > pstart_ref[ex])
            def _():
                zero_copy(ex).wait()
            return carry

        lax.fori_loop(0, N_EXPERTS, start, 0)
        lax.fori_loop(0, N_EXPERTS, wait, 0)

    slot_copy().wait()

    def row_copy(t, k):
        return pltpu.make_async_copy(hp_ref.at[_slab_rows(t, slab)],
                                     xs_hbm.at[_slab_rows(slot_s[k, t], slab)], sem_r)

    def start_rows(t, carry):
        for k in range(TOP_K):
            row_copy(t, k).start(priority=k % 2)
        return carry

    lax.fori_loop(0, tb // 2, start_rows, 0, unroll=2)
    h = h_ref[...]
    g = jnp.dot(h, wsg_ref[...], preferred_element_type=jnp.float32)
    u = jnp.dot(h, wsu_ref[...], preferred_element_type=jnp.float32)
    a = (_silu(g) * u).astype(jnp.bfloat16)
    lax.fori_loop(tb // 2, tb, start_rows, 0, unroll=2)
    ysh_ref[...] = jnp.dot(a, wsd_ref[...], preferred_element_type=jnp.float32).astype(ysh_ref.dtype)

    for k in range(TOP_K):
        pltpu.make_async_copy(hp_ref, xs_hbm.at[pl.ds(0, tb * slab)], sem_r).wait()


def _dispatch(pad_start, pad_end, h2p, h2, wsg, wsu, wsd, slot3, n_rows, slab):
    tb = MOE_TILE
    t, d = h2.shape
    f = wsg.shape[1]
    once = pl.Buffered(1)
    return pl.pallas_call(
        functools.partial(_dispatch_kernel, slab=slab),
        grid_spec=pltpu.PrefetchScalarGridSpec(
            num_scalar_prefetch=2,
            grid=(t // tb,),
            in_specs=[pl.BlockSpec((tb * slab, V7X_LANES), lambda i, ps, pe: (i, 0)),
                      pl.BlockSpec((tb, d), lambda i, ps, pe: (i, 0)),
                      pl.BlockSpec((d, f), lambda i, ps, pe: (0, 0), pipeline_mode=once),
                      pl.BlockSpec((d, f), lambda i, ps, pe: (0, 0), pipeline_mode=once),
                      pl.BlockSpec((f, d), lambda i, ps, pe: (0, 0), pipeline_mode=once),
                      pl.BlockSpec(memory_space=pl.ANY)],
            out_specs=[pl.BlockSpec(memory_space=pl.ANY),
                       pl.BlockSpec((tb, d), lambda i, ps, pe: (i, 0))],
            scratch_shapes=[pltpu.SMEM((TOP_K, tb), jnp.int32),
                            pltpu.VMEM((tb * slab, V7X_LANES), jnp.int32),
                            pltpu.SemaphoreType.DMA,
                            pltpu.SemaphoreType.DMA,
                            pltpu.SemaphoreType.DMA],
        ),
        out_shape=[jax.ShapeDtypeStruct((n_rows * slab, V7X_LANES), jnp.int32),
                   jax.ShapeDtypeStruct((t, d), jnp.bfloat16)],
        compiler_params=_params(("arbitrary",)),
        name="dispatch",
    )(pad_start, pad_end, h2p, h2, wsg, wsu, wsd, slot3)


def _expert_kernel(blk_e_ref, nused_ref, first_ref, ord_ref, next_e_ref, xs_hbm, wg_hbm, wu_hbm, wd_hbm,
                   ys_hbm, xt, yt, wg_v, wu_v, wd_v, sem_in, sem_out, sem_w, *, slab):
    j = pl.program_id(0)
    n_used = nused_ref[0]
    tb = xt.shape[1]
    lanes = V7X_LANES
    half = slab * lanes

    def w_copies(e, slot):
        return [pltpu.make_async_copy(src.at[e], dst.at[slot], sem_w.at[slot])
                for src, dst in ((wg_hbm, wg_v), (wu_hbm, wu_v), (wd_hbm, wd_v))]

    def in_copies(tile, buf):
        rows = pl.ds(pl.multiple_of(tile * tb, tb), tb)
        return [pltpu.make_async_copy(xs_hbm.at[rows, s], xt.at[buf, :, pl.ds(s * lanes, lanes)],
                                      sem_in.at[buf]) for s in range(slab)]

    def out_copies(tile, buf):
        rows = pl.ds(pl.multiple_of(tile * tb, tb), tb)
        return [pltpu.make_async_copy(yt.at[buf, :, pl.ds(s * lanes, lanes)], ys_hbm.at[rows, s],
                                      sem_out.at[buf]) for s in range(slab)]

    @pl.when(j == 0)
    def _():
        for c in in_copies(0, 0) + w_copies(blk_e_ref[0], 0):
            c.start()

        @pl.when(n_used > 1)
        def _():
            for c in in_copies(1, 1):
                c.start()

    @pl.when(j < n_used)
    def _():
        cur = j % 2
        cur_in = j % N_IN_TILES
        slot = ord_ref[j] % 2

        @pl.when(first_ref[j] == 1)
        def _():
            for c in w_copies(blk_e_ref[j], slot):
                c.wait()

            @pl.when(next_e_ref[j] >= 0)
            def _():
                for c in w_copies(next_e_ref[j], 1 - slot):
                    c.start()

        @pl.when(j + 2 < n_used)
        def _():
            for c in in_copies(j + 2, (j + 2) % N_IN_TILES):
                c.start()

        for c in in_copies(j, cur_in):
            c.wait()

        @pl.when(j >= 2)
        def _():
            for c in out_copies(j - 2, cur):
                c.wait()

        lo, hi = _unpack_pair(xt[cur_in])
        lo = lo.astype(jnp.bfloat16)
        hi = hi.astype(jnp.bfloat16)
        g = (jnp.dot(lo, wg_v[slot, :half, :], preferred_element_type=jnp.float32)
             + jnp.dot(hi, wg_v[slot, half:, :], preferred_element_type=jnp.float32))
        u = (jnp.dot(lo, wu_v[slot, :half, :], preferred_element_type=jnp.float32)
             + jnp.dot(hi, wu_v[slot, half:, :], preferred_element_type=jnp.float32))
        a = (_silu(g) * u).astype(jnp.bfloat16)
        y = jnp.dot(a, wd_v[slot], preferred_element_type=jnp.float32)
        yt[cur] = _pack_pair(y[:, :half], y[:, half:])
        for c in out_copies(j, cur):
            c.start(priority=1)

        @pl.when(j == n_used - 1)
        def _():
            @pl.when(j >= 1)
            def _():
                for c in out_copies(j - 1, 1 - cur):
                    c.wait()

            for c in out_copies(j, cur):
                c.wait()


def _experts(blk_e, n_used, xs, wg, wu, wd, slab):
    tb = MOE_TILE
    p = xs.shape[0] // slab
    half = slab * V7X_LANES
    d = 2 * half
    f = wg.shape[2]
    nblk = p // tb

    tiles = jnp.arange(nblk, dtype=jnp.int32)
    used = tiles < n_used[0]
    first = used & ((tiles == 0) | (blk_e != jnp.roll(blk_e, 1)))
    ordinal = jnp.cumsum(first.astype(jnp.int32)) - 1
    first_pos = jnp.where(first, tiles, nblk)
    next_first = lax.cummin(jnp.roll(first_pos, -1).at[-1].set(nblk), reverse=True)
    next_e = jnp.where(next_first < nblk, blk_e[jnp.minimum(next_first, nblk - 1)], -1)

    any_spec = pl.BlockSpec(memory_space=pl.ANY)
    ys = pl.pallas_call(
        functools.partial(_expert_kernel, slab=slab),
        grid_spec=pltpu.PrefetchScalarGridSpec(
            num_scalar_prefetch=5,
            grid=(nblk,),
            in_specs=[any_spec, any_spec, any_spec, any_spec],
            out_specs=any_spec,
            scratch_shapes=[pltpu.VMEM((N_IN_TILES, tb, half), jnp.int32),
                            pltpu.VMEM((2, tb, half), jnp.int32),
                            pltpu.VMEM((2, d, f), jnp.bfloat16),
                            pltpu.VMEM((2, d, f), jnp.bfloat16),
                            pltpu.VMEM((2, f, d), jnp.bfloat16),
                            pltpu.SemaphoreType.DMA((N_IN_TILES,)),
                            pltpu.SemaphoreType.DMA((2,)),
                            pltpu.SemaphoreType.DMA((2,))],
        ),
        out_shape=jax.ShapeDtypeStruct((p, slab, V7X_LANES), jnp.int32),
        compiler_params=_params(("arbitrary",)),
        name="experts",
    )(blk_e, n_used, first.astype(jnp.int32), ordinal.astype(jnp.int32), next_e.astype(jnp.int32),
      xs.reshape(p, slab, V7X_LANES), wg, wu, wd)
    return ys.reshape(xs.shape)


def _combine_kernel(x_ref, ysh_ref, wt_ref, gt_ref, gf_ref,
                    slot_hbm, ys_hbm, o_ref, slot_s, rows_v, wrep, acc_lo, acc_hi, sem_i, sem_r):
    i = pl.program_id(0)
    n = pl.num_programs(0)
    tb = x_ref.shape[0]
    half = x_ref.shape[1] // 2
    slab = half // V7X_LANES
    per_tile = MOE_TILE // tb
    cur = i % 2
    nxt = 1 - cur

    def slot_copy(step):
        win = pl.ds((step % per_tile) * tb, tb)
        return pltpu.make_async_copy(slot_hbm.at[step // per_tile, :, win], slot_s.at[step % 2],
                                     sem_i.at[step % 2])

    def request_token(buf, t):
        for k in range(TOP_K):
            pltpu.make_async_copy(ys_hbm.at[_slab_rows(slot_s[buf, k, t], slab)],
                                  rows_v.at[buf, k, _slab_rows(t, slab)],
                                  sem_r.at[buf]).start(priority=k % 2)

    def wait_rows(buf):
        for k in range(TOP_K):
            pltpu.make_async_copy(ys_hbm.at[pl.ds(0, tb * slab)], rows_v.at[buf, k], sem_r.at[buf]).wait()

    @pl.when(i == 0)
    def _():
        slot_copy(0).start()
        slot_copy(0).wait()

        def first(t, carry):
            request_token(0, t)
            return carry

        lax.fori_loop(0, tb, first, 0, unroll=2)

        @pl.when(n > 1)
        def _():
            slot_copy(1).start()
            slot_copy(1).wait()

    @pl.when(i + 2 < n)
    def _():
        slot_copy(i + 2).start()

    wt = wt_ref[...]
    for k in range(TOP_K):
        wrep[k] = jnp.broadcast_to(wt[:, k:k + 1], (tb, V7X_LANES))

    def token(t, carry, buf, request_next):
        rows = _slab_rows(t, slab)
        lo_acc = jnp.zeros((slab, V7X_LANES), jnp.float32)
        hi_acc = jnp.zeros((slab, V7X_LANES), jnp.float32)
        for k in range(TOP_K):
            wv = jnp.broadcast_to(wrep[k, pl.ds(t, 1), :], (slab, V7X_LANES))
            lo, hi = _unpack_pair(rows_v[buf, k, rows, :])
            lo_acc = lo_acc + wv * lo
            hi_acc = hi_acc + wv * hi
        acc_lo[rows, :] = lo_acc
        acc_hi[rows, :] = hi_acc
        if request_next:
            request_token(1 - buf, t)
        return carry

    for buf in range(2):
        for request_next in (True, False):
            @pl.when((cur == buf) & ((i + 1 < n) == request_next))
            def _():
                wait_rows(buf)
                lax.fori_loop(0, tb, functools.partial(token, buf=buf, request_next=request_next),
                              0, unroll=4)

    @pl.when(i + 2 < n)
    def _():
        slot_copy(i + 2).wait()

    yy = ysh_ref[...].astype(jnp.float32) + jnp.concatenate(
        [_slab_load(acc_lo, tb, slab), _slab_load(acc_hi, tb, slab)], axis=1)
    xo = x_ref[...] + gt_ref[0] * yy
    ms = jnp.mean(xo * xo, axis=-1, keepdims=True)
    o_ref[...] = xo * lax.rsqrt(ms + EPS) * gf_ref[...]


def _combine(x1, y_shared, w_t, mod3, gate_idx, g_final, slot3, ys, seq):
    t, d = x1.shape
    tb = COMBINE_TILE
    assert MOE_TILE % tb == 0 and seq % tb == 0
    per_b = seq // tb
    half = d // 2
    return pl.pallas_call(
        _combine_kernel,
        grid=(t // tb,),
        in_specs=[pl.BlockSpec((tb, d), lambda i: (i, 0)),
                  pl.BlockSpec((tb, d), lambda i: (i, 0)),
                  pl.BlockSpec((tb, TOP_K), lambda i: (i, 0)),
                  pl.BlockSpec((1, 1, d), lambda i: ((i // per_b) * 6 + gate_idx, 0, 0)),
                  pl.BlockSpec((1, d), lambda i: (0, 0)),
                  pl.BlockSpec(memory_space=pl.ANY),
                  pl.BlockSpec(memory_space=pl.ANY)],
        out_specs=pl.BlockSpec((tb, d), lambda i: (i, 0)),
        scratch_shapes=[pltpu.SMEM((2, TOP_K, tb), jnp.int32),
                        pltpu.VMEM((2, TOP_K, tb * half // V7X_LANES, V7X_LANES), jnp.int32),
                        pltpu.VMEM((TOP_K, tb, V7X_LANES), jnp.float32),
                        pltpu.VMEM((tb * half // V7X_LANES, V7X_LANES), jnp.float32),
                        pltpu.VMEM((tb * half // V7X_LANES, V7X_LANES), jnp.float32),
                        pltpu.SemaphoreType.DMA((2,)),
                        pltpu.SemaphoreType.DMA((2,))],
        out_shape=jax.ShapeDtypeStruct((t, d), jnp.float32),
        compiler_params=_params(("arbitrary",)),
        name="combine",
    )(x1, y_shared, w_t, mod3, g_final.reshape(1, d), slot3, ys)


def _in_layout(d):
    qw = ATT_HEADS * ATT_HEAD_DIM
    kvw = ATT_KV_HEADS * ATT_HEAD_DIM
    rqk = RET_HEADS * RET_QK_DIM
    rv = RET_HEADS * RET_V_DIM
    order = [("qa", qw), ("ka", kvw), ("va", kvw), ("qr", rqk), ("kr", rqk),
             ("vr", rv), ("gr", rv), ("ga", d), ("gb", d)]
    dst = {}
    off = 0
    for name, width in order:
        dst[name] = off
        off += width
    return dst


def kernel(x, c, w_ada, b_ada, g_norm_mix, w_in, attn_sinks, w_attn_out, w_ret_out, w_o, g_norm_ffn,
           w_router, b_router, w_gate, w_up, w_down, w_sh_gate, w_sh_up, w_sh_down, g_norm_final):
    b, s, d = x.shape
    t = b * s
    depth = w_ada.shape[0]
    bf = jnp.bfloat16
    dst = _in_layout(d)

    c_pad = jnp.zeros((V7X_SUBLANES, d), jnp.float32).at[:b].set(c)
    x2 = x.reshape(t, d)
    for l in range(depth):
        mod = _ada(c_pad, w_ada[l], b_ada[l])
        mod3 = mod[:b].reshape(b * 6, 1, d)

        h = _norm_mod(x2.reshape(b, s, d), g_norm_mix[l], mod3, 0, 1)
        proj, (wg_b, wu_b, wd_b) = _in_proj(h.reshape(t, d), w_in[l].astype(bf),
                                            ((w_gate[l], w_up[l], w_down[l]),), 1024, 768)
        wao_b, wro_b, wo_b = w_attn_out[l].astype(bf), w_ret_out[l].astype(bf), w_o[l].astype(bf)
        wsg_b, wsu_b, wsd_b = w_sh_gate[l].astype(bf), w_sh_up[l].astype(bf), w_sh_down[l].astype(bf)
        proj3 = proj.reshape(b, s, proj.shape[1])
        attn = _attention(proj3, attn_sinks[l], dst["qa"], dst["ka"], dst["va"])
        ret = _retention(proj3, dst["qr"], dst["kr"], dst["vr"], dst["gr"])
        mix = _mix(attn.reshape(t, -1), ret.reshape(t, -1), wao_b, wro_b, proj, dst["ga"], dst["gb"])
        x1 = _out_resid(mix, wo_b, x2, mod3, 2, s)

        h2, h2p, idx3, pos3, w3, cnt = _router(x1, g_norm_ffn[l], mod3, 3, 4,
                                               w_router[l].T.astype(bf), b_router[l], s)
        counts = cnt[:, 0]
        tile = MOE_TILE
        padded = (counts + tile - 1) // tile * tile
        pad_end = jnp.cumsum(padded).astype(jnp.int32)
        pad_start = (pad_end - padded).astype(jnp.int32)
        n_blocks = (t * TOP_K) // tile + N_EXPERTS
        n_used = (pad_end[-1] // tile).reshape(1).astype(jnp.int32)
        blk_first = jnp.arange(n_blocks, dtype=jnp.int32) * tile
        blk_e = jnp.minimum(jnp.sum((pad_end[None, :] <= blk_first[:, None]).astype(jnp.int32), axis=1),
                            N_EXPERTS - 1)
        slab = d // 2 // V7X_LANES
        slot3 = _slots(pad_start, idx3, pos3)
        xs, y_shared = _dispatch(pad_start, pad_end, h2p, h2, wsg_b, wsu_b, wsd_b, slot3,
                                 n_blocks * tile, slab)
        ys = _experts(blk_e, n_used, xs, wg_b, wu_b, wd_b, slab)
        w_t = w3.transpose(0, 2, 1).reshape(t, TOP_K)
        is_last = l == depth - 1
        assert is_last, "the final norm is fused into the last layer's combine"
        x2 = _combine(x1, y_shared, w_t, mod3, 5, g_norm_final, slot3, ys, s)
    return x2.reshape(b, s, d)
```

```python
import functools
import math

import jax
import jax.numpy as jnp
from jax import lax
from jax.experimental import pallas as pl
from jax.experimental.pallas import tpu as pltpu

ATT_HEADS = 32
ATT_KV_HEADS = 4
ATT_HEAD_DIM = 64
WINDOW = 128
ATT_BLOCK = 128
RET_HEADS = 8
RET_QK_DIM = 256
RET_V_DIM = 512
RET_CHUNK = 128
N_EXPERTS = 64
N_GROUPS = 8
TOPK_GROUPS = 4
TOP_K = 8
ROUTED_SCALE = 2.5
EPS = 1e-6

V7X_LANES = 128
V7X_SUBLANES = 8
V7X_VMEM_LIMIT_BYTES = 60000 * 1024

MOE_TILE = 256
COMBINE_TILE = 128
N_IN_TILES = 3
NEG_BIG = -1e30


def _div_block(n, target, align):
    best = None
    b = align
    while b <= min(n, target):
        if n % b == 0:
            best = b
        b += align
    assert best is not None, (n, target, align)
    return best


def _params(semantics):
    return pltpu.CompilerParams(dimension_semantics=semantics,
                                vmem_limit_bytes=V7X_VMEM_LIMIT_BYTES)


def _sigmoid(v):
    return 1.0 / (1.0 + jnp.exp(-v))


def _silu(v):
    return v * _sigmoid(v)


def _pack_pair(lo, hi):
    return pltpu.pack_elementwise([lo, hi], packed_dtype=jnp.bfloat16)


def _unpack_pair(p):
    lo = pltpu.unpack_elementwise(p, index=0, packed_dtype=jnp.bfloat16, unpacked_dtype=jnp.float32)
    hi = pltpu.unpack_elementwise(p, index=1, packed_dtype=jnp.bfloat16, unpacked_dtype=jnp.float32)
    return lo, hi


def _slab_load(ref, n_rows, slab):
    return jnp.concatenate([ref[pl.ds(s, n_rows, stride=slab), :] for s in range(slab)], axis=1)


def _slab_store(ref, val, n_rows, slab):
    for s in range(slab):
        ref[pl.ds(s, n_rows, stride=slab), :] = val[:, s * V7X_LANES:(s + 1) * V7X_LANES]


def _slab_rows(r, slab):
    return pl.ds(pl.multiple_of(r * slab, slab), slab)


def _ada_kernel(c_ref, w_ref, b_ref, o_ref):
    cs = _silu(c_ref[...]).astype(jnp.bfloat16)
    o_ref[...] = jnp.dot(cs, w_ref[...].astype(jnp.bfloat16),
                         preferred_element_type=jnp.float32) + b_ref[...]


def _ada(c_pad, w, b):
    m, d = c_pad.shape
    n = w.shape[1]
    tn = _div_block(n, 512, V7X_LANES)
    return pl.pallas_call(
        _ada_kernel,
        grid=(n // tn,),
        in_specs=[pl.BlockSpec((m, d), lambda j: (0, 0)),
                  pl.BlockSpec((d, tn), lambda j: (0, j)),
                  pl.BlockSpec((1, tn), lambda j: (0, j))],
        out_specs=pl.BlockSpec((m, tn), lambda j: (0, j)),
        out_shape=jax.ShapeDtypeStruct((m, n), jnp.float32),
        compiler_params=_params(("parallel",)),
        name="ada",
    )(c_pad, w, b.reshape(1, n))


def _norm_mod_kernel(x_ref, g_ref, sh_ref, sc_ref, o_ref):
    x = x_ref[0]
    ms = jnp.mean(x * x, axis=-1, keepdims=True)
    y = x * lax.rsqrt(ms + EPS) * g_ref[...]
    o_ref[0] = (y * (1.0 + sc_ref[0]) + sh_ref[0]).astype(o_ref.dtype)


def _norm_mod(x3, g, mod3, shift_idx, scale_idx):
    b, s, d = x3.shape
    ts = _div_block(s, 512, V7X_SUBLANES)
    return pl.pallas_call(
        _norm_mod_kernel,
        grid=(b, s // ts),
        in_specs=[pl.BlockSpec((1, ts, d), lambda bi, i: (bi, i, 0)),
                  pl.BlockSpec((1, d), lambda bi, i: (0, 0)),
                  pl.BlockSpec((1, 1, d), lambda bi, i: (bi * 6 + shift_idx, 0, 0)),
                  pl.BlockSpec((1, 1, d), lambda bi, i: (bi * 6 + scale_idx, 0, 0))],
        out_specs=pl.BlockSpec((1, ts, d), lambda bi, i: (bi, i, 0)),
        out_shape=jax.ShapeDtypeStruct((b, s, d), jnp.bfloat16),
        compiler_params=_params(("parallel", "parallel")),
        name="norm_mod",
    )(x3, g.reshape(1, d), mod3, mod3)


def _in_proj_kernel(a_ref, b_ref, *rest, windows, n_j):
    n_side = len(windows)
    src = rest[:n_side]
    o_ref = rest[n_side]
    dst = rest[n_side + 1:]
    o_ref[...] = jnp.dot(a_ref[...], b_ref[...],
                         preferred_element_type=jnp.float32).astype(o_ref.dtype)
    step = pl.program_id(0) * n_j + pl.program_id(1)

    for s_ref, d_ref, (first, count) in zip(src, dst, windows):
        @pl.when((step >= first) & (step < first + count))
        def _():
            d_ref[...] = s_ref[...].astype(d_ref.dtype)


def _in_proj(a, b, side_groups, tm_target, tn_target):
    m, k = a.shape
    n = b.shape[1]
    tm = _div_block(m, tm_target, V7X_SUBLANES)
    tn = _div_block(n, tn_target, V7X_LANES)
    n_i, n_j = m // tm, n // tn
    steps_left = n_i * n_j
    first = 0
    side, chunked, windows = [], [], []
    for group in side_groups:
        n_max = 1 << (steps_left.bit_length() - 1)
        used = 0
        for w in group:
            rows = w.size // w.shape[-1]
            count = n_max
            while rows % (count * 2 * V7X_SUBLANES):
                count //= 2
            side.append(w)
            chunked.append(w.reshape(count, rows // count, w.shape[-1]))
            windows.append((first, count))
            used = max(used, count)
        first += used
        steps_left -= used

    def side_spec(c, window):
        w_first, count = window
        return pl.BlockSpec((1,) + c.shape[1:],
                            lambda i, j: (jnp.clip(i * n_j + j - w_first, 0, count - 1), 0, 0))

    side_specs = [side_spec(c, w) for c, w in zip(chunked, windows)]
    outs = pl.pallas_call(
        functools.partial(_in_proj_kernel, windows=tuple(windows), n_j=n_j),
        grid=(n_i, n_j),
        in_specs=[pl.BlockSpec((tm, k), lambda i, j: (i, 0)),
                  pl.BlockSpec((k, tn), lambda i, j: (0, j))] + side_specs,
        out_specs=[pl.BlockSpec((tm, tn), lambda i, j: (i, j))] + side_specs,
        out_shape=[jax.ShapeDtypeStruct((m, n), jnp.bfloat16)]
        + [jax.ShapeDtypeStruct(c.shape, jnp.bfloat16) for c in chunked],
        compiler_params=_params(("arbitrary", "arbitrary")),
        name="in_proj",
    )(a, b, *chunked)
    return outs[0], [o.reshape(w.shape) for o, w in zip(outs[1:], side)]


def _attn_kernel(sink_ref, q_ref, kc_ref, kp_ref, vc_ref, vp_ref, bias_ref, o_ref):
    i = pl.program_id(1)
    blk = ATT_BLOCK
    hd = ATT_HEAD_DIM
    group = ATT_HEADS // ATT_KV_HEADS
    pairs = group // 2
    nt = (((1,), (1,)), ((), ()))
    tn = (((0,), (0,)), ((), ()))
    zpad = jnp.zeros((2 * blk, hd), jnp.bfloat16)

    def scores(h, par):
        sl = slice(h * hd, (h + 1) * hd)
        k2 = jnp.concatenate([kp_ref[0, :, sl], kc_ref[0, :, sl]], axis=0) * (hd ** -0.5)
        qp = jnp.concatenate([q_ref[0, :, (h * pairs + p) * 2 * hd:(h * pairs + p + 1) * 2 * hd]
                              for p in range(pairs)], axis=0)
        kz = jnp.concatenate([k2, zpad] if par == 0 else [zpad, k2], axis=1)
        return lax.dot_general(kz, qp, nt, preferred_element_type=jnp.float32)

    def softmax(s, h, par):
        s = s + bias_ref[h * 2 + par]
        s = jnp.concatenate([jnp.where(i == 0, NEG_BIG, s[:blk]), s[blk:]], axis=0)
        sink = jnp.concatenate([jnp.full((1, blk), sink_ref[h * group + 2 * p + par], jnp.float32)
                                for p in range(pairs)], axis=1)
        m = jnp.maximum(s.max(0, keepdims=True), sink)
        pr = jnp.exp(s - m)
        denom = pr.sum(0, keepdims=True) + jnp.exp(sink - m)
        return pr.astype(jnp.bfloat16), 1.0 / denom

    def values(pr, inv, h, par):
        sl = slice(h * hd, (h + 1) * hd)
        v2 = jnp.concatenate([vp_ref[0, :, sl], vc_ref[0, :, sl]], axis=0)
        vz = jnp.concatenate([v2, zpad] if par == 0 else [zpad, v2], axis=1)
        return lax.dot_general(vz, pr, tn, preferred_element_type=jnp.float32) * inv

    items = [(h, par) for h in range(ATT_KV_HEADS) for par in range(2)]
    s_of, p_of, acc = {}, {}, {}
    for n in range(len(items) + 2):
        if n < len(items):
            s_of[n] = scores(*items[n])
        if 1 <= n <= len(items):
            p_of[n - 1] = softmax(s_of.pop(n - 1), *items[n - 1])
        if n >= 2:
            h, par = items[n - 2]
            o = values(*p_of.pop(n - 2), h, par)
            acc[h] = o if par == 0 else acc[h] + o
            if par == 1:
                out = acc.pop(h)
                for p in range(pairs):
                    o_ref[0, :, (h * pairs + p) * 2 * hd:(h * pairs + p + 1) * 2 * hd] = (
                        out[:, p * blk:(p + 1) * blk].T.astype(o_ref.dtype))


def _attn_bias():
    blk = ATT_BLOCK
    group = ATT_HEADS // ATT_KV_HEADS
    pairs = group // 2
    qi = jnp.arange(blk)[:, None]
    kj = jnp.arange(2 * blk)[None, :]
    dist = qi + blk - kj
    valid = (dist >= 0) & (dist < WINDOW)
    slopes = jnp.exp2(-8.0 * jnp.arange(1, ATT_HEADS + 1, dtype=jnp.float32) / ATT_HEADS)
    slopes = slopes.reshape(ATT_KV_HEADS, pairs, 2)
    bias = jnp.where(valid, -slopes[..., None, None] * dist.astype(jnp.float32), NEG_BIG)
    return bias.transpose(0, 2, 4, 1, 3).reshape(ATT_KV_HEADS * 2, 2 * blk, pairs * blk)


def _attention(proj3, sinks, q_off, k_off, v_off):
    b, s, _ = proj3.shape
    qw = ATT_HEADS * ATT_HEAD_DIM
    kvw = ATT_KV_HEADS * ATT_HEAD_DIM
    nb = s // ATT_BLOCK
    group = ATT_HEADS // ATT_KV_HEADS
    assert q_off % qw == 0 and k_off % kvw == 0 and v_off % kvw == 0
    assert group % 2 == 0 and 2 * ATT_HEAD_DIM == V7X_LANES and WINDOW == ATT_BLOCK
    assert 4 ** round(math.log(ATT_HEAD_DIM, 4)) == ATT_HEAD_DIM, "score scale must be a power of two"
    qb, kb, vb = q_off // qw, k_off // kvw, v_off // kvw
    bias = _attn_bias()
    return pl.pallas_call(
        _attn_kernel,
        grid_spec=pltpu.PrefetchScalarGridSpec(
            num_scalar_prefetch=1,
            grid=(b, nb),
            in_specs=[pl.BlockSpec((1, ATT_BLOCK, qw), lambda bi, i, sk: (bi, i, qb)),
                      pl.BlockSpec((1, ATT_BLOCK, kvw), lambda bi, i, sk: (bi, i, kb)),
                      pl.BlockSpec((1, ATT_BLOCK, kvw), lambda bi, i, sk: (bi, jnp.maximum(i - 1, 0), kb)),
                      pl.BlockSpec((1, ATT_BLOCK, kvw), lambda bi, i, sk: (bi, i, vb)),
                      pl.BlockSpec((1, ATT_BLOCK, kvw), lambda bi, i, sk: (bi, jnp.maximum(i - 1, 0), vb)),
                      pl.BlockSpec(bias.shape, lambda bi, i, sk: (0, 0, 0), pipeline_mode=pl.Buffered(1))],
            out_specs=pl.BlockSpec((1, ATT_BLOCK, qw), lambda bi, i, sk: (bi, i, 0)),
        ),
        out_shape=jax.ShapeDtypeStruct((b, s, qw), jnp.bfloat16),
        compiler_params=_params(("parallel", "parallel")),
        name="attention",
    )(sinks, proj3, proj3, proj3, proj3, proj3, bias)


def _ret_kernel(q_ref, k_ref, v_ref, gr_ref, mask_ref, qd_ref, kd_ref, cd_ref, o_ref, state_ref):
    c = pl.program_id(1)

    @pl.when(c == 0)
    def _():
        state_ref[...] = jnp.zeros_like(state_ref)

    nt = (((1,), (1,)), ((), ()))
    tn = (((0,), (0,)), ((), ()))
    def decayed(bi):
        q = q_ref[bi]
        k = k_ref[bi]
        attn = lax.dot_general(q, k, nt, preferred_element_type=jnp.float32) * mask_ref[0]
        kd = (k.astype(jnp.float32) * kd_ref[0]).astype(k.dtype)
        return attn.astype(jnp.bfloat16), kd

    def recur(bi, attn, kd):
        q = q_ref[bi]
        v = v_ref[bi]
        intra = jnp.dot(attn, v, preferred_element_type=jnp.float32)
        state = state_ref[bi]
        inter = jnp.dot(q, state.astype(q.dtype), preferred_element_type=jnp.float32) * qd_ref[0]
        state_ref[bi] = state * cd_ref[0] + lax.dot_general(kd, v, tn, preferred_element_type=jnp.float32)
        return intra + inter

    def finish(bi, o):
        mu = jnp.mean(o, axis=-1, keepdims=True)
        oc = o - mu
        var = jnp.mean(oc * oc, axis=-1, keepdims=True)
        y = oc * lax.rsqrt(var + EPS)
        o_ref[bi] = (_silu(gr_ref[bi].astype(jnp.float32)) * y).astype(o_ref.dtype)

    nb = q_ref.shape[0]
    a_of, o_of = {}, {}
    for n in range(nb + 2):
        if n < nb:
            a_of[n] = decayed(n)
        if 1 <= n <= nb:
            o_of[n - 1] = recur(n - 1, *a_of.pop(n - 1))
        if n >= 2:
            finish(n - 2, o_of.pop(n - 2))


def _retention(proj3, q_off, k_off, v_off, g_off):
    b, s, _ = proj3.shape
    dk, dv, ch = RET_QK_DIM, RET_V_DIM, RET_CHUNK
    assert q_off % dk == 0 and k_off % dk == 0 and v_off % dv == 0 and g_off % dv == 0
    qb, kb, vb, gb = q_off // dk, k_off // dk, v_off // dv, g_off // dv
    n = s // ch
    log_g = jnp.log1p(-jnp.exp2(-5.0 - jnp.arange(RET_HEADS, dtype=jnp.float32)))
    pos = jnp.arange(ch, dtype=jnp.float32)
    rel = pos[:, None] - pos[None, :]
    scale = dk ** -0.5
    mask = jnp.where(rel[None] >= 0, jnp.exp(rel[None] * log_g[:, None, None]), 0.0) * scale
    q_decay = jnp.exp((pos[None, :, None] + 1.0) * log_g[:, None, None])
    k_decay = jnp.exp((ch - 1.0 - pos[None, :, None]) * log_g[:, None, None]) * scale
    c_decay = jnp.exp(ch * log_g)[:, None, None]
    return pl.pallas_call(
        _ret_kernel,
        grid=(RET_HEADS, n),
        in_specs=[pl.BlockSpec((b, ch, dk), lambda h, c: (0, c, qb + h)),
                  pl.BlockSpec((b, ch, dk), lambda h, c: (0, c, kb + h)),
                  pl.BlockSpec((b, ch, dv), lambda h, c: (0, c, vb + h)),
                  pl.BlockSpec((b, ch, dv), lambda h, c: (0, c, gb + h)),
                  pl.BlockSpec((1, ch, ch), lambda h, c: (h, 0, 0)),
                  pl.BlockSpec((1, ch, 1), lambda h, c: (h, 0, 0)),
                  pl.BlockSpec((1, ch, 1), lambda h, c: (h, 0, 0)),
                  pl.BlockSpec((1, 1, 1), lambda h, c: (h, 0, 0))],
        out_specs=pl.BlockSpec((b, ch, dv), lambda h, c: (0, c, h)),
        out_shape=jax.ShapeDtypeStruct((b, s, RET_HEADS * dv), jnp.bfloat16),
        scratch_shapes=[pltpu.VMEM((b, dk, dv), jnp.float32)],
        compiler_params=_params(("parallel", "arbitrary")),
        name="retention",
    )(proj3, proj3, proj3, proj3, mask, q_decay, k_decay, c_decay)


def _mix_kernel(a_ref, r_ref, wa_ref, wr_ref, ga_ref, gb_ref, o_ref):
    ya = jnp.dot(a_ref[...], wa_ref[...], preferred_element_type=jnp.float32)
    yr = jnp.dot(r_ref[...], wr_ref[...], preferred_element_type=jnp.float32)
    ga = _sigmoid(ga_ref[...].astype(jnp.float32))
    gb = _sigmoid(gb_ref[...].astype(jnp.float32))
    o_ref[...] = (ga * ya + gb * yr).astype(o_ref.dtype)


def _mix(attn2, ret2, wa, wr, proj2, ga_off, gb_off):
    m, ka = attn2.shape
    kr = ret2.shape[1]
    d = wa.shape[1]
    tm = _div_block(m, 1024, V7X_SUBLANES)
    tn = _div_block(d, 512, V7X_LANES)
    assert ga_off % tn == 0 and gb_off % tn == 0
    gab, gbb = ga_off // tn, gb_off // tn
    return pl.pallas_call(
        _mix_kernel,
        grid=(m // tm, d // tn),
        in_specs=[pl.BlockSpec((tm, ka), lambda i, j: (i, 0)),
                  pl.BlockSpec((tm, kr), lambda i, j: (i, 0)),
                  pl.BlockSpec((ka, tn), lambda i, j: (0, j)),
                  pl.BlockSpec((kr, tn), lambda i, j: (0, j)),
                  pl.BlockSpec((tm, tn), lambda i, j: (i, gab + j)),
                  pl.BlockSpec((tm, tn), lambda i, j: (i, gbb + j))],
        out_specs=pl.BlockSpec((tm, tn), lambda i, j: (i, j)),
        out_shape=jax.ShapeDtypeStruct((m, d), jnp.bfloat16),
        compiler_params=_params(("parallel", "parallel")),
        name="mix",
    )(attn2, ret2, wa, wr, proj2, proj2)


def _resid_kernel(a_ref, w_ref, x_ref, gt_ref, o_ref):
    y = jnp.dot(a_ref[...], w_ref[...], preferred_element_type=jnp.float32)
    o_ref[...] = x_ref[...] + gt_ref[0] * y


def _out_resid(mix2, w, x2, mod3, gate_idx, seq):
    m, k = mix2.shape
    d = w.shape[1]
    tm = _div_block(seq, 1024, V7X_SUBLANES)
    tn = _div_block(d, 1024, V7X_LANES)
    per_b = seq // tm
    return pl.pallas_call(
        _resid_kernel,
        grid=(m // tm, d // tn),
        in_specs=[pl.BlockSpec((tm, k), lambda i, j: (i, 0)),
                  pl.BlockSpec((k, tn), lambda i, j: (0, j)),
                  pl.BlockSpec((tm, tn), lambda i, j: (i, j)),
                  pl.BlockSpec((1, 1, tn), lambda i, j: ((i // per_b) * 6 + gate_idx, 0, j))],
        out_specs=pl.BlockSpec((tm, tn), lambda i, j: (i, j)),
        out_shape=jax.ShapeDtypeStruct((m, d), jnp.float32),
        compiler_params=_params(("parallel", "parallel")),
        name="out_resid",
    )(mix2, w, x2, mod3)


def _router_kernel(x_ref, g_ref, sh_ref, sc_ref, wr_ref, br_ref,
                   h_ref, hp_ref, idx_ref, pos_ref, w_ref, cnt_ref, carry_ref):
    i = pl.program_id(0)
    e = N_EXPERTS
    per_g = e // N_GROUPS
    tb = x_ref.shape[0]

    @pl.when(i == 0)
    def _():
        carry_ref[...] = jnp.zeros_like(carry_ref)

    x = x_ref[...]
    ms = jnp.mean(x * x, axis=-1, keepdims=True)
    h = x * lax.rsqrt(ms + EPS) * g_ref[...]
    h = h * (1.0 + sc_ref[0]) + sh_ref[0]
    hb = h.astype(jnp.bfloat16)
    h_ref[...] = hb
    half = h.shape[1] // 2
    _slab_store(hp_ref, _pack_pair(h[:, :half], h[:, half:]), tb, half // V7X_LANES)

    nt = (((1,), (1,)), ((), ()))
    logits = lax.dot_general(wr_ref[...], hb, nt, preferred_element_type=jnp.float32)
    scores = _sigmoid(logits)
    choice = scores + br_ref[...]

    c3 = choice.reshape(N_GROUPS, per_g, tb)
    j_iota = lax.broadcasted_iota(jnp.int32, c3.shape, 1).astype(jnp.float32)
    m1 = c3.max(axis=1, keepdims=True)
    first = jnp.min(jnp.where(c3 == m1, j_iota, float(per_g)), axis=1, keepdims=True)
    m2 = jnp.where(j_iota == first, -jnp.inf, c3).max(axis=1, keepdims=True)
    gs = (m1 + m2).reshape(N_GROUPS, tb)

    g_iota = lax.broadcasted_iota(jnp.int32, gs.shape, 0)
    grank = jnp.zeros(gs.shape, jnp.int32)
    for gp in range(N_GROUPS):
        row = gs[gp:gp + 1, :]
        ahead = (row > gs) | ((row == gs) & (gp < g_iota))
        grank = grank + ahead.astype(jnp.int32)
    gmask = grank < TOPK_GROUPS
    emask = jnp.broadcast_to(gmask.reshape(N_GROUPS, 1, tb), c3.shape).reshape(e, tb)
    masked = jnp.where(emask, choice, -jnp.inf)

    e_iota = lax.broadcasted_iota(jnp.int32, masked.shape, 0)
    erank = jnp.zeros(masked.shape, jnp.int32)
    for ep in range(e):
        row = masked[ep:ep + 1, :]
        ahead = (row > masked) | ((row == masked) & (ep < e_iota))
        erank = erank + ahead.astype(jnp.int32)
    sel = (erank < TOP_K) & emask
    self32 = sel.astype(jnp.float32)

    wsel = scores * self32
    wn = wsel / jnp.sum(wsel, axis=0, keepdims=True) * ROUTED_SCALE

    selb = self32.astype(jnp.bfloat16)
    t_r = lax.broadcasted_iota(jnp.int32, (tb, tb), 0)
    t_c = lax.broadcasted_iota(jnp.int32, (tb, tb), 1)
    upper = (t_r <= t_c).astype(jnp.bfloat16)
    incl = jnp.dot(selb, upper, preferred_element_type=jnp.float32)
    carry = carry_ref[...]
    rank_in_e = carry + incl - 1.0
    carry_new = carry + jnp.sum(self32, axis=1, keepdims=True)
    carry_ref[...] = carry_new
    cnt_ref[...] = jnp.broadcast_to(carry_new, cnt_ref.shape).astype(jnp.int32)

    e_r = lax.broadcasted_iota(jnp.int32, (e, e), 0)
    e_c = lax.broadcasted_iota(jnp.int32, (e, e), 1)
    lower = (e_c < e_r).astype(jnp.bfloat16)
    before = jnp.dot(lower, selb, preferred_element_type=jnp.float32)
    e_f = e_iota.astype(jnp.float32)
    idx_rows, pos_rows, w_rows = [], [], []
    for k in range(TOP_K):
        hit = jnp.where(sel & (before == float(k)), 1.0, 0.0)
        idx_rows.append(jnp.sum(hit * e_f, axis=0, keepdims=True))
        pos_rows.append(jnp.sum(hit * rank_in_e, axis=0, keepdims=True))
        w_rows.append(jnp.sum(hit * wn, axis=0, keepdims=True))
    idx_ref[0] = jnp.concatenate(idx_rows, axis=0).astype(jnp.int32)
    pos_ref[0] = jnp.concatenate(pos_rows, axis=0).astype(jnp.int32)
    w_ref[0] = jnp.concatenate(w_rows, axis=0)


def _router(x1, g, mod3, shift_idx, scale_idx, w_router_t, b_router, seq):
    t, d = x1.shape
    e = N_EXPERTS
    tb = MOE_TILE
    assert seq % tb == 0 and d % (2 * V7X_LANES * V7X_SUBLANES) == 0
    slab = d // 2 // V7X_LANES
    per_b = seq // tb
    nt = t // tb
    return pl.pallas_call(
        _router_kernel,
        grid=(nt,),
        in_specs=[pl.BlockSpec((tb, d), lambda i: (i, 0)),
                  pl.BlockSpec((1, d), lambda i: (0, 0)),
                  pl.BlockSpec((1, 1, d), lambda i: ((i // per_b) * 6 + shift_idx, 0, 0)),
                  pl.BlockSpec((1, 1, d), lambda i: ((i // per_b) * 6 + scale_idx, 0, 0)),
                  pl.BlockSpec((e, d), lambda i: (0, 0)),
                  pl.BlockSpec((e, 1), lambda i: (0, 0))],
        out_specs=[pl.BlockSpec((tb, d), lambda i: (i, 0)),
                   pl.BlockSpec((tb * slab, V7X_LANES), lambda i: (i, 0)),
                   pl.BlockSpec((1, TOP_K, tb), lambda i: (i, 0, 0)),
                   pl.BlockSpec((1, TOP_K, tb), lambda i: (i, 0, 0)),
                   pl.BlockSpec((1, TOP_K, tb), lambda i: (i, 0, 0)),
                   pl.BlockSpec((e, V7X_LANES), lambda i: (0, 0))],
        out_shape=[jax.ShapeDtypeStruct((t, d), jnp.bfloat16),
                   jax.ShapeDtypeStruct((t * slab, V7X_LANES), jnp.int32),
                   jax.ShapeDtypeStruct((nt, TOP_K, tb), jnp.int32),
                   jax.ShapeDtypeStruct((nt, TOP_K, tb), jnp.int32),
                   jax.ShapeDtypeStruct((nt, TOP_K, tb), jnp.float32),
                   jax.ShapeDtypeStruct((e, V7X_LANES), jnp.int32)],
        scratch_shapes=[pltpu.VMEM((e, 1), jnp.float32)],
        compiler_params=_params(("arbitrary",)),
        name="router",
    )(x1, g.reshape(1, d), mod3, mod3, w_router_t, b_router.reshape(e, 1))


def _slots_kernel(pstart_ref, idx_ref, pos_ref, o_ref):
    idx = idx_ref[...]
    base = jnp.zeros(idx.shape, jnp.int32)
    for ex in range(N_EXPERTS):
        base = jnp.where(idx == ex, pstart_ref[ex], base)
    o_ref[...] = base + pos_ref[...]


def _slots(pad_start, idx3, pos3):
    nt = idx3.shape[0]
    per_step = _div_block(nt, 8, 1)
    blk = (per_step,) + idx3.shape[1:]
    spec = pl.BlockSpec(blk, lambda i, ps: (i, 0, 0))
    return pl.pallas_call(
        _slots_kernel,
        grid_spec=pltpu.PrefetchScalarGridSpec(num_scalar_prefetch=1, grid=(nt // per_step,),
                                               in_specs=[spec, spec], out_specs=spec),
        out_shape=jax.ShapeDtypeStruct(idx3.shape, jnp.int32),
        compiler_params=_params(("parallel",)),
        name="slots",
    )(pad_start, idx3, pos3)


def _dispatch_kernel(pstart_ref, pend_ref, hp_ref, h_ref, wsg_ref, wsu_ref, wsd_ref, slot_hbm,
                     xs_hbm, ysh_ref, slot_s, zero_v, sem_i, sem_z, sem_r, *, slab):
    i = pl.program_id(0)
    tb = hp_ref.shape[0] // slab

    def slot_copy():
        return pltpu.make_async_copy(slot_hbm.at[i], slot_s, sem_i)

    slot_copy().start()

    def zero_copy(ex):
        first = pl.multiple_of((pend_ref[ex] - tb) * slab, tb * slab)
        return pltpu.make_async_copy(zero_v, xs_hbm.at[pl.ds(first, tb * slab)], sem_z)

    @pl.when(i == 0)
    def _():
        zero_v[...] = jnp.zeros_like(zero_v)

        def start(ex, carry):
            @pl.when(pend_ref[ex] > pstart_ref[ex])
            def _():
                zero_copy(ex).start()
            return carry

        def wait(ex, carry):
            @pl.when(pend_ref[ex] > pstart_ref[ex])
            def _():
                zero_copy(ex).wait()
            return carry

        lax.fori_loop(0, N_EXPERTS, start, 0)
        lax.fori_loop(0, N_EXPERTS, wait, 0)

    slot_copy().wait()

    def row_copy(t, k):
        return pltpu.make_async_copy(hp_ref.at[_slab_rows(t, slab)],
                                     xs_hbm.at[_slab_rows(slot_s[k, t], slab)], sem_r)

    def start_rows(t, carry):
        for k in range(TOP_K):
            row_copy(t, k).start(priority=k % 2)
        return carry

    lax.fori_loop(0, tb // 2, start_rows, 0, unroll=2)
    h = h_ref[...]
    g = jnp.dot(h, wsg_ref[...], preferred_element_type=jnp.float32)
    u = jnp.dot(h, wsu_ref[...], preferred_element_type=jnp.float32)
    a = (_silu(g) * u).astype(jnp.bfloat16)
    lax.fori_loop(tb // 2, tb, start_rows, 0, unroll=2)
    ysh_ref[...] = jnp.dot(a, wsd_ref[...], preferred_element_type=jnp.float32).astype(ysh_ref.dtype)

    for k in range(TOP_K):
        pltpu.make_async_copy(hp_ref, xs_hbm.at[pl.ds(0, tb * slab)], sem_r).wait()


def _dispatch(pad_start, pad_end, h2p, h2, wsg, wsu, wsd, slot3, n_rows, slab):
    tb = MOE_TILE
    t, d = h2.shape
    f = wsg.shape[1]
    once = pl.Buffered(1)
    return pl.pallas_call(
        functools.partial(_dispatch_kernel, slab=slab),
        grid_spec=pltpu.PrefetchScalarGridSpec(
            num_scalar_prefetch=2,
            grid=(t // tb,),
            in_specs=[pl.BlockSpec((tb * slab, V7X_LANES), lambda i, ps, pe: (i, 0)),
                      pl.BlockSpec((tb, d), lambda i, ps, pe: (i, 0)),
                      pl.BlockSpec((d, f), lambda i, ps, pe: (0, 0), pipeline_mode=once),
                      pl.BlockSpec((d, f), lambda i, ps, pe: (0, 0), pipeline_mode=once),
                      pl.BlockSpec((f, d), lambda i, ps, pe: (0, 0), pipeline_mode=once),
                      pl.BlockSpec(memory_space=pl.ANY)],
            out_specs=[pl.BlockSpec(memory_space=pl.ANY),
                       pl.BlockSpec((tb, d), lambda i, ps, pe: (i, 0))],
            scratch_shapes=[pltpu.SMEM((TOP_K, tb), jnp.int32),
                            pltpu.VMEM((tb * slab, V7X_LANES), jnp.int32),
                            pltpu.SemaphoreType.DMA,
                            pltpu.SemaphoreType.DMA,
                            pltpu.SemaphoreType.DMA],
        ),
        out_shape=[jax.ShapeDtypeStruct((n_rows * slab, V7X_LANES), jnp.int32),
                   jax.ShapeDtypeStruct((t, d), jnp.bfloat16)],
        compiler_params=_params(("arbitrary",)),
        name="dispatch",
    )(pad_start, pad_end, h2p, h2, wsg, wsu, wsd, slot3)


def _expert_kernel(blk_e_ref, nused_ref, first_ref, ord_ref, next_e_ref, xs_hbm, wg_hbm, wu_hbm, wd_hbm,
                   ys_hbm, xt, yt, wg_v, wu_v, wd_v, sem_in, sem_out, sem_w, *, slab):
    j = pl.program_id(0)
    n_used = nused_ref[0]
    tb = xt.shape[1]
    lanes = V7X_LANES
    half = slab * lanes

    def w_copies(e, slot):
        return [pltpu.make_async_copy(src.at[e], dst.at[slot], sem_w.at[slot])
                for src, dst in ((wg_hbm, wg_v), (wu_hbm, wu_v), (wd_hbm, wd_v))]

    def in_copies(tile, buf):
        rows = pl.ds(pl.multiple_of(tile * tb, tb), tb)
        return [pltpu.make_async_copy(xs_hbm.at[rows, s], xt.at[buf, :, pl.ds(s * lanes, lanes)],
                                      sem_in.at[buf]) for s in range(slab)]

    def out_copies(tile, buf):
        rows = pl.ds(pl.multiple_of(tile * tb, tb), tb)
        return [pltpu.make_async_copy(yt.at[buf, :, pl.ds(s * lanes, lanes)], ys_hbm.at[rows, s],
                                      sem_out.at[buf]) for s in range(slab)]

    def wait_in(buf):
        pltpu.make_async_copy(yt.at[0], xt.at[buf], sem_in.at[buf]).wait()

    def wait_out(buf):
        pltpu.make_async_copy(xt.at[0], yt.at[buf], sem_out.at[buf]).wait()

    @pl.when(j == 0)
    def _():
        for c in in_copies(0, 0) + w_copies(blk_e_ref[0], 0):
            c.start()

        @pl.when(n_used > 1)
        def _():
            for c in in_copies(1, 1):
                c.start()

    @pl.when(j < n_used)
    def _():
        cur = j % 2
        cur_in = j % N_IN_TILES
        slot = ord_ref[j] % 2

        @pl.when(first_ref[j] == 1)
        def _():
            for c in w_copies(blk_e_ref[j], slot):
                c.wait()

            @pl.when(next_e_ref[j] >= 0)
            def _():
                for c in w_copies(next_e_ref[j], 1 - slot):
                    c.start()

        @pl.when(j + 2 < n_used)
        def _():
            for c in in_copies(j + 2, (j + 2) % N_IN_TILES):
                c.start()

        wait_in(cur_in)

        @pl.when(j >= 2)
        def _():
            wait_out(cur)

        lo, hi = _unpack_pair(xt[cur_in])
        lo = lo.astype(jnp.bfloat16)
        hi = hi.astype(jnp.bfloat16)
        g = (jnp.dot(lo, wg_v[slot, :half, :], preferred_element_type=jnp.float32)
             + jnp.dot(hi, wg_v[slot, half:, :], preferred_element_type=jnp.float32))
        u = (jnp.dot(lo, wu_v[slot, :half, :], preferred_element_type=jnp.float32)
             + jnp.dot(hi, wu_v[slot, half:, :], preferred_element_type=jnp.float32))
        a = (_silu(g) * u).astype(jnp.bfloat16)
        y = jnp.dot(a, wd_v[slot], preferred_element_type=jnp.float32)
        yt[cur] = _pack_pair(y[:, :half], y[:, half:])
        for c in out_copies(j, cur):
            c.start(priority=1)

        @pl.when(j == n_used - 1)
        def _():
            @pl.when(j >= 1)
            def _():
                wait_out(1 - cur)

            wait_out(cur)


def _experts(blk_e, n_used, xs, wg, wu, wd, slab):
    tb = MOE_TILE
    p = xs.shape[0] // slab
    half = slab * V7X_LANES
    d = 2 * half
    f = wg.shape[2]
    nblk = p // tb

    tiles = jnp.arange(nblk, dtype=jnp.int32)
    used = tiles < n_used[0]
    first = used & ((tiles == 0) | (blk_e != jnp.roll(blk_e, 1)))
    ordinal = jnp.cumsum(first.astype(jnp.int32)) - 1
    first_pos = jnp.where(first, tiles, nblk)
    next_first = lax.cummin(jnp.roll(first_pos, -1).at[-1].set(nblk), reverse=True)
    next_e = jnp.where(next_first < nblk, blk_e[jnp.minimum(next_first, nblk - 1)], -1)

    any_spec = pl.BlockSpec(memory_space=pl.ANY)
    ys = pl.pallas_call(
        functools.partial(_expert_kernel, slab=slab),
        grid_spec=pltpu.PrefetchScalarGridSpec(
            num_scalar_prefetch=5,
            grid=(nblk,),
            in_specs=[any_spec, any_spec, any_spec, any_spec],
            out_specs=any_spec,
            scratch_shapes=[pltpu.VMEM((N_IN_TILES, tb, half), jnp.int32),
                            pltpu.VMEM((2, tb, half), jnp.int32),
                            pltpu.VMEM((2, d, f), jnp.bfloat16),
                            pltpu.VMEM((2, d, f), jnp.bfloat16),
                            pltpu.VMEM((2, f, d), jnp.bfloat16),
                            pltpu.SemaphoreType.DMA((N_IN_TILES,)),
                            pltpu.SemaphoreType.DMA((2,)),
                            pltpu.SemaphoreType.DMA((2,))],
        ),
        out_shape=jax.ShapeDtypeStruct((p, slab, V7X_LANES), jnp.int32),
        compiler_params=_params(("arbitrary",)),
        name="experts",
    )(blk_e, n_used, first.astype(jnp.int32), ordinal.astype(jnp.int32), next_e.astype(jnp.int32),
      xs.reshape(p, slab, V7X_LANES), wg, wu, wd)
    return ys.reshape(xs.shape)


def _combine_kernel(x_ref, ysh_ref, wt_ref, gt_ref, gf_ref,
                    slot_hbm, ys_hbm, o_ref, slot_s, rows_v, wrep, acc_lo, acc_hi, sem_i, sem_r):
    i = pl.program_id(0)
    n = pl.num_programs(0)
    tb = x_ref.shape[0]
    half = x_ref.shape[1] // 2
    slab = half // V7X_LANES
    per_tile = MOE_TILE // tb
    cur = i % 2
    nxt = 1 - cur

    def slot_copy(step):
        win = pl.ds((step % per_tile) * tb, tb)
        return pltpu.make_async_copy(slot_hbm.at[step // per_tile, :, win], slot_s.at[step % 2],
                                     sem_i.at[step % 2])

    def request_token(buf, t):
        for k in range(TOP_K):
            pltpu.make_async_copy(ys_hbm.at[_slab_rows(slot_s[buf, k, t], slab)],
                                  rows_v.at[buf, k, _slab_rows(t, slab)],
                                  sem_r.at[buf]).start(priority=k % 2)

    def wait_rows(buf):
        for k in range(TOP_K):
            pltpu.make_async_copy(ys_hbm.at[pl.ds(0, tb * slab)], rows_v.at[buf, k], sem_r.at[buf]).wait()

    @pl.when(i == 0)
    def _():
        slot_copy(0).start()
        slot_copy(0).wait()

        def first(t, carry):
            request_token(0, t)
            return carry

        lax.fori_loop(0, tb, first, 0, unroll=2)

        @pl.when(n > 1)
        def _():
            slot_copy(1).start()
            slot_copy(1).wait()

    @pl.when(i + 2 < n)
    def _():
        slot_copy(i + 2).start()

    wt = wt_ref[...]
    for k in range(TOP_K):
        wrep[k] = jnp.broadcast_to(wt[:, k:k + 1], (tb, V7X_LANES))

    def token(t, carry, buf, request_next):
        rows = _slab_rows(t, slab)
        lo_acc = jnp.zeros((slab, V7X_LANES), jnp.float32)
        hi_acc = jnp.zeros((slab, V7X_LANES), jnp.float32)
        for k in range(TOP_K):
            wv = jnp.broadcast_to(wrep[k, pl.ds(t, 1), :], (slab, V7X_LANES))
            lo, hi = _unpack_pair(rows_v[buf, k, rows, :])
            lo_acc = lo_acc + wv * lo
            hi_acc = hi_acc + wv * hi
        acc_lo[rows, :] = lo_acc
        acc_hi[rows, :] = hi_acc
        if request_next:
            request_token(1 - buf, t)
        return carry

    for buf in range(2):
        for request_next in (True, False):
            @pl.when((cur == buf) & ((i + 1 < n) == request_next))
            def _():
                wait_rows(buf)
                lax.fori_loop(0, tb, functools.partial(token, buf=buf, request_next=request_next),
                              0, unroll=4)

    @pl.when(i + 2 < n)
    def _():
        slot_copy(i + 2).wait()

    yy = ysh_ref[...].astype(jnp.float32) + jnp.concatenate(
        [_slab_load(acc_lo, tb, slab), _slab_load(acc_hi, tb, slab)], axis=1)
    xo = x_ref[...] + gt_ref[0] * yy
    ms = jnp.mean(xo * xo, axis=-1, keepdims=True)
    o_ref[...] = xo * lax.rsqrt(ms + EPS) * gf_ref[...]


def _combine(x1, y_shared, w_t, mod3, gate_idx, g_final, slot3, ys, seq):
    t, d = x1.shape
    tb = COMBINE_TILE
    assert MOE_TILE % tb == 0 and seq % tb == 0
    per_b = seq // tb
    half = d // 2
    return pl.pallas_call(
        _combine_kernel,
        grid=(t // tb,),
        in_specs=[pl.BlockSpec((tb, d), lambda i: (i, 0)),
                  pl.BlockSpec((tb, d), lambda i: (i, 0)),
                  pl.BlockSpec((tb, TOP_K), lambda i: (i, 0)),
                  pl.BlockSpec((1, 1, d), lambda i: ((i // per_b) * 6 + gate_idx, 0, 0)),
                  pl.BlockSpec((1, d), lambda i: (0, 0)),
                  pl.BlockSpec(memory_space=pl.ANY),
                  pl.BlockSpec(memory_space=pl.ANY)],
        out_specs=pl.BlockSpec((tb, d), lambda i: (i, 0)),
        scratch_shapes=[pltpu.SMEM((2, TOP_K, tb), jnp.int32),
                        pltpu.VMEM((2, TOP_K, tb * half // V7X_LANES, V7X_LANES), jnp.int32),
                        pltpu.VMEM((TOP_K, tb, V7X_LANES), jnp.float32),
                        pltpu.VMEM((tb * half // V7X_LANES, V7X_LANES), jnp.float32),
                        pltpu.VMEM((tb * half // V7X_LANES, V7X_LANES), jnp.float32),
                        pltpu.SemaphoreType.DMA((2,)),
                        pltpu.SemaphoreType.DMA((2,))],
        out_shape=jax.ShapeDtypeStruct((t, d), jnp.float32),
        compiler_params=_params(("arbitrary",)),
        name="combine",
    )(x1, y_shared, w_t, mod3, g_final.reshape(1, d), slot3, ys)


def _in_layout(d):
    qw = ATT_HEADS * ATT_HEAD_DIM
    kvw = ATT_KV_HEADS * ATT_HEAD_DIM
    rqk = RET_HEADS * RET_QK_DIM
    rv = RET_HEADS * RET_V_DIM
    order = [("qa", qw), ("ka", kvw), ("va", kvw), ("qr", rqk), ("kr", rqk),
             ("vr", rv), ("gr", rv), ("ga", d), ("gb", d)]
    dst = {}
    off = 0
    for name, width in order:
        dst[name] = off
        off += width
    return dst


def kernel(x, c, w_ada, b_ada, g_norm_mix, w_in, attn_sinks, w_attn_out, w_ret_out, w_o, g_norm_ffn,
           w_router, b_router, w_gate, w_up, w_down, w_sh_gate, w_sh_up, w_sh_down, g_norm_final):
    b, s, d = x.shape
    t = b * s
    depth = w_ada.shape[0]
    bf = jnp.bfloat16
    dst = _in_layout(d)

    c_pad = jnp.zeros((V7X_SUBLANES, d), jnp.float32).at[:b].set(c)
    x2 = x.reshape(t, d)
    for l in range(depth):
        mod = _ada(c_pad, w_ada[l], b_ada[l])
        mod3 = mod[:b].reshape(b * 6, 1, d)

        h = _norm_mod(x2.reshape(b, s, d), g_norm_mix[l], mod3, 0, 1)
        proj, (wg_b, wu_b, wd_b) = _in_proj(h.reshape(t, d), w_in[l].astype(bf),
                                            ((w_gate[l], w_up[l], w_down[l]),), 1024, 768)
        wao_b, wro_b, wo_b = w_attn_out[l].astype(bf), w_ret_out[l].astype(bf), w_o[l].astype(bf)
        wsg_b, wsu_b, wsd_b = w_sh_gate[l].astype(bf), w_sh_up[l].astype(bf), w_sh_down[l].astype(bf)
        proj3 = proj.reshape(b, s, proj.shape[1])
        attn = _attention(proj3, attn_sinks[l], dst["qa"], dst["ka"], dst["va"])
        ret = _retention(proj3, dst["qr"], dst["kr"], dst["vr"], dst["gr"])
        mix = _mix(attn.reshape(t, -1), ret.reshape(t, -1), wao_b, wro_b, proj, dst["ga"], dst["gb"])
        x1 = _out_resid(mix, wo_b, x2, mod3, 2, s)

        h2, h2p, idx3, pos3, w3, cnt = _router(x1, g_norm_ffn[l], mod3, 3, 4,
                                               w_router[l].T.astype(bf), b_router[l], s)
        counts = cnt[:, 0]
        tile = MOE_TILE
        padded = (counts + tile - 1) // tile * tile
        pad_end = jnp.cumsum(padded).astype(jnp.int32)
        pad_start = (pad_end - padded).astype(jnp.int32)
        n_blocks = (t * TOP_K) // tile + N_EXPERTS
        n_used = (pad_end[-1] // tile).reshape(1).astype(jnp.int32)
        blk_first = jnp.arange(n_blocks, dtype=jnp.int32) * tile
        blk_e = jnp.minimum(jnp.sum((pad_end[None, :] <= blk_first[:, None]).astype(jnp.int32), axis=1),
                            N_EXPERTS - 1)
        slab = d // 2 // V7X_LANES
        slot3 = _slots(pad_start, idx3, pos3)
        xs, y_shared = _dispatch(pad_start, pad_end, h2p, h2, wsg_b, wsu_b, wsd_b, slot3,
                                 n_blocks * tile, slab)
        ys = _experts(blk_e, n_used, xs, wg_b, wu_b, wd_b, slab)
        w_t = w3.transpose(0, 2, 1).reshape(t, TOP_K)
        is_last = l == depth - 1
        assert is_last, "the final norm is fused into the last layer's combine"
        x2 = _combine(x1, y_shared, w_t, mod3, 5, g_norm_final, slot3, ys, s)
    return x2.reshape(b, s, d)
```

```python
import functools
import math

import jax
import jax.numpy as jnp
from jax import lax
from jax.experimental import pallas as pl
from jax.experimental.pallas import tpu as pltpu

ATT_HEADS = 32
ATT_KV_HEADS = 4
ATT_HEAD_DIM = 64
WINDOW = 128
ATT_BLOCK = 128
RET_HEADS = 8
RET_QK_DIM = 256
RET_V_DIM = 512
RET_CHUNK = 128
N_EXPERTS = 64
N_GROUPS = 8
TOPK_GROUPS = 4
TOP_K = 8
ROUTED_SCALE = 2.5
EPS = 1e-6

V7X_LANES = 128
V7X_SUBLANES = 8
V7X_VMEM_LIMIT_BYTES = 60000 * 1024

MOE_TILE = 256
COMBINE_TILE = 128
N_IN_TILES = 3
NEG_BIG = -1e30


def _div_block(n, target, align):
    best = None
    b = align
    while b <= min(n, target):
        if n % b == 0:
            best = b
        b += align
    assert best is not None, (n, target, align)
    return best


def _params(semantics):
    return pltpu.CompilerParams(dimension_semantics=semantics,
                                vmem_limit_bytes=V7X_VMEM_LIMIT_BYTES)


def _sigmoid(v):
    return 1.0 / (1.0 + jnp.exp(-v))


def _silu(v):
    return v * _sigmoid(v)


def _pack_pair(lo, hi):
    return pltpu.pack_elementwise([lo, hi], packed_dtype=jnp.bfloat16)


def _unpack_pair(p):
    lo = pltpu.unpack_elementwise(p, index=0, packed_dtype=jnp.bfloat16, unpacked_dtype=jnp.float32)
    hi = pltpu.unpack_elementwise(p, index=1, packed_dtype=jnp.bfloat16, unpacked_dtype=jnp.float32)
    return lo, hi


def _slab_load(ref, n_rows, slab):
    return jnp.concatenate([ref[pl.ds(s, n_rows, stride=slab), :] for s in range(slab)], axis=1)


def _slab_store(ref, val, n_rows, slab):
    for s in range(slab):
        ref[pl.ds(s, n_rows, stride=slab), :] = val[:, s * V7X_LANES:(s + 1) * V7X_LANES]


def _slab_rows(r, slab):
    return pl.ds(pl.multiple_of(r * slab, slab), slab)


def _ada_kernel(c_ref, w_ref, b_ref, o_ref):
    cs = _silu(c_ref[...]).astype(jnp.bfloat16)
    o_ref[...] = jnp.dot(cs, w_ref[...].astype(jnp.bfloat16),
                         preferred_element_type=jnp.float32) + b_ref[...]


def _ada(c_pad, w, b):
    m, d = c_pad.shape
    n = w.shape[1]
    tn = _div_block(n, 512, V7X_LANES)
    return pl.pallas_call(
        _ada_kernel,
        grid=(n // tn,),
        in_specs=[pl.BlockSpec((m, d), lambda j: (0, 0)),
                  pl.BlockSpec((d, tn), lambda j: (0, j)),
                  pl.BlockSpec((1, tn), lambda j: (0, j))],
        out_specs=pl.BlockSpec((m, tn), lambda j: (0, j)),
        out_shape=jax.ShapeDtypeStruct((m, n), jnp.float32),
        compiler_params=_params(("parallel",)),
        name="ada",
    )(c_pad, w, b.reshape(1, n))


def _norm_mod_kernel(x_ref, g_ref, sh_ref, sc_ref, o_ref):
    x = x_ref[0]
    ms = jnp.mean(x * x, axis=-1, keepdims=True)
    y = x * lax.rsqrt(ms + EPS) * g_ref[...]
    o_ref[0] = (y * (1.0 + sc_ref[0]) + sh_ref[0]).astype(o_ref.dtype)


def _norm_mod(x3, g, mod3, shift_idx, scale_idx):
    b, s, d = x3.shape
    ts = _div_block(s, 512, V7X_SUBLANES)
    return pl.pallas_call(
        _norm_mod_kernel,
        grid=(b, s // ts),
        in_specs=[pl.BlockSpec((1, ts, d), lambda bi, i: (bi, i, 0)),
                  pl.BlockSpec((1, d), lambda bi, i: (0, 0)),
                  pl.BlockSpec((1, 1, d), lambda bi, i: (bi * 6 + shift_idx, 0, 0)),
                  pl.BlockSpec((1, 1, d), lambda bi, i: (bi * 6 + scale_idx, 0, 0))],
        out_specs=pl.BlockSpec((1, ts, d), lambda bi, i: (bi, i, 0)),
        out_shape=jax.ShapeDtypeStruct((b, s, d), jnp.bfloat16),
        compiler_params=_params(("parallel", "parallel")),
        name="norm_mod",
    )(x3, g.reshape(1, d), mod3, mod3)


def _in_proj_kernel(a_ref, b_ref, *rest, windows, n_j):
    n_side = len(windows)
    src = rest[:n_side]
    o_ref = rest[n_side]
    dst = rest[n_side + 1:]
    o_ref[...] = jnp.dot(a_ref[...], b_ref[...],
                         preferred_element_type=jnp.float32).astype(o_ref.dtype)
    step = pl.program_id(0) * n_j + pl.program_id(1)

    for s_ref, d_ref, (first, count) in zip(src, dst, windows):
        @pl.when((step >= first) & (step < first + count))
        def _():
            d_ref[...] = s_ref[...].astype(d_ref.dtype)


def _in_proj(a, b, side_groups, tm_target, tn_target):
    m, k = a.shape
    n = b.shape[1]
    tm = _div_block(m, tm_target, V7X_SUBLANES)
    tn = _div_block(n, tn_target, V7X_LANES)
    n_i, n_j = m // tm, n // tn
    steps_left = n_i * n_j
    first = 0
    side, chunked, windows = [], [], []
    for group in side_groups:
        n_max = 1 << (steps_left.bit_length() - 1)
        used = 0
        for w in group:
            rows = w.size // w.shape[-1]
            count = n_max
            while rows % (count * 2 * V7X_SUBLANES):
                count //= 2
            side.append(w)
            chunked.append(w.reshape(count, rows // count, w.shape[-1]))
            windows.append((first, count))
            used = max(used, count)
        first += used
        steps_left -= used

    def side_spec(c, window):
        w_first, count = window
        return pl.BlockSpec((1,) + c.shape[1:],
                            lambda i, j: (jnp.clip(i * n_j + j - w_first, 0, count - 1), 0, 0))

    side_specs = [side_spec(c, w) for c, w in zip(chunked, windows)]
    outs = pl.pallas_call(
        functools.partial(_in_proj_kernel, windows=tuple(windows), n_j=n_j),
        grid=(n_i, n_j),
        in_specs=[pl.BlockSpec((tm, k), lambda i, j: (i, 0)),
                  pl.BlockSpec((k, tn), lambda i, j: (0, j))] + side_specs,
        out_specs=[pl.BlockSpec((tm, tn), lambda i, j: (i, j))] + side_specs,
        out_shape=[jax.ShapeDtypeStruct((m, n), jnp.bfloat16)]
        + [jax.ShapeDtypeStruct(c.shape, jnp.bfloat16) for c in chunked],
        compiler_params=_params(("arbitrary", "arbitrary")),
        name="in_proj",
    )(a, b, *chunked)
    return outs[0], [o.reshape(w.shape) for o, w in zip(outs[1:], side)]


def _attn_kernel(sink_ref, q_ref, kc_ref, kp_ref, vc_ref, vp_ref, bias_ref, o_ref):
    i = pl.program_id(1)
    blk = ATT_BLOCK
    hd = ATT_HEAD_DIM
    group = ATT_HEADS // ATT_KV_HEADS
    pairs = group // 2
    nt = (((1,), (1,)), ((), ()))
    tn = (((0,), (0,)), ((), ()))
    zpad = jnp.zeros((2 * blk, hd), jnp.bfloat16)

    def scores(h, par):
        sl = slice(h * hd, (h + 1) * hd)
        k2 = jnp.concatenate([kp_ref[0, :, sl], kc_ref[0, :, sl]], axis=0) * (hd ** -0.5)
        qp = jnp.concatenate([q_ref[0, :, (h * pairs + p) * 2 * hd:(h * pairs + p + 1) * 2 * hd]
                              for p in range(pairs)], axis=0)
        kz = jnp.concatenate([k2, zpad] if par == 0 else [zpad, k2], axis=1)
        return lax.dot_general(kz, qp, nt, preferred_element_type=jnp.float32)

    def softmax(s, h, par):
        s = s + bias_ref[h * 2 + par]
        s = jnp.concatenate([jnp.where(i == 0, NEG_BIG, s[:blk]), s[blk:]], axis=0)
        sink = jnp.concatenate([jnp.full((1, blk), sink_ref[h * group + 2 * p + par], jnp.float32)
                                for p in range(pairs)], axis=1)
        m = jnp.maximum(s.max(0, keepdims=True), sink)
        pr = jnp.exp(s - m)
        denom = pr.sum(0, keepdims=True) + jnp.exp(sink - m)
        return pr.astype(jnp.bfloat16), 1.0 / denom

    def values(pr, inv, h, par):
        sl = slice(h * hd, (h + 1) * hd)
        v2 = jnp.concatenate([vp_ref[0, :, sl], vc_ref[0, :, sl]], axis=0)
        vz = jnp.concatenate([v2, zpad] if par == 0 else [zpad, v2], axis=1)
        return lax.dot_general(vz, pr, tn, preferred_element_type=jnp.float32) * inv

    items = [(h, par) for h in range(ATT_KV_HEADS) for par in range(2)]
    s_of, p_of, acc = {}, {}, {}
    for n in range(len(items) + 2):
        if n < len(items):
            s_of[n] = scores(*items[n])
        if 1 <= n <= len(items):
            p_of[n - 1] = softmax(s_of.pop(n - 1), *items[n - 1])
        if n >= 2:
            h, par = items[n - 2]
            o = values(*p_of.pop(n - 2), h, par)
            acc[h] = o if par == 0 else acc[h] + o
            if par == 1:
                out = acc.pop(h)
                for p in range(pairs):
                    o_ref[0, :, (h * pairs + p) * 2 * hd:(h * pairs + p + 1) * 2 * hd] = (
                        out[:, p * blk:(p + 1) * blk].T.astype(o_ref.dtype))


def _attn_bias():
    blk = ATT_BLOCK
    group = ATT_HEADS // ATT_KV_HEADS
    pairs = group // 2
    qi = jnp.arange(blk)[:, None]
    kj = jnp.arange(2 * blk)[None, :]
    dist = qi + blk - kj
    valid = (dist >= 0) & (dist < WINDOW)
    slopes = jnp.exp2(-8.0 * jnp.arange(1, ATT_HEADS + 1, dtype=jnp.float32) / ATT_HEADS)
    slopes = slopes.reshape(ATT_KV_HEADS, pairs, 2)
    bias = jnp.where(valid, -slopes[..., None, None] * dist.astype(jnp.float32), NEG_BIG)
    return bias.transpose(0, 2, 4, 1, 3).reshape(ATT_KV_HEADS * 2, 2 * blk, pairs * blk)


def _attention(proj3, sinks, q_off, k_off, v_off):
    b, s, _ = proj3.shape
    qw = ATT_HEADS * ATT_HEAD_DIM
    kvw = ATT_KV_HEADS * ATT_HEAD_DIM
    nb = s // ATT_BLOCK
    group = ATT_HEADS // ATT_KV_HEADS
    assert q_off % qw == 0 and k_off % kvw == 0 and v_off % kvw == 0
    assert group % 2 == 0 and 2 * ATT_HEAD_DIM == V7X_LANES and WINDOW == ATT_BLOCK
    assert 4 ** round(math.log(ATT_HEAD_DIM, 4)) == ATT_HEAD_DIM, "score scale must be a power of two"
    qb, kb, vb = q_off // qw, k_off // kvw, v_off // kvw
    bias = _attn_bias()
    return pl.pallas_call(
        _attn_kernel,
        grid_spec=pltpu.PrefetchScalarGridSpec(
            num_scalar_prefetch=1,
            grid=(b, nb),
            in_specs=[pl.BlockSpec((1, ATT_BLOCK, qw), lambda bi, i, sk: (bi, i, qb)),
                      pl.BlockSpec((1, ATT_BLOCK, kvw), lambda bi, i, sk: (bi, i, kb)),
                      pl.BlockSpec((1, ATT_BLOCK, kvw), lambda bi, i, sk: (bi, jnp.maximum(i - 1, 0), kb)),
                      pl.BlockSpec((1, ATT_BLOCK, kvw), lambda bi, i, sk: (bi, i, vb)),
                      pl.BlockSpec((1, ATT_BLOCK, kvw), lambda bi, i, sk: (bi, jnp.maximum(i - 1, 0), vb)),
                      pl.BlockSpec(bias.shape, lambda bi, i, sk: (0, 0, 0), pipeline_mode=pl.Buffered(1))],
            out_specs=pl.BlockSpec((1, ATT_BLOCK, qw), lambda bi, i, sk: (bi, i, 0)),
        ),
        out_shape=jax.ShapeDtypeStruct((b, s, qw), jnp.bfloat16),
        compiler_params=_params(("parallel", "parallel")),
        name="attention",
    )(sinks, proj3, proj3, proj3, proj3, proj3, bias)


def _ret_kernel(q_ref, k_ref, v_ref, gr_ref, mask_ref, qd_ref, kd_ref, cd_ref, o_ref, state_ref):
    c = pl.program_id(1)

    @pl.when(c == 0)
    def _():
        state_ref[...] = jnp.zeros_like(state_ref)

    nt = (((1,), (1,)), ((), ()))
    tn = (((0,), (0,)), ((), ()))
    def decayed(bi):
        q = q_ref[bi]
        k = k_ref[bi]
        attn = lax.dot_general(q, k, nt, preferred_element_type=jnp.float32) * mask_ref[0]
        kd = (k.astype(jnp.float32) * kd_ref[0]).astype(k.dtype)
        return attn.astype(jnp.bfloat16), kd

    def recur(bi, attn, kd):
        q = q_ref[bi]
        v = v_ref[bi]
        intra = jnp.dot(attn, v, preferred_element_type=jnp.float32)
        state = state_ref[bi]
        inter = jnp.dot(q, state.astype(q.dtype), preferred_element_type=jnp.float32) * qd_ref[0]
        state_ref[bi] = state * cd_ref[0] + lax.dot_general(kd, v, tn, preferred_element_type=jnp.float32)
        return intra + inter

    def finish(bi, o):
        mu = jnp.mean(o, axis=-1, keepdims=True)
        oc = o - mu
        var = jnp.mean(oc * oc, axis=-1, keepdims=True)
        y = oc * lax.rsqrt(var + EPS)
        o_ref[bi] = (_silu(gr_ref[bi].astype(jnp.float32)) * y).astype(o_ref.dtype)

    nb = q_ref.shape[0]
    a_of, o_of = {}, {}
    for n in range(nb + 2):
        if n < nb:
            a_of[n] = decayed(n)
        if 1 <= n <= nb:
            o_of[n - 1] = recur(n - 1, *a_of.pop(n - 1))
        if n >= 2:
            finish(n - 2, o_of.pop(n - 2))


def _retention(proj3, q_off, k_off, v_off, g_off):
    b, s, _ = proj3.shape
    dk, dv, ch = RET_QK_DIM, RET_V_DIM, RET_CHUNK
    assert q_off % dk == 0 and k_off % dk == 0 and v_off % dv == 0 and g_off % dv == 0
    qb, kb, vb, gb = q_off // dk, k_off // dk, v_off // dv, g_off // dv
    n = s // ch
    log_g = jnp.log1p(-jnp.exp2(-5.0 - jnp.arange(RET_HEADS, dtype=jnp.float32)))
    pos = jnp.arange(ch, dtype=jnp.float32)
    rel = pos[:, None] - pos[None, :]
    scale = dk ** -0.5
    mask = jnp.where(rel[None] >= 0, jnp.exp(rel[None] * log_g[:, None, None]), 0.0) * scale
    q_decay = jnp.exp((pos[None, :, None] + 1.0) * log_g[:, None, None])
    k_decay = jnp.exp((ch - 1.0 - pos[None, :, None]) * log_g[:, None, None]) * scale
    c_decay = jnp.exp(ch * log_g)[:, None, None]
    return pl.pallas_call(
        _ret_kernel,
        grid=(RET_HEADS, n),
        in_specs=[pl.BlockSpec((b, ch, dk), lambda h, c: (0, c, qb + h)),
                  pl.BlockSpec((b, ch, dk), lambda h, c: (0, c, kb + h)),
                  pl.BlockSpec((b, ch, dv), lambda h, c: (0, c, vb + h)),
                  pl.BlockSpec((b, ch, dv), lambda h, c: (0, c, gb + h)),
                  pl.BlockSpec((1, ch, ch), lambda h, c: (h, 0, 0)),
                  pl.BlockSpec((1, ch, 1), lambda h, c: (h, 0, 0)),
                  pl.BlockSpec((1, ch, 1), lambda h, c: (h, 0, 0)),
                  pl.BlockSpec((1, 1, 1), lambda h, c: (h, 0, 0))],
        out_specs=pl.BlockSpec((b, ch, dv), lambda h, c: (0, c, h)),
        out_shape=jax.ShapeDtypeStruct((b, s, RET_HEADS * dv), jnp.bfloat16),
        scratch_shapes=[pltpu.VMEM((b, dk, dv), jnp.float32)],
        compiler_params=_params(("parallel", "arbitrary")),
        name="retention",
    )(proj3, proj3, proj3, proj3, mask, q_decay, k_decay, c_decay)


def _mix_kernel(a_ref, r_ref, wa_ref, wr_ref, ga_ref, gb_ref, o_ref):
    ya = jnp.dot(a_ref[...], wa_ref[...], preferred_element_type=jnp.float32)
    yr = jnp.dot(r_ref[...], wr_ref[...], preferred_element_type=jnp.float32)
    ga = _sigmoid(ga_ref[...].astype(jnp.float32))
    gb = _sigmoid(gb_ref[...].astype(jnp.float32))
    o_ref[...] = (ga * ya + gb * yr).astype(o_ref.dtype)


def _mix(attn2, ret2, wa, wr, proj2, ga_off, gb_off):
    m, ka = attn2.shape
    kr = ret2.shape[1]
    d = wa.shape[1]
    tm = _div_block(m, 1024, V7X_SUBLANES)
    tn = _div_block(d, 512, V7X_LANES)
    assert ga_off % tn == 0 and gb_off % tn == 0
    gab, gbb = ga_off // tn, gb_off // tn
    return pl.pallas_call(
        _mix_kernel,
        grid=(m // tm, d // tn),
        in_specs=[pl.BlockSpec((tm, ka), lambda i, j: (i, 0)),
                  pl.BlockSpec((tm, kr), lambda i, j: (i, 0)),
                  pl.BlockSpec((ka, tn), lambda i, j: (0, j)),
                  pl.BlockSpec((kr, tn), lambda i, j: (0, j)),
                  pl.BlockSpec((tm, tn), lambda i, j: (i, gab + j)),
                  pl.BlockSpec((tm, tn), lambda i, j: (i, gbb + j))],
        out_specs=pl.BlockSpec((tm, tn), lambda i, j: (i, j)),
        out_shape=jax.ShapeDtypeStruct((m, d), jnp.bfloat16),
        compiler_params=_params(("parallel", "parallel")),
        name="mix",
    )(attn2, ret2, wa, wr, proj2, proj2)


def _resid_kernel(a_ref, w_ref, x_ref, gt_ref, o_ref):
    y = jnp.dot(a_ref[...], w_ref[...], preferred_element_type=jnp.float32)
    o_ref[...] = x_ref[...] + gt_ref[0] * y


def _out_resid(mix2, w, x2, mod3, gate_idx, seq):
    m, k = mix2.shape
    d = w.shape[1]
    tm = _div_block(seq, 1024, V7X_SUBLANES)
    tn = _div_block(d, 1024, V7X_LANES)
    per_b = seq // tm
    return pl.pallas_call(
        _resid_kernel,
        grid=(m // tm, d // tn),
        in_specs=[pl.BlockSpec((tm, k), lambda i, j: (i, 0)),
                  pl.BlockSpec((k, tn), lambda i, j: (0, j)),
                  pl.BlockSpec((tm, tn), lambda i, j: (i, j)),
                  pl.BlockSpec((1, 1, tn), lambda i, j: ((i // per_b) * 6 + gate_idx, 0, j))],
        out_specs=pl.BlockSpec((tm, tn), lambda i, j: (i, j)),
        out_shape=jax.ShapeDtypeStruct((m, d), jnp.float32),
        compiler_params=_params(("parallel", "parallel")),
        name="out_resid",
    )(mix2, w, x2, mod3)


def _router_kernel(x_ref, g_ref, sh_ref, sc_ref, wr_ref, br_ref,
                   h_ref, hp_ref, idx_ref, pos_ref, w_ref, cnt_ref, carry_ref):
    i = pl.program_id(0)
    e = N_EXPERTS
    per_g = e // N_GROUPS
    tb = x_ref.shape[0]

    @pl.when(i == 0)
    def _():
        carry_ref[...] = jnp.zeros_like(carry_ref)

    x = x_ref[...]
    ms = jnp.mean(x * x, axis=-1, keepdims=True)
    h = x * lax.rsqrt(ms + EPS) * g_ref[...]
    h = h * (1.0 + sc_ref[0]) + sh_ref[0]
    hb = h.astype(jnp.bfloat16)
    h_ref[...] = hb
    half = h.shape[1] // 2
    _slab_store(hp_ref, _pack_pair(h[:, :half], h[:, half:]), tb, half // V7X_LANES)

    nt = (((1,), (1,)), ((), ()))
    logits = lax.dot_general(wr_ref[...], hb, nt, preferred_element_type=jnp.float32)
    scores = _sigmoid(logits)
    choice = scores + br_ref[...]

    c3 = choice.reshape(N_GROUPS, per_g, tb)
    j_iota = lax.broadcasted_iota(jnp.int32, c3.shape, 1).astype(jnp.float32)
    m1 = c3.max(axis=1, keepdims=True)
    first = jnp.min(jnp.where(c3 == m1, j_iota, float(per_g)), axis=1, keepdims=True)
    m2 = jnp.where(j_iota == first, -jnp.inf, c3).max(axis=1, keepdims=True)
    gs = (m1 + m2).reshape(N_GROUPS, tb)

    g_iota = lax.broadcasted_iota(jnp.int32, gs.shape, 0)
    grank = jnp.zeros(gs.shape, jnp.int32)
    for gp in range(N_GROUPS):
        row = gs[gp:gp + 1, :]
        ahead = (row > gs) | ((row == gs) & (gp < g_iota))
        grank = grank + ahead.astype(jnp.int32)
    gmask = grank < TOPK_GROUPS
    emask = jnp.broadcast_to(gmask.reshape(N_GROUPS, 1, tb), c3.shape).reshape(e, tb)
    masked = jnp.where(emask, choice, -jnp.inf)

    e_iota = lax.broadcasted_iota(jnp.int32, masked.shape, 0)
    erank = jnp.zeros(masked.shape, jnp.int32)
    for ep in range(e):
        row = masked[ep:ep + 1, :]
        ahead = (row > masked) | ((row == masked) & (ep < e_iota))
        erank = erank + ahead.astype(jnp.int32)
    sel = (erank < TOP_K) & emask
    self32 = sel.astype(jnp.float32)

    wsel = scores * self32
    wn = wsel / jnp.sum(wsel, axis=0, keepdims=True) * ROUTED_SCALE

    selb = self32.astype(jnp.bfloat16)
    t_r = lax.broadcasted_iota(jnp.int32, (tb, tb), 0)
    t_c = lax.broadcasted_iota(jnp.int32, (tb, tb), 1)
    upper = (t_r <= t_c).astype(jnp.bfloat16)
    incl = jnp.dot(selb, upper, preferred_element_type=jnp.float32)
    carry = carry_ref[...]
    rank_in_e = carry + incl - 1.0
    carry_new = carry + jnp.sum(self32, axis=1, keepdims=True)
    carry_ref[...] = carry_new
    cnt_ref[...] = jnp.broadcast_to(carry_new, cnt_ref.shape).astype(jnp.int32)

    e_r = lax.broadcasted_iota(jnp.int32, (e, e), 0)
    e_c = lax.broadcasted_iota(jnp.int32, (e, e), 1)
    lower = (e_c < e_r).astype(jnp.bfloat16)
    before = jnp.dot(lower, selb, preferred_element_type=jnp.float32)
    e_f = e_iota.astype(jnp.float32)
    idx_rows, pos_rows, w_rows = [], [], []
    for k in range(TOP_K):
        hit = jnp.where(sel & (before == float(k)), 1.0, 0.0)
        idx_rows.append(jnp.sum(hit * e_f, axis=0, keepdims=True))
        pos_rows.append(jnp.sum(hit * rank_in_e, axis=0, keepdims=True))
        w_rows.append(jnp.sum(hit * wn, axis=0, keepdims=True))
    idx_ref[0] = jnp.concatenate(idx_rows, axis=0).astype(jnp.int32)
    pos_ref[0] = jnp.concatenate(pos_rows, axis=0).astype(jnp.int32)
    w_ref[0] = jnp.concatenate(w_rows, axis=0)


def _router(x1, g, mod3, shift_idx, scale_idx, w_router_t, b_router, seq):
    t, d = x1.shape
    e = N_EXPERTS
    tb = MOE_TILE
    assert seq % tb == 0 and d % (2 * V7X_LANES * V7X_SUBLANES) == 0
    slab = d // 2 // V7X_LANES
    per_b = seq // tb
    nt = t // tb
    return pl.pallas_call(
        _router_kernel,
        grid=(nt,),
        in_specs=[pl.BlockSpec((tb, d), lambda i: (i, 0)),
                  pl.BlockSpec((1, d), lambda i: (0, 0)),
                  pl.BlockSpec((1, 1, d), lambda i: ((i // per_b) * 6 + shift_idx, 0, 0)),
                  pl.BlockSpec((1, 1, d), lambda i: ((i // per_b) * 6 + scale_idx, 0, 0)),
                  pl.BlockSpec((e, d), lambda i: (0, 0)),
                  pl.BlockSpec((e, 1), lambda i: (0, 0))],
        out_specs=[pl.BlockSpec((tb, d), lambda i: (i, 0)),
                   pl.BlockSpec((tb * slab, V7X_LANES), lambda i: (i, 0)),
                   pl.BlockSpec((1, TOP_K, tb), lambda i: (i, 0, 0)),
                   pl.BlockSpec((1, TOP_K, tb), lambda i: (i, 0, 0)),
                   pl.BlockSpec((1, TOP_K, tb), lambda i: (i, 0, 0)),
                   pl.BlockSpec((e, V7X_LANES), lambda i: (0, 0))],
        out_shape=[jax.ShapeDtypeStruct((t, d), jnp.bfloat16),
                   jax.ShapeDtypeStruct((t * slab, V7X_LANES), jnp.int32),
                   jax.ShapeDtypeStruct((nt, TOP_K, tb), jnp.int32),
                   jax.ShapeDtypeStruct((nt, TOP_K, tb), jnp.int32),
                   jax.ShapeDtypeStruct((nt, TOP_K, tb), jnp.float32),
                   jax.ShapeDtypeStruct((e, V7X_LANES), jnp.int32)],
        scratch_shapes=[pltpu.VMEM((e, 1), jnp.float32)],
        compiler_params=_params(("arbitrary",)),
        name="router",
    )(x1, g.reshape(1, d), mod3, mod3, w_router_t, b_router.reshape(e, 1))


def _slots_kernel(pstart_ref, idx_ref, pos_ref, o_ref):
    idx = idx_ref[...]
    base = jnp.zeros(idx.shape, jnp.int32)
    for ex in range(N_EXPERTS):
        base = jnp.where(idx == ex, pstart_ref[ex], base)
    o_ref[...] = base + pos_ref[...]


def _slots(pad_start, idx3, pos3):
    nt = idx3.shape[0]
    per_step = _div_block(nt, 8, 1)
    blk = (per_step,) + idx3.shape[1:]
    spec = pl.BlockSpec(blk, lambda i, ps: (i, 0, 0))
    return pl.pallas_call(
        _slots_kernel,
        grid_spec=pltpu.PrefetchScalarGridSpec(num_scalar_prefetch=1, grid=(nt // per_step,),
                                               in_specs=[spec, spec], out_specs=spec),
        out_shape=jax.ShapeDtypeStruct(idx3.shape, jnp.int32),
        compiler_params=_params(("parallel",)),
        name="slots",
    )(pad_start, idx3, pos3)


def _dispatch_kernel(pstart_ref, pend_ref, hp_ref, h_ref, wsg_ref, wsu_ref, wsd_ref, slot_hbm,
                     xs_hbm, ysh_ref, slot_s, zero_v, sem_i, sem_z, sem_r, *, slab):
    i = pl.program_id(0)
    tb = hp_ref.shape[0] // slab

    def slot_copy():
        return pltpu.make_async_copy(slot_hbm.at[i], slot_s, sem_i)

    slot_copy().start()

    def zero_copy(ex):
        first = pl.multiple_of((pend_ref[ex] - tb) * slab, tb * slab)
        return pltpu.make_async_copy(zero_v, xs_hbm.at[pl.ds(first, tb * slab)], sem_z)

    @pl.when(i == 0)
    def _():
        zero_v[...] = jnp.zeros_like(zero_v)

        def start(ex, carry):
            @pl.when(pend_ref[ex] > pstart_ref[ex])
            def _():
                zero_copy(ex).start()
            return carry

        def wait(ex, carry):
            @pl.when(pend_ref[ex] > pstart_ref[ex])
            def _():
                zero_copy(ex).wait()
            return carry

        lax.fori_loop(0, N_EXPERTS, start, 0)
        lax.fori_loop(0, N_EXPERTS, wait, 0)

    slot_copy().wait()

    def row_copy(t, k):
        return pltpu.make_async_copy(hp_ref.at[_slab_rows(t, slab)],
                                     xs_hbm.at[_slab_rows(slot_s[k, t], slab)], sem_r)

    def start_rows(t, carry):
        for k in range(TOP_K):
            row_copy(t, k).start(priority=k % 2)
        return carry

    q4 = tb // 4
    d_half = ysh_ref.shape[1] // 2
    lax.fori_loop(0, q4, start_rows, 0, unroll=2)
    h = h_ref[...]
    g = jnp.dot(h, wsg_ref[...], preferred_element_type=jnp.float32)
    lax.fori_loop(q4, 2 * q4, start_rows, 0, unroll=2)
    u = jnp.dot(h, wsu_ref[...], preferred_element_type=jnp.float32)
    a = (_silu(g) * u).astype(jnp.bfloat16)
    lax.fori_loop(2 * q4, 3 * q4, start_rows, 0, unroll=2)
    ysh_ref[:, :d_half] = jnp.dot(a, wsd_ref[:, :d_half],
                                  preferred_element_type=jnp.float32).astype(ysh_ref.dtype)
    lax.fori_loop(3 * q4, tb, start_rows, 0, unroll=2)
    ysh_ref[:, d_half:] = jnp.dot(a, wsd_ref[:, d_half:],
                                  preferred_element_type=jnp.float32).astype(ysh_ref.dtype)

    for k in range(TOP_K):
        pltpu.make_async_copy(hp_ref, xs_hbm.at[pl.ds(0, tb * slab)], sem_r).wait()


def _dispatch(pad_start, pad_end, h2p, h2, wsg, wsu, wsd, slot3, n_rows, slab):
    tb = MOE_TILE
    t, d = h2.shape
    f = wsg.shape[1]
    once = pl.Buffered(1)
    return pl.pallas_call(
        functools.partial(_dispatch_kernel, slab=slab),
        grid_spec=pltpu.PrefetchScalarGridSpec(
            num_scalar_prefetch=2,
            grid=(t // tb,),
            in_specs=[pl.BlockSpec((tb * slab, V7X_LANES), lambda i, ps, pe: (i, 0)),
                      pl.BlockSpec((tb, d), lambda i, ps, pe: (i, 0)),
                      pl.BlockSpec((d, f), lambda i, ps, pe: (0, 0), pipeline_mode=once),
                      pl.BlockSpec((d, f), lambda i, ps, pe: (0, 0), pipeline_mode=once),
                      pl.BlockSpec((f, d), lambda i, ps, pe: (0, 0), pipeline_mode=once),
                      pl.BlockSpec(memory_space=pl.ANY)],
            out_specs=[pl.BlockSpec(memory_space=pl.ANY),
                       pl.BlockSpec((tb, d), lambda i, ps, pe: (i, 0))],
            scratch_shapes=[pltpu.SMEM((TOP_K, tb), jnp.int32),
                            pltpu.VMEM((tb * slab, V7X_LANES), jnp.int32),
                            pltpu.SemaphoreType.DMA,
                            pltpu.SemaphoreType.DMA,
                            pltpu.SemaphoreType.DMA],
        ),
        out_shape=[jax.ShapeDtypeStruct((n_rows * slab, V7X_LANES), jnp.int32),
                   jax.ShapeDtypeStruct((t, d), jnp.bfloat16)],
        compiler_params=_params(("arbitrary",)),
        name="dispatch",
    )(pad_start, pad_end, h2p, h2, wsg, wsu, wsd, slot3)


def _expert_kernel(blk_e_ref, nused_ref, first_ref, ord_ref, next_e_ref, xs_hbm, wg_hbm, wu_hbm, wd_hbm,
                   ys_hbm, xt, yt, wg_v, wu_v, wd_v, sem_in, sem_out, sem_w, *, slab):
    j = pl.program_id(0)
    n_used = nused_ref[0]
    tb = xt.shape[1]
    lanes = V7X_LANES
    half = slab * lanes

    def w_copies(e, slot):
        return [pltpu.make_async_copy(src.at[e], dst.at[slot], sem_w.at[slot])
                for src, dst in ((wg_hbm, wg_v), (wu_hbm, wu_v), (wd_hbm, wd_v))]

    def in_copies(tile, buf):
        rows = pl.ds(pl.multiple_of(tile * tb, tb), tb)
        return [pltpu.make_async_copy(xs_hbm.at[rows, s], xt.at[buf, :, pl.ds(s * lanes, lanes)],
                                      sem_in.at[buf]) for s in range(slab)]

    def out_copies(tile, buf):
        rows = pl.ds(pl.multiple_of(tile * tb, tb), tb)
        return [pltpu.make_async_copy(yt.at[buf, :, pl.ds(s * lanes, lanes)], ys_hbm.at[rows, s],
                                      sem_out.at[buf]) for s in range(slab)]

    def wait_in(buf):
        pltpu.make_async_copy(yt.at[0], xt.at[buf], sem_in.at[buf]).wait()

    def wait_out(buf):
        pltpu.make_async_copy(xt.at[0], yt.at[buf], sem_out.at[buf]).wait()

    @pl.when(j == 0)
    def _():
        for c in in_copies(0, 0) + w_copies(blk_e_ref[0], 0):
            c.start()

        @pl.when(n_used > 1)
        def _():
            for c in in_copies(1, 1):
                c.start()

    @pl.when(j < n_used)
    def _():
        cur = j % 2
        cur_in = j % N_IN_TILES
        slot = ord_ref[j] % 2

        @pl.when(first_ref[j] == 1)
        def _():
            for c in w_copies(blk_e_ref[j], slot):
                c.wait()

            @pl.when(next_e_ref[j] >= 0)
            def _():
                for c in w_copies(next_e_ref[j], 1 - slot):
                    c.start()

        @pl.when(j + 2 < n_used)
        def _():
            for c in in_copies(j + 2, (j + 2) % N_IN_TILES):
                c.start()

        wait_in(cur_in)

        @pl.when(j >= 2)
        def _():
            wait_out(cur)

        lo, hi = _unpack_pair(xt[cur_in])
        lo = lo.astype(jnp.bfloat16)
        hi = hi.astype(jnp.bfloat16)
        g = (jnp.dot(lo, wg_v[slot, :half, :], preferred_element_type=jnp.float32)
             + jnp.dot(hi, wg_v[slot, half:, :], preferred_element_type=jnp.float32))
        u = (jnp.dot(lo, wu_v[slot, :half, :], preferred_element_type=jnp.float32)
             + jnp.dot(hi, wu_v[slot, half:, :], preferred_element_type=jnp.float32))
        a = (_silu(g) * u).astype(jnp.bfloat16)
        y = jnp.dot(a, wd_v[slot], preferred_element_type=jnp.float32)
        yt[cur] = _pack_pair(y[:, :half], y[:, half:])
        for c in out_copies(j, cur):
            c.start(priority=1)

        @pl.when(j == n_used - 1)
        def _():
            @pl.when(j >= 1)
            def _():
                wait_out(1 - cur)

            wait_out(cur)


def _experts(blk_e, n_used, xs, wg, wu, wd, slab):
    tb = MOE_TILE
    p = xs.shape[0] // slab
    half = slab * V7X_LANES
    d = 2 * half
    f = wg.shape[2]
    nblk = p // tb

    tiles = jnp.arange(nblk, dtype=jnp.int32)
    used = tiles < n_used[0]
    first = used & ((tiles == 0) | (blk_e != jnp.roll(blk_e, 1)))
    ordinal = jnp.cumsum(first.astype(jnp.int32)) - 1
    first_pos = jnp.where(first, tiles, nblk)
    next_first = lax.cummin(jnp.roll(first_pos, -1).at[-1].set(nblk), reverse=True)
    next_e = jnp.where(next_first < nblk, blk_e[jnp.minimum(next_first, nblk - 1)], -1)

    any_spec = pl.BlockSpec(memory_space=pl.ANY)
    ys = pl.pallas_call(
        functools.partial(_expert_kernel, slab=slab),
        grid_spec=pltpu.PrefetchScalarGridSpec(
            num_scalar_prefetch=5,
            grid=(nblk,),
            in_specs=[any_spec, any_spec, any_spec, any_spec],
            out_specs=any_spec,
            scratch_shapes=[pltpu.VMEM((N_IN_TILES, tb, half), jnp.int32),
                            pltpu.VMEM((2, tb, half), jnp.int32),
                            pltpu.VMEM((2, d, f), jnp.bfloat16),
                            pltpu.VMEM((2, d, f), jnp.bfloat16),
                            pltpu.VMEM((2, f, d), jnp.bfloat16),
                            pltpu.SemaphoreType.DMA((N_IN_TILES,)),
                            pltpu.SemaphoreType.DMA((2,)),
                            pltpu.SemaphoreType.DMA((2,))],
        ),
        out_shape=jax.ShapeDtypeStruct((p, slab, V7X_LANES), jnp.int32),
        compiler_params=_params(("arbitrary",)),
        name="experts",
    )(blk_e, n_used, first.astype(jnp.int32), ordinal.astype(jnp.int32), next_e.astype(jnp.int32),
      xs.reshape(p, slab, V7X_LANES), wg, wu, wd)
    return ys.reshape(xs.shape)


def _combine_kernel(x_ref, ysh_ref, wt_ref, gt_ref, gf_ref,
                    slot_hbm, ys_hbm, o_ref, slot_s, rows_v, wrep, acc_lo, acc_hi, sem_i, sem_r):
    i = pl.program_id(0)
    n = pl.num_programs(0)
    tb = x_ref.shape[0]
    half = x_ref.shape[1] // 2
    slab = half // V7X_LANES
    per_tile = MOE_TILE // tb
    cur = i % 2

    def slot_copy(step):
        win = pl.ds((step % per_tile) * tb, tb)
        return pltpu.make_async_copy(slot_hbm.at[step // per_tile, :, win], slot_s.at[step % 2],
                                     sem_i.at[step % 2])

    def request_token(buf, t):
        for k in range(TOP_K):
            pltpu.make_async_copy(ys_hbm.at[_slab_rows(slot_s[buf, k, t], slab)],
                                  rows_v.at[buf, k, _slab_rows(t, slab)],
                                  sem_r.at[buf]).start(priority=k % 2)

    def wait_rows(buf):
        for k in range(TOP_K):
            pltpu.make_async_copy(ys_hbm.at[pl.ds(0, tb * slab)], rows_v.at[buf, k], sem_r.at[buf]).wait()

    @pl.when(i == 0)
    def _():
        slot_copy(0).start()
        slot_copy(0).wait()

        def first(t, carry):
            request_token(0, t)
            return carry

        lax.fori_loop(0, tb, first, 0, unroll=2)

        @pl.when(n > 1)
        def _():
            slot_copy(1).start()
            slot_copy(1).wait()

    @pl.when(i + 2 < n)
    def _():
        slot_copy(i + 2).start()

    wt = wt_ref[...]
    for k in range(TOP_K):
        wrep[k] = jnp.broadcast_to(wt[:, k:k + 1], (tb, V7X_LANES))

    def token(t, carry, buf, request_next):
        rows = _slab_rows(t, slab)
        lo_acc = jnp.zeros((slab, V7X_LANES), jnp.float32)
        hi_acc = jnp.zeros((slab, V7X_LANES), jnp.float32)
        for k in range(TOP_K):
            wv = jnp.broadcast_to(wrep[k, pl.ds(t, 1), :], (slab, V7X_LANES))
            lo, hi = _unpack_pair(rows_v[buf, k, rows, :])
            lo_acc = lo_acc + wv * lo
            hi_acc = hi_acc + wv * hi
        acc_lo[rows, :] = lo_acc
        acc_hi[rows, :] = hi_acc
        if request_next:
            request_token(1 - buf, t)
        return carry

    for buf in range(2):
        for request_next in (True, False):
            @pl.when((cur == buf) & ((i + 1 < n) == request_next))
            def _():
                wait_rows(buf)
                lax.fori_loop(0, tb, functools.partial(token, buf=buf, request_next=request_next),
                              0, unroll=4)

    @pl.when(i + 2 < n)
    def _():
        slot_copy(i + 2).wait()

    yy = ysh_ref[...].astype(jnp.float32) + jnp.concatenate(
        [_slab_load(acc_lo, tb, slab), _slab_load(acc_hi, tb, slab)], axis=1)
    xo = x_ref[...] + gt_ref[0] * yy
    ms = jnp.mean(xo * xo, axis=-1, keepdims=True)
    o_ref[...] = xo * lax.rsqrt(ms + EPS) * gf_ref[...]


def _combine(x1, y_shared, w_t, mod3, gate_idx, g_final, slot3, ys, seq):
    t, d = x1.shape
    tb = COMBINE_TILE
    assert MOE_TILE % tb == 0 and seq % tb == 0
    per_b = seq // tb
    half = d // 2
    return pl.pallas_call(
        _combine_kernel,
        grid=(t // tb,),
        in_specs=[pl.BlockSpec((tb, d), lambda i: (i, 0)),
                  pl.BlockSpec((tb, d), lambda i: (i, 0)),
                  pl.BlockSpec((tb, TOP_K), lambda i: (i, 0)),
                  pl.BlockSpec((1, 1, d), lambda i: ((i // per_b) * 6 + gate_idx, 0, 0)),
                  pl.BlockSpec((1, d), lambda i: (0, 0)),
                  pl.BlockSpec(memory_space=pl.ANY),
                  pl.BlockSpec(memory_space=pl.ANY)],
        out_specs=pl.BlockSpec((tb, d), lambda i: (i, 0)),
        scratch_shapes=[pltpu.SMEM((2, TOP_K, tb), jnp.int32),
                        pltpu.VMEM((2, TOP_K, tb * half // V7X_LANES, V7X_LANES), jnp.int32),
                        pltpu.VMEM((TOP_K, tb, V7X_LANES), jnp.float32),
                        pltpu.VMEM((tb * half // V7X_LANES, V7X_LANES), jnp.float32),
                        pltpu.VMEM((tb * half // V7X_LANES, V7X_LANES), jnp.float32),
                        pltpu.SemaphoreType.DMA((2,)),
                        pltpu.SemaphoreType.DMA((2,))],
        out_shape=jax.ShapeDtypeStruct((t, d), jnp.float32),
        compiler_params=_params(("arbitrary",)),
        name="combine",
    )(x1, y_shared, w_t, mod3, g_final.reshape(1, d), slot3, ys)


def _in_layout(d):
    qw = ATT_HEADS * ATT_HEAD_DIM
    kvw = ATT_KV_HEADS * ATT_HEAD_DIM
    rqk = RET_HEADS * RET_QK_DIM
    rv = RET_HEADS * RET_V_DIM
    order = [("qa", qw), ("ka", kvw), ("va", kvw), ("qr", rqk), ("kr", rqk),
             ("vr", rv), ("gr", rv), ("ga", d), ("gb", d)]
    dst = {}
    off = 0
    for name, width in order:
        dst[name] = off
        off += width
    return dst


def kernel(x, c, w_ada, b_ada, g_norm_mix, w_in, attn_sinks, w_attn_out, w_ret_out, w_o, g_norm_ffn,
           w_router, b_router, w_gate, w_up, w_down, w_sh_gate, w_sh_up, w_sh_down, g_norm_final):
    b, s, d = x.shape
    t = b * s
    depth = w_ada.shape[0]
    bf = jnp.bfloat16
    dst = _in_layout(d)

    c_pad = jnp.zeros((V7X_SUBLANES, d), jnp.float32).at[:b].set(c)
    x2 = x.reshape(t, d)
    for l in range(depth):
        mod = _ada(c_pad, w_ada[l], b_ada[l])
        mod3 = mod[:b].reshape(b * 6, 1, d)

        h = _norm_mod(x2.reshape(b, s, d), g_norm_mix[l], mod3, 0, 1)
        proj, (wg_b, wu_b, wd_b) = _in_proj(h.reshape(t, d), w_in[l].astype(bf),
                                            ((w_gate[l], w_up[l], w_down[l]),), 1024, 768)
        wao_b, wro_b, wo_b = w_attn_out[l].astype(bf), w_ret_out[l].astype(bf), w_o[l].astype(bf)
        wsg_b, wsu_b, wsd_b = w_sh_gate[l].astype(bf), w_sh_up[l].astype(bf), w_sh_down[l].astype(bf)
        proj3 = proj.reshape(b, s, proj.shape[1])
        attn = _attention(proj3, attn_sinks[l], dst["qa"], dst["ka"], dst["va"])
        ret = _retention(proj3, dst["qr"], dst["kr"], dst["vr"], dst["gr"])
        mix = _mix(attn.reshape(t, -1), ret.reshape(t, -1), wao_b, wro_b, proj, dst["ga"], dst["gb"])
        x1 = _out_resid(mix, wo_b, x2, mod3, 2, s)

        h2, h2p, idx3, pos3, w3, cnt = _router(x1, g_norm_ffn[l], mod3, 3, 4,
                                               w_router[l].T.astype(bf), b_router[l], s)
        counts = cnt[:, 0]
        tile = MOE_TILE
        padded = (counts + tile - 1) // tile * tile
        pad_end = jnp.cumsum(padded).astype(jnp.int32)
        pad_start = (pad_end - padded).astype(jnp.int32)
        n_blocks = (t * TOP_K) // tile + N_EXPERTS
        n_used = (pad_end[-1] // tile).reshape(1).astype(jnp.int32)
        blk_first = jnp.arange(n_blocks, dtype=jnp.int32) * tile
        blk_e = jnp.minimum(jnp.sum((pad_end[None, :] <= blk_first[:, None]).astype(jnp.int32), axis=1),
                            N_EXPERTS - 1)
        slab = d // 2 // V7X_LANES
        slot3 = _slots(pad_start, idx3, pos3)
        xs, y_shared = _dispatch(pad_start, pad_end, h2p, h2, wsg_b, wsu_b, wsd_b, slot3,
                                 n_blocks * tile, slab)
        ys = _experts(blk_e, n_used, xs, wg_b, wu_b, wd_b, slab)
        w_t = w3.transpose(0, 2, 1).reshape(t, TOP_K)
        is_last = l == depth - 1
        assert is_last, "the final norm is fused into the last layer's combine"
        x2 = _combine(x1, y_shared, w_t, mod3, 5, g_norm_final, slot3, ys, s)
    return x2.reshape(b, s, d)
```

```python
import functools
import math

import jax
import jax.numpy as jnp
from jax import lax
from jax.experimental import pallas as pl
from jax.experimental.pallas import tpu as pltpu

ATT_HEADS = 32
ATT_KV_HEADS = 4
ATT_HEAD_DIM = 64
WINDOW = 128
ATT_BLOCK = 128
RET_HEADS = 8
RET_QK_DIM = 256
RET_V_DIM = 512
RET_CHUNK = 128
N_EXPERTS = 64
N_GROUPS = 8
TOPK_GROUPS = 4
TOP_K = 8
ROUTED_SCALE = 2.5
EPS = 1e-6

V7X_LANES = 128
V7X_SUBLANES = 8
V7X_VMEM_LIMIT_BYTES = 60000 * 1024

MOE_TILE = 256
COMBINE_TILE = 128
N_IN_TILES = 3
NEG_BIG = -1e30


def _div_block(n, target, align):
    best = None
    b = align
    while b <= min(n, target):
        if n % b == 0:
            best = b
        b += align
    assert best is not None, (n, target, align)
    return best


def _params(semantics):
    return pltpu.CompilerParams(dimension_semantics=semantics,
                                vmem_limit_bytes=V7X_VMEM_LIMIT_BYTES)


def _sigmoid(v):
    return 1.0 / (1.0 + jnp.exp(-v))


def _silu(v):
    return v * _sigmoid(v)


def _pack_pair(lo, hi):
    return pltpu.pack_elementwise([lo, hi], packed_dtype=jnp.bfloat16)


def _unpack_pair(p):
    lo = pltpu.unpack_elementwise(p, index=0, packed_dtype=jnp.bfloat16, unpacked_dtype=jnp.float32)
    hi = pltpu.unpack_elementwise(p, index=1, packed_dtype=jnp.bfloat16, unpacked_dtype=jnp.float32)
    return lo, hi


def _slab_load(ref, n_rows, slab):
    return jnp.concatenate([ref[pl.ds(s, n_rows, stride=slab), :] for s in range(slab)], axis=1)


def _slab_store(ref, val, n_rows, slab):
    for s in range(slab):
        ref[pl.ds(s, n_rows, stride=slab), :] = val[:, s * V7X_LANES:(s + 1) * V7X_LANES]


def _slab_rows(r, slab):
    return pl.ds(pl.multiple_of(r * slab, slab), slab)


def _ada_kernel(c_ref, w_ref, b_ref, o_ref):
    cs = _silu(c_ref[...]).astype(jnp.bfloat16)
    o_ref[...] = jnp.dot(cs, w_ref[...].astype(jnp.bfloat16),
                         preferred_element_type=jnp.float32) + b_ref[...]


def _ada(c_pad, w, b):
    m, d = c_pad.shape
    n = w.shape[1]
    tn = _div_block(n, 512, V7X_LANES)
    return pl.pallas_call(
        _ada_kernel,
        grid=(n // tn,),
        in_specs=[pl.BlockSpec((m, d), lambda j: (0, 0)),
                  pl.BlockSpec((d, tn), lambda j: (0, j)),
                  pl.BlockSpec((1, tn), lambda j: (0, j))],
        out_specs=pl.BlockSpec((m, tn), lambda j: (0, j)),
        out_shape=jax.ShapeDtypeStruct((m, n), jnp.float32),
        compiler_params=_params(("parallel",)),
        name="ada",
    )(c_pad, w, b.reshape(1, n))


def _norm_mod_kernel(x_ref, g_ref, sh_ref, sc_ref, o_ref):
    x = x_ref[0]
    ms = jnp.mean(x * x, axis=-1, keepdims=True)
    y = x * lax.rsqrt(ms + EPS) * g_ref[...]
    o_ref[0] = (y * (1.0 + sc_ref[0]) + sh_ref[0]).astype(o_ref.dtype)


def _norm_mod(x3, g, mod3, shift_idx, scale_idx):
    b, s, d = x3.shape
    ts = _div_block(s, 512, V7X_SUBLANES)
    return pl.pallas_call(
        _norm_mod_kernel,
        grid=(b, s // ts),
        in_specs=[pl.BlockSpec((1, ts, d), lambda bi, i: (bi, i, 0)),
                  pl.BlockSpec((1, d), lambda bi, i: (0, 0)),
                  pl.BlockSpec((1, 1, d), lambda bi, i: (bi * 6 + shift_idx, 0, 0)),
                  pl.BlockSpec((1, 1, d), lambda bi, i: (bi * 6 + scale_idx, 0, 0))],
        out_specs=pl.BlockSpec((1, ts, d), lambda bi, i: (bi, i, 0)),
        out_shape=jax.ShapeDtypeStruct((b, s, d), jnp.bfloat16),
        compiler_params=_params(("parallel", "parallel")),
        name="norm_mod",
    )(x3, g.reshape(1, d), mod3, mod3)


def _in_proj_kernel(a_ref, b_ref, *rest, windows, n_j):
    n_side = len(windows)
    src = rest[:n_side]
    o_ref = rest[n_side]
    dst = rest[n_side + 1:]
    o_ref[...] = jnp.dot(a_ref[...], b_ref[...],
                         preferred_element_type=jnp.float32).astype(o_ref.dtype)
    step = pl.program_id(0) * n_j + pl.program_id(1)

    for s_ref, d_ref, (first, count) in zip(src, dst, windows):
        @pl.when((step >= first) & (step < first + count))
        def _():
            d_ref[...] = s_ref[...].astype(d_ref.dtype)


def _in_proj(a, b, side_groups, tm_target, tn_target):
    m, k = a.shape
    n = b.shape[1]
    tm = _div_block(m, tm_target, V7X_SUBLANES)
    tn = _div_block(n, tn_target, V7X_LANES)
    n_i, n_j = m // tm, n // tn
    steps_left = n_i * n_j
    first = 0
    side, chunked, windows = [], [], []
    for group in side_groups:
        n_max = 1 << (steps_left.bit_length() - 1)
        used = 0
        for w in group:
            rows = w.size // w.shape[-1]
            count = n_max
            while rows % (count * 2 * V7X_SUBLANES):
                count //= 2
            side.append(w)
            chunked.append(w.reshape(count, rows // count, w.shape[-1]))
            windows.append((first, count))
            used = max(used, count)
        first += used
        steps_left -= used

    def side_spec(c, window):
        w_first, count = window
        return pl.BlockSpec((1,) + c.shape[1:],
                            lambda i, j: (jnp.clip(i * n_j + j - w_first, 0, count - 1), 0, 0))

    side_specs = [side_spec(c, w) for c, w in zip(chunked, windows)]
    outs = pl.pallas_call(
        functools.partial(_in_proj_kernel, windows=tuple(windows), n_j=n_j),
        grid=(n_i, n_j),
        in_specs=[pl.BlockSpec((tm, k), lambda i, j: (i, 0)),
                  pl.BlockSpec((k, tn), lambda i, j: (0, j))] + side_specs,
        out_specs=[pl.BlockSpec((tm, tn), lambda i, j: (i, j))] + side_specs,
        out_shape=[jax.ShapeDtypeStruct((m, n), jnp.bfloat16)]
        + [jax.ShapeDtypeStruct(c.shape, jnp.bfloat16) for c in chunked],
        compiler_params=_params(("arbitrary", "arbitrary")),
        name="in_proj",
    )(a, b, *chunked)
    return outs[0], [o.reshape(w.shape) for o, w in zip(outs[1:], side)]


def _attn_kernel(sink_ref, q_ref, kc_ref, kp_ref, vc_ref, vp_ref, bias_ref, o_ref):
    i = pl.program_id(1)
    blk = ATT_BLOCK
    hd = ATT_HEAD_DIM
    group = ATT_HEADS // ATT_KV_HEADS
    pairs = group // 2
    nt = (((1,), (1,)), ((), ()))
    tn = (((0,), (0,)), ((), ()))
    zpad = jnp.zeros((2 * blk, hd), jnp.bfloat16)

    def scores(h, par):
        sl = slice(h * hd, (h + 1) * hd)
        k2 = jnp.concatenate([kp_ref[0, :, sl], kc_ref[0, :, sl]], axis=0) * (hd ** -0.5)
        qp = jnp.concatenate([q_ref[0, :, (h * pairs + p) * 2 * hd:(h * pairs + p + 1) * 2 * hd]
                              for p in range(pairs)], axis=0)
        kz = jnp.concatenate([k2, zpad] if par == 0 else [zpad, k2], axis=1)
        return lax.dot_general(kz, qp, nt, preferred_element_type=jnp.float32)

    def softmax(s, h, par):
        s = s + bias_ref[h * 2 + par]
        s = jnp.concatenate([jnp.where(i == 0, NEG_BIG, s[:blk]), s[blk:]], axis=0)
        sink = jnp.concatenate([jnp.full((1, blk), sink_ref[h * group + 2 * p + par], jnp.float32)
                                for p in range(pairs)], axis=1)
        m = jnp.maximum(s.max(0, keepdims=True), sink)
        pr = jnp.exp(s - m)
        denom = pr.sum(0, keepdims=True) + jnp.exp(sink - m)
        return pr.astype(jnp.bfloat16), 1.0 / denom

    def values(pr, inv, h, par):
        sl = slice(h * hd, (h + 1) * hd)
        v2 = jnp.concatenate([vp_ref[0, :, sl], vc_ref[0, :, sl]], axis=0)
        vz = jnp.concatenate([v2, zpad] if par == 0 else [zpad, v2], axis=1)
        return lax.dot_general(vz, pr, tn, preferred_element_type=jnp.float32) * inv

    items = [(h, par) for h in range(ATT_KV_HEADS) for par in range(2)]
    s_of, p_of, acc = {}, {}, {}
    for n in range(len(items) + 2):
        if n < len(items):
            s_of[n] = scores(*items[n])
        if 1 <= n <= len(items):
            p_of[n - 1] = softmax(s_of.pop(n - 1), *items[n - 1])
        if n >= 2:
            h, par = items[n - 2]
            o = values(*p_of.pop(n - 2), h, par)
            acc[h] = o if par == 0 else acc[h] + o
            if par == 1:
                out = acc.pop(h)
                for p in range(pairs):
                    o_ref[0, :, (h * pairs + p) * 2 * hd:(h * pairs + p + 1) * 2 * hd] = (
                        out[:, p * blk:(p + 1) * blk].T.astype(o_ref.dtype))


def _attn_bias():
    blk = ATT_BLOCK
    group = ATT_HEADS // ATT_KV_HEADS
    pairs = group // 2
    qi = jnp.arange(blk)[:, None]
    kj = jnp.arange(2 * blk)[None, :]
    dist = qi + blk - kj
    valid = (dist >= 0) & (dist < WINDOW)
    slopes = jnp.exp2(-8.0 * jnp.arange(1, ATT_HEADS + 1, dtype=jnp.float32) / ATT_HEADS)
    slopes = slopes.reshape(ATT_KV_HEADS, pairs, 2)
    bias = jnp.where(valid, -slopes[..., None, None] * dist.astype(jnp.float32), NEG_BIG)
    return bias.transpose(0, 2, 4, 1, 3).reshape(ATT_KV_HEADS * 2, 2 * blk, pairs * blk)


def _attention(proj3, sinks, q_off, k_off, v_off):
    b, s, _ = proj3.shape
    qw = ATT_HEADS * ATT_HEAD_DIM
    kvw = ATT_KV_HEADS * ATT_HEAD_DIM
    nb = s // ATT_BLOCK
    group = ATT_HEADS // ATT_KV_HEADS
    assert q_off % qw == 0 and k_off % kvw == 0 and v_off % kvw == 0
    assert group % 2 == 0 and 2 * ATT_HEAD_DIM == V7X_LANES and WINDOW == ATT_BLOCK
    assert 4 ** round(math.log(ATT_HEAD_DIM, 4)) == ATT_HEAD_DIM, "score scale must be a power of two"
    qb, kb, vb = q_off // qw, k_off // kvw, v_off // kvw
    bias = _attn_bias()
    return pl.pallas_call(
        _attn_kernel,
        grid_spec=pltpu.PrefetchScalarGridSpec(
            num_scalar_prefetch=1,
            grid=(b, nb),
            in_specs=[pl.BlockSpec((1, ATT_BLOCK, qw), lambda bi, i, sk: (bi, i, qb)),
                      pl.BlockSpec((1, ATT_BLOCK, kvw), lambda bi, i, sk: (bi, i, kb)),
                      pl.BlockSpec((1, ATT_BLOCK, kvw), lambda bi, i, sk: (bi, jnp.maximum(i - 1, 0), kb)),
                      pl.BlockSpec((1, ATT_BLOCK, kvw), lambda bi, i, sk: (bi, i, vb)),
                      pl.BlockSpec((1, ATT_BLOCK, kvw), lambda bi, i, sk: (bi, jnp.maximum(i - 1, 0), vb)),
                      pl.BlockSpec(bias.shape, lambda bi, i, sk: (0, 0, 0), pipeline_mode=pl.Buffered(1))],
            out_specs=pl.BlockSpec((1, ATT_BLOCK, qw), lambda bi, i, sk: (bi, i, 0)),
        ),
        out_shape=jax.ShapeDtypeStruct((b, s, qw), jnp.bfloat16),
        compiler_params=_params(("parallel", "parallel")),
        name="attention",
    )(sinks, proj3, proj3, proj3, proj3, proj3, bias)


def _ret_kernel(q_ref, k_ref, v_ref, gr_ref, mask_ref, qd_ref, kd_ref, cd_ref, o_ref, state_ref):
    c = pl.program_id(1)

    @pl.when(c == 0)
    def _():
        state_ref[...] = jnp.zeros_like(state_ref)

    nt = (((1,), (1,)), ((), ()))
    tn = (((0,), (0,)), ((), ()))
    def decayed(bi):
        q = q_ref[bi]
        k = k_ref[bi]
        attn = lax.dot_general(q, k, nt, preferred_element_type=jnp.float32) * mask_ref[0]
        kd = (k.astype(jnp.float32) * kd_ref[0]).astype(k.dtype)
        return attn.astype(jnp.bfloat16), kd

    def recur(bi, attn, kd):
        q = q_ref[bi]
        v = v_ref[bi]
        intra = jnp.dot(attn, v, preferred_element_type=jnp.float32)
        state = state_ref[bi]
        inter = jnp.dot(q, state.astype(q.dtype), preferred_element_type=jnp.float32) * qd_ref[0]
        state_ref[bi] = state * cd_ref[0] + lax.dot_general(kd, v, tn, preferred_element_type=jnp.float32)
        return intra + inter

    def finish(bi, o):
        mu = jnp.mean(o, axis=-1, keepdims=True)
        oc = o - mu
        var = jnp.mean(oc * oc, axis=-1, keepdims=True)
        y = oc * lax.rsqrt(var + EPS)
        o_ref[bi] = (_silu(gr_ref[bi].astype(jnp.float32)) * y).astype(o_ref.dtype)

    nb = q_ref.shape[0]
    a_of, o_of = {}, {}
    for n in range(nb + 2):
        if n < nb:
            a_of[n] = decayed(n)
        if 1 <= n <= nb:
            o_of[n - 1] = recur(n - 1, *a_of.pop(n - 1))
        if n >= 2:
            finish(n - 2, o_of.pop(n - 2))


def _retention(proj3, q_off, k_off, v_off, g_off):
    b, s, _ = proj3.shape
    dk, dv, ch = RET_QK_DIM, RET_V_DIM, RET_CHUNK
    assert q_off % dk == 0 and k_off % dk == 0 and v_off % dv == 0 and g_off % dv == 0
    qb, kb, vb, gb = q_off // dk, k_off // dk, v_off // dv, g_off // dv
    n = s // ch
    log_g = jnp.log1p(-jnp.exp2(-5.0 - jnp.arange(RET_HEADS, dtype=jnp.float32)))
    pos = jnp.arange(ch, dtype=jnp.float32)
    rel = pos[:, None] - pos[None, :]
    scale = dk ** -0.5
    mask = jnp.where(rel[None] >= 0, jnp.exp(rel[None] * log_g[:, None, None]), 0.0) * scale
    q_decay = jnp.exp((pos[None, :, None] + 1.0) * log_g[:, None, None])
    k_decay = jnp.exp((ch - 1.0 - pos[None, :, None]) * log_g[:, None, None]) * scale
    c_decay = jnp.exp(ch * log_g)[:, None, None]
    return pl.pallas_call(
        _ret_kernel,
        grid=(RET_HEADS, n),
        in_specs=[pl.BlockSpec((b, ch, dk), lambda h, c: (0, c, qb + h)),
                  pl.BlockSpec((b, ch, dk), lambda h, c: (0, c, kb + h)),
                  pl.BlockSpec((b, ch, dv), lambda h, c: (0, c, vb + h)),
                  pl.BlockSpec((b, ch, dv), lambda h, c: (0, c, gb + h)),
                  pl.BlockSpec((1, ch, ch), lambda h, c: (h, 0, 0)),
                  pl.BlockSpec((1, ch, 1), lambda h, c: (h, 0, 0)),
                  pl.BlockSpec((1, ch, 1), lambda h, c: (h, 0, 0)),
                  pl.BlockSpec((1, 1, 1), lambda h, c: (h, 0, 0))],
        out_specs=pl.BlockSpec((b, ch, dv), lambda h, c: (0, c, h)),
        out_shape=jax.ShapeDtypeStruct((b, s, RET_HEADS * dv), jnp.bfloat16),
        scratch_shapes=[pltpu.VMEM((b, dk, dv), jnp.float32)],
        compiler_params=_params(("parallel", "arbitrary")),
        name="retention",
    )(proj3, proj3, proj3, proj3, mask, q_decay, k_decay, c_decay)


def _mix_kernel(a_ref, r_ref, wa_ref, wr_ref, ga_ref, gb_ref, o_ref):
    ya = jnp.dot(a_ref[...], wa_ref[...], preferred_element_type=jnp.float32)
    yr = jnp.dot(r_ref[...], wr_ref[...], preferred_element_type=jnp.float32)
    ga = _sigmoid(ga_ref[...].astype(jnp.float32))
    gb = _sigmoid(gb_ref[...].astype(jnp.float32))
    o_ref[...] = (ga * ya + gb * yr).astype(o_ref.dtype)


def _mix(attn2, ret2, wa, wr, proj2, ga_off, gb_off):
    m, ka = attn2.shape
    kr = ret2.shape[1]
    d = wa.shape[1]
    tm = _div_block(m, 1024, V7X_SUBLANES)
    tn = _div_block(d, 512, V7X_LANES)
    assert ga_off % tn == 0 and gb_off % tn == 0
    gab, gbb = ga_off // tn, gb_off // tn
    return pl.pallas_call(
        _mix_kernel,
        grid=(m // tm, d // tn),
        in_specs=[pl.BlockSpec((tm, ka), lambda i, j: (i, 0)),
                  pl.BlockSpec((tm, kr), lambda i, j: (i, 0)),
                  pl.BlockSpec((ka, tn), lambda i, j: (0, j)),
                  pl.BlockSpec((kr, tn), lambda i, j: (0, j)),
                  pl.BlockSpec((tm, tn), lambda i, j: (i, gab + j)),
                  pl.BlockSpec((tm, tn), lambda i, j: (i, gbb + j))],
        out_specs=pl.BlockSpec((tm, tn), lambda i, j: (i, j)),
        out_shape=jax.ShapeDtypeStruct((m, d), jnp.bfloat16),
        compiler_params=_params(("parallel", "parallel")),
        name="mix",
    )(attn2, ret2, wa, wr, proj2, proj2)


def _resid_kernel(a_ref, w_ref, x_ref, gt_ref, o_ref):
    y = jnp.dot(a_ref[...], w_ref[...], preferred_element_type=jnp.float32)
    o_ref[...] = x_ref[...] + gt_ref[0] * y


def _out_resid(mix2, w, x2, mod3, gate_idx, seq):
    m, k = mix2.shape
    d = w.shape[1]
    tm = _div_block(seq, 1024, V7X_SUBLANES)
    tn = _div_block(d, 1024, V7X_LANES)
    per_b = seq // tm
    return pl.pallas_call(
        _resid_kernel,
        grid=(m // tm, d // tn),
        in_specs=[pl.BlockSpec((tm, k), lambda i, j: (i, 0)),
                  pl.BlockSpec((k, tn), lambda i, j: (0, j)),
                  pl.BlockSpec((tm, tn), lambda i, j: (i, j)),
                  pl.BlockSpec((1, 1, tn), lambda i, j: ((i // per_b) * 6 + gate_idx, 0, j))],
        out_specs=pl.BlockSpec((tm, tn), lambda i, j: (i, j)),
        out_shape=jax.ShapeDtypeStruct((m, d), jnp.float32),
        compiler_params=_params(("parallel", "parallel")),
        name="out_resid",
    )(mix2, w, x2, mod3)


def _router_kernel(x_ref, g_ref, sh_ref, sc_ref, wr_ref, br_ref,
                   h_ref, hp_ref, idx_ref, pos_ref, w_ref, cnt_ref, carry_ref):
    i = pl.program_id(0)
    e = N_EXPERTS
    per_g = e // N_GROUPS
    tb = x_ref.shape[0]

    @pl.when(i == 0)
    def _():
        carry_ref[...] = jnp.zeros_like(carry_ref)

    x = x_ref[...]
    ms = jnp.mean(x * x, axis=-1, keepdims=True)
    h = x * lax.rsqrt(ms + EPS) * g_ref[...]
    h = h * (1.0 + sc_ref[0]) + sh_ref[0]
    hb = h.astype(jnp.bfloat16)
    h_ref[...] = hb
    half = h.shape[1] // 2
    _slab_store(hp_ref, _pack_pair(h[:, :half], h[:, half:]), tb, half // V7X_LANES)

    nt = (((1,), (1,)), ((), ()))
    logits = lax.dot_general(wr_ref[...], hb, nt, preferred_element_type=jnp.float32)
    scores = _sigmoid(logits)
    choice = scores + br_ref[...]

    c3 = choice.reshape(N_GROUPS, per_g, tb)
    j_iota = lax.broadcasted_iota(jnp.int32, c3.shape, 1).astype(jnp.float32)
    m1 = c3.max(axis=1, keepdims=True)
    first = jnp.min(jnp.where(c3 == m1, j_iota, float(per_g)), axis=1, keepdims=True)
    m2 = jnp.where(j_iota == first, -jnp.inf, c3).max(axis=1, keepdims=True)
    gs = (m1 + m2).reshape(N_GROUPS, tb)

    g_iota = lax.broadcasted_iota(jnp.int32, gs.shape, 0)
    grank = jnp.zeros(gs.shape, jnp.int32)
    for gp in range(N_GROUPS):
        row = gs[gp:gp + 1, :]
        ahead = (row > gs) | ((row == gs) & (gp < g_iota))
        grank = grank + ahead.astype(jnp.int32)
    gmask = grank < TOPK_GROUPS
    emask = jnp.broadcast_to(gmask.reshape(N_GROUPS, 1, tb), c3.shape).reshape(e, tb)
    masked = jnp.where(emask, choice, -jnp.inf)

    e_iota = lax.broadcasted_iota(jnp.int32, masked.shape, 0)
    erank = jnp.zeros(masked.shape, jnp.int32)
    for ep in range(e):
        row = masked[ep:ep + 1, :]
        ahead = (row > masked) | ((row == masked) & (ep < e_iota))
        erank = erank + ahead.astype(jnp.int32)
    sel = (erank < TOP_K) & emask
    self32 = sel.astype(jnp.float32)

    wsel = scores * self32
    wn = wsel / jnp.sum(wsel, axis=0, keepdims=True) * ROUTED_SCALE

    selb = self32.astype(jnp.bfloat16)
    t_r = lax.broadcasted_iota(jnp.int32, (tb, tb), 0)
    t_c = lax.broadcasted_iota(jnp.int32, (tb, tb), 1)
    upper = (t_r <= t_c).astype(jnp.bfloat16)
    incl = jnp.dot(selb, upper, preferred_element_type=jnp.float32)
    carry = carry_ref[...]
    rank_in_e = carry + incl - 1.0
    carry_new = carry + jnp.sum(self32, axis=1, keepdims=True)
    carry_ref[...] = carry_new
    cnt_ref[...] = jnp.broadcast_to(carry_new, cnt_ref.shape).astype(jnp.int32)

    e_r = lax.broadcasted_iota(jnp.int32, (e, e), 0)
    e_c = lax.broadcasted_iota(jnp.int32, (e, e), 1)
    lower = (e_c < e_r).astype(jnp.bfloat16)
    before = jnp.dot(lower, selb, preferred_element_type=jnp.float32)
    e_f = e_iota.astype(jnp.float32)
    idx_rows, pos_rows, w_rows = [], [], []
    for k in range(TOP_K):
        hit = jnp.where(sel & (before == float(k)), 1.0, 0.0)
        idx_rows.append(jnp.sum(hit * e_f, axis=0, keepdims=True))
        pos_rows.append(jnp.sum(hit * rank_in_e, axis=0, keepdims=True))
        w_rows.append(jnp.sum(hit * wn, axis=0, keepdims=True))
    idx_ref[0] = jnp.concatenate(idx_rows, axis=0).astype(jnp.int32)
    pos_ref[0] = jnp.concatenate(pos_rows, axis=0).astype(jnp.int32)
    w_ref[0] = jnp.concatenate(w_rows, axis=0)


def _router(x1, g, mod3, shift_idx, scale_idx, w_router_t, b_router, seq):
    t, d = x1.shape
    e = N_EXPERTS
    tb = MOE_TILE
    assert seq % tb == 0 and d % (2 * V7X_LANES * V7X_SUBLANES) == 0
    slab = d // 2 // V7X_LANES
    per_b = seq // tb
    nt = t // tb
    return pl.pallas_call(
        _router_kernel,
        grid=(nt,),
        in_specs=[pl.BlockSpec((tb, d), lambda i: (i, 0)),
                  pl.BlockSpec((1, d), lambda i: (0, 0)),
                  pl.BlockSpec((1, 1, d), lambda i: ((i // per_b) * 6 + shift_idx, 0, 0)),
                  pl.BlockSpec((1, 1, d), lambda i: ((i // per_b) * 6 + scale_idx, 0, 0)),
                  pl.BlockSpec((e, d), lambda i: (0, 0)),
                  pl.BlockSpec((e, 1), lambda i: (0, 0))],
        out_specs=[pl.BlockSpec((tb, d), lambda i: (i, 0)),
                   pl.BlockSpec((tb * slab, V7X_LANES), lambda i: (i, 0)),
                   pl.BlockSpec((1, TOP_K, tb), lambda i: (i, 0, 0)),
                   pl.BlockSpec((1, TOP_K, tb), lambda i: (i, 0, 0)),
                   pl.BlockSpec((1, TOP_K, tb), lambda i: (i, 0, 0)),
                   pl.BlockSpec((e, V7X_LANES), lambda i: (0, 0))],
        out_shape=[jax.ShapeDtypeStruct((t, d), jnp.bfloat16),
                   jax.ShapeDtypeStruct((t * slab, V7X_LANES), jnp.int32),
                   jax.ShapeDtypeStruct((nt, TOP_K, tb), jnp.int32),
                   jax.ShapeDtypeStruct((nt, TOP_K, tb), jnp.int32),
                   jax.ShapeDtypeStruct((nt, TOP_K, tb), jnp.float32),
                   jax.ShapeDtypeStruct((e, V7X_LANES), jnp.int32)],
        scratch_shapes=[pltpu.VMEM((e, 1), jnp.float32)],
        compiler_params=_params(("arbitrary",)),
        name="router",
    )(x1, g.reshape(1, d), mod3, mod3, w_router_t, b_router.reshape(e, 1))


def _slots_kernel(pstart_ref, idx_ref, pos_ref, o_ref):
    idx = idx_ref[...]
    base = jnp.zeros(idx.shape, jnp.int32)
    for ex in range(N_EXPERTS):
        base = jnp.where(idx == ex, pstart_ref[ex], base)
    o_ref[...] = base + pos_ref[...]


def _slots(pad_start, idx3, pos3):
    nt = idx3.shape[0]
    per_step = _div_block(nt, 8, 1)
    blk = (per_step,) + idx3.shape[1:]
    spec = pl.BlockSpec(blk, lambda i, ps: (i, 0, 0))
    return pl.pallas_call(
        _slots_kernel,
        grid_spec=pltpu.PrefetchScalarGridSpec(num_scalar_prefetch=1, grid=(nt // per_step,),
                                               in_specs=[spec, spec], out_specs=spec),
        out_shape=jax.ShapeDtypeStruct(idx3.shape, jnp.int32),
        compiler_params=_params(("parallel",)),
        name="slots",
    )(pad_start, idx3, pos3)


def _dispatch_kernel(pstart_ref, pend_ref, hp_hbm, h_ref, wsg_ref, wsu_ref, wsd_ref, slot_hbm,
                     xs_hbm, ysh_ref, slot_s, zero_v, sem_i, sem_z, sem_r, *, slab):
    i = pl.program_id(0)
    tb = h_ref.shape[0]

    def slot_copy():
        return pltpu.make_async_copy(slot_hbm.at[i], slot_s, sem_i)

    slot_copy().start()

    def zero_copy(ex):
        first = pl.multiple_of((pend_ref[ex] - tb) * slab, tb * slab)
        return pltpu.make_async_copy(zero_v, xs_hbm.at[pl.ds(first, tb * slab)], sem_z)

    @pl.when(i == 0)
    def _():
        zero_v[...] = jnp.zeros_like(zero_v)

        def start(ex, carry):
            @pl.when(pend_ref[ex] > pstart_ref[ex])
            def _():
                zero_copy(ex).start()
            return carry

        def wait(ex, carry):
            @pl.when(pend_ref[ex] > pstart_ref[ex])
            def _():
                zero_copy(ex).wait()
            return carry

        lax.fori_loop(0, N_EXPERTS, start, 0)
        lax.fori_loop(0, N_EXPERTS, wait, 0)

    slot_copy().wait()

    def row_copy(t, k):
        return pltpu.make_async_copy(hp_hbm.at[_slab_rows(i * tb + t, slab)],
                                     xs_hbm.at[_slab_rows(slot_s[k, t], slab)], sem_r)

    def start_rows(t, carry):
        for k in range(TOP_K):
            row_copy(t, k).start(priority=k % 2)
        return carry

    lax.fori_loop(0, tb // 2, start_rows, 0, unroll=2)
    h = h_ref[...]
    g = jnp.dot(h, wsg_ref[...], preferred_element_type=jnp.float32)
    u = jnp.dot(h, wsu_ref[...], preferred_element_type=jnp.float32)
    a = (_silu(g) * u).astype(jnp.bfloat16)
    lax.fori_loop(tb // 2, tb, start_rows, 0, unroll=2)
    ysh_ref[...] = jnp.dot(a, wsd_ref[...], preferred_element_type=jnp.float32).astype(ysh_ref.dtype)

    tile_rows = pl.ds(0, tb * slab)
    for k in range(TOP_K):
        pltpu.make_async_copy(hp_hbm.at[tile_rows], xs_hbm.at[tile_rows], sem_r).wait()


def _dispatch(pad_start, pad_end, h2p, h2, wsg, wsu, wsd, slot3, n_rows, slab):
    tb = MOE_TILE
    t, d = h2.shape
    f = wsg.shape[1]
    once = pl.Buffered(1)
    return pl.pallas_call(
        functools.partial(_dispatch_kernel, slab=slab),
        grid_spec=pltpu.PrefetchScalarGridSpec(
            num_scalar_prefetch=2,
            grid=(t // tb,),
            in_specs=[pl.BlockSpec(memory_space=pl.ANY),
                      pl.BlockSpec((tb, d), lambda i, ps, pe: (i, 0)),
                      pl.BlockSpec((d, f), lambda i, ps, pe: (0, 0), pipeline_mode=once),
                      pl.BlockSpec((d, f), lambda i, ps, pe: (0, 0), pipeline_mode=once),
                      pl.BlockSpec((f, d), lambda i, ps, pe: (0, 0), pipeline_mode=once),
                      pl.BlockSpec(memory_space=pl.ANY)],
            out_specs=[pl.BlockSpec(memory_space=pl.ANY),
                       pl.BlockSpec((tb, d), lambda i, ps, pe: (i, 0))],
            scratch_shapes=[pltpu.SMEM((TOP_K, tb), jnp.int32),
                            pltpu.VMEM((tb * slab, V7X_LANES), jnp.int32),
                            pltpu.SemaphoreType.DMA,
                            pltpu.SemaphoreType.DMA,
                            pltpu.SemaphoreType.DMA],
        ),
        out_shape=[jax.ShapeDtypeStruct((n_rows * slab, V7X_LANES), jnp.int32),
                   jax.ShapeDtypeStruct((t, d), jnp.bfloat16)],
        compiler_params=_params(("arbitrary",)),
        name="dispatch",
    )(pad_start, pad_end, h2p, h2, wsg, wsu, wsd, slot3)


def _expert_kernel(blk_e_ref, nused_ref, first_ref, ord_ref, next_e_ref, xs_hbm, wg_hbm, wu_hbm, wd_hbm,
                   ys_hbm, xt, yt, wg_v, wu_v, wd_v, sem_in, sem_out, sem_w, *, slab):
    j = pl.program_id(0)
    n_used = nused_ref[0]
    tb = xt.shape[1]
    lanes = V7X_LANES
    half = slab * lanes

    def w_copies(e, slot):
        return [pltpu.make_async_copy(src.at[e], dst.at[slot], sem_w.at[slot])
                for src, dst in ((wg_hbm, wg_v), (wu_hbm, wu_v), (wd_hbm, wd_v))]

    def in_copies(tile, buf):
        rows = pl.ds(pl.multiple_of(tile * tb, tb), tb)
        return [pltpu.make_async_copy(xs_hbm.at[rows, s], xt.at[buf, :, pl.ds(s * lanes, lanes)],
                                      sem_in.at[buf]) for s in range(slab)]

    def out_copies(tile, buf):
        rows = pl.ds(pl.multiple_of(tile * tb, tb), tb)
        return [pltpu.make_async_copy(yt.at[buf, :, pl.ds(s * lanes, lanes)], ys_hbm.at[rows, s],
                                      sem_out.at[buf]) for s in range(slab)]

    def wait_in(buf):
        pltpu.make_async_copy(yt.at[0], xt.at[buf], sem_in.at[buf]).wait()

    def wait_out(buf):
        pltpu.make_async_copy(xt.at[0], yt.at[buf], sem_out.at[buf]).wait()

    @pl.when(j == 0)
    def _():
        for c in in_copies(0, 0) + w_copies(blk_e_ref[0], 0):
            c.start()

        @pl.when(n_used > 1)
        def _():
            for c in in_copies(1, 1):
                c.start()

    @pl.when(j < n_used)
    def _():
        cur = j % 2
        cur_in = j % N_IN_TILES
        slot = ord_ref[j] % 2

        @pl.when(first_ref[j] == 1)
        def _():
            for c in w_copies(blk_e_ref[j], slot):
                c.wait()

            @pl.when(next_e_ref[j] >= 0)
            def _():
                for c in w_copies(next_e_ref[j], 1 - slot):
                    c.start()

        @pl.when(j + 2 < n_used)
        def _():
            for c in in_copies(j + 2, (j + 2) % N_IN_TILES):
                c.start()

        wait_in(cur_in)

        @pl.when(j >= 2)
        def _():
            wait_out(cur)

        lo, hi = _unpack_pair(xt[cur_in])
        lo = lo.astype(jnp.bfloat16)
        hi = hi.astype(jnp.bfloat16)
        g = (jnp.dot(lo, wg_v[slot, :half, :], preferred_element_type=jnp.float32)
             + jnp.dot(hi, wg_v[slot, half:, :], preferred_element_type=jnp.float32))
        u = (jnp.dot(lo, wu_v[slot, :half, :], preferred_element_type=jnp.float32)
             + jnp.dot(hi, wu_v[slot, half:, :], preferred_element_type=jnp.float32))
        a = (_silu(g) * u).astype(jnp.bfloat16)
        y = jnp.dot(a, wd_v[slot], preferred_element_type=jnp.float32)
        yt[cur] = _pack_pair(y[:, :half], y[:, half:])
        for c in out_copies(j, cur):
            c.start(priority=1)

        @pl.when(j == n_used - 1)
        def _():
            @pl.when(j >= 1)
            def _():
                wait_out(1 - cur)

            wait_out(cur)


def _experts(blk_e, n_used, xs, wg, wu, wd, slab):
    tb = MOE_TILE
    p = xs.shape[0] // slab
    half = slab * V7X_LANES
    d = 2 * half
    f = wg.shape[2]
    nblk = p // tb

    tiles = jnp.arange(nblk, dtype=jnp.int32)
    used = tiles < n_used[0]
    first = used & ((tiles == 0) | (blk_e != jnp.roll(blk_e, 1)))
    ordinal = jnp.cumsum(first.astype(jnp.int32)) - 1
    first_pos = jnp.where(first, tiles, nblk)
    next_first = lax.cummin(jnp.roll(first_pos, -1).at[-1].set(nblk), reverse=True)
    next_e = jnp.where(next_first < nblk, blk_e[jnp.minimum(next_first, nblk - 1)], -1)

    any_spec = pl.BlockSpec(memory_space=pl.ANY)
    ys = pl.pallas_call(
        functools.partial(_expert_kernel, slab=slab),
        grid_spec=pltpu.PrefetchScalarGridSpec(
            num_scalar_prefetch=5,
            grid=(nblk,),
            in_specs=[any_spec, any_spec, any_spec, any_spec],
            out_specs=any_spec,
            scratch_shapes=[pltpu.VMEM((N_IN_TILES, tb, half), jnp.int32),
                            pltpu.VMEM((2, tb, half), jnp.int32),
                            pltpu.VMEM((2, d, f), jnp.bfloat16),
                            pltpu.VMEM((2, d, f), jnp.bfloat16),
                            pltpu.VMEM((2, f, d), jnp.bfloat16),
                            pltpu.SemaphoreType.DMA((N_IN_TILES,)),
                            pltpu.SemaphoreType.DMA((2,)),
                            pltpu.SemaphoreType.DMA((2,))],
        ),
        out_shape=jax.ShapeDtypeStruct((p, slab, V7X_LANES), jnp.int32),
        compiler_params=_params(("arbitrary",)),
        name="experts",
    )(blk_e, n_used, first.astype(jnp.int32), ordinal.astype(jnp.int32), next_e.astype(jnp.int32),
      xs.reshape(p, slab, V7X_LANES), wg, wu, wd)
    return ys.reshape(xs.shape)


def _combine_kernel(x_ref, ysh_ref, wt_ref, gt_ref, gf_ref,
                    slot_hbm, ys_hbm, o_ref, slot_s, rows_v, wrep, acc_lo, acc_hi, sem_i, sem_r):
    i = pl.program_id(0)
    n = pl.num_programs(0)
    tb = x_ref.shape[0]
    half = x_ref.shape[1] // 2
    slab = half // V7X_LANES
    per_tile = MOE_TILE // tb
    cur = i % 2

    def slot_copy(step):
        win = pl.ds((step % per_tile) * tb, tb)
        return pltpu.make_async_copy(slot_hbm.at[step // per_tile, :, win], slot_s.at[step % 2],
                                     sem_i.at[step % 2])

    def request_token(buf, t):
        for k in range(TOP_K):
            pltpu.make_async_copy(ys_hbm.at[_slab_rows(slot_s[buf, k, t], slab)],
                                  rows_v.at[buf, k, _slab_rows(t, slab)],
                                  sem_r.at[buf]).start(priority=k % 2)

    def wait_rows(buf):
        for k in range(TOP_K):
            pltpu.make_async_copy(ys_hbm.at[pl.ds(0, tb * slab)], rows_v.at[buf, k], sem_r.at[buf]).wait()

    @pl.when(i == 0)
    def _():
        slot_copy(0).start()
        slot_copy(0).wait()

        def first(t, carry):
            request_token(0, t)
            return carry

        lax.fori_loop(0, tb, first, 0, unroll=2)

        @pl.when(n > 1)
        def _():
            slot_copy(1).start()
            slot_copy(1).wait()

    @pl.when(i + 2 < n)
    def _():
        slot_copy(i + 2).start()

    wt = wt_ref[...]
    for k in range(TOP_K):
        wrep[k] = jnp.broadcast_to(wt[:, k:k + 1], (tb, V7X_LANES))

    def token(t, carry, buf, request_next):
        rows = _slab_rows(t, slab)
        lo_acc = jnp.zeros((slab, V7X_LANES), jnp.float32)
        hi_acc = jnp.zeros((slab, V7X_LANES), jnp.float32)
        for k in range(TOP_K):
            wv = jnp.broadcast_to(wrep[k, pl.ds(t, 1), :], (slab, V7X_LANES))
            lo, hi = _unpack_pair(rows_v[buf, k, rows, :])
            lo_acc = lo_acc + wv * lo
            hi_acc = hi_acc + wv * hi
        acc_lo[rows, :] = lo_acc
        acc_hi[rows, :] = hi_acc
        if request_next:
            request_token(1 - buf, t)
        return carry

    for buf in range(2):
        for request_next in (True, False):
            @pl.when((cur == buf) & ((i + 1 < n) == request_next))
            def _():
                wait_rows(buf)
                lax.fori_loop(0, tb, functools.partial(token, buf=buf, request_next=request_next),
                              0, unroll=4)

    @pl.when(i + 2 < n)
    def _():
        slot_copy(i + 2).wait()

    yy = ysh_ref[...].astype(jnp.float32) + jnp.concatenate(
        [_slab_load(acc_lo, tb, slab), _slab_load(acc_hi, tb, slab)], axis=1)
    xo = x_ref[...] + gt_ref[0] * yy
    ms = jnp.mean(xo * xo, axis=-1, keepdims=True)
    o_ref[...] = xo * lax.rsqrt(ms + EPS) * gf_ref[...]


def _combine(x1, y_shared, w_t, mod3, gate_idx, g_final, slot3, ys, seq):
    t, d = x1.shape
    tb = COMBINE_TILE
    assert MOE_TILE % tb == 0 and seq % tb == 0
    per_b = seq // tb
    half = d // 2
    return pl.pallas_call(
        _combine_kernel,
        grid=(t // tb,),
        in_specs=[pl.BlockSpec((tb, d), lambda i: (i, 0)),
                  pl.BlockSpec((tb, d), lambda i: (i, 0)),
                  pl.BlockSpec((tb, TOP_K), lambda i: (i, 0)),
                  pl.BlockSpec((1, 1, d), lambda i: ((i // per_b) * 6 + gate_idx, 0, 0)),
                  pl.BlockSpec((1, d), lambda i: (0, 0)),
                  pl.BlockSpec(memory_space=pl.ANY),
                  pl.BlockSpec(memory_space=pl.ANY)],
        out_specs=pl.BlockSpec((tb, d), lambda i: (i, 0)),
        scratch_shapes=[pltpu.SMEM((2, TOP_K, tb), jnp.int32),
                        pltpu.VMEM((2, TOP_K, tb * half // V7X_LANES, V7X_LANES), jnp.int32),
                        pltpu.VMEM((TOP_K, tb, V7X_LANES), jnp.float32),
                        pltpu.VMEM((tb * half // V7X_LANES, V7X_LANES), jnp.float32),
                        pltpu.VMEM((tb * half // V7X_LANES, V7X_LANES), jnp.float32),
                        pltpu.SemaphoreType.DMA((2,)),
                        pltpu.SemaphoreType.DMA((2,))],
        out_shape=jax.ShapeDtypeStruct((t, d), jnp.float32),
        compiler_params=_params(("arbitrary",)),
        name="combine",
    )(x1, y_shared, w_t, mod3, g_final.reshape(1, d), slot3, ys)


def _in_layout(d):
    qw = ATT_HEADS * ATT_HEAD_DIM
    kvw = ATT_KV_HEADS * ATT_HEAD_DIM
    rqk = RET_HEADS * RET_QK_DIM
    rv = RET_HEADS * RET_V_DIM
    order = [("qa", qw), ("ka", kvw), ("va", kvw), ("qr", rqk), ("kr", rqk),
             ("vr", rv), ("gr", rv), ("ga", d), ("gb", d)]
    dst = {}
    off = 0
    for name, width in order:
        dst[name] = off
        off += width
    return dst


def kernel(x, c, w_ada, b_ada, g_norm_mix, w_in, attn_sinks, w_attn_out, w_ret_out, w_o, g_norm_ffn,
           w_router, b_router, w_gate, w_up, w_down, w_sh_gate, w_sh_up, w_sh_down, g_norm_final):
    b, s, d = x.shape
    t = b * s
    depth = w_ada.shape[0]
    bf = jnp.bfloat16
    dst = _in_layout(d)

    c_pad = jnp.zeros((V7X_SUBLANES, d), jnp.float32).at[:b].set(c)
    x2 = x.reshape(t, d)
    for l in range(depth):
        mod = _ada(c_pad, w_ada[l], b_ada[l])
        mod3 = mod[:b].reshape(b * 6, 1, d)

        h = _norm_mod(x2.reshape(b, s, d), g_norm_mix[l], mod3, 0, 1)
        proj, (wg_b, wu_b, wd_b) = _in_proj(h.reshape(t, d), w_in[l].astype(bf),
                                            ((w_gate[l], w_up[l], w_down[l]),), 1024, 768)
        wao_b, wro_b, wo_b = w_attn_out[l].astype(bf), w_ret_out[l].astype(bf), w_o[l].astype(bf)
        wsg_b, wsu_b, wsd_b = w_sh_gate[l].astype(bf), w_sh_up[l].astype(bf), w_sh_down[l].astype(bf)
        proj3 = proj.reshape(b, s, proj.shape[1])
        attn = _attention(proj3, attn_sinks[l], dst["qa"], dst["ka"], dst["va"])
        ret = _retention(proj3, dst["qr"], dst["kr"], dst["vr"], dst["gr"])
        mix = _mix(attn.reshape(t, -1), ret.reshape(t, -1), wao_b, wro_b, proj, dst["ga"], dst["gb"])
        x1 = _out_resid(mix, wo_b, x2, mod3, 2, s)

        h2, h2p, idx3, pos3, w3, cnt = _router(x1, g_norm_ffn[l], mod3, 3, 4,
                                               w_router[l].T.astype(bf), b_router[l], s)
        counts = cnt[:, 0]
        tile = MOE_TILE
        padded = (counts + tile - 1) // tile * tile
        pad_end = jnp.cumsum(padded).astype(jnp.int32)
        pad_start = (pad_end - padded).astype(jnp.int32)
        n_blocks = (t * TOP_K) // tile + N_EXPERTS
        n_used = (pad_end[-1] // tile).reshape(1).astype(jnp.int32)
        blk_first = jnp.arange(n_blocks, dtype=jnp.int32) * tile
        blk_e = jnp.minimum(jnp.sum((pad_end[None, :] <= blk_first[:, None]).astype(jnp.int32), axis=1),
                            N_EXPERTS - 1)
        slab = d // 2 // V7X_LANES
        slot3 = _slots(pad_start, idx3, pos3)
        xs, y_shared = _dispatch(pad_start, pad_end, h2p, h2, wsg_b, wsu_b, wsd_b, slot3,
                                 n_blocks * tile, slab)
        ys = _experts(blk_e, n_used, xs, wg_b, wu_b, wd_b, slab)
        w_t = w3.transpose(0, 2, 1).reshape(t, TOP_K)
        is_last = l == depth - 1
        assert is_last, "the final norm is fused into the last layer's combine"
        x2 = _combine(x1, y_shared, w_t, mod3, 5, g_norm_final, slot3, ys, s)
    return x2.reshape(b, s, d)
```

```python
import functools
import math

import jax
import jax.numpy as jnp
from jax import lax
from jax.experimental import pallas as pl
from jax.experimental.pallas import tpu as pltpu

ATT_HEADS = 32
ATT_KV_HEADS = 4
ATT_HEAD_DIM = 64
WINDOW = 128
ATT_BLOCK = 128
RET_HEADS = 8
RET_QK_DIM = 256
RET_V_DIM = 512
RET_CHUNK = 128
N_EXPERTS = 64
N_GROUPS = 8
TOPK_GROUPS = 4
TOP_K = 8
ROUTED_SCALE = 2.5
EPS = 1e-6

V7X_LANES = 128
V7X_SUBLANES = 8
V7X_VMEM_LIMIT_BYTES = 60000 * 1024

MOE_TILE = 256
COMBINE_TILE = 128
N_IN_TILES = 3
NEG_BIG = -1e30


def _div_block(n, target, align):
    best = None
    b = align
    while b <= min(n, target):
        if n % b == 0:
            best = b
        b += align
    assert best is not None, (n, target, align)
    return best


def _params(semantics):
    return pltpu.CompilerParams(dimension_semantics=semantics,
                                vmem_limit_bytes=V7X_VMEM_LIMIT_BYTES)


def _sigmoid(v):
    return 1.0 / (1.0 + jnp.exp(-v))


def _silu(v):
    return v * _sigmoid(v)


def _pack_pair(lo, hi):
    return pltpu.pack_elementwise([lo, hi], packed_dtype=jnp.bfloat16)


def _unpack_pair(p):
    lo = pltpu.unpack_elementwise(p, index=0, packed_dtype=jnp.bfloat16, unpacked_dtype=jnp.float32)
    hi = pltpu.unpack_elementwise(p, index=1, packed_dtype=jnp.bfloat16, unpacked_dtype=jnp.float32)
    return lo, hi


def _slab_load(ref, n_rows, slab):
    return jnp.concatenate([ref[pl.ds(s, n_rows, stride=slab), :] for s in range(slab)], axis=1)


def _slab_store(ref, val, n_rows, slab):
    for s in range(slab):
        ref[pl.ds(s, n_rows, stride=slab), :] = val[:, s * V7X_LANES:(s + 1) * V7X_LANES]


def _slab_rows(r, slab):
    return pl.ds(pl.multiple_of(r * slab, slab), slab)


def _ada_kernel(c_ref, w_ref, b_ref, o_ref):
    cs = _silu(c_ref[...]).astype(jnp.bfloat16)
    o_ref[...] = jnp.dot(cs, w_ref[...].astype(jnp.bfloat16),
                         preferred_element_type=jnp.float32) + b_ref[...]


def _ada(c_pad, w, b):
    m, d = c_pad.shape
    n = w.shape[1]
    tn = _div_block(n, 1024, V7X_LANES)
    return pl.pallas_call(
        _ada_kernel,
        grid=(n // tn,),
        in_specs=[pl.BlockSpec((m, d), lambda j: (0, 0)),
                  pl.BlockSpec((d, tn), lambda j: (0, j)),
                  pl.BlockSpec((1, tn), lambda j: (0, j))],
        out_specs=pl.BlockSpec((m, tn), lambda j: (0, j)),
        out_shape=jax.ShapeDtypeStruct((m, n), jnp.float32),
        compiler_params=_params(("parallel",)),
        name="ada",
    )(c_pad, w, b.reshape(1, n))


def _norm_mod_kernel(x_ref, g_ref, sh_ref, sc_ref, o_ref):
    x = x_ref[0]
    ms = jnp.mean(x * x, axis=-1, keepdims=True)
    y = x * lax.rsqrt(ms + EPS) * g_ref[...]
    o_ref[0] = (y * (1.0 + sc_ref[0]) + sh_ref[0]).astype(o_ref.dtype)


def _norm_mod(x3, g, mod3, shift_idx, scale_idx):
    b, s, d = x3.shape
    ts = _div_block(s, 512, V7X_SUBLANES)
    return pl.pallas_call(
        _norm_mod_kernel,
        grid=(b, s // ts),
        in_specs=[pl.BlockSpec((1, ts, d), lambda bi, i: (bi, i, 0)),
                  pl.BlockSpec((1, d), lambda bi, i: (0, 0)),
                  pl.BlockSpec((1, 1, d), lambda bi, i: (bi * 6 + shift_idx, 0, 0)),
                  pl.BlockSpec((1, 1, d), lambda bi, i: (bi * 6 + scale_idx, 0, 0))],
        out_specs=pl.BlockSpec((1, ts, d), lambda bi, i: (bi, i, 0)),
        out_shape=jax.ShapeDtypeStruct((b, s, d), jnp.bfloat16),
        compiler_params=_params(("parallel", "parallel")),
        name="norm_mod",
    )(x3, g.reshape(1, d), mod3, mod3)


def _in_proj_kernel(a_ref, b_ref, *rest, windows, n_j):
    n_side = len(windows)
    src = rest[:n_side]
    o_ref = rest[n_side]
    dst = rest[n_side + 1:]
    o_ref[...] = jnp.dot(a_ref[...], b_ref[...],
                         preferred_element_type=jnp.float32).astype(o_ref.dtype)
    step = pl.program_id(0) * n_j + pl.program_id(1)

    for s_ref, d_ref, (first, count) in zip(src, dst, windows):
        @pl.when((step >= first) & (step < first + count))
        def _():
            d_ref[...] = s_ref[...].astype(d_ref.dtype)


def _in_proj(a, b, side_groups, tm_target, tn_target):
    m, k = a.shape
    n = b.shape[1]
    tm = _div_block(m, tm_target, V7X_SUBLANES)
    tn = _div_block(n, tn_target, V7X_LANES)
    n_i, n_j = m // tm, n // tn
    steps_left = n_i * n_j
    first = 0
    side, chunked, windows = [], [], []
    for group in side_groups:
        n_max = 1 << (steps_left.bit_length() - 1)
        used = 0
        for w in group:
            rows = w.size // w.shape[-1]
            count = n_max
            while rows % (count * 2 * V7X_SUBLANES):
                count //= 2
            side.append(w)
            chunked.append(w.reshape(count, rows // count, w.shape[-1]))
            windows.append((first, count))
            used = max(used, count)
        first += used
        steps_left -= used

    def side_spec(c, window):
        w_first, count = window
        return pl.BlockSpec((1,) + c.shape[1:],
                            lambda i, j: (jnp.clip(i * n_j + j - w_first, 0, count - 1), 0, 0))

    side_specs = [side_spec(c, w) for c, w in zip(chunked, windows)]
    outs = pl.pallas_call(
        functools.partial(_in_proj_kernel, windows=tuple(windows), n_j=n_j),
        grid=(n_i, n_j),
        in_specs=[pl.BlockSpec((tm, k), lambda i, j: (i, 0)),
                  pl.BlockSpec((k, tn), lambda i, j: (0, j))] + side_specs,
        out_specs=[pl.BlockSpec((tm, tn), lambda i, j: (i, j))] + side_specs,
        out_shape=[jax.ShapeDtypeStruct((m, n), jnp.bfloat16)]
        + [jax.ShapeDtypeStruct(c.shape, jnp.bfloat16) for c in chunked],
        compiler_params=_params(("arbitrary", "arbitrary")),
        name="in_proj",
    )(a, b, *chunked)
    return outs[0], [o.reshape(w.shape) for o, w in zip(outs[1:], side)]


def _attn_kernel(sink_ref, q_ref, kc_ref, kp_ref, vc_ref, vp_ref, bias_ref, o_ref):
    i = pl.program_id(1)
    blk = ATT_BLOCK
    hd = ATT_HEAD_DIM
    group = ATT_HEADS // ATT_KV_HEADS
    pairs = group // 2
    nt = (((1,), (1,)), ((), ()))
    tn = (((0,), (0,)), ((), ()))
    zpad = jnp.zeros((2 * blk, hd), jnp.bfloat16)

    def scores(h, par):
        sl = slice(h * hd, (h + 1) * hd)
        k2 = jnp.concatenate([kp_ref[0, :, sl], kc_ref[0, :, sl]], axis=0) * (hd ** -0.5)
        qp = jnp.concatenate([q_ref[0, :, (h * pairs + p) * 2 * hd:(h * pairs + p + 1) * 2 * hd]
                              for p in range(pairs)], axis=0)
        kz = jnp.concatenate([k2, zpad] if par == 0 else [zpad, k2], axis=1)
        return lax.dot_general(kz, qp, nt, preferred_element_type=jnp.float32)

    def softmax(s, h, par):
        s = s + bias_ref[h * 2 + par]
        s = jnp.concatenate([jnp.where(i == 0, NEG_BIG, s[:blk]), s[blk:]], axis=0)
        sink = jnp.concatenate([jnp.full((1, blk), sink_ref[h * group + 2 * p + par], jnp.float32)
                                for p in range(pairs)], axis=1)
        m = jnp.maximum(s.max(0, keepdims=True), sink)
        pr = jnp.exp(s - m)
        denom = pr.sum(0, keepdims=True) + jnp.exp(sink - m)
        return pr.astype(jnp.bfloat16), 1.0 / denom

    def values(pr, inv, h, par):
        sl = slice(h * hd, (h + 1) * hd)
        v2 = jnp.concatenate([vp_ref[0, :, sl], vc_ref[0, :, sl]], axis=0)
        vz = jnp.concatenate([v2, zpad] if par == 0 else [zpad, v2], axis=1)
        return lax.dot_general(vz, pr, tn, preferred_element_type=jnp.float32) * inv

    items = [(h, par) for h in range(ATT_KV_HEADS) for par in range(2)]
    s_of, p_of, acc = {}, {}, {}
    for n in range(len(items) + 2):
        if n < len(items):
            s_of[n] = scores(*items[n])
        if 1 <= n <= len(items):
            p_of[n - 1] = softmax(s_of.pop(n - 1), *items[n - 1])
        if n >= 2:
            h, par = items[n - 2]
            o = values(*p_of.pop(n - 2), h, par)
            acc[h] = o if par == 0 else acc[h] + o
            if par == 1:
                out = acc.pop(h)
                for p in range(pairs):
                    o_ref[0, :, (h * pairs + p) * 2 * hd:(h * pairs + p + 1) * 2 * hd] = (
                        out[:, p * blk:(p + 1) * blk].T.astype(o_ref.dtype))


def _attn_bias():
    blk = ATT_BLOCK
    group = ATT_HEADS // ATT_KV_HEADS
    pairs = group // 2
    qi = jnp.arange(blk)[:, None]
    kj = jnp.arange(2 * blk)[None, :]
    dist = qi + blk - kj
    valid = (dist >= 0) & (dist < WINDOW)
    slopes = jnp.exp2(-8.0 * jnp.arange(1, ATT_HEADS + 1, dtype=jnp.float32) / ATT_HEADS)
    slopes = slopes.reshape(ATT_KV_HEADS, pairs, 2)
    bias = jnp.where(valid, -slopes[..., None, None] * dist.astype(jnp.float32), NEG_BIG)
    return bias.transpose(0, 2, 4, 1, 3).reshape(ATT_KV_HEADS * 2, 2 * blk, pairs * blk)


def _attention(proj3, sinks, q_off, k_off, v_off):
    b, s, _ = proj3.shape
    qw = ATT_HEADS * ATT_HEAD_DIM
    kvw = ATT_KV_HEADS * ATT_HEAD_DIM
    nb = s // ATT_BLOCK
    group = ATT_HEADS // ATT_KV_HEADS
    assert q_off % qw == 0 and k_off % kvw == 0 and v_off % kvw == 0
    assert group % 2 == 0 and 2 * ATT_HEAD_DIM == V7X_LANES and WINDOW == ATT_BLOCK
    assert 4 ** round(math.log(ATT_HEAD_DIM, 4)) == ATT_HEAD_DIM, "score scale must be a power of two"
    qb, kb, vb = q_off // qw, k_off // kvw, v_off // kvw
    bias = _attn_bias()
    return pl.pallas_call(
        _attn_kernel,
        grid_spec=pltpu.PrefetchScalarGridSpec(
            num_scalar_prefetch=1,
            grid=(b, nb),
            in_specs=[pl.BlockSpec((1, ATT_BLOCK, qw), lambda bi, i, sk: (bi, i, qb)),
                      pl.BlockSpec((1, ATT_BLOCK, kvw), lambda bi, i, sk: (bi, i, kb)),
                      pl.BlockSpec((1, ATT_BLOCK, kvw), lambda bi, i, sk: (bi, jnp.maximum(i - 1, 0), kb)),
                      pl.BlockSpec((1, ATT_BLOCK, kvw), lambda bi, i, sk: (bi, i, vb)),
                      pl.BlockSpec((1, ATT_BLOCK, kvw), lambda bi, i, sk: (bi, jnp.maximum(i - 1, 0), vb)),
                      pl.BlockSpec(bias.shape, lambda bi, i, sk: (0, 0, 0), pipeline_mode=pl.Buffered(1))],
            out_specs=pl.BlockSpec((1, ATT_BLOCK, qw), lambda bi, i, sk: (bi, i, 0)),
        ),
        out_shape=jax.ShapeDtypeStruct((b, s, qw), jnp.bfloat16),
        compiler_params=_params(("parallel", "parallel")),
        name="attention",
    )(sinks, proj3, proj3, proj3, proj3, proj3, bias)


def _ret_kernel(q_ref, k_ref, v_ref, gr_ref, mask_ref, qd_ref, kd_ref, cd_ref, o_ref, state_ref):
    c = pl.program_id(1)

    @pl.when(c == 0)
    def _():
        state_ref[...] = jnp.zeros_like(state_ref)

    nt = (((1,), (1,)), ((), ()))
    tn = (((0,), (0,)), ((), ()))
    def decayed(bi):
        q = q_ref[bi]
        k = k_ref[bi]
        attn = lax.dot_general(q, k, nt, preferred_element_type=jnp.float32) * mask_ref[0]
        kd = (k.astype(jnp.float32) * kd_ref[0]).astype(k.dtype)
        return attn.astype(jnp.bfloat16), kd

    def recur(bi, attn, kd):
        q = q_ref[bi]
        v = v_ref[bi]
        intra = jnp.dot(attn, v, preferred_element_type=jnp.float32)
        state = state_ref[bi]
        inter = jnp.dot(q, state.astype(q.dtype), preferred_element_type=jnp.float32) * qd_ref[0]
        state_ref[bi] = state * cd_ref[0] + lax.dot_general(kd, v, tn, preferred_element_type=jnp.float32)
        return intra + inter

    def finish(bi, o):
        mu = jnp.mean(o, axis=-1, keepdims=True)
        oc = o - mu
        var = jnp.mean(oc * oc, axis=-1, keepdims=True)
        y = oc * lax.rsqrt(var + EPS)
        o_ref[bi] = (_silu(gr_ref[bi].astype(jnp.float32)) * y).astype(o_ref.dtype)

    nb = q_ref.shape[0]
    a_of, o_of = {}, {}
    for n in range(nb + 2):
        if n < nb:
            a_of[n] = decayed(n)
        if 1 <= n <= nb:
            o_of[n - 1] = recur(n - 1, *a_of.pop(n - 1))
        if n >= 2:
            finish(n - 2, o_of.pop(n - 2))


def _retention(proj3, q_off, k_off, v_off, g_off):
    b, s, _ = proj3.shape
    dk, dv, ch = RET_QK_DIM, RET_V_DIM, RET_CHUNK
    assert q_off % dk == 0 and k_off % dk == 0 and v_off % dv == 0 and g_off % dv == 0
    qb, kb, vb, gb = q_off // dk, k_off // dk, v_off // dv, g_off // dv
    n = s // ch
    log_g = jnp.log1p(-jnp.exp2(-5.0 - jnp.arange(RET_HEADS, dtype=jnp.float32)))
    pos = jnp.arange(ch, dtype=jnp.float32)
    rel = pos[:, None] - pos[None, :]
    scale = dk ** -0.5
    mask = jnp.where(rel[None] >= 0, jnp.exp(rel[None] * log_g[:, None, None]), 0.0) * scale
    q_decay = jnp.exp((pos[None, :, None] + 1.0) * log_g[:, None, None])
    k_decay = jnp.exp((ch - 1.0 - pos[None, :, None]) * log_g[:, None, None]) * scale
    c_decay = jnp.exp(ch * log_g)[:, None, None]
    return pl.pallas_call(
        _ret_kernel,
        grid=(RET_HEADS, n),
        in_specs=[pl.BlockSpec((b, ch, dk), lambda h, c: (0, c, qb + h)),
                  pl.BlockSpec((b, ch, dk), lambda h, c: (0, c, kb + h)),
                  pl.BlockSpec((b, ch, dv), lambda h, c: (0, c, vb + h)),
                  pl.BlockSpec((b, ch, dv), lambda h, c: (0, c, gb + h)),
                  pl.BlockSpec((1, ch, ch), lambda h, c: (h, 0, 0)),
                  pl.BlockSpec((1, ch, 1), lambda h, c: (h, 0, 0)),
                  pl.BlockSpec((1, ch, 1), lambda h, c: (h, 0, 0)),
                  pl.BlockSpec((1, 1, 1), lambda h, c: (h, 0, 0))],
        out_specs=pl.BlockSpec((b, ch, dv), lambda h, c: (0, c, h)),
        out_shape=jax.ShapeDtypeStruct((b, s, RET_HEADS * dv), jnp.bfloat16),
        scratch_shapes=[pltpu.VMEM((b, dk, dv), jnp.float32)],
        compiler_params=_params(("parallel", "arbitrary")),
        name="retention",
    )(proj3, proj3, proj3, proj3, mask, q_decay, k_decay, c_decay)


def _mix_kernel(a_ref, r_ref, wa_ref, wr_ref, ga_ref, gb_ref, o_ref):
    ya = jnp.dot(a_ref[...], wa_ref[...], preferred_element_type=jnp.float32)
    yr = jnp.dot(r_ref[...], wr_ref[...], preferred_element_type=jnp.float32)
    ga = _sigmoid(ga_ref[...].astype(jnp.float32))
    gb = _sigmoid(gb_ref[...].astype(jnp.float32))
    o_ref[...] = (ga * ya + gb * yr).astype(o_ref.dtype)


def _mix(attn2, ret2, wa, wr, proj2, ga_off, gb_off):
    m, ka = attn2.shape
    kr = ret2.shape[1]
    d = wa.shape[1]
    tm = _div_block(m, 1024, V7X_SUBLANES)
    tn = _div_block(d, 512, V7X_LANES)
    assert ga_off % tn == 0 and gb_off % tn == 0
    gab, gbb = ga_off // tn, gb_off // tn
    return pl.pallas_call(
        _mix_kernel,
        grid=(m // tm, d // tn),
        in_specs=[pl.BlockSpec((tm, ka), lambda i, j: (i, 0)),
                  pl.BlockSpec((tm, kr), lambda i, j: (i, 0)),
                  pl.BlockSpec((ka, tn), lambda i, j: (0, j)),
                  pl.BlockSpec((kr, tn), lambda i, j: (0, j)),
                  pl.BlockSpec((tm, tn), lambda i, j: (i, gab + j)),
                  pl.BlockSpec((tm, tn), lambda i, j: (i, gbb + j))],
        out_specs=pl.BlockSpec((tm, tn), lambda i, j: (i, j)),
        out_shape=jax.ShapeDtypeStruct((m, d), jnp.bfloat16),
        compiler_params=_params(("parallel", "parallel")),
        name="mix",
    )(attn2, ret2, wa, wr, proj2, proj2)


def _resid_kernel(a_ref, w_ref, x_ref, gt_ref, o_ref):
    y = jnp.dot(a_ref[...], w_ref[...], preferred_element_type=jnp.float32)
    o_ref[...] = x_ref[...] + gt_ref[0] * y


def _out_resid(mix2, w, x2, mod3, gate_idx, seq):
    m, k = mix2.shape
    d = w.shape[1]
    tm = _div_block(seq, 1024, V7X_SUBLANES)
    tn = _div_block(d, 1024, V7X_LANES)
    per_b = seq // tm
    return pl.pallas_call(
        _resid_kernel,
        grid=(m // tm, d // tn),
        in_specs=[pl.BlockSpec((tm, k), lambda i, j: (i, 0)),
                  pl.BlockSpec((k, tn), lambda i, j: (0, j)),
                  pl.BlockSpec((tm, tn), lambda i, j: (i, j)),
                  pl.BlockSpec((1, 1, tn), lambda i, j: ((i // per_b) * 6 + gate_idx, 0, j))],
        out_specs=pl.BlockSpec((tm, tn), lambda i, j: (i, j)),
        out_shape=jax.ShapeDtypeStruct((m, d), jnp.float32),
        compiler_params=_params(("parallel", "parallel")),
        name="out_resid",
    )(mix2, w, x2, mod3)


def _router_kernel(x_ref, g_ref, sh_ref, sc_ref, wr_ref, br_ref,
                   h_ref, hp_ref, idx_ref, pos_ref, w_ref, cnt_ref, carry_ref):
    i = pl.program_id(0)
    e = N_EXPERTS
    per_g = e // N_GROUPS
    tb = x_ref.shape[0]

    @pl.when(i == 0)
    def _():
        carry_ref[...] = jnp.zeros_like(carry_ref)

    x = x_ref[...]
    ms = jnp.mean(x * x, axis=-1, keepdims=True)
    h = x * lax.rsqrt(ms + EPS) * g_ref[...]
    h = h * (1.0 + sc_ref[0]) + sh_ref[0]
    hb = h.astype(jnp.bfloat16)
    h_ref[...] = hb
    half = h.shape[1] // 2
    _slab_store(hp_ref, _pack_pair(h[:, :half], h[:, half:]), tb, half // V7X_LANES)

    nt = (((1,), (1,)), ((), ()))
    logits = lax.dot_general(wr_ref[...], hb, nt, preferred_element_type=jnp.float32)
    scores = _sigmoid(logits)
    choice = scores + br_ref[...]

    c3 = choice.reshape(N_GROUPS, per_g, tb)
    j_iota = lax.broadcasted_iota(jnp.int32, c3.shape, 1).astype(jnp.float32)
    m1 = c3.max(axis=1, keepdims=True)
    first = jnp.min(jnp.where(c3 == m1, j_iota, float(per_g)), axis=1, keepdims=True)
    m2 = jnp.where(j_iota == first, -jnp.inf, c3).max(axis=1, keepdims=True)
    gs = (m1 + m2).reshape(N_GROUPS, tb)

    g_iota = lax.broadcasted_iota(jnp.int32, gs.shape, 0)
    grank = jnp.zeros(gs.shape, jnp.int32)
    for gp in range(N_GROUPS):
        row = gs[gp:gp + 1, :]
        ahead = (row > gs) | ((row == gs) & (gp < g_iota))
        grank = grank + ahead.astype(jnp.int32)
    gmask = grank < TOPK_GROUPS
    emask = jnp.broadcast_to(gmask.reshape(N_GROUPS, 1, tb), c3.shape).reshape(e, tb)
    masked = jnp.where(emask, choice, -jnp.inf)

    e_iota = lax.broadcasted_iota(jnp.int32, masked.shape, 0)
    erank = jnp.zeros(masked.shape, jnp.int32)
    for ep in range(e):
        row = masked[ep:ep + 1, :]
        ahead = (row > masked) | ((row == masked) & (ep < e_iota))
        erank = erank + ahead.astype(jnp.int32)
    sel = (erank < TOP_K) & emask
    self32 = sel.astype(jnp.float32)

    wsel = scores * self32
    wn = wsel / jnp.sum(wsel, axis=0, keepdims=True) * ROUTED_SCALE

    selb = self32.astype(jnp.bfloat16)
    t_r = lax.broadcasted_iota(jnp.int32, (tb, tb), 0)
    t_c = lax.broadcasted_iota(jnp.int32, (tb, tb), 1)
    upper = (t_r <= t_c).astype(jnp.bfloat16)
    incl = jnp.dot(selb, upper, preferred_element_type=jnp.float32)
    carry = carry_ref[...]
    rank_in_e = carry + incl - 1.0
    carry_new = carry + jnp.sum(self32, axis=1, keepdims=True)
    carry_ref[...] = carry_new
    cnt_ref[...] = jnp.broadcast_to(carry_new, cnt_ref.shape).astype(jnp.int32)

    e_r = lax.broadcasted_iota(jnp.int32, (e, e), 0)
    e_c = lax.broadcasted_iota(jnp.int32, (e, e), 1)
    lower = (e_c < e_r).astype(jnp.bfloat16)
    before = jnp.dot(lower, selb, preferred_element_type=jnp.float32)
    e_f = e_iota.astype(jnp.float32)
    idx_rows, pos_rows, w_rows = [], [], []
    for k in range(TOP_K):
        hit = jnp.where(sel & (before == float(k)), 1.0, 0.0)
        idx_rows.append(jnp.sum(hit * e_f, axis=0, keepdims=True))
        pos_rows.append(jnp.sum(hit * rank_in_e, axis=0, keepdims=True))
        w_rows.append(jnp.sum(hit * wn, axis=0, keepdims=True))
    idx_ref[0] = jnp.concatenate(idx_rows, axis=0).astype(jnp.int32)
    pos_ref[0] = jnp.concatenate(pos_rows, axis=0).astype(jnp.int32)
    w_ref[0] = jnp.concatenate(w_rows, axis=0)


def _router(x1, g, mod3, shift_idx, scale_idx, w_router_t, b_router, seq):
    t, d = x1.shape
    e = N_EXPERTS
    tb = MOE_TILE
    assert seq % tb == 0 and d % (2 * V7X_LANES * V7X_SUBLANES) == 0
    slab = d // 2 // V7X_LANES
    per_b = seq // tb
    nt = t // tb
    return pl.pallas_call(
        _router_kernel,
        grid=(nt,),
        in_specs=[pl.BlockSpec((tb, d), lambda i: (i, 0)),
                  pl.BlockSpec((1, d), lambda i: (0, 0)),
                  pl.BlockSpec((1, 1, d), lambda i: ((i // per_b) * 6 + shift_idx, 0, 0)),
                  pl.BlockSpec((1, 1, d), lambda i: ((i // per_b) * 6 + scale_idx, 0, 0)),
                  pl.BlockSpec((e, d), lambda i: (0, 0)),
                  pl.BlockSpec((e, 1), lambda i: (0, 0))],
        out_specs=[pl.BlockSpec((tb, d), lambda i: (i, 0)),
                   pl.BlockSpec((tb * slab, V7X_LANES), lambda i: (i, 0)),
                   pl.BlockSpec((1, TOP_K, tb), lambda i: (i, 0, 0)),
                   pl.BlockSpec((1, TOP_K, tb), lambda i: (i, 0, 0)),
                   pl.BlockSpec((1, TOP_K, tb), lambda i: (i, 0, 0)),
                   pl.BlockSpec((e, V7X_LANES), lambda i: (0, 0))],
        out_shape=[jax.ShapeDtypeStruct((t, d), jnp.bfloat16),
                   jax.ShapeDtypeStruct((t * slab, V7X_LANES), jnp.int32),
                   jax.ShapeDtypeStruct((nt, TOP_K, tb), jnp.int32),
                   jax.ShapeDtypeStruct((nt, TOP_K, tb), jnp.int32),
                   jax.ShapeDtypeStruct((nt, TOP_K, tb), jnp.float32),
                   jax.ShapeDtypeStruct((e, V7X_LANES), jnp.int32)],
        scratch_shapes=[pltpu.VMEM((e, 1), jnp.float32)],
        compiler_params=_params(("arbitrary",)),
        name="router",
    )(x1, g.reshape(1, d), mod3, mod3, w_router_t, b_router.reshape(e, 1))


def _slots_kernel(pstart_ref, idx_ref, pos_ref, o_ref):
    idx = idx_ref[...]
    base = jnp.zeros(idx.shape, jnp.int32)
    for ex in range(N_EXPERTS):
        base = jnp.where(idx == ex, pstart_ref[ex], base)
    o_ref[...] = base + pos_ref[...]


def _slots(pad_start, idx3, pos3):
    nt = idx3.shape[0]
    per_step = _div_block(nt, 8, 1)
    blk = (per_step,) + idx3.shape[1:]
    spec = pl.BlockSpec(blk, lambda i, ps: (i, 0, 0))
    return pl.pallas_call(
        _slots_kernel,
        grid_spec=pltpu.PrefetchScalarGridSpec(num_scalar_prefetch=1, grid=(nt // per_step,),
                                               in_specs=[spec, spec], out_specs=spec),
        out_shape=jax.ShapeDtypeStruct(idx3.shape, jnp.int32),
        compiler_params=_params(("parallel",)),
        name="slots",
    )(pad_start, idx3, pos3)


def _dispatch_kernel(pstart_ref, pend_ref, hp_ref, h_ref, wsg_ref, wsu_ref, wsd_ref, slot_hbm,
                     xs_hbm, ysh_ref, slot_s, zero_v, sem_i, sem_z, sem_r, *, slab):
    i = pl.program_id(0)
    tb = hp_ref.shape[0] // slab

    def slot_copy():
        return pltpu.make_async_copy(slot_hbm.at[i], slot_s, sem_i)

    slot_copy().start()

    def zero_copy(ex):
        first = pl.multiple_of((pend_ref[ex] - tb) * slab, tb * slab)
        return pltpu.make_async_copy(zero_v, xs_hbm.at[pl.ds(first, tb * slab)], sem_z)

    @pl.when(i == 0)
    def _():
        zero_v[...] = jnp.zeros_like(zero_v)

        def start(ex, carry):
            @pl.when(pend_ref[ex] > pstart_ref[ex])
            def _():
                zero_copy(ex).start()
            return carry

        def wait(ex, carry):
            @pl.when(pend_ref[ex] > pstart_ref[ex])
            def _():
                zero_copy(ex).wait()
            return carry

        lax.fori_loop(0, N_EXPERTS, start, 0)
        lax.fori_loop(0, N_EXPERTS, wait, 0)

    slot_copy().wait()

    def row_copy(t, k):
        return pltpu.make_async_copy(hp_ref.at[_slab_rows(t, slab)],
                                     xs_hbm.at[_slab_rows(slot_s[k, t], slab)], sem_r)

    def start_rows(t, carry):
        for k in range(TOP_K):
            row_copy(t, k).start(priority=k % 2)
        return carry

    lax.fori_loop(0, tb // 2, start_rows, 0, unroll=2)
    h = h_ref[...]
    g = jnp.dot(h, wsg_ref[...], preferred_element_type=jnp.float32)
    u = jnp.dot(h, wsu_ref[...], preferred_element_type=jnp.float32)
    a = (_silu(g) * u).astype(jnp.bfloat16)
    lax.fori_loop(tb // 2, tb, start_rows, 0, unroll=2)
    ysh_ref[...] = jnp.dot(a, wsd_ref[...], preferred_element_type=jnp.float32).astype(ysh_ref.dtype)

    for k in range(TOP_K):
        pltpu.make_async_copy(hp_ref, xs_hbm.at[pl.ds(0, tb * slab)], sem_r).wait()


def _dispatch(pad_start, pad_end, h2p, h2, wsg, wsu, wsd, slot3, n_rows, slab):
    tb = MOE_TILE
    t, d = h2.shape
    f = wsg.shape[1]
    once = pl.Buffered(1)
    return pl.pallas_call(
        functools.partial(_dispatch_kernel, slab=slab),
        grid_spec=pltpu.PrefetchScalarGridSpec(
            num_scalar_prefetch=2,
            grid=(t // tb,),
            in_specs=[pl.BlockSpec((tb * slab, V7X_LANES), lambda i, ps, pe: (i, 0)),
                      pl.BlockSpec((tb, d), lambda i, ps, pe: (i, 0)),
                      pl.BlockSpec((d, f), lambda i, ps, pe: (0, 0), pipeline_mode=once),
                      pl.BlockSpec((d, f), lambda i, ps, pe: (0, 0), pipeline_mode=once),
                      pl.BlockSpec((f, d), lambda i, ps, pe: (0, 0), pipeline_mode=once),
                      pl.BlockSpec(memory_space=pl.ANY)],
            out_specs=[pl.BlockSpec(memory_space=pl.ANY),
                       pl.BlockSpec((tb, d), lambda i, ps, pe: (i, 0))],
            scratch_shapes=[pltpu.SMEM((TOP_K, tb), jnp.int32),
                            pltpu.VMEM((tb * slab, V7X_LANES), jnp.int32),
                            pltpu.SemaphoreType.DMA,
                            pltpu.SemaphoreType.DMA,
                            pltpu.SemaphoreType.DMA],
        ),
        out_shape=[jax.ShapeDtypeStruct((n_rows * slab, V7X_LANES), jnp.int32),
                   jax.ShapeDtypeStruct((t, d), jnp.bfloat16)],
        compiler_params=_params(("arbitrary",)),
        name="dispatch",
    )(pad_start, pad_end, h2p, h2, wsg, wsu, wsd, slot3)


def _expert_kernel(blk_e_ref, nused_ref, first_ref, ord_ref, next_e_ref, xs_hbm, wg_hbm, wu_hbm, wd_hbm,
                   ys_hbm, xt, yt, wg_v, wu_v, wd_v, sem_in, sem_out, sem_w, *, slab):
    j = pl.program_id(0)
    n_used = nused_ref[0]
    tb = xt.shape[1]
    lanes = V7X_LANES
    half = slab * lanes

    def w_copies(e, slot):
        return [pltpu.make_async_copy(src.at[e], dst.at[slot], sem_w.at[slot])
                for src, dst in ((wg_hbm, wg_v), (wu_hbm, wu_v), (wd_hbm, wd_v))]

    def in_copies(tile, buf):
        rows = pl.ds(pl.multiple_of(tile * tb, tb), tb)
        return [pltpu.make_async_copy(xs_hbm.at[rows, s], xt.at[buf, :, pl.ds(s * lanes, lanes)],
                                      sem_in.at[buf]) for s in range(slab)]

    def out_copies(tile, buf):
        rows = pl.ds(pl.multiple_of(tile * tb, tb), tb)
        return [pltpu.make_async_copy(yt.at[buf, :, pl.ds(s * lanes, lanes)], ys_hbm.at[rows, s],
                                      sem_out.at[buf]) for s in range(slab)]

    def wait_in(buf):
        pltpu.make_async_copy(yt.at[0], xt.at[buf], sem_in.at[buf]).wait()

    def wait_out(buf):
        pltpu.make_async_copy(xt.at[0], yt.at[buf], sem_out.at[buf]).wait()

    @pl.when(j == 0)
    def _():
        for c in in_copies(0, 0) + w_copies(blk_e_ref[0], 0):
            c.start()

        @pl.when(n_used > 1)
        def _():
            for c in in_copies(1, 1):
                c.start()

    @pl.when(j < n_used)
    def _():
        cur = j % 2
        cur_in = j % N_IN_TILES
        slot = ord_ref[j] % 2

        @pl.when(first_ref[j] == 1)
        def _():
            for c in w_copies(blk_e_ref[j], slot):
                c.wait()

            @pl.when(next_e_ref[j] >= 0)
            def _():
                for c in w_copies(next_e_ref[j], 1 - slot):
                    c.start()

        @pl.when(j + 2 < n_used)
        def _():
            for c in in_copies(j + 2, (j + 2) % N_IN_TILES):
                c.start()

        wait_in(cur_in)

        @pl.when(j >= 2)
        def _():
            wait_out(cur)

        lo, hi = _unpack_pair(xt[cur_in])
        lo = lo.astype(jnp.bfloat16)
        hi = hi.astype(jnp.bfloat16)
        g = (jnp.dot(lo, wg_v[slot, :half, :], preferred_element_type=jnp.float32)
             + jnp.dot(hi, wg_v[slot, half:, :], preferred_element_type=jnp.float32))
        u = (jnp.dot(lo, wu_v[slot, :half, :], preferred_element_type=jnp.float32)
             + jnp.dot(hi, wu_v[slot, half:, :], preferred_element_type=jnp.float32))
        a = (_silu(g) * u).astype(jnp.bfloat16)
        y = jnp.dot(a, wd_v[slot], preferred_element_type=jnp.float32)
        yt[cur] = _pack_pair(y[:, :half], y[:, half:])
        for c in out_copies(j, cur):
            c.start(priority=1)

        @pl.when(j == n_used - 1)
        def _():
            @pl.when(j >= 1)
            def _():
                wait_out(1 - cur)

            wait_out(cur)


def _experts(blk_e, n_used, xs, wg, wu, wd, slab):
    tb = MOE_TILE
    p = xs.shape[0] // slab
    half = slab * V7X_LANES
    d = 2 * half
    f = wg.shape[2]
    nblk = p // tb

    tiles = jnp.arange(nblk, dtype=jnp.int32)
    used = tiles < n_used[0]
    first = used & ((tiles == 0) | (blk_e != jnp.roll(blk_e, 1)))
    ordinal = jnp.cumsum(first.astype(jnp.int32)) - 1
    first_pos = jnp.where(first, tiles, nblk)
    next_first = lax.cummin(jnp.roll(first_pos, -1).at[-1].set(nblk), reverse=True)
    next_e = jnp.where(next_first < nblk, blk_e[jnp.minimum(next_first, nblk - 1)], -1)

    any_spec = pl.BlockSpec(memory_space=pl.ANY)
    ys = pl.pallas_call(
        functools.partial(_expert_kernel, slab=slab),
        grid_spec=pltpu.PrefetchScalarGridSpec(
            num_scalar_prefetch=5,
            grid=(nblk,),
            in_specs=[any_spec, any_spec, any_spec, any_spec],
            out_specs=any_spec,
            scratch_shapes=[pltpu.VMEM((N_IN_TILES, tb, half), jnp.int32),
                            pltpu.VMEM((2, tb, half), jnp.int32),
                            pltpu.VMEM((2, d, f), jnp.bfloat16),
                            pltpu.VMEM((2, d, f), jnp.bfloat16),
                            pltpu.VMEM((2, f, d), jnp.bfloat16),
                            pltpu.SemaphoreType.DMA((N_IN_TILES,)),
                            pltpu.SemaphoreType.DMA((2,)),
                            pltpu.SemaphoreType.DMA((2,))],
        ),
        out_shape=jax.ShapeDtypeStruct((p, slab, V7X_LANES), jnp.int32),
        compiler_params=_params(("arbitrary",)),
        name="experts",
    )(blk_e, n_used, first.astype(jnp.int32), ordinal.astype(jnp.int32), next_e.astype(jnp.int32),
      xs.reshape(p, slab, V7X_LANES), wg, wu, wd)
    return ys.reshape(xs.shape)


def _combine_kernel(x_ref, ysh_ref, wt_ref, gt_ref, gf_ref,
                    slot_hbm, ys_hbm, o_ref, slot_s, rows_v, wrep, acc_lo, acc_hi, sem_i, sem_r):
    i = pl.program_id(0)
    n = pl.num_programs(0)
    tb = x_ref.shape[0]
    half = x_ref.shape[1] // 2
    slab = half // V7X_LANES
    per_tile = MOE_TILE // tb
    cur = i % 2

    def slot_copy(step):
        win = pl.ds((step % per_tile) * tb, tb)
        return pltpu.make_async_copy(slot_hbm.at[step // per_tile, :, win], slot_s.at[step % 2],
                                     sem_i.at[step % 2])

    def request_token(buf, t):
        for k in range(TOP_K):
            pltpu.make_async_copy(ys_hbm.at[_slab_rows(slot_s[buf, k, t], slab)],
                                  rows_v.at[buf, k, _slab_rows(t, slab)],
                                  sem_r.at[buf]).start(priority=k % 2)

    def wait_rows(buf):
        for k in range(TOP_K):
            pltpu.make_async_copy(ys_hbm.at[pl.ds(0, tb * slab)], rows_v.at[buf, k], sem_r.at[buf]).wait()

    @pl.when(i == 0)
    def _():
        slot_copy(0).start()
        slot_copy(0).wait()

        def first(t, carry):
            request_token(0, t)
            return carry

        lax.fori_loop(0, tb, first, 0, unroll=2)

        @pl.when(n > 1)
        def _():
            slot_copy(1).start()
            slot_copy(1).wait()

    @pl.when(i + 2 < n)
    def _():
        slot_copy(i + 2).start()

    wt = wt_ref[...]
    for k in range(TOP_K):
        wrep[k] = jnp.broadcast_to(wt[:, k:k + 1], (tb, V7X_LANES))

    def token(t, carry, buf, request_next):
        rows = _slab_rows(t, slab)
        lo_acc = jnp.zeros((slab, V7X_LANES), jnp.float32)
        hi_acc = jnp.zeros((slab, V7X_LANES), jnp.float32)
        for k in range(TOP_K):
            wv = jnp.broadcast_to(wrep[k, pl.ds(t, 1), :], (slab, V7X_LANES))
            lo, hi = _unpack_pair(rows_v[buf, k, rows, :])
            lo_acc = lo_acc + wv * lo
            hi_acc = hi_acc + wv * hi
        acc_lo[rows, :] = lo_acc
        acc_hi[rows, :] = hi_acc
        if request_next:
            request_token(1 - buf, t)
        return carry

    for buf in range(2):
        for request_next in (True, False):
            @pl.when((cur == buf) & ((i + 1 < n) == request_next))
            def _():
                wait_rows(buf)
                lax.fori_loop(0, tb, functools.partial(token, buf=buf, request_next=request_next),
                              0, unroll=4)

    @pl.when(i + 2 < n)
    def _():
        slot_copy(i + 2).wait()

    yy = ysh_ref[...].astype(jnp.float32) + jnp.concatenate(
        [_slab_load(acc_lo, tb, slab), _slab_load(acc_hi, tb, slab)], axis=1)
    xo = x_ref[...] + gt_ref[0] * yy
    ms = jnp.mean(xo * xo, axis=-1, keepdims=True)
    o_ref[...] = xo * lax.rsqrt(ms + EPS) * gf_ref[...]


def _combine(x1, y_shared, w_t, mod3, gate_idx, g_final, slot3, ys, seq):
    t, d = x1.shape
    tb = COMBINE_TILE
    assert MOE_TILE % tb == 0 and seq % tb == 0
    per_b = seq // tb
    half = d // 2
    return pl.pallas_call(
        _combine_kernel,
        grid=(t // tb,),
        in_specs=[pl.BlockSpec((tb, d), lambda i: (i, 0)),
                  pl.BlockSpec((tb, d), lambda i: (i, 0)),
                  pl.BlockSpec((tb, TOP_K), lambda i: (i, 0)),
                  pl.BlockSpec((1, 1, d), lambda i: ((i // per_b) * 6 + gate_idx, 0, 0)),
                  pl.BlockSpec((1, d), lambda i: (0, 0)),
                  pl.BlockSpec(memory_space=pl.ANY),
                  pl.BlockSpec(memory_space=pl.ANY)],
        out_specs=pl.BlockSpec((tb, d), lambda i: (i, 0)),
        scratch_shapes=[pltpu.SMEM((2, TOP_K, tb), jnp.int32),
                        pltpu.VMEM((2, TOP_K, tb * half // V7X_LANES, V7X_LANES), jnp.int32),
                        pltpu.VMEM((TOP_K, tb, V7X_LANES), jnp.float32),
                        pltpu.VMEM((tb * half // V7X_LANES, V7X_LANES), jnp.float32),
                        pltpu.VMEM((tb * half // V7X_LANES, V7X_LANES), jnp.float32),
                        pltpu.SemaphoreType.DMA((2,)),
                        pltpu.SemaphoreType.DMA((2,))],
        out_shape=jax.ShapeDtypeStruct((t, d), jnp.float32),
        compiler_params=_params(("arbitrary",)),
        name="combine",
    )(x1, y_shared, w_t, mod3, g_final.reshape(1, d), slot3, ys)


def _in_layout(d):
    qw = ATT_HEADS * ATT_HEAD_DIM
    kvw = ATT_KV_HEADS * ATT_HEAD_DIM
    rqk = RET_HEADS * RET_QK_DIM
    rv = RET_HEADS * RET_V_DIM
    order = [("qa", qw), ("ka", kvw), ("va", kvw), ("qr", rqk), ("kr", rqk),
             ("vr", rv), ("gr", rv), ("ga", d), ("gb", d)]
    dst = {}
    off = 0
    for name, width in order:
        dst[name] = off
        off += width
    return dst


def kernel(x, c, w_ada, b_ada, g_norm_mix, w_in, attn_sinks, w_attn_out, w_ret_out, w_o, g_norm_ffn,
           w_router, b_router, w_gate, w_up, w_down, w_sh_gate, w_sh_up, w_sh_down, g_norm_final):
    b, s, d = x.shape
    t = b * s
    depth = w_ada.shape[0]
    bf = jnp.bfloat16
    dst = _in_layout(d)

    c_pad = jnp.zeros((V7X_SUBLANES, d), jnp.float32).at[:b].set(c)
    x2 = x.reshape(t, d)
    for l in range(depth):
        mod = _ada(c_pad, w_ada[l], b_ada[l])
        mod3 = mod[:b].reshape(b * 6, 1, d)

        h = _norm_mod(x2.reshape(b, s, d), g_norm_mix[l], mod3, 0, 1)
        proj, (wg_b, wu_b, wd_b) = _in_proj(h.reshape(t, d), w_in[l].astype(bf),
                                            ((w_gate[l], w_up[l], w_down[l]),), 1024, 768)
        wao_b, wro_b, wo_b = w_attn_out[l].astype(bf), w_ret_out[l].astype(bf), w_o[l].astype(bf)
        wsg_b, wsu_b, wsd_b = w_sh_gate[l].astype(bf), w_sh_up[l].astype(bf), w_sh_down[l].astype(bf)
        proj3 = proj.reshape(b, s, proj.shape[1])
        attn = _attention(proj3, attn_sinks[l], dst["qa"], dst["ka"], dst["va"])
        ret = _retention(proj3, dst["qr"], dst["kr"], dst["vr"], dst["gr"])
        mix = _mix(attn.reshape(t, -1), ret.reshape(t, -1), wao_b, wro_b, proj, dst["ga"], dst["gb"])
        x1 = _out_resid(mix, wo_b, x2, mod3, 2, s)

        h2, h2p, idx3, pos3, w3, cnt = _router(x1, g_norm_ffn[l], mod3, 3, 4,
                                               w_router[l].T.astype(bf), b_router[l], s)
        counts = cnt[:, 0]
        tile = MOE_TILE
        padded = (counts + tile - 1) // tile * tile
        pad_end = jnp.cumsum(padded).astype(jnp.int32)
        pad_start = (pad_end - padded).astype(jnp.int32)
        n_blocks = (t * TOP_K) // tile + N_EXPERTS
        n_used = (pad_end[-1] // tile).reshape(1).astype(jnp.int32)
        blk_first = jnp.arange(n_blocks, dtype=jnp.int32) * tile
        blk_e = jnp.minimum(jnp.sum((pad_end[None, :] <= blk_first[:, None]).astype(jnp.int32), axis=1),
                            N_EXPERTS - 1)
        slab = d // 2 // V7X_LANES
        slot3 = _slots(pad_start, idx3, pos3)
        xs, y_shared = _dispatch(pad_start, pad_end, h2p, h2, wsg_b, wsu_b, wsd_b, slot3,
                                 n_blocks * tile, slab)
        ys = _experts(blk_e, n_used, xs, wg_b, wu_b, wd_b, slab)
        w_t = w3.transpose(0, 2, 1).reshape(t, TOP_K)
        is_last = l == depth - 1
        assert is_last, "the final norm is fused into the last layer's combine"
        x2 = _combine(x1, y_shared, w_t, mod3, 5, g_norm_final, slot3, ys, s)
    return x2.reshape(b, s, d)
```

```python
import functools
import math

import jax
import jax.numpy as jnp
from jax import lax
from jax.experimental import pallas as pl
from jax.experimental.pallas import tpu as pltpu

ATT_HEADS = 32
ATT_KV_HEADS = 4
ATT_HEAD_DIM = 64
WINDOW = 128
ATT_BLOCK = 128
RET_HEADS = 8
RET_QK_DIM = 256
RET_V_DIM = 512
RET_CHUNK = 128
N_EXPERTS = 64
N_GROUPS = 8
TOPK_GROUPS = 4
TOP_K = 8
ROUTED_SCALE = 2.5
EPS = 1e-6

V7X_LANES = 128
V7X_SUBLANES = 8
V7X_VMEM_LIMIT_BYTES = 60000 * 1024

MOE_TILE = 256
COMBINE_TILE = 128
N_IN_TILES = 3
ATT_STEP_BLOCKS = 2
NEG_BIG = -1e30


def _div_block(n, target, align):
    best = None
    b = align
    while b <= min(n, target):
        if n % b == 0:
            best = b
        b += align
    assert best is not None, (n, target, align)
    return best


def _params(semantics):
    return pltpu.CompilerParams(dimension_semantics=semantics,
                                vmem_limit_bytes=V7X_VMEM_LIMIT_BYTES)


def _sigmoid(v):
    return 1.0 / (1.0 + jnp.exp(-v))


def _silu(v):
    return v * _sigmoid(v)


def _pack_pair(lo, hi):
    return pltpu.pack_elementwise([lo, hi], packed_dtype=jnp.bfloat16)


def _unpack_pair(p):
    lo = pltpu.unpack_elementwise(p, index=0, packed_dtype=jnp.bfloat16, unpacked_dtype=jnp.float32)
    hi = pltpu.unpack_elementwise(p, index=1, packed_dtype=jnp.bfloat16, unpacked_dtype=jnp.float32)
    return lo, hi


def _slab_load(ref, n_rows, slab):
    return jnp.concatenate([ref[pl.ds(s, n_rows, stride=slab), :] for s in range(slab)], axis=1)


def _slab_store(ref, val, n_rows, slab):
    for s in range(slab):
        ref[pl.ds(s, n_rows, stride=slab), :] = val[:, s * V7X_LANES:(s + 1) * V7X_LANES]


def _slab_rows(r, slab):
    return pl.ds(pl.multiple_of(r * slab, slab), slab)


def _ada_kernel(c_ref, w_ref, b_ref, o_ref):
    cs = _silu(c_ref[...]).astype(jnp.bfloat16)
    o_ref[...] = jnp.dot(cs, w_ref[...].astype(jnp.bfloat16),
                         preferred_element_type=jnp.float32) + b_ref[...]


def _ada(c_pad, w, b):
    m, d = c_pad.shape
    n = w.shape[1]
    tn = _div_block(n, 1024, V7X_LANES)
    return pl.pallas_call(
        _ada_kernel,
        grid=(n // tn,),
        in_specs=[pl.BlockSpec((m, d), lambda j: (0, 0)),
                  pl.BlockSpec((d, tn), lambda j: (0, j)),
                  pl.BlockSpec((1, tn), lambda j: (0, j))],
        out_specs=pl.BlockSpec((m, tn), lambda j: (0, j)),
        out_shape=jax.ShapeDtypeStruct((m, n), jnp.float32),
        compiler_params=_params(("parallel",)),
        name="ada",
    )(c_pad, w, b.reshape(1, n))


def _norm_mod_kernel(x_ref, g_ref, sh_ref, sc_ref, o_ref):
    x = x_ref[0]
    ms = jnp.mean(x * x, axis=-1, keepdims=True)
    y = x * lax.rsqrt(ms + EPS) * g_ref[...]
    o_ref[0] = (y * (1.0 + sc_ref[0]) + sh_ref[0]).astype(o_ref.dtype)


def _norm_mod(x3, g, mod3, shift_idx, scale_idx):
    b, s, d = x3.shape
    ts = _div_block(s, 512, V7X_SUBLANES)
    return pl.pallas_call(
        _norm_mod_kernel,
        grid=(b, s // ts),
        in_specs=[pl.BlockSpec((1, ts, d), lambda bi, i: (bi, i, 0)),
                  pl.BlockSpec((1, d), lambda bi, i: (0, 0)),
                  pl.BlockSpec((1, 1, d), lambda bi, i: (bi * 6 + shift_idx, 0, 0)),
                  pl.BlockSpec((1, 1, d), lambda bi, i: (bi * 6 + scale_idx, 0, 0))],
        out_specs=pl.BlockSpec((1, ts, d), lambda bi, i: (bi, i, 0)),
        out_shape=jax.ShapeDtypeStruct((b, s, d), jnp.bfloat16),
        compiler_params=_params(("parallel", "parallel")),
        name="norm_mod",
    )(x3, g.reshape(1, d), mod3, mod3)


def _in_proj_kernel(a_ref, b_ref, *rest, windows, n_j):
    n_side = len(windows)
    src = rest[:n_side]
    o_ref = rest[n_side]
    dst = rest[n_side + 1:]
    o_ref[...] = jnp.dot(a_ref[...], b_ref[...],
                         preferred_element_type=jnp.float32).astype(o_ref.dtype)
    step = pl.program_id(0) * n_j + pl.program_id(1)

    for s_ref, d_ref, (first, count) in zip(src, dst, windows):
        @pl.when((step >= first) & (step < first + count))
        def _():
            d_ref[...] = s_ref[...].astype(d_ref.dtype)


def _in_proj(a, b, side_groups, tm_target, tn_target):
    m, k = a.shape
    n = b.shape[1]
    tm = _div_block(m, tm_target, V7X_SUBLANES)
    tn = _div_block(n, tn_target, V7X_LANES)
    n_i, n_j = m // tm, n // tn
    steps_left = n_i * n_j
    first = 0
    side, chunked, windows = [], [], []
    for group in side_groups:
        n_max = 1 << (steps_left.bit_length() - 1)
        used = 0
        for w in group:
            rows = w.size // w.shape[-1]
            count = n_max
            while rows % (count * 2 * V7X_SUBLANES):
                count //= 2
            side.append(w)
            chunked.append(w.reshape(count, rows // count, w.shape[-1]))
            windows.append((first, count))
            used = max(used, count)
        first += used
        steps_left -= used

    def side_spec(c, window):
        w_first, count = window
        return pl.BlockSpec((1,) + c.shape[1:],
                            lambda i, j: (jnp.clip(i * n_j + j - w_first, 0, count - 1), 0, 0))

    side_specs = [side_spec(c, w) for c, w in zip(chunked, windows)]
    outs = pl.pallas_call(
        functools.partial(_in_proj_kernel, windows=tuple(windows), n_j=n_j),
        grid=(n_i, n_j),
        in_specs=[pl.BlockSpec((tm, k), lambda i, j: (i, 0)),
                  pl.BlockSpec((k, tn), lambda i, j: (0, j))] + side_specs,
        out_specs=[pl.BlockSpec((tm, tn), lambda i, j: (i, j))] + side_specs,
        out_shape=[jax.ShapeDtypeStruct((m, n), jnp.bfloat16)]
        + [jax.ShapeDtypeStruct(c.shape, jnp.bfloat16) for c in chunked],
        compiler_params=_params(("arbitrary", "arbitrary")),
        name="in_proj",
    )(a, b, *chunked)
    return outs[0], [o.reshape(w.shape) for o, w in zip(outs[1:], side)]


def _attn_kernel(sink_ref, q_ref, kc_ref, kp_ref, vc_ref, vp_ref, bias_ref, o_ref):
    i = pl.program_id(1)
    for sb in range(ATT_STEP_BLOCKS):
        _attn_block(sb, i, sink_ref, q_ref, kc_ref, kp_ref, vc_ref, vp_ref, bias_ref, o_ref)


def _attn_block(sb, i, sink_ref, q_ref, kc_ref, kp_ref, vc_ref, vp_ref, bias_ref, o_ref):
    blk = ATT_BLOCK
    hd = ATT_HEAD_DIM
    group = ATT_HEADS // ATT_KV_HEADS
    pairs = group // 2
    rows = slice(sb * blk, (sb + 1) * blk)
    prow = slice((sb - 1) * blk, sb * blk)
    nt = (((1,), (1,)), ((), ()))
    tn = (((0,), (0,)), ((), ()))
    zpad = jnp.zeros((2 * blk, hd), jnp.bfloat16)

    def scores(h, par):
        sl = slice(h * hd, (h + 1) * hd)
        k_prev = kp_ref[0, :, sl] if sb == 0 else kc_ref[0, prow, sl]
        k2 = jnp.concatenate([k_prev, kc_ref[0, rows, sl]], axis=0) * (hd ** -0.5)
        qp = jnp.concatenate([q_ref[0, rows, (h * pairs + p) * 2 * hd:(h * pairs + p + 1) * 2 * hd]
                              for p in range(pairs)], axis=0)
        kz = jnp.concatenate([k2, zpad] if par == 0 else [zpad, k2], axis=1)
        return lax.dot_general(kz, qp, nt, preferred_element_type=jnp.float32)

    def softmax(s, h, par):
        s = s + bias_ref[h * 2 + par]
        if sb == 0:
            s = jnp.concatenate([jnp.where(i == 0, NEG_BIG, s[:blk]), s[blk:]], axis=0)
        sink = jnp.concatenate([jnp.full((1, blk), sink_ref[h * group + 2 * p + par], jnp.float32)
                                for p in range(pairs)], axis=1)
        m = jnp.maximum(s.max(0, keepdims=True), sink)
        pr = jnp.exp(s - m)
        denom = pr.sum(0, keepdims=True) + jnp.exp(sink - m)
        return pr.astype(jnp.bfloat16), 1.0 / denom

    def values(pr, inv, h, par):
        sl = slice(h * hd, (h + 1) * hd)
        v_prev = vp_ref[0, :, sl] if sb == 0 else vc_ref[0, prow, sl]
        v2 = jnp.concatenate([v_prev, vc_ref[0, rows, sl]], axis=0)
        vz = jnp.concatenate([v2, zpad] if par == 0 else [zpad, v2], axis=1)
        return lax.dot_general(vz, pr, tn, preferred_element_type=jnp.float32) * inv

    items = [(h, par) for h in range(ATT_KV_HEADS) for par in range(2)]
    s_of, p_of, acc = {}, {}, {}
    for n in range(len(items) + 2):
        if n < len(items):
            s_of[n] = scores(*items[n])
        if 1 <= n <= len(items):
            p_of[n - 1] = softmax(s_of.pop(n - 1), *items[n - 1])
        if n >= 2:
            h, par = items[n - 2]
            o = values(*p_of.pop(n - 2), h, par)
            acc[h] = o if par == 0 else acc[h] + o
            if par == 1:
                out = acc.pop(h)
                for p in range(pairs):
                    o_ref[0, rows, (h * pairs + p) * 2 * hd:(h * pairs + p + 1) * 2 * hd] = (
                        out[:, p * blk:(p + 1) * blk].T.astype(o_ref.dtype))


def _attn_bias():
    blk = ATT_BLOCK
    group = ATT_HEADS // ATT_KV_HEADS
    pairs = group // 2
    qi = jnp.arange(blk)[:, None]
    kj = jnp.arange(2 * blk)[None, :]
    dist = qi + blk - kj
    valid = (dist >= 0) & (dist < WINDOW)
    slopes = jnp.exp2(-8.0 * jnp.arange(1, ATT_HEADS + 1, dtype=jnp.float32) / ATT_HEADS)
    slopes = slopes.reshape(ATT_KV_HEADS, pairs, 2)
    bias = jnp.where(valid, -slopes[..., None, None] * dist.astype(jnp.float32), NEG_BIG)
    return bias.transpose(0, 2, 4, 1, 3).reshape(ATT_KV_HEADS * 2, 2 * blk, pairs * blk)


def _attention(proj3, sinks, q_off, k_off, v_off):
    b, s, _ = proj3.shape
    qw = ATT_HEADS * ATT_HEAD_DIM
    kvw = ATT_KV_HEADS * ATT_HEAD_DIM
    step_rows = ATT_STEP_BLOCKS * ATT_BLOCK
    assert s % step_rows == 0
    nb = s // step_rows
    group = ATT_HEADS // ATT_KV_HEADS
    assert q_off % qw == 0 and k_off % kvw == 0 and v_off % kvw == 0
    assert group % 2 == 0 and 2 * ATT_HEAD_DIM == V7X_LANES and WINDOW == ATT_BLOCK
    assert 4 ** round(math.log(ATT_HEAD_DIM, 4)) == ATT_HEAD_DIM, "score scale must be a power of two"
    qb, kb, vb = q_off // qw, k_off // kvw, v_off // kvw
    bias = _attn_bias()
    return pl.pallas_call(
        _attn_kernel,
        grid_spec=pltpu.PrefetchScalarGridSpec(
            num_scalar_prefetch=1,
            grid=(b, nb),
            in_specs=[pl.BlockSpec((1, step_rows, qw), lambda bi, i, sk: (bi, i, qb)),
                      pl.BlockSpec((1, step_rows, kvw), lambda bi, i, sk: (bi, i, kb)),
                      pl.BlockSpec((1, ATT_BLOCK, kvw),
                                   lambda bi, i, sk: (bi, jnp.maximum(ATT_STEP_BLOCKS * i - 1, 0), kb)),
                      pl.BlockSpec((1, step_rows, kvw), lambda bi, i, sk: (bi, i, vb)),
                      pl.BlockSpec((1, ATT_BLOCK, kvw),
                                   lambda bi, i, sk: (bi, jnp.maximum(ATT_STEP_BLOCKS * i - 1, 0), vb)),
                      pl.BlockSpec(bias.shape, lambda bi, i, sk: (0, 0, 0), pipeline_mode=pl.Buffered(1))],
            out_specs=pl.BlockSpec((1, step_rows, qw), lambda bi, i, sk: (bi, i, 0)),
        ),
        out_shape=jax.ShapeDtypeStruct((b, s, qw), jnp.bfloat16),
        compiler_params=_params(("parallel", "parallel")),
        name="attention",
    )(sinks, proj3, proj3, proj3, proj3, proj3, bias)


def _ret_kernel(q_ref, k_ref, v_ref, gr_ref, mask_ref, qd_ref, kd_ref, cd_ref, o_ref, state_ref):
    c = pl.program_id(1)

    @pl.when(c == 0)
    def _():
        state_ref[...] = jnp.zeros_like(state_ref)

    nt = (((1,), (1,)), ((), ()))
    tn = (((0,), (0,)), ((), ()))
    def decayed(bi):
        q = q_ref[bi]
        k = k_ref[bi]
        attn = lax.dot_general(q, k, nt, preferred_element_type=jnp.float32) * mask_ref[0]
        kd = (k.astype(jnp.float32) * kd_ref[0]).astype(k.dtype)
        return attn.astype(jnp.bfloat16), kd

    def recur(bi, attn, kd):
        q = q_ref[bi]
        v = v_ref[bi]
        intra = jnp.dot(attn, v, preferred_element_type=jnp.float32)
        state = state_ref[bi]
        inter = jnp.dot(q, state.astype(q.dtype), preferred_element_type=jnp.float32) * qd_ref[0]
        state_ref[bi] = state * cd_ref[0] + lax.dot_general(kd, v, tn, preferred_element_type=jnp.float32)
        return intra + inter

    def finish(bi, o):
        mu = jnp.mean(o, axis=-1, keepdims=True)
        oc = o - mu
        var = jnp.mean(oc * oc, axis=-1, keepdims=True)
        y = oc * lax.rsqrt(var + EPS)
        o_ref[bi] = (_silu(gr_ref[bi].astype(jnp.float32)) * y).astype(o_ref.dtype)

    nb = q_ref.shape[0]
    a_of, o_of = {}, {}
    for n in range(nb + 2):
        if n < nb:
            a_of[n] = decayed(n)
        if 1 <= n <= nb:
            o_of[n - 1] = recur(n - 1, *a_of.pop(n - 1))
        if n >= 2:
            finish(n - 2, o_of.pop(n - 2))


def _retention(proj3, q_off, k_off, v_off, g_off):
    b, s, _ = proj3.shape
    dk, dv, ch = RET_QK_DIM, RET_V_DIM, RET_CHUNK
    assert q_off % dk == 0 and k_off % dk == 0 and v_off % dv == 0 and g_off % dv == 0
    qb, kb, vb, gb = q_off // dk, k_off // dk, v_off // dv, g_off // dv
    n = s // ch
    log_g = jnp.log1p(-jnp.exp2(-5.0 - jnp.arange(RET_HEADS, dtype=jnp.float32)))
    pos = jnp.arange(ch, dtype=jnp.float32)
    rel = pos[:, None] - pos[None, :]
    scale = dk ** -0.5
    mask = jnp.where(rel[None] >= 0, jnp.exp(rel[None] * log_g[:, None, None]), 0.0) * scale
    q_decay = jnp.exp((pos[None, :, None] + 1.0) * log_g[:, None, None])
    k_decay = jnp.exp((ch - 1.0 - pos[None, :, None]) * log_g[:, None, None]) * scale
    c_decay = jnp.exp(ch * log_g)[:, None, None]
    return pl.pallas_call(
        _ret_kernel,
        grid=(RET_HEADS, n),
        in_specs=[pl.BlockSpec((b, ch, dk), lambda h, c: (0, c, qb + h)),
                  pl.BlockSpec((b, ch, dk), lambda h, c: (0, c, kb + h)),
                  pl.BlockSpec((b, ch, dv), lambda h, c: (0, c, vb + h)),
                  pl.BlockSpec((b, ch, dv), lambda h, c: (0, c, gb + h)),
                  pl.BlockSpec((1, ch, ch), lambda h, c: (h, 0, 0)),
                  pl.BlockSpec((1, ch, 1), lambda h, c: (h, 0, 0)),
                  pl.BlockSpec((1, ch, 1), lambda h, c: (h, 0, 0)),
                  pl.BlockSpec((1, 1, 1), lambda h, c: (h, 0, 0))],
        out_specs=pl.BlockSpec((b, ch, dv), lambda h, c: (0, c, h)),
        out_shape=jax.ShapeDtypeStruct((b, s, RET_HEADS * dv), jnp.bfloat16),
        scratch_shapes=[pltpu.VMEM((b, dk, dv), jnp.float32)],
        compiler_params=_params(("parallel", "arbitrary")),
        name="retention",
    )(proj3, proj3, proj3, proj3, mask, q_decay, k_decay, c_decay)


def _mix_kernel(a_ref, r_ref, wa_ref, wr_ref, ga_ref, gb_ref, o_ref):
    ya = jnp.dot(a_ref[...], wa_ref[...], preferred_element_type=jnp.float32)
    yr = jnp.dot(r_ref[...], wr_ref[...], preferred_element_type=jnp.float32)
    ga = _sigmoid(ga_ref[...].astype(jnp.float32))
    gb = _sigmoid(gb_ref[...].astype(jnp.float32))
    o_ref[...] = (ga * ya + gb * yr).astype(o_ref.dtype)


def _mix(attn2, ret2, wa, wr, proj2, ga_off, gb_off):
    m, ka = attn2.shape
    kr = ret2.shape[1]
    d = wa.shape[1]
    tm = _div_block(m, 1024, V7X_SUBLANES)
    tn = _div_block(d, 512, V7X_LANES)
    assert ga_off % tn == 0 and gb_off % tn == 0
    gab, gbb = ga_off // tn, gb_off // tn
    return pl.pallas_call(
        _mix_kernel,
        grid=(m // tm, d // tn),
        in_specs=[pl.BlockSpec((tm, ka), lambda i, j: (i, 0)),
                  pl.BlockSpec((tm, kr), lambda i, j: (i, 0)),
                  pl.BlockSpec((ka, tn), lambda i, j: (0, j)),
                  pl.BlockSpec((kr, tn), lambda i, j: (0, j)),
                  pl.BlockSpec((tm, tn), lambda i, j: (i, gab + j)),
                  pl.BlockSpec((tm, tn), lambda i, j: (i, gbb + j))],
        out_specs=pl.BlockSpec((tm, tn), lambda i, j: (i, j)),
        out_shape=jax.ShapeDtypeStruct((m, d), jnp.bfloat16),
        compiler_params=_params(("parallel", "parallel")),
        name="mix",
    )(attn2, ret2, wa, wr, proj2, proj2)


def _resid_kernel(a_ref, w_ref, x_ref, gt_ref, o_ref):
    y = jnp.dot(a_ref[...], w_ref[...], preferred_element_type=jnp.float32)
    o_ref[...] = x_ref[...] + gt_ref[0] * y


def _out_resid(mix2, w, x2, mod3, gate_idx, seq):
    m, k = mix2.shape
    d = w.shape[1]
    tm = _div_block(seq, 1024, V7X_SUBLANES)
    tn = _div_block(d, 1024, V7X_LANES)
    per_b = seq // tm
    return pl.pallas_call(
        _resid_kernel,
        grid=(m // tm, d // tn),
        in_specs=[pl.BlockSpec((tm, k), lambda i, j: (i, 0)),
                  pl.BlockSpec((k, tn), lambda i, j: (0, j)),
                  pl.BlockSpec((tm, tn), lambda i, j: (i, j)),
                  pl.BlockSpec((1, 1, tn), lambda i, j: ((i // per_b) * 6 + gate_idx, 0, j))],
        out_specs=pl.BlockSpec((tm, tn), lambda i, j: (i, j)),
        out_shape=jax.ShapeDtypeStruct((m, d), jnp.float32),
        compiler_params=_params(("parallel", "parallel")),
        name="out_resid",
    )(mix2, w, x2, mod3)


def _router_kernel(x_ref, g_ref, sh_ref, sc_ref, wr_ref, br_ref,
                   h_ref, hp_ref, idx_ref, pos_ref, w_ref, cnt_ref, carry_ref):
    i = pl.program_id(0)
    e = N_EXPERTS
    per_g = e // N_GROUPS
    tb = x_ref.shape[0]

    @pl.when(i == 0)
    def _():
        carry_ref[...] = jnp.zeros_like(carry_ref)

    x = x_ref[...]
    ms = jnp.mean(x * x, axis=-1, keepdims=True)
    h = x * lax.rsqrt(ms + EPS) * g_ref[...]
    h = h * (1.0 + sc_ref[0]) + sh_ref[0]
    hb = h.astype(jnp.bfloat16)
    h_ref[...] = hb
    half = h.shape[1] // 2
    _slab_store(hp_ref, _pack_pair(h[:, :half], h[:, half:]), tb, half // V7X_LANES)

    nt = (((1,), (1,)), ((), ()))
    logits = lax.dot_general(wr_ref[...], hb, nt, preferred_element_type=jnp.float32)
    scores = _sigmoid(logits)
    choice = scores + br_ref[...]

    c3 = choice.reshape(N_GROUPS, per_g, tb)
    j_iota = lax.broadcasted_iota(jnp.int32, c3.shape, 1).astype(jnp.float32)
    m1 = c3.max(axis=1, keepdims=True)
    first = jnp.min(jnp.where(c3 == m1, j_iota, float(per_g)), axis=1, keepdims=True)
    m2 = jnp.where(j_iota == first, -jnp.inf, c3).max(axis=1, keepdims=True)
    gs = (m1 + m2).reshape(N_GROUPS, tb)

    g_iota = lax.broadcasted_iota(jnp.int32, gs.shape, 0)
    grank = jnp.zeros(gs.shape, jnp.int32)
    for gp in range(N_GROUPS):
        row = gs[gp:gp + 1, :]
        ahead = (row > gs) | ((row == gs) & (gp < g_iota))
        grank = grank + ahead.astype(jnp.int32)
    gmask = grank < TOPK_GROUPS
    emask = jnp.broadcast_to(gmask.reshape(N_GROUPS, 1, tb), c3.shape).reshape(e, tb)
    masked = jnp.where(emask, choice, -jnp.inf)

    e_iota = lax.broadcasted_iota(jnp.int32, masked.shape, 0)
    erank = jnp.zeros(masked.shape, jnp.int32)
    for ep in range(e):
        row = masked[ep:ep + 1, :]
        ahead = (row > masked) | ((row == masked) & (ep < e_iota))
        erank = erank + ahead.astype(jnp.int32)
    sel = (erank < TOP_K) & emask
    self32 = sel.astype(jnp.float32)

    wsel = scores * self32
    wn = wsel / jnp.sum(wsel, axis=0, keepdims=True) * ROUTED_SCALE

    selb = self32.astype(jnp.bfloat16)
    t_r = lax.broadcasted_iota(jnp.int32, (tb, tb), 0)
    t_c = lax.broadcasted_iota(jnp.int32, (tb, tb), 1)
    upper = (t_r <= t_c).astype(jnp.bfloat16)
    incl = jnp.dot(selb, upper, preferred_element_type=jnp.float32)
    carry = carry_ref[...]
    rank_in_e = carry + incl - 1.0
    carry_new = carry + jnp.sum(self32, axis=1, keepdims=True)
    carry_ref[...] = carry_new
    cnt_ref[...] = jnp.broadcast_to(carry_new, cnt_ref.shape).astype(jnp.int32)

    e_r = lax.broadcasted_iota(jnp.int32, (e, e), 0)
    e_c = lax.broadcasted_iota(jnp.int32, (e, e), 1)
    lower = (e_c < e_r).astype(jnp.bfloat16)
    before = jnp.dot(lower, selb, preferred_element_type=jnp.float32)
    e_f = e_iota.astype(jnp.float32)
    idx_rows, pos_rows, w_rows = [], [], []
    for k in range(TOP_K):
        hit = jnp.where(sel & (before == float(k)), 1.0, 0.0)
        idx_rows.append(jnp.sum(hit * e_f, axis=0, keepdims=True))
        pos_rows.append(jnp.sum(hit * rank_in_e, axis=0, keepdims=True))
        w_rows.append(jnp.sum(hit * wn, axis=0, keepdims=True))
    idx_ref[0] = jnp.concatenate(idx_rows, axis=0).astype(jnp.int32)
    pos_ref[0] = jnp.concatenate(pos_rows, axis=0).astype(jnp.int32)
    w_ref[0] = jnp.concatenate(w_rows, axis=0)


def _router(x1, g, mod3, shift_idx, scale_idx, w_router_t, b_router, seq):
    t, d = x1.shape
    e = N_EXPERTS
    tb = MOE_TILE
    assert seq % tb == 0 and d % (2 * V7X_LANES * V7X_SUBLANES) == 0
    slab = d // 2 // V7X_LANES
    per_b = seq // tb
    nt = t // tb
    return pl.pallas_call(
        _router_kernel,
        grid=(nt,),
        in_specs=[pl.BlockSpec((tb, d), lambda i: (i, 0)),
                  pl.BlockSpec((1, d), lambda i: (0, 0)),
                  pl.BlockSpec((1, 1, d), lambda i: ((i // per_b) * 6 + shift_idx, 0, 0)),
                  pl.BlockSpec((1, 1, d), lambda i: ((i // per_b) * 6 + scale_idx, 0, 0)),
                  pl.BlockSpec((e, d), lambda i: (0, 0)),
                  pl.BlockSpec((e, 1), lambda i: (0, 0))],
        out_specs=[pl.BlockSpec((tb, d), lambda i: (i, 0)),
                   pl.BlockSpec((tb * slab, V7X_LANES), lambda i: (i, 0)),
                   pl.BlockSpec((1, TOP_K, tb), lambda i: (i, 0, 0)),
                   pl.BlockSpec((1, TOP_K, tb), lambda i: (i, 0, 0)),
                   pl.BlockSpec((1, TOP_K, tb), lambda i: (i, 0, 0)),
                   pl.BlockSpec((e, V7X_LANES), lambda i: (0, 0))],
        out_shape=[jax.ShapeDtypeStruct((t, d), jnp.bfloat16),
                   jax.ShapeDtypeStruct((t * slab, V7X_LANES), jnp.int32),
                   jax.ShapeDtypeStruct((nt, TOP_K, tb), jnp.int32),
                   jax.ShapeDtypeStruct((nt, TOP_K, tb), jnp.int32),
                   jax.ShapeDtypeStruct((nt, TOP_K, tb), jnp.float32),
                   jax.ShapeDtypeStruct((e, V7X_LANES), jnp.int32)],
        scratch_shapes=[pltpu.VMEM((e, 1), jnp.float32)],
        compiler_params=_params(("arbitrary",)),
        name="router",
    )(x1, g.reshape(1, d), mod3, mod3, w_router_t, b_router.reshape(e, 1))


def _slots_kernel(pstart_ref, idx_ref, pos_ref, o_ref):
    idx = idx_ref[...]
    base = jnp.zeros(idx.shape, jnp.int32)
    for ex in range(N_EXPERTS):
        base = jnp.where(idx == ex, pstart_ref[ex], base)
    o_ref[...] = base + pos_ref[...]


def _slots(pad_start, idx3, pos3):
    nt = idx3.shape[0]
    per_step = _div_block(nt, 8, 1)
    blk = (per_step,) + idx3.shape[1:]
    spec = pl.BlockSpec(blk, lambda i, ps: (i, 0, 0))
    return pl.pallas_call(
        _slots_kernel,
        grid_spec=pltpu.PrefetchScalarGridSpec(num_scalar_prefetch=1, grid=(nt // per_step,),
                                               in_specs=[spec, spec], out_specs=spec),
        out_shape=jax.ShapeDtypeStruct(idx3.shape, jnp.int32),
        compiler_params=_params(("parallel",)),
        name="slots",
    )(pad_start, idx3, pos3)


def _dispatch_kernel(pstart_ref, pend_ref, hp_ref, h_ref, wsg_ref, wsu_ref, wsd_ref, slot_hbm,
                     xs_hbm, ysh_ref, slot_s, zero_v, sem_i, sem_z, sem_r, *, slab):
    i = pl.program_id(0)
    tb = hp_ref.shape[0] // slab

    def slot_copy():
        return pltpu.make_async_copy(slot_hbm.at[i], slot_s, sem_i)

    slot_copy().start()

    def zero_copy(ex):
        first = pl.multiple_of((pend_ref[ex] - tb) * slab, tb * slab)
        return pltpu.make_async_copy(zero_v, xs_hbm.at[pl.ds(first, tb * slab)], sem_z)

    @pl.when(i == 0)
    def _():
        zero_v[...] = jnp.zeros_like(zero_v)

        def start(ex, carry):
            @pl.when(pend_ref[ex] > pstart_ref[ex])
            def _():
                zero_copy(ex).start()
            return carry

        def wait(ex, carry):
            @pl.when(pend_ref[ex] > pstart_ref[ex])
            def _():
                zero_copy(ex).wait()
            return carry

        lax.fori_loop(0, N_EXPERTS, start, 0)
        lax.fori_loop(0, N_EXPERTS, wait, 0)

    slot_copy().wait()

    def row_copy(t, k):
        return pltpu.make_async_copy(hp_ref.at[_slab_rows(t, slab)],
                                     xs_hbm.at[_slab_rows(slot_s[k, t], slab)], sem_r)

    def start_rows(t, carry):
        for k in range(TOP_K):
            row_copy(t, k).start(priority=k % 2)
        return carry

    lax.fori_loop(0, tb // 2, start_rows, 0, unroll=2)
    h = h_ref[...]
    g = jnp.dot(h, wsg_ref[...], preferred_element_type=jnp.float32)
    u = jnp.dot(h, wsu_ref[...], preferred_element_type=jnp.float32)
    a = (_silu(g) * u).astype(jnp.bfloat16)
    lax.fori_loop(tb // 2, tb, start_rows, 0, unroll=2)
    ysh_ref[...] = jnp.dot(a, wsd_ref[...], preferred_element_type=jnp.float32).astype(ysh_ref.dtype)

    for k in range(TOP_K):
        pltpu.make_async_copy(hp_ref, xs_hbm.at[pl.ds(0, tb * slab)], sem_r).wait()


def _dispatch(pad_start, pad_end, h2p, h2, wsg, wsu, wsd, slot3, n_rows, slab):
    tb = MOE_TILE
    t, d = h2.shape
    f = wsg.shape[1]
    once = pl.Buffered(1)
    return pl.pallas_call(
        functools.partial(_dispatch_kernel, slab=slab),
        grid_spec=pltpu.PrefetchScalarGridSpec(
            num_scalar_prefetch=2,
            grid=(t // tb,),
            in_specs=[pl.BlockSpec((tb * slab, V7X_LANES), lambda i, ps, pe: (i, 0)),
                      pl.BlockSpec((tb, d), lambda i, ps, pe: (i, 0)),
                      pl.BlockSpec((d, f), lambda i, ps, pe: (0, 0), pipeline_mode=once),
                      pl.BlockSpec((d, f), lambda i, ps, pe: (0, 0), pipeline_mode=once),
                      pl.BlockSpec((f, d), lambda i, ps, pe: (0, 0), pipeline_mode=once),
                      pl.BlockSpec(memory_space=pl.ANY)],
            out_specs=[pl.BlockSpec(memory_space=pl.ANY),
                       pl.BlockSpec((tb, d), lambda i, ps, pe: (i, 0))],
            scratch_shapes=[pltpu.SMEM((TOP_K, tb), jnp.int32),
                            pltpu.VMEM((tb * slab, V7X_LANES), jnp.int32),
                            pltpu.SemaphoreType.DMA,
                            pltpu.SemaphoreType.DMA,
                            pltpu.SemaphoreType.DMA],
        ),
        out_shape=[jax.ShapeDtypeStruct((n_rows * slab, V7X_LANES), jnp.int32),
                   jax.ShapeDtypeStruct((t, d), jnp.bfloat16)],
        compiler_params=_params(("arbitrary",)),
        name="dispatch",
    )(pad_start, pad_end, h2p, h2, wsg, wsu, wsd, slot3)


def _expert_kernel(blk_e_ref, nused_ref, first_ref, ord_ref, next_e_ref, xs_hbm, wg_hbm, wu_hbm, wd_hbm,
                   ys_hbm, xt, yt, wg_v, wu_v, wd_v, sem_in, sem_out, sem_w, *, slab):
    j = pl.program_id(0)
    n_used = nused_ref[0]
    tb = xt.shape[1]
    lanes = V7X_LANES
    half = slab * lanes

    def w_copies(e, slot):
        return [pltpu.make_async_copy(src.at[e], dst.at[slot], sem_w.at[slot])
                for src, dst in ((wg_hbm, wg_v), (wu_hbm, wu_v), (wd_hbm, wd_v))]

    def in_copies(tile, buf):
        rows = pl.ds(pl.multiple_of(tile * tb, tb), tb)
        return [pltpu.make_async_copy(xs_hbm.at[rows, s], xt.at[buf, :, pl.ds(s * lanes, lanes)],
                                      sem_in.at[buf]) for s in range(slab)]

    def out_copies(tile, buf):
        rows = pl.ds(pl.multiple_of(tile * tb, tb), tb)
        return [pltpu.make_async_copy(yt.at[buf, :, pl.ds(s * lanes, lanes)], ys_hbm.at[rows, s],
                                      sem_out.at[buf]) for s in range(slab)]

    def wait_in(buf):
        pltpu.make_async_copy(yt.at[0], xt.at[buf], sem_in.at[buf]).wait()

    def wait_out(buf):
        pltpu.make_async_copy(xt.at[0], yt.at[buf], sem_out.at[buf]).wait()

    @pl.when(j == 0)
    def _():
        for c in in_copies(0, 0) + w_copies(blk_e_ref[0], 0):
            c.start()

        @pl.when(n_used > 1)
        def _():
            for c in in_copies(1, 1):
                c.start()

    @pl.when(j < n_used)
    def _():
        cur = j % 2
        cur_in = j % N_IN_TILES
        slot = ord_ref[j] % 2

        @pl.when(first_ref[j] == 1)
        def _():
            for c in w_copies(blk_e_ref[j], slot):
                c.wait()

            @pl.when(next_e_ref[j] >= 0)
            def _():
                for c in w_copies(next_e_ref[j], 1 - slot):
                    c.start()

        @pl.when(j + 2 < n_used)
        def _():
            for c in in_copies(j + 2, (j + 2) % N_IN_TILES):
                c.start()

        wait_in(cur_in)

        @pl.when(j >= 2)
        def _():
            wait_out(cur)

        lo, hi = _unpack_pair(xt[cur_in])
        lo = lo.astype(jnp.bfloat16)
        hi = hi.astype(jnp.bfloat16)
        g = (jnp.dot(lo, wg_v[slot, :half, :], preferred_element_type=jnp.float32)
             + jnp.dot(hi, wg_v[slot, half:, :], preferred_element_type=jnp.float32))
        u = (jnp.dot(lo, wu_v[slot, :half, :], preferred_element_type=jnp.float32)
             + jnp.dot(hi, wu_v[slot, half:, :], preferred_element_type=jnp.float32))
        a = (_silu(g) * u).astype(jnp.bfloat16)
        y = jnp.dot(a, wd_v[slot], preferred_element_type=jnp.float32)
        yt[cur] = _pack_pair(y[:, :half], y[:, half:])
        for c in out_copies(j, cur):
            c.start(priority=1)

        @pl.when(j == n_used - 1)
        def _():
            @pl.when(j >= 1)
            def _():
                wait_out(1 - cur)

            wait_out(cur)


def _experts(blk_e, n_used, xs, wg, wu, wd, slab):
    tb = MOE_TILE
    p = xs.shape[0] // slab
    half = slab * V7X_LANES
    d = 2 * half
    f = wg.shape[2]
    nblk = p // tb

    tiles = jnp.arange(nblk, dtype=jnp.int32)
    used = tiles < n_used[0]
    first = used & ((tiles == 0) | (blk_e != jnp.roll(blk_e, 1)))
    ordinal = jnp.cumsum(first.astype(jnp.int32)) - 1
    first_pos = jnp.where(first, tiles, nblk)
    next_first = lax.cummin(jnp.roll(first_pos, -1).at[-1].set(nblk), reverse=True)
    next_e = jnp.where(next_first < nblk, blk_e[jnp.minimum(next_first, nblk - 1)], -1)

    any_spec = pl.BlockSpec(memory_space=pl.ANY)
    ys = pl.pallas_call(
        functools.partial(_expert_kernel, slab=slab),
        grid_spec=pltpu.PrefetchScalarGridSpec(
            num_scalar_prefetch=5,
            grid=(nblk,),
            in_specs=[any_spec, any_spec, any_spec, any_spec],
            out_specs=any_spec,
            scratch_shapes=[pltpu.VMEM((N_IN_TILES, tb, half), jnp.int32),
                            pltpu.VMEM((2, tb, half), jnp.int32),
                            pltpu.VMEM((2, d, f), jnp.bfloat16),
                            pltpu.VMEM((2, d, f), jnp.bfloat16),
                            pltpu.VMEM((2, f, d), jnp.bfloat16),
                            pltpu.SemaphoreType.DMA((N_IN_TILES,)),
                            pltpu.SemaphoreType.DMA((2,)),
                            pltpu.SemaphoreType.DMA((2,))],
        ),
        out_shape=jax.ShapeDtypeStruct((p, slab, V7X_LANES), jnp.int32),
        compiler_params=_params(("arbitrary",)),
        name="experts",
    )(blk_e, n_used, first.astype(jnp.int32), ordinal.astype(jnp.int32), next_e.astype(jnp.int32),
      xs.reshape(p, slab, V7X_LANES), wg, wu, wd)
    return ys.reshape(xs.shape)


def _combine_kernel(x_ref, ysh_ref, wt_ref, gt_ref, gf_ref,
                    slot_hbm, ys_hbm, o_ref, slot_s, rows_v, wrep, acc_lo, acc_hi, sem_i, sem_r):
    i = pl.program_id(0)
    n = pl.num_programs(0)
    tb = x_ref.shape[0]
    half = x_ref.shape[1] // 2
    slab = half // V7X_LANES
    per_tile = MOE_TILE // tb
    cur = i % 2

    def slot_copy(step):
        win = pl.ds((step % per_tile) * tb, tb)
        return pltpu.make_async_copy(slot_hbm.at[step // per_tile, :, win], slot_s.at[step % 2],
                                     sem_i.at[step % 2])

    def request_token(buf, t):
        for k in range(TOP_K):
            pltpu.make_async_copy(ys_hbm.at[_slab_rows(slot_s[buf, k, t], slab)],
                                  rows_v.at[buf, k, _slab_rows(t, slab)],
                                  sem_r.at[buf]).start(priority=k % 2)

    def wait_rows(buf):
        for k in range(TOP_K):
            pltpu.make_async_copy(ys_hbm.at[pl.ds(0, tb * slab)], rows_v.at[buf, k], sem_r.at[buf]).wait()

    @pl.when(i == 0)
    def _():
        slot_copy(0).start()
        slot_copy(0).wait()

        def first(t, carry):
            request_token(0, t)
            return carry

        lax.fori_loop(0, tb, first, 0, unroll=2)

        @pl.when(n > 1)
        def _():
            slot_copy(1).start()
            slot_copy(1).wait()

    @pl.when(i + 2 < n)
    def _():
        slot_copy(i + 2).start()

    wt = wt_ref[...]
    for k in range(TOP_K):
        wrep[k] = jnp.broadcast_to(wt[:, k:k + 1], (tb, V7X_LANES))

    def token(t, carry, buf, request_next):
        rows = _slab_rows(t, slab)
        lo_acc = jnp.zeros((slab, V7X_LANES), jnp.float32)
        hi_acc = jnp.zeros((slab, V7X_LANES), jnp.float32)
        for k in range(TOP_K):
            wv = jnp.broadcast_to(wrep[k, pl.ds(t, 1), :], (slab, V7X_LANES))
            lo, hi = _unpack_pair(rows_v[buf, k, rows, :])
            lo_acc = lo_acc + wv * lo
            hi_acc = hi_acc + wv * hi
        acc_lo[rows, :] = lo_acc
        acc_hi[rows, :] = hi_acc
        if request_next:
            request_token(1 - buf, t)
        return carry

    for buf in range(2):
        for request_next in (True, False):
            @pl.when((cur == buf) & ((i + 1 < n) == request_next))
            def _():
                wait_rows(buf)
                lax.fori_loop(0, tb, functools.partial(token, buf=buf, request_next=request_next),
                              0, unroll=4)

    @pl.when(i + 2 < n)
    def _():
        slot_copy(i + 2).wait()

    yy = ysh_ref[...].astype(jnp.float32) + jnp.concatenate(
        [_slab_load(acc_lo, tb, slab), _slab_load(acc_hi, tb, slab)], axis=1)
    xo = x_ref[...] + gt_ref[0] * yy
    ms = jnp.mean(xo * xo, axis=-1, keepdims=True)
    o_ref[...] = xo * lax.rsqrt(ms + EPS) * gf_ref[...]


def _combine(x1, y_shared, w_t, mod3, gate_idx, g_final, slot3, ys, seq):
    t, d = x1.shape
    tb = COMBINE_TILE
    assert MOE_TILE % tb == 0 and seq % tb == 0
    per_b = seq // tb
    half = d // 2
    return pl.pallas_call(
        _combine_kernel,
        grid=(t // tb,),
        in_specs=[pl.BlockSpec((tb, d), lambda i: (i, 0)),
                  pl.BlockSpec((tb, d), lambda i: (i, 0)),
                  pl.BlockSpec((tb, TOP_K), lambda i: (i, 0)),
                  pl.BlockSpec((1, 1, d), lambda i: ((i // per_b) * 6 + gate_idx, 0, 0)),
                  pl.BlockSpec((1, d), lambda i: (0, 0)),
                  pl.BlockSpec(memory_space=pl.ANY),
                  pl.BlockSpec(memory_space=pl.ANY)],
        out_specs=pl.BlockSpec((tb, d), lambda i: (i, 0)),
        scratch_shapes=[pltpu.SMEM((2, TOP_K, tb), jnp.int32),
                        pltpu.VMEM((2, TOP_K, tb * half // V7X_LANES, V7X_LANES), jnp.int32),
                        pltpu.VMEM((TOP_K, tb, V7X_LANES), jnp.float32),
                        pltpu.VMEM((tb * half // V7X_LANES, V7X_LANES), jnp.float32),
                        pltpu.VMEM((tb * half // V7X_LANES, V7X_LANES), jnp.float32),
                        pltpu.SemaphoreType.DMA((2,)),
                        pltpu.SemaphoreType.DMA((2,))],
        out_shape=jax.ShapeDtypeStruct((t, d), jnp.float32),
        compiler_params=_params(("arbitrary",)),
        name="combine",
    )(x1, y_shared, w_t, mod3, g_final.reshape(1, d), slot3, ys)


def _in_layout(d):
    qw = ATT_HEADS * ATT_HEAD_DIM
    kvw = ATT_KV_HEADS * ATT_HEAD_DIM
    rqk = RET_HEADS * RET_QK_DIM
    rv = RET_HEADS * RET_V_DIM
    order = [("qa", qw), ("ka", kvw), ("va", kvw), ("qr", rqk), ("kr", rqk),
             ("vr", rv), ("gr", rv), ("ga", d), ("gb", d)]
    dst = {}
    off = 0
    for name, width in order:
        dst[name] = off
        off += width
    return dst


def kernel(x, c, w_ada, b_ada, g_norm_mix, w_in, attn_sinks, w_attn_out, w_ret_out, w_o, g_norm_ffn,
           w_router, b_router, w_gate, w_up, w_down, w_sh_gate, w_sh_up, w_sh_down, g_norm_final):
    b, s, d = x.shape
    t = b * s
    depth = w_ada.shape[0]
    bf = jnp.bfloat16
    dst = _in_layout(d)

    c_pad = jnp.zeros((V7X_SUBLANES, d), jnp.float32).at[:b].set(c)
    x2 = x.reshape(t, d)
    for l in range(depth):
        mod = _ada(c_pad, w_ada[l], b_ada[l])
        mod3 = mod[:b].reshape(b * 6, 1, d)

        h = _norm_mod(x2.reshape(b, s, d), g_norm_mix[l], mod3, 0, 1)
        proj, (wg_b, wu_b, wd_b) = _in_proj(h.reshape(t, d), w_in[l].astype(bf),
                                            ((w_gate[l], w_up[l], w_down[l]),), 1024, 768)
        wao_b, wro_b, wo_b = w_attn_out[l].astype(bf), w_ret_out[l].astype(bf), w_o[l].astype(bf)
        wsg_b, wsu_b, wsd_b = w_sh_gate[l].astype(bf), w_sh_up[l].astype(bf), w_sh_down[l].astype(bf)
        proj3 = proj.reshape(b, s, proj.shape[1])
        attn = _attention(proj3, attn_sinks[l], dst["qa"], dst["ka"], dst["va"])
        ret = _retention(proj3, dst["qr"], dst["kr"], dst["vr"], dst["gr"])
        mix = _mix(attn.reshape(t, -1), ret.reshape(t, -1), wao_b, wro_b, proj, dst["ga"], dst["gb"])
        x1 = _out_resid(mix, wo_b, x2, mod3, 2, s)

        h2, h2p, idx3, pos3, w3, cnt = _router(x1, g_norm_ffn[l], mod3, 3, 4,
                                               w_router[l].T.astype(bf), b_router[l], s)
        counts = cnt[:, 0]
        tile = MOE_TILE
        padded = (counts + tile - 1) // tile * tile
        pad_end = jnp.cumsum(padded).astype(jnp.int32)
        pad_start = (pad_end - padded).astype(jnp.int32)
        n_blocks = (t * TOP_K) // tile + N_EXPERTS
        n_used = (pad_end[-1] // tile).reshape(1).astype(jnp.int32)
        blk_first = jnp.arange(n_blocks, dtype=jnp.int32) * tile
        blk_e = jnp.minimum(jnp.sum((pad_end[None, :] <= blk_first[:, None]).astype(jnp.int32), axis=1),
                            N_EXPERTS - 1)
        slab = d // 2 // V7X_LANES
        slot3 = _slots(pad_start, idx3, pos3)
        xs, y_shared = _dispatch(pad_start, pad_end, h2p, h2, wsg_b, wsu_b, wsd_b, slot3,
                                 n_blocks * tile, slab)
        ys = _experts(blk_e, n_used, xs, wg_b, wu_b, wd_b, slab)
        w_t = w3.transpose(0, 2, 1).reshape(t, TOP_K)
        is_last = l == depth - 1
        assert is_last, "the final norm is fused into the last layer's combine"
        x2 = _combine(x1, y_shared, w_t, mod3, 5, g_norm_final, slot3, ys, s)
    return x2.reshape(b, s, d)
```

```python
import functools
import math

import jax
import jax.numpy as jnp
from jax import lax
from jax.experimental import pallas as pl
from jax.experimental.pallas import tpu as pltpu

ATT_HEADS = 32
ATT_KV_HEADS = 4
ATT_HEAD_DIM = 64
WINDOW = 128
ATT_BLOCK = 128
RET_HEADS = 8
RET_QK_DIM = 256
RET_V_DIM = 512
RET_CHUNK = 128
N_EXPERTS = 64
N_GROUPS = 8
TOPK_GROUPS = 4
TOP_K = 8
ROUTED_SCALE = 2.5
EPS = 1e-6

V7X_LANES = 128
V7X_SUBLANES = 8
V7X_VMEM_LIMIT_BYTES = 60000 * 1024

MOE_TILE = 256
COMBINE_TILE = 128
N_IN_TILES = 3
ATT_STEP_BLOCKS = 2
NEG_BIG = -1e30


def _div_block(n, target, align):
    best = None
    b = align
    while b <= min(n, target):
        if n % b == 0:
            best = b
        b += align
    assert best is not None, (n, target, align)
    return best


def _params(semantics):
    return pltpu.CompilerParams(dimension_semantics=semantics,
                                vmem_limit_bytes=V7X_VMEM_LIMIT_BYTES)


def _sigmoid(v):
    return 1.0 / (1.0 + jnp.exp(-v))


def _silu(v):
    return v * _sigmoid(v)


def _pack_pair(lo, hi):
    return pltpu.pack_elementwise([lo, hi], packed_dtype=jnp.bfloat16)


def _unpack_pair(p):
    lo = pltpu.unpack_elementwise(p, index=0, packed_dtype=jnp.bfloat16, unpacked_dtype=jnp.float32)
    hi = pltpu.unpack_elementwise(p, index=1, packed_dtype=jnp.bfloat16, unpacked_dtype=jnp.float32)
    return lo, hi


def _slab_load(ref, n_rows, slab):
    return jnp.concatenate([ref[pl.ds(s, n_rows, stride=slab), :] for s in range(slab)], axis=1)


def _slab_store(ref, val, n_rows, slab):
    for s in range(slab):
        ref[pl.ds(s, n_rows, stride=slab), :] = val[:, s * V7X_LANES:(s + 1) * V7X_LANES]


def _slab_rows(r, slab):
    return pl.ds(pl.multiple_of(r * slab, slab), slab)


def _ada_kernel(c_ref, w_ref, b_ref, o_ref):
    cs = _silu(c_ref[...]).astype(jnp.bfloat16)
    o_ref[...] = jnp.dot(cs, w_ref[...].astype(jnp.bfloat16),
                         preferred_element_type=jnp.float32) + b_ref[...]


def _ada(c_pad, w, b):
    m, d = c_pad.shape
    n = w.shape[1]
    tn = _div_block(n, 1024, V7X_LANES)
    return pl.pallas_call(
        _ada_kernel,
        grid=(n // tn,),
        in_specs=[pl.BlockSpec((m, d), lambda j: (0, 0)),
                  pl.BlockSpec((d, tn), lambda j: (0, j)),
                  pl.BlockSpec((1, tn), lambda j: (0, j))],
        out_specs=pl.BlockSpec((m, tn), lambda j: (0, j)),
        out_shape=jax.ShapeDtypeStruct((m, n), jnp.float32),
        compiler_params=_params(("parallel",)),
        name="ada",
    )(c_pad, w, b.reshape(1, n))


def _norm_mod_kernel(x_ref, g_ref, sh_ref, sc_ref, o_ref):
    x = x_ref[0]
    ms = jnp.mean(x * x, axis=-1, keepdims=True)
    y = x * lax.rsqrt(ms + EPS) * g_ref[...]
    o_ref[0] = (y * (1.0 + sc_ref[0]) + sh_ref[0]).astype(o_ref.dtype)


def _norm_mod(x3, g, mod3, shift_idx, scale_idx):
    b, s, d = x3.shape
    ts = _div_block(s, 512, V7X_SUBLANES)
    return pl.pallas_call(
        _norm_mod_kernel,
        grid=(b, s // ts),
        in_specs=[pl.BlockSpec((1, ts, d), lambda bi, i: (bi, i, 0)),
                  pl.BlockSpec((1, d), lambda bi, i: (0, 0)),
                  pl.BlockSpec((1, 1, d), lambda bi, i: (bi * 6 + shift_idx, 0, 0)),
                  pl.BlockSpec((1, 1, d), lambda bi, i: (bi * 6 + scale_idx, 0, 0))],
        out_specs=pl.BlockSpec((1, ts, d), lambda bi, i: (bi, i, 0)),
        out_shape=jax.ShapeDtypeStruct((b, s, d), jnp.bfloat16),
        compiler_params=_params(("parallel", "parallel")),
        name="norm_mod",
    )(x3, g.reshape(1, d), mod3, mod3)


def _in_proj_kernel(a_ref, b_ref, *rest, windows, n_j):
    n_side = len(windows)
    src = rest[:n_side]
    o_ref = rest[n_side]
    dst = rest[n_side + 1:]
    o_ref[...] = jnp.dot(a_ref[...], b_ref[...],
                         preferred_element_type=jnp.float32).astype(o_ref.dtype)
    step = pl.program_id(0) * n_j + pl.program_id(1)

    for s_ref, d_ref, (first, count) in zip(src, dst, windows):
        @pl.when((step >= first) & (step < first + count))
        def _():
            d_ref[...] = s_ref[...].astype(d_ref.dtype)


def _in_proj(a, b, side_groups, tm_target, tn_target):
    m, k = a.shape
    n = b.shape[1]
    tm = _div_block(m, tm_target, V7X_SUBLANES)
    tn = _div_block(n, tn_target, V7X_LANES)
    n_i, n_j = m // tm, n // tn
    steps_left = n_i * n_j
    first = 0
    side, chunked, windows = [], [], []
    for group in side_groups:
        n_max = 1 << (steps_left.bit_length() - 1)
        used = 0
        for w in group:
            rows = w.size // w.shape[-1]
            count = n_max
            while rows % (count * 2 * V7X_SUBLANES):
                count //= 2
            side.append(w)
            chunked.append(w.reshape(count, rows // count, w.shape[-1]))
            windows.append((first, count))
            used = max(used, count)
        first += used
        steps_left -= used

    def side_spec(c, window):
        w_first, count = window
        return pl.BlockSpec((1,) + c.shape[1:],
                            lambda i, j: (jnp.clip(i * n_j + j - w_first, 0, count - 1), 0, 0))

    side_specs = [side_spec(c, w) for c, w in zip(chunked, windows)]
    outs = pl.pallas_call(
        functools.partial(_in_proj_kernel, windows=tuple(windows), n_j=n_j),
        grid=(n_i, n_j),
        in_specs=[pl.BlockSpec((tm, k), lambda i, j: (i, 0)),
                  pl.BlockSpec((k, tn), lambda i, j: (0, j))] + side_specs,
        out_specs=[pl.BlockSpec((tm, tn), lambda i, j: (i, j))] + side_specs,
        out_shape=[jax.ShapeDtypeStruct((m, n), jnp.bfloat16)]
        + [jax.ShapeDtypeStruct(c.shape, jnp.bfloat16) for c in chunked],
        compiler_params=_params(("arbitrary", "arbitrary")),
        name="in_proj",
    )(a, b, *chunked)
    return outs[0], [o.reshape(w.shape) for o, w in zip(outs[1:], side)]


def _attn_kernel(sink_ref, q_ref, kc_ref, kp_ref, vc_ref, vp_ref, bias_ref, o_ref):
    i = pl.program_id(1)
    for sb in range(ATT_STEP_BLOCKS):
        _attn_block(sb, i, sink_ref, q_ref, kc_ref, kp_ref, vc_ref, vp_ref, bias_ref, o_ref)


def _attn_block(sb, i, sink_ref, q_ref, kc_ref, kp_ref, vc_ref, vp_ref, bias_ref, o_ref):
    blk = ATT_BLOCK
    hd = ATT_HEAD_DIM
    group = ATT_HEADS // ATT_KV_HEADS
    pairs = group // 2
    rows = slice(sb * blk, (sb + 1) * blk)
    prow = slice((sb - 1) * blk, sb * blk)
    nt = (((1,), (1,)), ((), ()))
    tn = (((0,), (0,)), ((), ()))
    zpad = jnp.zeros((2 * blk, hd), jnp.bfloat16)

    def scores(h, par):
        sl = slice(h * hd, (h + 1) * hd)
        k_prev = kp_ref[0, :, sl] if sb == 0 else kc_ref[0, prow, sl]
        k2 = jnp.concatenate([k_prev, kc_ref[0, rows, sl]], axis=0) * (hd ** -0.5)
        qp = jnp.concatenate([q_ref[0, rows, (h * pairs + p) * 2 * hd:(h * pairs + p + 1) * 2 * hd]
                              for p in range(pairs)], axis=0)
        kz = jnp.concatenate([k2, zpad] if par == 0 else [zpad, k2], axis=1)
        return lax.dot_general(kz, qp, nt, preferred_element_type=jnp.float32)

    def softmax(s, h, par):
        s = s + bias_ref[h * 2 + par]
        if sb == 0:
            s = jnp.concatenate([jnp.where(i == 0, NEG_BIG, s[:blk]), s[blk:]], axis=0)
        sink = jnp.concatenate([jnp.full((1, blk), sink_ref[h * group + 2 * p + par], jnp.float32)
                                for p in range(pairs)], axis=1)
        m = jnp.maximum(s.max(0, keepdims=True), sink)
        pr = jnp.exp(s - m)
        denom = pr.sum(0, keepdims=True) + jnp.exp(sink - m)
        return pr.astype(jnp.bfloat16), 1.0 / denom

    def values(pr, inv, h, par):
        sl = slice(h * hd, (h + 1) * hd)
        v_prev = vp_ref[0, :, sl] if sb == 0 else vc_ref[0, prow, sl]
        v2 = jnp.concatenate([v_prev, vc_ref[0, rows, sl]], axis=0)
        vz = jnp.concatenate([v2, zpad] if par == 0 else [zpad, v2], axis=1)
        return lax.dot_general(vz, pr, tn, preferred_element_type=jnp.float32) * inv

    items = [(h, par) for h in range(ATT_KV_HEADS) for par in range(2)]
    s_of, p_of, acc = {}, {}, {}
    for n in range(len(items) + 2):
        if n < len(items):
            s_of[n] = scores(*items[n])
        if 1 <= n <= len(items):
            p_of[n - 1] = softmax(s_of.pop(n - 1), *items[n - 1])
        if n >= 2:
            h, par = items[n - 2]
            o = values(*p_of.pop(n - 2), h, par)
            acc[h] = o if par == 0 else acc[h] + o
            if par == 1:
                out = acc.pop(h)
                for p in range(pairs):
                    o_ref[0, rows, (h * pairs + p) * 2 * hd:(h * pairs + p + 1) * 2 * hd] = (
                        out[:, p * blk:(p + 1) * blk].T.astype(o_ref.dtype))


def _attn_bias():
    blk = ATT_BLOCK
    group = ATT_HEADS // ATT_KV_HEADS
    pairs = group // 2
    qi = jnp.arange(blk)[:, None]
    kj = jnp.arange(2 * blk)[None, :]
    dist = qi + blk - kj
    valid = (dist >= 0) & (dist < WINDOW)
    slopes = jnp.exp2(-8.0 * jnp.arange(1, ATT_HEADS + 1, dtype=jnp.float32) / ATT_HEADS)
    slopes = slopes.reshape(ATT_KV_HEADS, pairs, 2)
    bias = jnp.where(valid, -slopes[..., None, None] * dist.astype(jnp.float32), NEG_BIG)
    return bias.transpose(0, 2, 4, 1, 3).reshape(ATT_KV_HEADS * 2, 2 * blk, pairs * blk)


def _attention(proj3, sinks, q_off, k_off, v_off):
    b, s, _ = proj3.shape
    qw = ATT_HEADS * ATT_HEAD_DIM
    kvw = ATT_KV_HEADS * ATT_HEAD_DIM
    step_rows = ATT_STEP_BLOCKS * ATT_BLOCK
    assert s % step_rows == 0
    nb = s // step_rows
    group = ATT_HEADS // ATT_KV_HEADS
    assert q_off % qw == 0 and k_off % kvw == 0 and v_off % kvw == 0
    assert group % 2 == 0 and 2 * ATT_HEAD_DIM == V7X_LANES and WINDOW == ATT_BLOCK
    assert 4 ** round(math.log(ATT_HEAD_DIM, 4)) == ATT_HEAD_DIM, "score scale must be a power of two"
    qb, kb, vb = q_off // qw, k_off // kvw, v_off // kvw
    bias = _attn_bias()
    return pl.pallas_call(
        _attn_kernel,
        grid_spec=pltpu.PrefetchScalarGridSpec(
            num_scalar_prefetch=1,
            grid=(b, nb),
            in_specs=[pl.BlockSpec((1, step_rows, qw), lambda bi, i, sk: (bi, i, qb)),
                      pl.BlockSpec((1, step_rows, kvw), lambda bi, i, sk: (bi, i, kb)),
                      pl.BlockSpec((1, ATT_BLOCK, kvw),
                                   lambda bi, i, sk: (bi, jnp.maximum(ATT_STEP_BLOCKS * i - 1, 0), kb)),
                      pl.BlockSpec((1, step_rows, kvw), lambda bi, i, sk: (bi, i, vb)),
                      pl.BlockSpec((1, ATT_BLOCK, kvw),
                                   lambda bi, i, sk: (bi, jnp.maximum(ATT_STEP_BLOCKS * i - 1, 0), vb)),
                      pl.BlockSpec(bias.shape, lambda bi, i, sk: (0, 0, 0), pipeline_mode=pl.Buffered(1))],
            out_specs=pl.BlockSpec((1, step_rows, qw), lambda bi, i, sk: (bi, i, 0)),
        ),
        out_shape=jax.ShapeDtypeStruct((b, s, qw), jnp.bfloat16),
        compiler_params=_params(("parallel", "parallel")),
        name="attention",
    )(sinks, proj3, proj3, proj3, proj3, proj3, bias)


def _ret_kernel(q_ref, k_ref, v_ref, gr_ref, mask_ref, qd_ref, kd_ref, cd_ref, o_ref, state_ref):
    c = pl.program_id(1)

    @pl.when(c == 0)
    def _():
        state_ref[...] = jnp.zeros_like(state_ref)

    nt = (((1,), (1,)), ((), ()))
    tn = (((0,), (0,)), ((), ()))
    def decayed(bi):
        q = q_ref[bi]
        k = k_ref[bi]
        attn = lax.dot_general(q, k, nt, preferred_element_type=jnp.float32) * mask_ref[0]
        kd = (k.astype(jnp.float32) * kd_ref[0]).astype(k.dtype)
        return attn.astype(jnp.bfloat16), kd

    def recur(bi, attn, kd):
        q = q_ref[bi]
        v = v_ref[bi]
        intra = jnp.dot(attn, v, preferred_element_type=jnp.float32)
        state = state_ref[bi]
        inter = jnp.dot(q, state.astype(q.dtype), preferred_element_type=jnp.float32) * qd_ref[0]
        state_ref[bi] = state * cd_ref[0] + lax.dot_general(kd, v, tn, preferred_element_type=jnp.float32)
        return intra + inter

    def finish(bi, o):
        mu = jnp.mean(o, axis=-1, keepdims=True)
        oc = o - mu
        var = jnp.mean(oc * oc, axis=-1, keepdims=True)
        y = oc * lax.rsqrt(var + EPS)
        o_ref[bi] = (_silu(gr_ref[bi].astype(jnp.float32)) * y).astype(o_ref.dtype)

    nb = q_ref.shape[0]
    a_of, o_of = {}, {}
    for n in range(nb + 2):
        if n < nb:
            a_of[n] = decayed(n)
        if 1 <= n <= nb:
            o_of[n - 1] = recur(n - 1, *a_of.pop(n - 1))
        if n >= 2:
            finish(n - 2, o_of.pop(n - 2))


def _retention(proj3, q_off, k_off, v_off, g_off):
    b, s, _ = proj3.shape
    dk, dv, ch = RET_QK_DIM, RET_V_DIM, RET_CHUNK
    assert q_off % dk == 0 and k_off % dk == 0 and v_off % dv == 0 and g_off % dv == 0
    qb, kb, vb, gb = q_off // dk, k_off // dk, v_off // dv, g_off // dv
    n = s // ch
    log_g = jnp.log1p(-jnp.exp2(-5.0 - jnp.arange(RET_HEADS, dtype=jnp.float32)))
    pos = jnp.arange(ch, dtype=jnp.float32)
    rel = pos[:, None] - pos[None, :]
    scale = dk ** -0.5
    mask = jnp.where(rel[None] >= 0, jnp.exp(rel[None] * log_g[:, None, None]), 0.0) * scale
    q_decay = jnp.exp((pos[None, :, None] + 1.0) * log_g[:, None, None])
    k_decay = jnp.exp((ch - 1.0 - pos[None, :, None]) * log_g[:, None, None]) * scale
    c_decay = jnp.exp(ch * log_g)[:, None, None]
    return pl.pallas_call(
        _ret_kernel,
        grid=(RET_HEADS, n),
        in_specs=[pl.BlockSpec((b, ch, dk), lambda h, c: (0, c, qb + h)),
                  pl.BlockSpec((b, ch, dk), lambda h, c: (0, c, kb + h)),
                  pl.BlockSpec((b, ch, dv), lambda h, c: (0, c, vb + h)),
                  pl.BlockSpec((b, ch, dv), lambda h, c: (0, c, gb + h)),
                  pl.BlockSpec((1, ch, ch), lambda h, c: (h, 0, 0)),
                  pl.BlockSpec((1, ch, 1), lambda h, c: (h, 0, 0)),
                  pl.BlockSpec((1, ch, 1), lambda h, c: (h, 0, 0)),
                  pl.BlockSpec((1, 1, 1), lambda h, c: (h, 0, 0))],
        out_specs=pl.BlockSpec((b, ch, dv), lambda h, c: (0, c, h)),
        out_shape=jax.ShapeDtypeStruct((b, s, RET_HEADS * dv), jnp.bfloat16),
        scratch_shapes=[pltpu.VMEM((b, dk, dv), jnp.float32)],
        compiler_params=_params(("parallel", "arbitrary")),
        name="retention",
    )(proj3, proj3, proj3, proj3, mask, q_decay, k_decay, c_decay)


def _mix_kernel(a_ref, r_ref, wa_ref, wr_ref, ga_ref, gb_ref, o_ref):
    ya = jnp.dot(a_ref[...], wa_ref[...], preferred_element_type=jnp.float32)
    yr = jnp.dot(r_ref[...], wr_ref[...], preferred_element_type=jnp.float32)
    ga = _sigmoid(ga_ref[...].astype(jnp.float32))
    gb = _sigmoid(gb_ref[...].astype(jnp.float32))
    o_ref[...] = (ga * ya + gb * yr).astype(o_ref.dtype)


def _mix(attn2, ret2, wa, wr, proj2, ga_off, gb_off):
    m, ka = attn2.shape
    kr = ret2.shape[1]
    d = wa.shape[1]
    tm = _div_block(m, 1024, V7X_SUBLANES)
    tn = _div_block(d, 512, V7X_LANES)
    assert ga_off % tn == 0 and gb_off % tn == 0
    gab, gbb = ga_off // tn, gb_off // tn
    return pl.pallas_call(
        _mix_kernel,
        grid=(m // tm, d // tn),
        in_specs=[pl.BlockSpec((tm, ka), lambda i, j: (i, 0)),
                  pl.BlockSpec((tm, kr), lambda i, j: (i, 0)),
                  pl.BlockSpec((ka, tn), lambda i, j: (0, j)),
                  pl.BlockSpec((kr, tn), lambda i, j: (0, j)),
                  pl.BlockSpec((tm, tn), lambda i, j: (i, gab + j)),
                  pl.BlockSpec((tm, tn), lambda i, j: (i, gbb + j))],
        out_specs=pl.BlockSpec((tm, tn), lambda i, j: (i, j)),
        out_shape=jax.ShapeDtypeStruct((m, d), jnp.bfloat16),
        compiler_params=_params(("parallel", "parallel")),
        name="mix",
    )(attn2, ret2, wa, wr, proj2, proj2)


def _resid_kernel(a_ref, w_ref, x_ref, gt_ref, o_ref):
    y = jnp.dot(a_ref[...], w_ref[...], preferred_element_type=jnp.float32)
    o_ref[...] = x_ref[...] + gt_ref[0] * y


def _out_resid(mix2, w, x2, mod3, gate_idx, seq):
    m, k = mix2.shape
    d = w.shape[1]
    tm = _div_block(seq, 1024, V7X_SUBLANES)
    tn = _div_block(d, 1024, V7X_LANES)
    per_b = seq // tm
    return pl.pallas_call(
        _resid_kernel,
        grid=(m // tm, d // tn),
        in_specs=[pl.BlockSpec((tm, k), lambda i, j: (i, 0)),
                  pl.BlockSpec((k, tn), lambda i, j: (0, j)),
                  pl.BlockSpec((tm, tn), lambda i, j: (i, j)),
                  pl.BlockSpec((1, 1, tn), lambda i, j: ((i // per_b) * 6 + gate_idx, 0, j))],
        out_specs=pl.BlockSpec((tm, tn), lambda i, j: (i, j)),
        out_shape=jax.ShapeDtypeStruct((m, d), jnp.float32),
        compiler_params=_params(("parallel", "parallel")),
        name="out_resid",
    )(mix2, w, x2, mod3)


def _router_kernel(x_ref, g_ref, sh_ref, sc_ref, wr_ref, br_ref,
                   h_ref, hp_ref, idx_ref, pos_ref, w_ref, cnt_ref, carry_ref):
    i = pl.program_id(0)
    e = N_EXPERTS
    per_g = e // N_GROUPS
    tb = x_ref.shape[0]

    @pl.when(i == 0)
    def _():
        carry_ref[...] = jnp.zeros_like(carry_ref)

    x = x_ref[...]
    ms = jnp.mean(x * x, axis=-1, keepdims=True)
    h = x * lax.rsqrt(ms + EPS) * g_ref[...]
    h = h * (1.0 + sc_ref[0]) + sh_ref[0]
    hb = h.astype(jnp.bfloat16)
    h_ref[...] = hb
    half = h.shape[1] // 2
    _slab_store(hp_ref, _pack_pair(h[:, :half], h[:, half:]), tb, half // V7X_LANES)

    nt = (((1,), (1,)), ((), ()))
    logits = lax.dot_general(wr_ref[...], hb, nt, preferred_element_type=jnp.float32)
    scores = _sigmoid(logits)
    choice = scores + br_ref[...]

    c3 = choice.reshape(N_GROUPS, per_g, tb)
    j_iota = lax.broadcasted_iota(jnp.int32, c3.shape, 1).astype(jnp.float32)
    m1 = c3.max(axis=1, keepdims=True)
    first = jnp.min(jnp.where(c3 == m1, j_iota, float(per_g)), axis=1, keepdims=True)
    m2 = jnp.where(j_iota == first, -jnp.inf, c3).max(axis=1, keepdims=True)
    gs = (m1 + m2).reshape(N_GROUPS, tb)

    g_iota = lax.broadcasted_iota(jnp.int32, gs.shape, 0)
    grank = jnp.zeros(gs.shape, jnp.int32)
    for gp in range(N_GROUPS):
        row = gs[gp:gp + 1, :]
        ahead = (row > gs) | ((row == gs) & (gp < g_iota))
        grank = grank + ahead.astype(jnp.int32)
    gmask = grank < TOPK_GROUPS
    emask = jnp.broadcast_to(gmask.reshape(N_GROUPS, 1, tb), c3.shape).reshape(e, tb)
    masked = jnp.where(emask, choice, -jnp.inf)

    e_iota = lax.broadcasted_iota(jnp.int32, masked.shape, 0)
    e_pos = e_iota.astype(jnp.float32)
    work = masked
    sel = jnp.zeros(masked.shape, jnp.bool_)
    for _ in range(TOP_K):
        top = work.max(axis=0, keepdims=True)
        first_e = jnp.min(jnp.where(work == top, e_pos, float(e)), axis=0, keepdims=True)
        hit = e_pos == first_e
        sel = sel | hit
        work = jnp.where(hit, -jnp.inf, work)
    self32 = sel.astype(jnp.float32)

    wsel = scores * self32
    wn = wsel / jnp.sum(wsel, axis=0, keepdims=True) * ROUTED_SCALE

    selb = self32.astype(jnp.bfloat16)
    t_r = lax.broadcasted_iota(jnp.int32, (tb, tb), 0)
    t_c = lax.broadcasted_iota(jnp.int32, (tb, tb), 1)
    upper = (t_r <= t_c).astype(jnp.bfloat16)
    incl = jnp.dot(selb, upper, preferred_element_type=jnp.float32)
    carry = carry_ref[...]
    rank_in_e = carry + incl - 1.0
    carry_new = carry + jnp.sum(self32, axis=1, keepdims=True)
    carry_ref[...] = carry_new
    cnt_ref[...] = jnp.broadcast_to(carry_new, cnt_ref.shape).astype(jnp.int32)

    e_r = lax.broadcasted_iota(jnp.int32, (e, e), 0)
    e_c = lax.broadcasted_iota(jnp.int32, (e, e), 1)
    lower = (e_c < e_r).astype(jnp.bfloat16)
    before = jnp.dot(lower, selb, preferred_element_type=jnp.float32)
    e_f = e_iota.astype(jnp.float32)
    idx_rows, pos_rows, w_rows = [], [], []
    for k in range(TOP_K):
        hit = jnp.where(sel & (before == float(k)), 1.0, 0.0)
        idx_rows.append(jnp.sum(hit * e_f, axis=0, keepdims=True))
        pos_rows.append(jnp.sum(hit * rank_in_e, axis=0, keepdims=True))
        w_rows.append(jnp.sum(hit * wn, axis=0, keepdims=True))
    idx_ref[0] = jnp.concatenate(idx_rows, axis=0).astype(jnp.int32)
    pos_ref[0] = jnp.concatenate(pos_rows, axis=0).astype(jnp.int32)
    w_ref[0] = jnp.concatenate(w_rows, axis=0)


def _router(x1, g, mod3, shift_idx, scale_idx, w_router_t, b_router, seq):
    t, d = x1.shape
    e = N_EXPERTS
    tb = MOE_TILE
    assert seq % tb == 0 and d % (2 * V7X_LANES * V7X_SUBLANES) == 0
    slab = d // 2 // V7X_LANES
    per_b = seq // tb
    nt = t // tb
    return pl.pallas_call(
        _router_kernel,
        grid=(nt,),
        in_specs=[pl.BlockSpec((tb, d), lambda i: (i, 0)),
                  pl.BlockSpec((1, d), lambda i: (0, 0)),
                  pl.BlockSpec((1, 1, d), lambda i: ((i // per_b) * 6 + shift_idx, 0, 0)),
                  pl.BlockSpec((1, 1, d), lambda i: ((i // per_b) * 6 + scale_idx, 0, 0)),
                  pl.BlockSpec((e, d), lambda i: (0, 0)),
                  pl.BlockSpec((e, 1), lambda i: (0, 0))],
        out_specs=[pl.BlockSpec((tb, d), lambda i: (i, 0)),
                   pl.BlockSpec((tb * slab, V7X_LANES), lambda i: (i, 0)),
                   pl.BlockSpec((1, TOP_K, tb), lambda i: (i, 0, 0)),
                   pl.BlockSpec((1, TOP_K, tb), lambda i: (i, 0, 0)),
                   pl.BlockSpec((1, TOP_K, tb), lambda i: (i, 0, 0)),
                   pl.BlockSpec((e, V7X_LANES), lambda i: (0, 0))],
        out_shape=[jax.ShapeDtypeStruct((t, d), jnp.bfloat16),
                   jax.ShapeDtypeStruct((t * slab, V7X_LANES), jnp.int32),
                   jax.ShapeDtypeStruct((nt, TOP_K, tb), jnp.int32),
                   jax.ShapeDtypeStruct((nt, TOP_K, tb), jnp.int32),
                   jax.ShapeDtypeStruct((nt, TOP_K, tb), jnp.float32),
                   jax.ShapeDtypeStruct((e, V7X_LANES), jnp.int32)],
        scratch_shapes=[pltpu.VMEM((e, 1), jnp.float32)],
        compiler_params=_params(("arbitrary",)),
        name="router",
    )(x1, g.reshape(1, d), mod3, mod3, w_router_t, b_router.reshape(e, 1))


def _slots_kernel(pstart_ref, idx_ref, pos_ref, o_ref):
    idx = idx_ref[...]
    base = jnp.zeros(idx.shape, jnp.int32)
    for ex in range(N_EXPERTS):
        base = jnp.where(idx == ex, pstart_ref[ex], base)
    o_ref[...] = base + pos_ref[...]


def _slots(pad_start, idx3, pos3):
    nt = idx3.shape[0]
    per_step = _div_block(nt, 8, 1)
    blk = (per_step,) + idx3.shape[1:]
    spec = pl.BlockSpec(blk, lambda i, ps: (i, 0, 0))
    return pl.pallas_call(
        _slots_kernel,
        grid_spec=pltpu.PrefetchScalarGridSpec(num_scalar_prefetch=1, grid=(nt // per_step,),
                                               in_specs=[spec, spec], out_specs=spec),
        out_shape=jax.ShapeDtypeStruct(idx3.shape, jnp.int32),
        compiler_params=_params(("parallel",)),
        name="slots",
    )(pad_start, idx3, pos3)


def _dispatch_kernel(pstart_ref, pend_ref, hp_ref, h_ref, wsg_ref, wsu_ref, wsd_ref, slot_hbm,
                     xs_hbm, ysh_ref, slot_s, zero_v, sem_i, sem_z, sem_r, *, slab):
    i = pl.program_id(0)
    tb = hp_ref.shape[0] // slab

    def slot_copy():
        return pltpu.make_async_copy(slot_hbm.at[i], slot_s, sem_i)

    slot_copy().start()

    def zero_copy(ex):
        first = pl.multiple_of((pend_ref[ex] - tb) * slab, tb * slab)
        return pltpu.make_async_copy(zero_v, xs_hbm.at[pl.ds(first, tb * slab)], sem_z)

    @pl.when(i == 0)
    def _():
        zero_v[...] = jnp.zeros_like(zero_v)

        def start(ex, carry):
            @pl.when(pend_ref[ex] > pstart_ref[ex])
            def _():
                zero_copy(ex).start()
            return carry

        def wait(ex, carry):
            @pl.when(pend_ref[ex] > pstart_ref[ex])
            def _():
                zero_copy(ex).wait()
            return carry

        lax.fori_loop(0, N_EXPERTS, start, 0)
        lax.fori_loop(0, N_EXPERTS, wait, 0)

    slot_copy().wait()

    def row_copy(t, k):
        return pltpu.make_async_copy(hp_ref.at[_slab_rows(t, slab)],
                                     xs_hbm.at[_slab_rows(slot_s[k, t], slab)], sem_r)

    def start_rows(t, carry):
        for k in range(TOP_K):
            row_copy(t, k).start(priority=k % 2)
        return carry

    lax.fori_loop(0, tb // 2, start_rows, 0, unroll=2)
    h = h_ref[...]
    g = jnp.dot(h, wsg_ref[...], preferred_element_type=jnp.float32)
    u = jnp.dot(h, wsu_ref[...], preferred_element_type=jnp.float32)
    a = (_silu(g) * u).astype(jnp.bfloat16)
    lax.fori_loop(tb // 2, tb, start_rows, 0, unroll=2)
    ysh_ref[...] = jnp.dot(a, wsd_ref[...], preferred_element_type=jnp.float32).astype(ysh_ref.dtype)

    for k in range(TOP_K):
        pltpu.make_async_copy(hp_ref, xs_hbm.at[pl.ds(0, tb * slab)], sem_r).wait()


def _dispatch(pad_start, pad_end, h2p, h2, wsg, wsu, wsd, slot3, n_rows, slab):
    tb = MOE_TILE
    t, d = h2.shape
    f = wsg.shape[1]
    once = pl.Buffered(1)
    return pl.pallas_call(
        functools.partial(_dispatch_kernel, slab=slab),
        grid_spec=pltpu.PrefetchScalarGridSpec(
            num_scalar_prefetch=2,
            grid=(t // tb,),
            in_specs=[pl.BlockSpec((tb * slab, V7X_LANES), lambda i, ps, pe: (i, 0)),
                      pl.BlockSpec((tb, d), lambda i, ps, pe: (i, 0)),
                      pl.BlockSpec((d, f), lambda i, ps, pe: (0, 0), pipeline_mode=once),
                      pl.BlockSpec((d, f), lambda i, ps, pe: (0, 0), pipeline_mode=once),
                      pl.BlockSpec((f, d), lambda i, ps, pe: (0, 0), pipeline_mode=once),
                      pl.BlockSpec(memory_space=pl.ANY)],
            out_specs=[pl.BlockSpec(memory_space=pl.ANY),
                       pl.BlockSpec((tb, d), lambda i, ps, pe: (i, 0))],
            scratch_shapes=[pltpu.SMEM((TOP_K, tb), jnp.int32),
                            pltpu.VMEM((tb * slab, V7X_LANES), jnp.int32),
                            pltpu.SemaphoreType.DMA,
                            pltpu.SemaphoreType.DMA,
                            pltpu.SemaphoreType.DMA],
        ),
        out_shape=[jax.ShapeDtypeStruct((n_rows * slab, V7X_LANES), jnp.int32),
                   jax.ShapeDtypeStruct((t, d), jnp.bfloat16)],
        compiler_params=_params(("arbitrary",)),
        name="dispatch",
    )(pad_start, pad_end, h2p, h2, wsg, wsu, wsd, slot3)


def _expert_kernel(blk_e_ref, nused_ref, first_ref, ord_ref, next_e_ref, xs_hbm, wg_hbm, wu_hbm, wd_hbm,
                   ys_hbm, xt, yt, wg_v, wu_v, wd_v, sem_in, sem_out, sem_w, *, slab):
    j = pl.program_id(0)
    n_used = nused_ref[0]
    tb = xt.shape[1]
    lanes = V7X_LANES
    half = slab * lanes

    def w_copies(e, slot):
        return [pltpu.make_async_copy(src.at[e], dst.at[slot], sem_w.at[slot])
                for src, dst in ((wg_hbm, wg_v), (wu_hbm, wu_v), (wd_hbm, wd_v))]

    def in_copies(tile, buf):
        rows = pl.ds(pl.multiple_of(tile * tb, tb), tb)
        return [pltpu.make_async_copy(xs_hbm.at[rows, s], xt.at[buf, :, pl.ds(s * lanes, lanes)],
                                      sem_in.at[buf]) for s in range(slab)]

    def out_copies(tile, buf):
        rows = pl.ds(pl.multiple_of(tile * tb, tb), tb)
        return [pltpu.make_async_copy(yt.at[buf, :, pl.ds(s * lanes, lanes)], ys_hbm.at[rows, s],
                                      sem_out.at[buf]) for s in range(slab)]

    def wait_in(buf):
        pltpu.make_async_copy(yt.at[0], xt.at[buf], sem_in.at[buf]).wait()

    def wait_out(buf):
        pltpu.make_async_copy(xt.at[0], yt.at[buf], sem_out.at[buf]).wait()

    @pl.when(j == 0)
    def _():
        for c in in_copies(0, 0) + w_copies(blk_e_ref[0], 0):
            c.start()

        @pl.when(n_used > 1)
        def _():
            for c in in_copies(1, 1):
                c.start()

    @pl.when(j < n_used)
    def _():
        cur = j % 2
        cur_in = j % N_IN_TILES
        slot = ord_ref[j] % 2

        @pl.when(first_ref[j] == 1)
        def _():
            for c in w_copies(blk_e_ref[j], slot):
                c.wait()

            @pl.when(next_e_ref[j] >= 0)
            def _():
                for c in w_copies(next_e_ref[j], 1 - slot):
                    c.start()

        @pl.when(j + 2 < n_used)
        def _():
            for c in in_copies(j + 2, (j + 2) % N_IN_TILES):
                c.start()

        wait_in(cur_in)

        @pl.when(j >= 2)
        def _():
            wait_out(cur)

        lo, hi = _unpack_pair(xt[cur_in])
        lo = lo.astype(jnp.bfloat16)
        hi = hi.astype(jnp.bfloat16)
        g = (jnp.dot(lo, wg_v[slot, :half, :], preferred_element_type=jnp.float32)
             + jnp.dot(hi, wg_v[slot, half:, :], preferred_element_type=jnp.float32))
        u = (jnp.dot(lo, wu_v[slot, :half, :], preferred_element_type=jnp.float32)
             + jnp.dot(hi, wu_v[slot, half:, :], preferred_element_type=jnp.float32))
        a = (_silu(g) * u).astype(jnp.bfloat16)
        y = jnp.dot(a, wd_v[slot], preferred_element_type=jnp.float32)
        yt[cur] = _pack_pair(y[:, :half], y[:, half:])
        for c in out_copies(j, cur):
            c.start(priority=1)

        @pl.when(j == n_used - 1)
        def _():
            @pl.when(j >= 1)
            def _():
                wait_out(1 - cur)

            wait_out(cur)


def _experts(blk_e, n_used, xs, wg, wu, wd, slab):
    tb = MOE_TILE
    p = xs.shape[0] // slab
    half = slab * V7X_LANES
    d = 2 * half
    f = wg.shape[2]
    nblk = p // tb

    tiles = jnp.arange(nblk, dtype=jnp.int32)
    used = tiles < n_used[0]
    first = used & ((tiles == 0) | (blk_e != jnp.roll(blk_e, 1)))
    ordinal = jnp.cumsum(first.astype(jnp.int32)) - 1
    first_pos = jnp.where(first, tiles, nblk)
    next_first = lax.cummin(jnp.roll(first_pos, -1).at[-1].set(nblk), reverse=True)
    next_e = jnp.where(next_first < nblk, blk_e[jnp.minimum(next_first, nblk - 1)], -1)

    any_spec = pl.BlockSpec(memory_space=pl.ANY)
    ys = pl.pallas_call(
        functools.partial(_expert_kernel, slab=slab),
        grid_spec=pltpu.PrefetchScalarGridSpec(
            num_scalar_prefetch=5,
            grid=(nblk,),
            in_specs=[any_spec, any_spec, any_spec, any_spec],
            out_specs=any_spec,
            scratch_shapes=[pltpu.VMEM((N_IN_TILES, tb, half), jnp.int32),
                            pltpu.VMEM((2, tb, half), jnp.int32),
                            pltpu.VMEM((2, d, f), jnp.bfloat16),
                            pltpu.VMEM((2, d, f), jnp.bfloat16),
                            pltpu.VMEM((2, f, d), jnp.bfloat16),
                            pltpu.SemaphoreType.DMA((N_IN_TILES,)),
                            pltpu.SemaphoreType.DMA((2,)),
                            pltpu.SemaphoreType.DMA((2,))],
        ),
        out_shape=jax.ShapeDtypeStruct((p, slab, V7X_LANES), jnp.int32),
        compiler_params=_params(("arbitrary",)),
        name="experts",
    )(blk_e, n_used, first.astype(jnp.int32), ordinal.astype(jnp.int32), next_e.astype(jnp.int32),
      xs.reshape(p, slab, V7X_LANES), wg, wu, wd)
    return ys.reshape(xs.shape)


def _combine_kernel(x_ref, ysh_ref, wt_ref, gt_ref, gf_ref,
                    slot_hbm, ys_hbm, o_ref, slot_s, rows_v, wrep, acc_lo, acc_hi, sem_i, sem_r):
    i = pl.program_id(0)
    n = pl.num_programs(0)
    tb = x_ref.shape[0]
    half = x_ref.shape[1] // 2
    slab = half // V7X_LANES
    per_tile = MOE_TILE // tb
    cur = i % 2

    def slot_copy(step):
        win = pl.ds((step % per_tile) * tb, tb)
        return pltpu.make_async_copy(slot_hbm.at[step // per_tile, :, win], slot_s.at[step % 2],
                                     sem_i.at[step % 2])

    def request_token(buf, t):
        for k in range(TOP_K):
            pltpu.make_async_copy(ys_hbm.at[_slab_rows(slot_s[buf, k, t], slab)],
                                  rows_v.at[buf, k, _slab_rows(t, slab)],
                                  sem_r.at[buf]).start(priority=k % 2)

    def wait_rows(buf):
        for k in range(TOP_K):
            pltpu.make_async_copy(ys_hbm.at[pl.ds(0, tb * slab)], rows_v.at[buf, k], sem_r.at[buf]).wait()

    @pl.when(i == 0)
    def _():
        slot_copy(0).start()
        slot_copy(0).wait()

        def first(t, carry):
            request_token(0, t)
            return carry

        lax.fori_loop(0, tb, first, 0, unroll=2)

        @pl.when(n > 1)
        def _():
            slot_copy(1).start()
            slot_copy(1).wait()

    @pl.when(i + 2 < n)
    def _():
        slot_copy(i + 2).start()

    wt = wt_ref[...]
    for k in range(TOP_K):
        wrep[k] = jnp.broadcast_to(wt[:, k:k + 1], (tb, V7X_LANES))

    def token(t, carry, buf, request_next):
        rows = _slab_rows(t, slab)
        lo_acc = jnp.zeros((slab, V7X_LANES), jnp.float32)
        hi_acc = jnp.zeros((slab, V7X_LANES), jnp.float32)
        for k in range(TOP_K):
            wv = jnp.broadcast_to(wrep[k, pl.ds(t, 1), :], (slab, V7X_LANES))
            lo, hi = _unpack_pair(rows_v[buf, k, rows, :])
            lo_acc = lo_acc + wv * lo
            hi_acc = hi_acc + wv * hi
        acc_lo[rows, :] = lo_acc
        acc_hi[rows, :] = hi_acc
        if request_next:
            request_token(1 - buf, t)
        return carry

    for buf in range(2):
        for request_next in (True, False):
            @pl.when((cur == buf) & ((i + 1 < n) == request_next))
            def _():
                wait_rows(buf)
                lax.fori_loop(0, tb, functools.partial(token, buf=buf, request_next=request_next),
                              0, unroll=4)

    @pl.when(i + 2 < n)
    def _():
        slot_copy(i + 2).wait()

    yy = ysh_ref[...].astype(jnp.float32) + jnp.concatenate(
        [_slab_load(acc_lo, tb, slab), _slab_load(acc_hi, tb, slab)], axis=1)
    xo = x_ref[...] + gt_ref[0] * yy
    ms = jnp.mean(xo * xo, axis=-1, keepdims=True)
    o_ref[...] = xo * lax.rsqrt(ms + EPS) * gf_ref[...]


def _combine(x1, y_shared, w_t, mod3, gate_idx, g_final, slot3, ys, seq):
    t, d = x1.shape
    tb = COMBINE_TILE
    assert MOE_TILE % tb == 0 and seq % tb == 0
    per_b = seq // tb
    half = d // 2
    return pl.pallas_call(
        _combine_kernel,
        grid=(t // tb,),
        in_specs=[pl.BlockSpec((tb, d), lambda i: (i, 0)),
                  pl.BlockSpec((tb, d), lambda i: (i, 0)),
                  pl.BlockSpec((tb, TOP_K), lambda i: (i, 0)),
                  pl.BlockSpec((1, 1, d), lambda i: ((i // per_b) * 6 + gate_idx, 0, 0)),
                  pl.BlockSpec((1, d), lambda i: (0, 0)),
                  pl.BlockSpec(memory_space=pl.ANY),
                  pl.BlockSpec(memory_space=pl.ANY)],
        out_specs=pl.BlockSpec((tb, d), lambda i: (i, 0)),
        scratch_shapes=[pltpu.SMEM((2, TOP_K, tb), jnp.int32),
                        pltpu.VMEM((2, TOP_K, tb * half // V7X_LANES, V7X_LANES), jnp.int32),
                        pltpu.VMEM((TOP_K, tb, V7X_LANES), jnp.float32),
                        pltpu.VMEM((tb * half // V7X_LANES, V7X_LANES), jnp.float32),
                        pltpu.VMEM((tb * half // V7X_LANES, V7X_LANES), jnp.float32),
                        pltpu.SemaphoreType.DMA((2,)),
                        pltpu.SemaphoreType.DMA((2,))],
        out_shape=jax.ShapeDtypeStruct((t, d), jnp.float32),
        compiler_params=_params(("arbitrary",)),
        name="combine",
    )(x1, y_shared, w_t, mod3, g_final.reshape(1, d), slot3, ys)


def _in_layout(d):
    qw = ATT_HEADS * ATT_HEAD_DIM
    kvw = ATT_KV_HEADS * ATT_HEAD_DIM
    rqk = RET_HEADS * RET_QK_DIM
    rv = RET_HEADS * RET_V_DIM
    order = [("qa", qw), ("ka", kvw), ("va", kvw), ("qr", rqk), ("kr", rqk),
             ("vr", rv), ("gr", rv), ("ga", d), ("gb", d)]
    dst = {}
    off = 0
    for name, width in order:
        dst[name] = off
        off += width
    return dst


def kernel(x, c, w_ada, b_ada, g_norm_mix, w_in, attn_sinks, w_attn_out, w_ret_out, w_o, g_norm_ffn,
           w_router, b_router, w_gate, w_up, w_down, w_sh_gate, w_sh_up, w_sh_down, g_norm_final):
    b, s, d = x.shape
    t = b * s
    depth = w_ada.shape[0]
    bf = jnp.bfloat16
    dst = _in_layout(d)

    c_pad = jnp.zeros((V7X_SUBLANES, d), jnp.float32).at[:b].set(c)
    x2 = x.reshape(t, d)
    for l in range(depth):
        mod = _ada(c_pad, w_ada[l], b_ada[l])
        mod3 = mod[:b].reshape(b * 6, 1, d)

        h = _norm_mod(x2.reshape(b, s, d), g_norm_mix[l], mod3, 0, 1)
        proj, (wg_b, wu_b, wd_b) = _in_proj(h.reshape(t, d), w_in[l].astype(bf),
                                            ((w_gate[l], w_up[l], w_down[l]),), 1024, 768)
        wao_b, wro_b, wo_b = w_attn_out[l].astype(bf), w_ret_out[l].astype(bf), w_o[l].astype(bf)
        wsg_b, wsu_b, wsd_b = w_sh_gate[l].astype(bf), w_sh_up[l].astype(bf), w_sh_down[l].astype(bf)
        proj3 = proj.reshape(b, s, proj.shape[1])
        attn = _attention(proj3, attn_sinks[l], dst["qa"], dst["ka"], dst["va"])
        ret = _retention(proj3, dst["qr"], dst["kr"], dst["vr"], dst["gr"])
        mix = _mix(attn.reshape(t, -1), ret.reshape(t, -1), wao_b, wro_b, proj, dst["ga"], dst["gb"])
        x1 = _out_resid(mix, wo_b, x2, mod3, 2, s)

        h2, h2p, idx3, pos3, w3, cnt = _router(x1, g_norm_ffn[l], mod3, 3, 4,
                                               w_router[l].T.astype(bf), b_router[l], s)
        counts = cnt[:, 0]
        tile = MOE_TILE
        padded = (counts + tile - 1) // tile * tile
        pad_end = jnp.cumsum(padded).astype(jnp.int32)
        pad_start = (pad_end - padded).astype(jnp.int32)
        n_blocks = (t * TOP_K) // tile + N_EXPERTS
        n_used = (pad_end[-1] // tile).reshape(1).astype(jnp.int32)
        blk_first = jnp.arange(n_blocks, dtype=jnp.int32) * tile
        blk_e = jnp.minimum(jnp.sum((pad_end[None, :] <= blk_first[:, None]).astype(jnp.int32), axis=1),
                            N_EXPERTS - 1)
        slab = d // 2 // V7X_LANES
        slot3 = _slots(pad_start, idx3, pos3)
        xs, y_shared = _dispatch(pad_start, pad_end, h2p, h2, wsg_b, wsu_b, wsd_b, slot3,
                                 n_blocks * tile, slab)
        ys = _experts(blk_e, n_used, xs, wg_b, wu_b, wd_b, slab)
        w_t = w3.transpose(0, 2, 1).reshape(t, TOP_K)
        is_last = l == depth - 1
        assert is_last, "the final norm is fused into the last layer's combine"
        x2 = _combine(x1, y_shared, w_t, mod3, 5, g_norm_final, slot3, ys, s)
    return x2.reshape(b, s, d)
```
